```python
import jax, jax.numpy as jnp
from jax import lax
import numpy as np

D_MODEL = 1024
BATCH = 32
SEQ = 256
DEPTH = 1
DEC_BATCH = 4
DEC_SEQ = 4096
PAST_LEN = 512

GRID_W = 64
D_MIX = 1024
HEAD_DIM = 64
N_ATTN_HEADS = 8
N_KV_HEADS = 2
KV_GROUP = N_ATTN_HEADS // N_KV_HEADS
ATTN_WIDTH = N_ATTN_HEADS * HEAD_DIM
KV_WIDTH = N_KV_HEADS * HEAD_DIM
WINDOW = 128
BLOCK = 128
N_GLA_HEADS = 4
GLA_DK = 64
GLA_DV = 128
GLA_K_WIDTH = N_GLA_HEADS * GLA_DK
GLA_V_WIDTH = N_GLA_HEADS * GLA_DV
GATE_RANK = 16
GATE_NORMALIZER = 16.0
CHUNK = 64
ROPE_BASE = 10000.0
EPS = 1e-6
NEG_INF = -1e30
PROJ_SIZES = (ATTN_WIDTH, KV_WIDTH, KV_WIDTH, ATTN_WIDTH, GLA_K_WIDTH, GLA_K_WIDTH, GLA_V_WIDTH, GATE_RANK, GATE_RANK, GLA_V_WIDTH)
D_IN = 2848

kernel_name = "hymba_window_gqa_gla_prefix_dit_step"


def rmsnorm(x, g):
    xf = x.astype(jnp.float32)
    xf = xf * lax.rsqrt(jnp.mean(xf * xf, axis=-1, keepdims=True) + EPS)
    return xf.astype(x.dtype) * g


def adaln(cvec, w_mod, b_mod):
    m = jax.nn.silu(cvec) @ w_mod + b_mod
    shift, scale, gate = jnp.split(m, 3, axis=-1)
    return shift[:, None, :], scale[:, None, :], gate[:, None, :]


def axial_rope(x):
    n_tok = x.shape[1]
    rows = n_tok // GRID_W
    r = jnp.repeat(jnp.arange(rows, dtype=jnp.float32), GRID_W)
    col = jnp.tile(jnp.arange(GRID_W, dtype=jnp.float32), rows)
    half = HEAD_DIM // 2
    nf = half // 2
    freqs = ROPE_BASE ** (-jnp.arange(nf, dtype=jnp.float32) / nf)

    def rot(xs, pos):
        ang = pos[:, None] * freqs[None, :]
        cos = jnp.cos(ang)[:, None, :]
        sin = jnp.sin(ang)[:, None, :]
        x1, x2 = xs[..., :nf], xs[..., nf:]
        return jnp.concatenate([x1 * cos - x2 * sin, x2 * cos + x1 * sin], axis=-1)

    xf = x.astype(jnp.float32)
    out = jnp.concatenate([rot(xf[..., :half], r), rot(xf[..., half:], col)], axis=-1)
    return out.astype(x.dtype)


def sink_softmax_attend(q, parts, sink):
    bsz, nq = q.shape[0], q.shape[1]
    scale = HEAD_DIM ** -0.5
    logits = []
    for k, v, mask in parts:
        s = jnp.einsum('bqkgd,bskd->bkgqs', q, k).astype(jnp.float32) * scale
        if mask is not None:
            s = jnp.where(mask, s, NEG_INF)
        logits.append(s)
    sink_l = jnp.broadcast_to(sink.astype(jnp.float32).reshape(1, N_KV_HEADS, KV_GROUP, 1, 1), (bsz, N_KV_HEADS, KV_GROUP, nq, 1))
    p = jax.nn.softmax(jnp.concatenate(logits + [sink_l], axis=-1), axis=-1)
    outs = []
    off = 0
    for k, v, _ in parts:
        n_k = k.shape[1]
        outs.append(jnp.einsum('bkgqs,bskd->bqkgd', p[..., off:off + n_k].astype(v.dtype), v))
        off += n_k
    return sum(outs).reshape(bsz, nq, ATTN_WIDTH)


def context_attention(q, k, v, sink):
    bsz, n_tok = q.shape[0], q.shape[1]
    nb = n_tok // BLOCK
    qb = q.reshape(bsz, nb, BLOCK, N_KV_HEADS, KV_GROUP, HEAD_DIM).transpose(1, 0, 2, 3, 4, 5)
    out = lax.map(lambda qi: sink_softmax_attend(qi, [(k, v, None)], sink), qb)
    return out.transpose(1, 0, 2, 3).reshape(bsz, n_tok, ATTN_WIDTH)


def latent_attention(q, k, v, k_ctx, v_ctx, sink):
    bsz, n_tok = q.shape[0], q.shape[1]
    nb = n_tok // BLOCK
    qb = q.reshape(bsz, nb, BLOCK, N_KV_HEADS, KV_GROUP, HEAD_DIM).transpose(1, 0, 2, 3, 4, 5)
    pad = ((0, 0), (BLOCK, BLOCK), (0, 0), (0, 0))
    kp = jnp.pad(k, pad).reshape(bsz, nb + 2, BLOCK, N_KV_HEADS, HEAD_DIM)
    vp = jnp.pad(v, pad).reshape(bsz, nb + 2, BLOCK, N_KV_HEADS, HEAD_DIM)

    def window(t):
        return jnp.concatenate([t[:, 0:nb], t[:, 1:nb + 1], t[:, 2:nb + 2]], axis=2).transpose(1, 0, 2, 3, 4)

    blk = jnp.arange(nb)[:, None] * BLOCK
    qpos = blk + jnp.arange(BLOCK)[None, :]
    kpos = blk - BLOCK + jnp.arange(3 * BLOCK)[None, :]
    mask = (jnp.abs(qpos[:, :, None] - kpos[:, None, :]) <= WINDOW) & ((kpos >= 0) & (kpos < n_tok))[:, None, :]

    def block(args):
        qi, ki, vi, mi = args
        return sink_softmax_attend(qi, [(ki, vi, mi), (k_ctx, v_ctx, None)], sink)

    out = lax.map(block, (qb, window(kp), window(vp), mask))
    return out.transpose(1, 0, 2, 3).reshape(bsz, n_tok, ATTN_WIDTH)


def gla_scan(q, k, v, g, s0):
    bsz, n_tok = q.shape[0], q.shape[1]
    n = n_tok // CHUNK

    def rs(t):
        return t.reshape(bsz, n, CHUNK, t.shape[2], t.shape[3]).astype(jnp.float32)

    qf, kf, vf, gf = rs(q), rs(k), rs(v), rs(g)
    gc = jnp.cumsum(gf, axis=2)
    gtot = gc[:, :, -1]
    q_dec = qf * jnp.exp(gc) * (GLA_DK ** -0.5)
    k_inv = kf * jnp.exp(-gc)
    k_end = kf * jnp.exp(gtot[:, :, None] - gc)
    causal = jnp.tril(jnp.ones((CHUNK, CHUNK), dtype=bool))
    a = jnp.where(causal, jnp.einsum('bnihd,bnjhd->bnhij', q_dec, k_inv), 0.0)
    o_intra = jnp.einsum('bnhij,bnjhv->bnihv', a, vf)
    ds = jnp.einsum('bnjhd,bnjhv->bnhdv', k_end, vf)

    def step(s, xs):
        decay, d = xs
        return jnp.exp(decay)[..., None] * s + d, s

    s_fin, s_prev = lax.scan(step, s0.astype(jnp.float32), (gtot.transpose(1, 0, 2, 3), ds.transpose(1, 0, 2, 3, 4)))
    o_inter = jnp.einsum('bnihd,nbhdv->bnihv', q_dec, s_prev)
    o = (o_inter + o_intra).reshape(bsz, n_tok, N_GLA_HEADS, GLA_DV)
    return o.astype(v.dtype), s_fin.astype(v.dtype)


def gla_bidir(q, k, v, g_f, g_b, s0_f, s0_b):
    o_f, s_f = gla_scan(q, k, v, g_f, s0_f)
    o_b, s_b = gla_scan(q[:, ::-1], k[:, ::-1], v[:, ::-1], g_b[:, ::-1], s0_b)
    return o_f + o_b[:, ::-1], s_f, s_b


def project(x, shift, scale, norm_g, w_in, w_gk_f, b_gk_f, w_gk_b, b_gk_b):
    bsz, n_tok, _ = x.shape
    h = rmsnorm(x, norm_g) * (1 + scale) + shift
    z = h @ w_in
    q, k, v, za, qg, kg, vg, lr_f, lr_b, zg = jnp.split(z, list(np.cumsum(PROJ_SIZES)[:-1]), axis=-1)
    q = q.reshape(bsz, n_tok, N_ATTN_HEADS, HEAD_DIM)
    k = k.reshape(bsz, n_tok, N_KV_HEADS, HEAD_DIM)
    v = v.reshape(bsz, n_tok, N_KV_HEADS, HEAD_DIM)
    qg = qg.reshape(bsz, n_tok, N_GLA_HEADS, GLA_DK)
    kg = kg.reshape(bsz, n_tok, N_GLA_HEADS, GLA_DK)
    vg = vg.reshape(bsz, n_tok, N_GLA_HEADS, GLA_DV)
    g_f = (jax.nn.log_sigmoid((lr_f @ w_gk_f + b_gk_f).astype(jnp.float32)) / GATE_NORMALIZER).reshape(bsz, n_tok, N_GLA_HEADS, GLA_DK)
    g_b = (jax.nn.log_sigmoid((lr_b @ w_gk_b + b_gk_b).astype(jnp.float32)) / GATE_NORMALIZER).reshape(bsz, n_tok, N_GLA_HEADS, GLA_DK)
    return q, k, v, za, qg, kg, vg, g_f, g_b, zg


def merge(o_a, za, o_g, zg, gla_norm_g, w_out):
    bsz, n_tok = o_a.shape[0], o_a.shape[1]
    ya = o_a * jax.nn.silu(za)
    yg = (rmsnorm(o_g, gla_norm_g).reshape(bsz, n_tok, GLA_V_WIDTH)) * jax.nn.silu(zg)
    return jnp.concatenate([ya, yg], axis=-1) @ w_out


def setup_inputs(seed: int = 0) -> dict:
    key = jax.random.key(seed)
    ks = jax.random.split(key, 20)
    f32 = jnp.float32
    nrm = lambda k, shape, s: jax.random.normal(k, shape, f32) * s
    return {
        "x_prompt": nrm(ks[0], (BATCH, SEQ, D_MODEL), 1.0),
        "x_sample": nrm(ks[1], (DEC_BATCH, DEC_SEQ, D_MODEL), 1.0),
        "c": nrm(ks[2], (DEC_BATCH, D_MODEL), 1.0),
        "cache_k": nrm(ks[3], (DEC_BATCH, DEPTH, PAST_LEN, N_KV_HEADS, HEAD_DIM), 1.0),
        "cache_v": nrm(ks[4], (DEC_BATCH, DEPTH, PAST_LEN, N_KV_HEADS, HEAD_DIM), 1.0),
        "state_gla": nrm(ks[5], (DEC_BATCH, DEPTH, 2, N_GLA_HEADS, GLA_DK, GLA_DV), 0.5),
        "c_ctx": nrm(ks[6], (D_MODEL,), 1.0),
        "w_mod": nrm(ks[7], (DEPTH, D_MODEL, 3 * D_MODEL), 0.5 * D_MODEL ** -0.5),
        "b_mod": nrm(ks[8], (DEPTH, 3 * D_MODEL), 0.02),
        "norm_g": 1.0 + nrm(ks[9], (DEPTH, D_MODEL), 0.02),
        "w_in": nrm(ks[10], (DEPTH, D_MODEL, D_IN), D_MODEL ** -0.5),
        "w_gk_f": nrm(ks[11], (DEPTH, GATE_RANK, GLA_K_WIDTH), GATE_RANK ** -0.5),
        "b_gk_f": nrm(ks[12], (DEPTH, GLA_K_WIDTH), 0.1),
        "w_gk_b": nrm(ks[13], (DEPTH, GATE_RANK, GLA_K_WIDTH), GATE_RANK ** -0.5),
        "b_gk_b": nrm(ks[14], (DEPTH, GLA_K_WIDTH), 0.1),
        "sink": nrm(ks[15], (DEPTH, N_ATTN_HEADS), 0.5),
        "gla_norm_g": 1.0 + nrm(ks[16], (DEPTH, GLA_DV), 0.02),
        "w_out": nrm(ks[17], (DEPTH, D_MIX, D_MODEL), D_MIX ** -0.5),
        "final_norm_g": 1.0 + nrm(ks[18], (D_MODEL,), 0.02),
    }


def reference(x_prompt, x_sample, c, cache_k, cache_v, state_gla, c_ctx, w_mod, b_mod, norm_g, w_in,
              w_gk_f, b_gk_f, w_gk_b, b_gk_b, sink, gla_norm_g, w_out, final_norm_g):
    xp = x_prompt
    xs = x_sample
    bp = xp.shape[0]
    new_k_list, new_v_list, new_s_list = [], [], []
    for l in range(DEPTH):
        shift, scale, gate = adaln(c_ctx[None, :], w_mod[l], b_mod[l])
        q, k, v, za, qg, kg, vg, g_f, g_b, zg = project(xp, shift, scale, norm_g[l], w_in[l], w_gk_f[l], b_gk_f[l], w_gk_b[l], b_gk_b[l])
        o_a = context_attention(q, k, v, sink[l])
        s_zero = jnp.zeros((bp, N_GLA_HEADS, GLA_DK, GLA_DV), xp.dtype)
        o_g, s_f, s_b = gla_bidir(qg, kg, vg, g_f, g_b, s_zero, s_zero)
        xp = xp + gate * merge(o_a, za, o_g, zg, gla_norm_g[l], w_out[l])
        new_k_list.append(k)
        new_v_list.append(v)
        new_s_list.append(jnp.stack([s_f, s_b], axis=1))

        shift, scale, gate = adaln(c, w_mod[l], b_mod[l])
        q, k, v, za, qg, kg, vg, g_f, g_b, zg = project(xs, shift, scale, norm_g[l], w_in[l], w_gk_f[l], b_gk_f[l], w_gk_b[l], b_gk_b[l])
        q = axial_rope(q)
        k = axial_rope(k)
        o_a = latent_attention(q, k, v, cache_k[:, l], cache_v[:, l], sink[l])
        o_g, _, _ = gla_bidir(qg, kg, vg, g_f, g_b, state_gla[:, l, 0], state_gla[:, l, 1])
        xs = xs + gate * merge(o_a, za, o_g, zg, gla_norm_g[l], w_out[l])

    y_prompt = rmsnorm(xp, final_norm_g)
    y_sample = rmsnorm(xs, final_norm_g)
    new_k = jnp.stack(new_k_list, axis=1)
    new_v = jnp.stack(new_v_list, axis=1)
    new_state = jnp.stack(new_s_list, axis=1)
    return (y_prompt, y_sample, new_k, new_v, new_state)
```

```python
import functools

import jax
import jax.numpy as jnp
from jax import lax
from jax.experimental import pallas as pl
from jax.experimental.pallas import tpu as pltpu

D_MODEL = 1024
GRID_W = 64
HEAD_DIM = 64
N_ATTN_HEADS = 8
N_KV_HEADS = 2
ATTN_WIDTH = 512
KV_WIDTH = 128
BLOCK = 128
N_GLA_HEADS = 4
GLA_DK = 64
GLA_DV = 128
GLA_K_WIDTH = 256
GLA_V_WIDTH = 512
GATE_RANK = 16
GATE_NORMALIZER = 16.0
CHUNK = 64
ROPE_BASE = 10000.0
EPS = 1e-6
NEG_INF = -1e30

LANES = 128
TOKEN_TILE = 256
MOD_TILE = 512
VMEM_LIMIT = 48 * 1024 * 1024

F32 = jnp.float32
BF16 = jnp.bfloat16


def _dot(a, b):
    return jnp.dot(a, b, preferred_element_type=F32)


def _dot_nt(a, b):
    return lax.dot_general(a, b, (((1,), (1,)), ((), ())), preferred_element_type=F32)


def _dot_tn(a, b):
    return lax.dot_general(a, b, (((0,), (0,)), ((), ())), preferred_element_type=F32)


def _silu(x):
    return x / (1.0 + jnp.exp(-x))


def _log_sigmoid(x):
    return jnp.minimum(x, 0.0) - jnp.log1p(jnp.exp(-jnp.abs(x)))


def _params(*sem):
    return pltpu.CompilerParams(dimension_semantics=sem, vmem_limit_bytes=VMEM_LIMIT)


def _mod_kernel(c_ref, w_ref, b_ref, o_ref):
    c = c_ref[...]
    o_ref[...] = jnp.dot(_silu(c), w_ref[...], preferred_element_type=F32,
                         precision=lax.Precision.HIGHEST) + b_ref[...]


def _modulation(cvecs, w_mod, b_mod):
    rows = cvecs.shape[0]
    n_out = w_mod.shape[1]
    return pl.pallas_call(
        _mod_kernel,
        grid=(n_out // MOD_TILE,),
        in_specs=[
            pl.BlockSpec((rows, D_MODEL), lambda j: (0, 0)),
            pl.BlockSpec((D_MODEL, MOD_TILE), lambda j: (0, j)),
            pl.BlockSpec((1, MOD_TILE), lambda j: (0, j)),
        ],
        out_specs=pl.BlockSpec((rows, MOD_TILE), lambda j: (0, j)),
        out_shape=jax.ShapeDtypeStruct((rows, n_out), F32),
        compiler_params=_params("arbitrary"),
        name="adaln_mod",
    )(cvecs, w_mod, b_mod)


_Q0, _K0, _V0, _ZA0, _QG0, _KG0, _VG0, _ZG0, _W1_COLS = 0, 512, 640, 768, 1280, 1536, 1792, 2304, 2816


def _rope(x, cos, sin_signed):
    n = x.shape[1] // LANES
    lane = lax.broadcasted_iota(jnp.int32, (x.shape[0], LANES), 1)
    first = (lane % 32) < 16
    outs = []
    for c in range(n):
        xc = x[:, c * LANES:(c + 1) * LANES]
        partner = jnp.where(first, pltpu.roll(xc, LANES - 16, 1), pltpu.roll(xc, 16, 1))
        outs.append(xc * cos + partner * sin_signed)
    return outs[0] if n == 1 else jnp.concatenate(outs, axis=-1)


def _proj_kernel(*refs, rope):
    if rope:
        (x_ref, mod_ref, ng_ref, w1_ref, wlr_ref, wgk_ref, bgk_ref, cos_ref, sin_ref,
         q_ref, k_ref, v_ref, za_ref, qg_ref, kg_ref, vg_ref, zg_ref, gf_ref, gb_ref) = refs
    else:
        (x_ref, mod_ref, ng_ref, w1_ref, wlr_ref, wgk_ref, bgk_ref,
         q_ref, k_ref, v_ref, za_ref, qg_ref, kg_ref, vg_ref, zg_ref, gf_ref, gb_ref) = refs
    x = x_ref[...]
    xn = x * lax.rsqrt(jnp.mean(x * x, axis=-1, keepdims=True) + EPS)
    mod = mod_ref[0]
    shift = mod[:, 0:D_MODEL]
    scale = mod[:, D_MODEL:2 * D_MODEL]
    h = ((xn * ng_ref[...]) * (1.0 + scale) + shift).astype(BF16)

    def piece(c0, c1):
        return _dot(h, w1_ref[:, c0:c1])

    q = piece(_Q0, _K0)
    k = piece(_K0, _V0)
    if rope:
        cos = cos_ref[...]
        sin = sin_ref[...]
        q = _rope(q, cos, sin)
        k = _rope(k, cos, sin)
    q_ref[...] = q.astype(BF16)
    k_ref[...] = k
    v_ref[...] = piece(_V0, _ZA0)
    za_ref[...] = piece(_ZA0, _QG0).astype(BF16)
    qg_ref[...] = piece(_QG0, _KG0)
    kg_ref[...] = piece(_KG0, _VG0)
    vg_ref[...] = piece(_VG0, _ZG0).astype(BF16)
    zg_ref[...] = piece(_ZG0, _W1_COLS).astype(BF16)
    lr = _dot(h, wlr_ref[...]).astype(BF16)
    g = _log_sigmoid(_dot(lr, wgk_ref[...]) + bgk_ref[...]) * (1.0 / GATE_NORMALIZER)
    gf_ref[...] = g[:, 0:GLA_K_WIDTH]
    gb_ref[...] = g[:, GLA_K_WIDTH:2 * GLA_K_WIDTH]


def _project(x2d, mod3, mod_row_of_tile, norm_g, w1, wlr, wgk, bgk, rope_tabs):
    n_tok = x2d.shape[0]
    tm = TOKEN_TILE
    rope = rope_tabs is not None
    row = lambda i: (i, 0)
    const = lambda i: (0, 0)
    in_specs = [
        pl.BlockSpec((tm, D_MODEL), row),
        pl.BlockSpec((1, 1, 3 * D_MODEL), lambda i: (mod_row_of_tile(i), 0, 0)),
        pl.BlockSpec((1, D_MODEL), const),
        pl.BlockSpec((D_MODEL, _W1_COLS), const),
        pl.BlockSpec((D_MODEL, LANES), const),
        pl.BlockSpec((LANES, 2 * GLA_K_WIDTH), const),
        pl.BlockSpec((1, 2 * GLA_K_WIDTH), const),
    ]
    args = [x2d, mod3, norm_g, w1, wlr, wgk, bgk]
    if rope:
        tiles_per_seq = rope_tabs[0].shape[0] // tm
        pos = lambda i: (i % tiles_per_seq, 0)
        in_specs += [pl.BlockSpec((tm, LANES), pos), pl.BlockSpec((tm, LANES), pos)]
        args += list(rope_tabs)
    widths = [(ATTN_WIDTH, BF16), (KV_WIDTH, F32), (KV_WIDTH, F32), (ATTN_WIDTH, BF16),
              (GLA_K_WIDTH, F32), (GLA_K_WIDTH, F32), (GLA_V_WIDTH, BF16), (GLA_V_WIDTH, BF16),
              (GLA_K_WIDTH, F32), (GLA_K_WIDTH, F32)]
    return pl.pallas_call(
        functools.partial(_proj_kernel, rope=rope),
        grid=(n_tok // tm,),
        in_specs=in_specs,
        out_specs=[pl.BlockSpec((tm, w), row) for w, _ in widths],
        out_shape=[jax.ShapeDtypeStruct((n_tok, w), dt) for w, dt in widths],
        compiler_params=_params("arbitrary"),
        name="project_rope" if rope else "project",
    )(*args)


def _split_kv(x):
    lane = lax.broadcasted_iota(jnp.int32, x.shape, 1)
    lo = lane < HEAD_DIM
    xr = pltpu.roll(x, HEAD_DIM, 1)
    zero = jnp.zeros_like(x)
    h0 = (jnp.where(lo, x, zero).astype(BF16), jnp.where(lo, zero, xr).astype(BF16))
    h1 = (jnp.where(lo, xr, zero).astype(BF16), jnp.where(lo, zero, x).astype(BF16))
    return h0, h1


def _attend_pair(qp, parts, sink_e, sink_o):
    scale = HEAD_DIM ** -0.5
    out = None
    for e, sink in ((0, sink_e), (1, sink_o)):
        logits = []
        m = None
        for part in parts:
            s = _dot_nt(qp, part[e]) * scale
            if part[4] is not None:
                s = jnp.where(part[4], s, NEG_INF)
            logits.append(s)
            pm = jnp.max(s, axis=-1, keepdims=True)
            m = pm if m is None else jnp.maximum(m, pm)
        m = jnp.maximum(m, sink)
        den = jnp.exp(sink - m)
        acc = None
        for s, part in zip(logits, parts):
            p = jnp.exp(s - m)
            den = den + jnp.sum(p, axis=-1, keepdims=True)
            pv = _dot(p.astype(BF16), part[2 + e])
            acc = pv if acc is None else acc + pv
        acc = acc / den
        out = acc if out is None else out + acc
    return out


def _ctx_attn_kernel(sink_ref, q_ref, k_ref, v_ref, o_ref):
    kh = _split_kv(k_ref[...])
    vh = _split_kv(v_ref[...])
    for p in range(N_ATTN_HEADS // 2):
        j = p // 2
        parts = [(kh[j][0], kh[j][1], vh[j][0], vh[j][1], None)]
        o = _attend_pair(q_ref[:, p * LANES:(p + 1) * LANES], parts, sink_ref[2 * p], sink_ref[2 * p + 1])
        o_ref[:, p * LANES:(p + 1) * LANES] = o.astype(o_ref.dtype)


def _context_attention(q, k, v, sink, seq):
    n_tok = q.shape[0]
    row = lambda b: (b, 0)
    return pl.pallas_call(
        _ctx_attn_kernel,
        grid=(n_tok // seq,),
        in_specs=[
            pl.BlockSpec(memory_space=pltpu.SMEM),
            pl.BlockSpec((seq, ATTN_WIDTH), row),
            pl.BlockSpec((seq, KV_WIDTH), row),
            pl.BlockSpec((seq, KV_WIDTH), row),
        ],
        out_specs=pl.BlockSpec((seq, ATTN_WIDTH), row),
        out_shape=jax.ShapeDtypeStruct((n_tok, ATTN_WIDTH), BF16),
        compiler_params=_params("arbitrary"),
        name="context_attention",
    )(sink, q, k, v)


def _lat_attn_kernel(sink_ref, q_ref, kp_ref, ko_ref, kn_ref, vp_ref, vo_ref, vn_ref, ck_ref, cv_ref, o_ref,
                     *, n_blocks):
    i = pl.program_id(1)
    kwin = jnp.concatenate([kp_ref[...], ko_ref[...], kn_ref[...]], axis=0)
    vwin = jnp.concatenate([vp_ref[...], vo_ref[...], vn_ref[...]], axis=0)
    kh = _split_kv(kwin)
    vh = _split_kv(vwin)
    ckh = _split_kv(ck_ref[0])
    cvh = _split_kv(cv_ref[0])
    r = lax.broadcasted_iota(jnp.int32, (BLOCK, 3 * BLOCK), 0)
    c = lax.broadcasted_iota(jnp.int32, (BLOCK, 3 * BLOCK), 1)
    d = c - r
    c_lo = jnp.where(i > 0, 0, BLOCK)
    c_hi = jnp.where(i < n_blocks - 1, 3 * BLOCK, 2 * BLOCK)
    mask = (d >= 0) & (d <= 2 * BLOCK) & (c >= c_lo) & (c < c_hi)
    for p in range(N_ATTN_HEADS // 2):
        j = p // 2
        parts = [(kh[j][0], kh[j][1], vh[j][0], vh[j][1], mask),
                 (ckh[j][0], ckh[j][1], cvh[j][0], cvh[j][1], None)]
        o = _attend_pair(q_ref[:, p * LANES:(p + 1) * LANES], parts, sink_ref[2 * p], sink_ref[2 * p + 1])
        o_ref[:, p * LANES:(p + 1) * LANES] = o.astype(o_ref.dtype)


def _latent_attention(q, k, v, ck, cv, sink, n_batch, seq):
    nb = seq // BLOCK
    own = lambda b, i: (b * nb + i, 0)
    prev = lambda b, i: (b * nb + jnp.maximum(i - 1, 0), 0)
    nxt = lambda b, i: (b * nb + jnp.minimum(i + 1, nb - 1), 0)
    ctx = lambda b, i: (b, 0, 0)
    kv_spec = lambda f: pl.BlockSpec((BLOCK, KV_WIDTH), f)
    past = ck.shape[1]
    return pl.pallas_call(
        functools.partial(_lat_attn_kernel, n_blocks=nb),
        grid=(n_batch, nb),
        in_specs=[
            pl.BlockSpec(memory_space=pltpu.SMEM),
            pl.BlockSpec((BLOCK, ATTN_WIDTH), own),
            kv_spec(prev), kv_spec(own), kv_spec(nxt),
            kv_spec(prev), kv_spec(own), kv_spec(nxt),
            pl.BlockSpec((1, past, KV_WIDTH), ctx),
            pl.BlockSpec((1, past, KV_WIDTH), ctx),
        ],
        out_specs=pl.BlockSpec((BLOCK, ATTN_WIDTH), own),
        out_shape=jax.ShapeDtypeStruct((n_batch * seq, ATTN_WIDTH), BF16),
        compiler_params=_params("arbitrary", "arbitrary"),
        name="latent_attention",
    )(sink, q, k, k, k, v, v, v, ck, cv)


def _cumsum_rows(x, reverse):
    n = x.shape[0]
    row = lax.broadcasted_iota(jnp.int32, x.shape, 0)
    s = 1
    while s < n:
        if reverse:
            x = x + jnp.where(row < n - s, pltpu.roll(x, n - s, 0), 0.0)
        else:
            x = x + jnp.where(row >= s, pltpu.roll(x, s, 0), 0.0)
        s *= 2
    return x


def _gla_chunk(q, k, v, g, st_ref, d, o_ref, rows, reverse):
    gc = _cumsum_rows(g, reverse)
    gtot = gc[0:1, :] if reverse else gc[CHUNK - 1:CHUNK, :]
    q_dec = q * jnp.exp(gc) * (GLA_DK ** -0.5)
    k_inv = k * jnp.exp(-gc)
    k_end = k * jnp.exp(gtot - gc)
    decay = jnp.exp(gtot)
    pk = 2 * GLA_DK
    pv = 2 * GLA_DV
    lane_k = lax.broadcasted_iota(jnp.int32, (CHUNK, pk), 1)
    lo_k = lane_k < GLA_DK
    row_k = lax.broadcasted_iota(jnp.int32, (CHUNK, pk), 0)
    key = lane_k % CHUNK
    tri = (key >= row_k) if reverse else (key <= row_k)
    lane_v = lax.broadcasted_iota(jnp.int32, (CHUNK, pv), 1)
    lo_v = lane_v < GLA_DV
    srow = lax.broadcasted_iota(jnp.int32, (pv, pk), 0)
    slane = lax.broadcasted_iota(jnp.int32, (pv, pk), 1)
    diag = (srow < GLA_DV) == (slane < GLA_DK)
    for p in range(N_GLA_HEADS // 2):
        qd = q_dec[:, p * pk:(p + 1) * pk].astype(BF16)
        ki = k_inv[:, p * pk:(p + 1) * pk]
        ke = k_end[:, p * pk:(p + 1) * pk].astype(BF16)
        vp = v[:, p * pv:(p + 1) * pv]
        vpf = vp.astype(F32)
        kst = jnp.concatenate([jnp.where(lo_k, ki, 0.0), jnp.where(lo_k, 0.0, ki)], axis=0).astype(BF16)
        a = jnp.where(tri, _dot_nt(qd, kst), 0.0).astype(BF16)
        vbd = jnp.concatenate([jnp.where(lo_v, vpf, 0.0), jnp.where(lo_v, 0.0, vpf)], axis=0).astype(BF16)
        st = st_ref[d, p]
        o = _dot(a, vbd) + _dot_nt(qd, st.astype(BF16))
        o_ref[rows, p * pv:(p + 1) * pv] = o.astype(o_ref.dtype)
        ds = _dot_tn(vp, ke)
        st_ref[d, p] = st * decay[:, p * pk:(p + 1) * pk] + jnp.where(diag, ds, 0.0)


def _gla_kernel(*refs, n_steps, has_init):
    if has_init:
        (qf, kf, vf, gf, qb, kb, vb, gb, s0_ref, of_ref, ob_ref, sout_ref, st_ref) = refs
    else:
        (qf, kf, vf, gf, qb, kb, vb, gb, of_ref, ob_ref, sout_ref, st_ref) = refs
        s0_ref = None
    i = pl.program_id(1)

    def blocks():
        for d in range(2):
            for h in range(N_GLA_HEADS):
                p, e = divmod(h, 2)
                yield d, h, (d, p, slice(e * GLA_DV, (e + 1) * GLA_DV), slice(e * GLA_DK, (e + 1) * GLA_DK))

    @pl.when(i == 0)
    def _():
        st_ref[...] = jnp.zeros_like(st_ref)
        if has_init:
            for d, h, idx in blocks():
                st_ref[idx] = s0_ref[0, d, h]

    n_chunks = qf.shape[0] // CHUNK
    for c in range(n_chunks):
        rows = slice(c * CHUNK, (c + 1) * CHUNK)
        _gla_chunk(qf[rows, :], kf[rows, :], vf[rows, :], gf[rows, :], st_ref, 0, of_ref, rows, False)
    for c in reversed(range(n_chunks)):
        rows = slice(c * CHUNK, (c + 1) * CHUNK)
        _gla_chunk(qb[rows, :], kb[rows, :], vb[rows, :], gb[rows, :], st_ref, 1, ob_ref, rows, True)

    @pl.when(i == n_steps - 1)
    def _():
        for d, h, idx in blocks():
            sout_ref[0, d, h] = st_ref[idx]


def _gla(qg, kg, vg, gf, gb, s0t, n_batch, seq):
    tm = TOKEN_TILE
    ns = seq // tm
    fwd = lambda b, i: (b * ns + i, 0)
    bwd = lambda b, i: (b * ns + ns - 1 - i, 0)
    st_map = lambda b, i: (b, 0, 0, 0, 0)
    st_block = (1, 2, N_GLA_HEADS, GLA_DV, GLA_DK)
    in_specs = []
    args = []
    for m in (fwd, bwd):
        in_specs += [pl.BlockSpec((tm, GLA_K_WIDTH), m), pl.BlockSpec((tm, GLA_K_WIDTH), m),
                     pl.BlockSpec((tm, GLA_V_WIDTH), m), pl.BlockSpec((tm, GLA_K_WIDTH), m)]
    args = [qg, kg, vg, gf, qg, kg, vg, gb]
    if s0t is not None:
        in_specs.append(pl.BlockSpec(st_block, st_map))
        args.append(s0t)
    n_tok = n_batch * seq
    return pl.pallas_call(
        functools.partial(_gla_kernel, n_steps=ns, has_init=s0t is not None),
        grid=(n_batch, ns),
        in_specs=in_specs,
        out_specs=[pl.BlockSpec((tm, GLA_V_WIDTH), fwd), pl.BlockSpec((tm, GLA_V_WIDTH), bwd),
                   pl.BlockSpec(st_block, st_map)],
        out_shape=[jax.ShapeDtypeStruct((n_tok, GLA_V_WIDTH), BF16),
                   jax.ShapeDtypeStruct((n_tok, GLA_V_WIDTH), BF16),
                   jax.ShapeDtypeStruct((n_batch,) + st_block[1:], F32)],
        scratch_shapes=[pltpu.VMEM((2, N_GLA_HEADS // 2, 2 * GLA_DV, 2 * GLA_DK), F32)],
        compiler_params=_params("arbitrary", "arbitrary"),
        name="gla_init" if s0t is not None else "gla_zero",
    )(*args)


def _merge_kernel(x_ref, mod_ref, oa_ref, za_ref, of_ref, ob_ref, zg_ref, gng_ref, wo_ref, fng_ref, y_ref):
    ya = oa_ref[...].astype(F32) * _silu(za_ref[...].astype(F32))
    og = of_ref[...].astype(F32) + ob_ref[...].astype(F32)
    zg = zg_ref[...].astype(F32)
    gng = gng_ref[...]
    parts = [ya]
    for h in range(N_GLA_HEADS):
        oh = og[:, h * GLA_DV:(h + 1) * GLA_DV]
        nh = oh * lax.rsqrt(jnp.mean(oh * oh, axis=-1, keepdims=True) + EPS)
        parts.append((nh * gng) * _silu(zg[:, h * GLA_DV:(h + 1) * GLA_DV]))
    yin = jnp.concatenate(parts, axis=-1).astype(BF16)
    gate = mod_ref[0][:, 2 * D_MODEL:3 * D_MODEL]
    y = x_ref[...] + gate * _dot(yin, wo_ref[...])
    y_ref[...] = (y * lax.rsqrt(jnp.mean(y * y, axis=-1, keepdims=True) + EPS)) * fng_ref[...]


def _merge(x2d, mod3, mod_row_of_tile, oa, za, of, ob, zg, gla_norm_g, w_out, final_norm_g):
    n_tok = x2d.shape[0]
    tm = TOKEN_TILE
    row = lambda i: (i, 0)
    const = lambda i: (0, 0)
    wide = pl.BlockSpec((tm, GLA_V_WIDTH), row)
    return pl.pallas_call(
        _merge_kernel,
        grid=(n_tok // tm,),
        in_specs=[
            pl.BlockSpec((tm, D_MODEL), row),
            pl.BlockSpec((1, 1, 3 * D_MODEL), lambda i: (mod_row_of_tile(i), 0, 0)),
            wide, wide, wide, wide, wide,
            pl.BlockSpec((1, GLA_DV), const),
            pl.BlockSpec((D_MODEL, D_MODEL), const),
            pl.BlockSpec((1, D_MODEL), const),
        ],
        out_specs=pl.BlockSpec((tm, D_MODEL), row),
        out_shape=jax.ShapeDtypeStruct((n_tok, D_MODEL), F32),
        compiler_params=_params("arbitrary"),
        name="merge",
    )(x2d, mod3, oa, za, of, ob, zg, gla_norm_g, w_out, final_norm_g)


def _rope_tables(n_tok):
    t = jnp.arange(n_tok)
    rowp = (t // GRID_W).astype(F32)
    colp = (t % GRID_W).astype(F32)
    nf = HEAD_DIM // 4
    freqs = ROPE_BASE ** (-jnp.arange(nf, dtype=F32) / nf)
    lane = jnp.arange(LANES)
    pos = jnp.where(((lane % HEAD_DIM) < HEAD_DIM // 2)[None, :], rowp[:, None], colp[:, None])
    ang = pos * freqs[lane % nf][None, :]
    sign = jnp.where((lane % (2 * nf)) < nf, -1.0, 1.0).astype(F32)
    return jnp.cos(ang), jnp.sin(ang) * sign[None, :]


def kernel(x_prompt, x_sample, c, cache_k, cache_v, state_gla, c_ctx, w_mod, b_mod, norm_g, w_in,
           w_gk_f, b_gk_f, w_gk_b, b_gk_b, sink, gla_norm_g, w_out, final_norm_g):
    depth = w_in.shape[0]
    assert depth == 1, "single-layer step"
    l = 0
    bp, sp, _ = x_prompt.shape
    bs, ss, _ = x_sample.shape

    wl = w_in[l]
    lr0 = _VG0 + GLA_V_WIDTH
    w1 = jnp.concatenate([wl[:, :lr0], wl[:, lr0 + 2 * GATE_RANK:]], axis=1).astype(BF16)
    wlr = jnp.pad(wl[:, lr0:lr0 + 2 * GATE_RANK], ((0, 0), (0, LANES - 2 * GATE_RANK))).astype(BF16)
    wgk = jnp.zeros((LANES, 2 * GLA_K_WIDTH), F32)
    wgk = wgk.at[0:GATE_RANK, 0:GLA_K_WIDTH].set(w_gk_f[l])
    wgk = wgk.at[GATE_RANK:2 * GATE_RANK, GLA_K_WIDTH:].set(w_gk_b[l]).astype(BF16)
    bgk = jnp.concatenate([b_gk_f[l], b_gk_b[l]])[None, :]
    wo = w_out[l].astype(BF16)
    ng = norm_g[l][None, :]
    gng = gla_norm_g[l][None, :]
    fng = final_norm_g[None, :]
    sk = sink[l]

    n_rows = 8
    cvecs = jnp.concatenate([c_ctx[None, :], c, jnp.zeros((n_rows - 1 - bs, D_MODEL), F32)], axis=0)
    mod3 = _modulation(cvecs, w_mod[l], b_mod[l][None, :]).reshape(n_rows, 1, 3 * D_MODEL)

    xp2 = x_prompt.reshape(bp * sp, D_MODEL)
    ctx_row = lambda i: 0
    q, k, v, za, qg, kg, vg, zg, gf, gb = _project(xp2, mod3, ctx_row, ng, w1, wlr, wgk, bgk, None)
    oa = _context_attention(q, k, v, sk, sp)
    of, ob, st = _gla(qg, kg, vg, gf, gb, None, bp, sp)
    y_prompt = _merge(xp2, mod3, ctx_row, oa, za, of, ob, zg, gng, wo, fng).reshape(bp, sp, D_MODEL)
    new_k = k.reshape(bp, 1, sp, N_KV_HEADS, HEAD_DIM)
    new_v = v.reshape(bp, 1, sp, N_KV_HEADS, HEAD_DIM)
    new_state = jnp.swapaxes(st, -1, -2)[:, None]

    xs2 = x_sample.reshape(bs * ss, D_MODEL)
    tiles_per_seq = ss // TOKEN_TILE
    lat_row = lambda i: 1 + i // tiles_per_seq
    q, k, v, za, qg, kg, vg, zg, gf, gb = _project(xs2, mod3, lat_row, ng, w1, wlr, wgk, bgk, _rope_tables(ss))
    past = cache_k.shape[2]
    ck = cache_k[:, l].reshape(bs, past, KV_WIDTH)
    cv = cache_v[:, l].reshape(bs, past, KV_WIDTH)
    oa = _latent_attention(q, k, v, ck, cv, sk, bs, ss)
    s0t = jnp.swapaxes(state_gla[:, l], -1, -2)
    of, ob, _ = _gla(qg, kg, vg, gf, gb, s0t, bs, ss)
    y_sample = _merge(xs2, mod3, lat_row, oa, za, of, ob, zg, gng, wo, fng).reshape(bs, ss, D_MODEL)

    return (y_prompt, y_sample, new_k, new_v, new_state)
```

```python
import functools
import math

import numpy as np
import jax
import jax.numpy as jnp
from jax import lax
from jax.experimental import pallas as pl
from jax.experimental.pallas import tpu as pltpu

D_MODEL = 1024
GRID_W = 64
HEAD_DIM = 64
N_ATTN_HEADS = 8
N_KV_HEADS = 2
ATTN_WIDTH = 512
KV_WIDTH = 128
BLOCK = 128
N_GLA_HEADS = 4
GLA_DK = 64
GLA_DV = 128
GLA_K_WIDTH = 256
GLA_V_WIDTH = 512
GATE_RANK = 16
GATE_NORMALIZER = 16.0
CHUNK = 64
ROPE_BASE = 10000.0
EPS = 1e-6
NEG_INF = -1e30

LANES = 128
TOKEN_TILE = 512
GLA_TILE = 256
ATTN_TILE = 2048
CTX_ATTN_BATCH = 8
PIPE_SLOTS = 4
KVP_WIDTH = 8 * LANES
N_PLACEMENTS = 2 * N_KV_HEADS
SOFTMAX_ROWS = 32
MOD_TILE = 512
VMEM_LIMIT = 56 * 1024 * 1024
LOG2E = math.log2(math.e)

F32 = jnp.float32
BF16 = jnp.bfloat16


def _dot(a, b):
    return jnp.dot(a, b, preferred_element_type=F32)


def _dot_nt(a, b):
    return lax.dot_general(a, b, (((1,), (1,)), ((), ())), preferred_element_type=F32)


def _dot_tn(a, b):
    return lax.dot_general(a, b, (((0,), (0,)), ((), ())), preferred_element_type=F32)


def _silu(x):
    return x / (1.0 + jnp.exp(-x))


def _log_sigmoid(x):
    return jnp.minimum(x, 0.0) - jnp.log1p(jnp.exp(-jnp.abs(x)))


def _params(*sem):
    return pltpu.CompilerParams(dimension_semantics=sem, vmem_limit_bytes=VMEM_LIMIT)


def _mod_kernel(c_ref, w_ref, b_ref, o_ref):
    c = c_ref[...]
    o_ref[...] = jnp.dot(_silu(c), w_ref[...], preferred_element_type=F32,
                         precision=lax.Precision.HIGHEST) + b_ref[...]


def _modulation(cvecs, w_mod, b_mod):
    rows = cvecs.shape[0]
    n_out = w_mod.shape[1]
    return pl.pallas_call(
        _mod_kernel,
        grid=(n_out // MOD_TILE,),
        in_specs=[
            pl.BlockSpec((rows, D_MODEL), lambda j: (0, 0)),
            pl.BlockSpec((D_MODEL, MOD_TILE), lambda j: (0, j)),
            pl.BlockSpec((1, MOD_TILE), lambda j: (0, j)),
        ],
        out_specs=pl.BlockSpec((rows, MOD_TILE), lambda j: (0, j)),
        out_shape=jax.ShapeDtypeStruct((rows, n_out), F32),
        compiler_params=_params("arbitrary"),
        name="adaln_mod",
    )(cvecs, w_mod, b_mod)


_Q0, _K0, _V0, _ZA0, _QG0, _KG0, _VG0, _ZG0, _W1_COLS = 0, 512, 640, 768, 1280, 1536, 1792, 2304, 2816
_Q_PRESCALE = (HEAD_DIM ** -0.5) * LOG2E


def _rope(x, cos, sin_signed):
    n = x.shape[1] // LANES
    lane = lax.broadcasted_iota(jnp.int32, (x.shape[0], LANES), 1)
    first = (lane % 32) < 16
    outs = []
    for c in range(n):
        xc = x[:, c * LANES:(c + 1) * LANES]
        partner = jnp.where(first, pltpu.roll(xc, LANES - 16, 1), pltpu.roll(xc, 16, 1))
        outs.append(xc * cos + partner * sin_signed)
    return outs[0] if n == 1 else jnp.concatenate(outs, axis=-1)


def _kv_placements(k, v):
    lane = lax.broadcasted_iota(jnp.int32, k.shape, 1)
    lo = lane < HEAD_DIM
    kr = pltpu.roll(k, HEAD_DIM, 1)
    vr = pltpu.roll(v, HEAD_DIM, 1)
    pieces = [jnp.where(lo, k, 0.0), jnp.where(lo, 0.0, kr), jnp.where(lo, kr, 0.0), jnp.where(lo, 0.0, k),
              jnp.where(lo, v, 1.0), jnp.where(lo, 1.0, vr), jnp.where(lo, vr, 1.0), jnp.where(lo, 1.0, v)]
    return jnp.concatenate(pieces, axis=-1).astype(BF16)


def _proj_kernel(*refs, rope):
    if rope:
        (x_ref, mod_ref, ng_ref, w1_ref, wlr_ref, wgk_ref, bgk_ref, cos_ref, sin_ref,
         q_ref, kvp_ref, za_ref, qg_ref, kg_ref, vg_ref, zg_ref, gf_ref, gb_ref) = refs
    else:
        (x_ref, mod_ref, ng_ref, w1_ref, wlr_ref, wgk_ref, bgk_ref,
         q_ref, kvp_ref, za_ref, qg_ref, kg_ref, vg_ref, zg_ref, gf_ref, gb_ref, k_ref, v_ref) = refs
    x = x_ref[...]
    xn = x * lax.rsqrt(jnp.mean(x * x, axis=-1, keepdims=True) + EPS)
    mod = mod_ref[0]
    shift = mod[:, 0:D_MODEL]
    scale = mod[:, D_MODEL:2 * D_MODEL]
    h = ((xn * ng_ref[...]) * (1.0 + scale) + shift).astype(BF16)

    def piece(c0, c1):
        return _dot(h, w1_ref[:, c0:c1])

    q = piece(_Q0, _K0)
    kv = piece(_K0, _ZA0)
    k = kv[:, 0:KV_WIDTH]
    v = kv[:, KV_WIDTH:2 * KV_WIDTH]
    if rope:
        cos = cos_ref[...]
        sin = sin_ref[...]
        q = _rope(q, cos, sin)
        k = _rope(k, cos, sin)
    else:
        k_ref[...] = k
        v_ref[...] = v
    q_ref[...] = (q * _Q_PRESCALE).astype(BF16)
    kvp_ref[...] = _kv_placements(k, v)
    za_ref[...] = piece(_ZA0, _QG0).astype(BF16)
    qg_ref[...] = piece(_QG0, _KG0)
    kg_ref[...] = piece(_KG0, _VG0)
    vg_ref[...] = piece(_VG0, _ZG0).astype(BF16)
    zg_ref[...] = piece(_ZG0, _W1_COLS).astype(BF16)
    lr = _dot(h, wlr_ref[...]).astype(BF16)
    g = _log_sigmoid(_dot(lr, wgk_ref[...]) + bgk_ref[...]) * (1.0 / GATE_NORMALIZER)
    gf_ref[...] = g[:, 0:GLA_K_WIDTH]
    gb_ref[...] = g[:, GLA_K_WIDTH:2 * GLA_K_WIDTH]


def _project(x2d, mod3, mod_row_of_tile, norm_g, w1, wlr, wgk, bgk, rope_tabs, tm):
    n_tok = x2d.shape[0]
    rope = rope_tabs is not None
    row = lambda i: (i, 0)
    const = lambda i: (0, 0)
    in_specs = [
        pl.BlockSpec((tm, D_MODEL), row),
        pl.BlockSpec((1, 1, 3 * D_MODEL), lambda i: (mod_row_of_tile(i), 0, 0)),
        pl.BlockSpec((1, D_MODEL), const),
        pl.BlockSpec((D_MODEL, _W1_COLS), const),
        pl.BlockSpec((D_MODEL, LANES), const),
        pl.BlockSpec((LANES, 2 * GLA_K_WIDTH), const),
        pl.BlockSpec((1, 2 * GLA_K_WIDTH), const),
    ]
    args = [x2d, mod3, norm_g, w1, wlr, wgk, bgk]
    if rope:
        tiles_per_seq = rope_tabs[0].shape[0] // tm
        pos = lambda i: (i % tiles_per_seq, 0)
        in_specs += [pl.BlockSpec((tm, LANES), pos), pl.BlockSpec((tm, LANES), pos)]
        args += list(rope_tabs)
    widths = [(ATTN_WIDTH, BF16), (KVP_WIDTH, BF16), (ATTN_WIDTH, BF16),
              (GLA_K_WIDTH, F32), (GLA_K_WIDTH, F32), (GLA_V_WIDTH, BF16), (GLA_V_WIDTH, BF16),
              (GLA_K_WIDTH, F32), (GLA_K_WIDTH, F32)]
    if not rope:
        widths += [(KV_WIDTH, F32), (KV_WIDTH, F32)]
    return pl.pallas_call(
        functools.partial(_proj_kernel, rope=rope),
        grid=(n_tok // tm,),
        in_specs=in_specs,
        out_specs=[pl.BlockSpec((tm, w), row) for w, _ in widths],
        out_shape=[jax.ShapeDtypeStruct((n_tok, w), dt) for w, dt in widths],
        compiler_params=_params("arbitrary"),
        name="project_rope" if rope else "project",
    )(*args)


def _key_rows(start, size):
    return pl.ds(pl.multiple_of(start, BLOCK), size)


def _softmax_rows(s_buf, p_buf, t_buf, slot, n_rows, n_keys, masks_of, sink_of):
    for r0 in range(0, n_rows, SOFTMAX_ROWS):
        rs = slice(r0, r0 + SOFTMAX_ROWS)
        masks = masks_of(r0)
        cols = []
        for c in range(n_keys // LANES):
            sc = s_buf[slot, rs, c * LANES:(c + 1) * LANES]
            cols.append(jnp.where(masks[c], sc, NEG_INF) if c in masks else sc)
        mx = cols[0]
        for sc in cols[1:]:
            mx = jnp.maximum(mx, sc)
        sink = sink_of(r0)
        m = jnp.maximum(jnp.max(mx, axis=-1, keepdims=True), sink)
        for c, sc in enumerate(cols):
            p_buf[slot, rs, c * LANES:(c + 1) * LANES] = jnp.exp2(sc - m).astype(BF16)
        t_buf[slot, rs, :] = jnp.broadcast_to(jnp.exp2(sink - m), (SOFTMAX_ROWS, LANES))


def _normalize(acc, t_buf, slot):
    return acc / (pltpu.roll(acc, HEAD_DIM, 1) + t_buf[slot])


def _run_pipeline(n_groups, qk, softmax, pv):
    last = n_groups - 1
    n = PIPE_SLOTS

    def body(k, carry):
        for u in range(n):
            pv(jnp.clip(n * k + u - 2 * n, 0, last), u)
        for u in range(n):
            softmax(jnp.clip(n * k + u - n, 0, last), u)
        for u in range(n):
            qk(jnp.minimum(n * k + u, last), u)
        return carry

    lax.fori_loop(0, n_groups // n + 2, body, 0)


def _init_pipeline_scratch(first, s_buf, p_buf, t_buf):
    @pl.when(first)
    def _():
        s_buf[...] = jnp.zeros_like(s_buf)
        p_buf[...] = jnp.zeros_like(p_buf)
        t_buf[...] = jnp.ones_like(t_buf)


def _attn_scratch(n_groups, n_keys):
    rows = 2 * BLOCK
    return [pltpu.VMEM((n_groups // 2, rows, LANES), BF16),
            pltpu.VMEM((PIPE_SLOTS, rows, n_keys), F32), pltpu.VMEM((PIPE_SLOTS, rows, n_keys), BF16),
            pltpu.VMEM((PIPE_SLOTS, rows, LANES), F32), pltpu.VMEM((n_groups, rows, LANES), F32)]


def _stack_queries(q_ref, q2s, n_blocks):
    for blk in range(n_blocks):
        rows = slice(blk * BLOCK, (blk + 1) * BLOCK)
        for j in range(N_KV_HEADS):
            idx = blk * N_KV_HEADS + j
            q2s[idx, 0:BLOCK, :] = q_ref[rows, (2 * j) * LANES:(2 * j + 1) * LANES]
            q2s[idx, BLOCK:2 * BLOCK, :] = q_ref[rows, (2 * j + 1) * LANES:(2 * j + 2) * LANES]


def _write_heads(o_buf, o_ref, n_blocks):
    lo = lax.broadcasted_iota(jnp.int32, (2 * BLOCK, LANES), 1) < HEAD_DIM
    for blk in range(n_blocks):
        rows = slice(blk * BLOCK, (blk + 1) * BLOCK)
        for j in range(N_KV_HEADS):
            g0 = blk * N_PLACEMENTS + 2 * j
            o = jnp.where(lo, o_buf[g0], o_buf[g0 + 1]).astype(o_ref.dtype)
            o_ref[rows, (2 * j) * LANES:(2 * j + 1) * LANES] = o[0:BLOCK]
            o_ref[rows, (2 * j + 1) * LANES:(2 * j + 2) * LANES] = o[BLOCK:2 * BLOCK]


def _group_sinks(sink_ref, g):
    j, e = (g // 2) % N_KV_HEADS, g % 2
    sink_a = sink_ref[4 * j + e] * LOG2E
    sink_b = sink_ref[4 * j + 2 + e] * LOG2E
    return lambda r0: sink_a if r0 < BLOCK else sink_b


def _ctx_attn_kernel(sink_ref, q_ref, kvp_ref, o_ref, kown, vown, q2s, s_buf, p_buf, t_buf, o_buf, *, seq):
    n_blocks = q_ref.shape[0] // BLOCK
    _init_pipeline_scratch(pl.program_id(0) == 0, s_buf, p_buf, t_buf)
    for idx in range(N_PLACEMENTS):
        kown[idx] = kvp_ref[:, idx * LANES:(idx + 1) * LANES]
        vown[idx] = kvp_ref[:, (N_PLACEMENTS + idx) * LANES:(N_PLACEMENTS + idx + 1) * LANES]
    _stack_queries(q_ref, q2s, n_blocks)
    blocks_per_seq = seq // BLOCK

    def keys_of(g):
        return _key_rows(((g // N_PLACEMENTS) // blocks_per_seq) * seq, seq)

    def qk(g, u):
        s_buf[u] = _dot_nt(q2s[g // 2], kown[g % N_PLACEMENTS, keys_of(g), :])

    def softmax(g, u):
        _softmax_rows(s_buf, p_buf, t_buf, u, 2 * BLOCK, seq, lambda r0: {}, _group_sinks(sink_ref, g))

    def pv(g, u):
        o_buf[g] = _normalize(_dot(p_buf[u], vown[g % N_PLACEMENTS, keys_of(g), :]), t_buf, u)

    _run_pipeline(n_blocks * N_PLACEMENTS, qk, softmax, pv)
    _write_heads(o_buf, o_ref, n_blocks)


def _context_attention(q, kvp, sink, seq):
    n_tok = q.shape[0]
    rows = CTX_ATTN_BATCH * seq
    row = lambda b: (b, 0)
    n_groups = (rows // BLOCK) * N_PLACEMENTS
    return pl.pallas_call(
        functools.partial(_ctx_attn_kernel, seq=seq),
        grid=(n_tok // rows,),
        in_specs=[
            pl.BlockSpec(memory_space=pltpu.SMEM),
            pl.BlockSpec((rows, ATTN_WIDTH), row),
            pl.BlockSpec((rows, KVP_WIDTH), row),
        ],
        out_specs=pl.BlockSpec((rows, ATTN_WIDTH), row),
        out_shape=jax.ShapeDtypeStruct((n_tok, ATTN_WIDTH), BF16),
        scratch_shapes=[pltpu.VMEM((N_PLACEMENTS, rows, LANES), BF16)] * 2 + _attn_scratch(n_groups, seq),
        compiler_params=_params("arbitrary"),
        name="context_attention",
    )(sink, q, kvp)


def _lat_attn_kernel(sink_ref, q_ref, kvp_prev, kvp_own, kvp_next, ck_ref, cv_ref, o_ref,
                     kwin, vwin, kctx, vctx, q2s, s_buf, p_buf, t_buf, o_buf, *, n_steps, past):
    b, i = pl.program_id(0), pl.program_id(1)
    n_blocks = q_ref.shape[0] // BLOCK
    own_end = BLOCK + q_ref.shape[0]
    win = 3 * BLOCK
    _init_pipeline_scratch(jnp.logical_and(b == 0, i == 0), s_buf, p_buf, t_buf)

    @pl.when(i == 0)
    def _():
        cpl = _kv_placements(ck_ref[0], cv_ref[0])
        for idx in range(N_PLACEMENTS):
            kctx[idx] = cpl[:, idx * LANES:(idx + 1) * LANES]
            vctx[idx] = cpl[:, (N_PLACEMENTS + idx) * LANES:(N_PLACEMENTS + idx + 1) * LANES]

    for src, r0, r1 in ((kvp_prev, 0, BLOCK), (kvp_own, BLOCK, own_end), (kvp_next, own_end, own_end + BLOCK)):
        for idx in range(N_PLACEMENTS):
            kwin[idx, r0:r1, :] = src[:, idx * LANES:(idx + 1) * LANES]
            vwin[idx, r0:r1, :] = src[:, (N_PLACEMENTS + idx) * LANES:(N_PLACEMENTS + idx + 1) * LANES]
    _stack_queries(q_ref, q2s, n_blocks)

    row = lax.broadcasted_iota(jnp.int32, (SOFTMAX_ROWS, LANES), 0)
    c = lax.broadcasted_iota(jnp.int32, (SOFTMAX_ROWS, LANES), 1)
    off_prev = jnp.where(i > 0, 0, BLOCK)
    off_next = jnp.where(i < n_steps - 1, 0, BLOCK)

    def window_of(g):
        return _key_rows((g // N_PLACEMENTS) * BLOCK, win)

    def qk(g, u):
        q2 = q2s[g // 2]
        s_buf[u, :, 0:win] = _dot_nt(q2, kwin[g % N_PLACEMENTS, window_of(g), :])
        s_buf[u, :, win:win + past] = _dot_nt(q2, kctx[g % N_PLACEMENTS])

    def softmax(g, u):
        blk = g // N_PLACEMENTS
        off0 = jnp.where(blk == 0, off_prev, 0)
        off2 = jnp.where(blk == n_blocks - 1, off_next, 0)

        def masks_of(r0):
            r = row + r0 % BLOCK
            return {0: c >= r + off0, 2: c <= r - off2}

        _softmax_rows(s_buf, p_buf, t_buf, u, 2 * BLOCK, win + past, masks_of, _group_sinks(sink_ref, g))

    def pv(g, u):
        acc = _dot(p_buf[u, :, 0:win], vwin[g % N_PLACEMENTS, window_of(g), :])
        acc = acc + _dot(p_buf[u, :, win:win + past], vctx[g % N_PLACEMENTS])
        o_buf[g] = _normalize(acc, t_buf, u)

    _run_pipeline(n_blocks * N_PLACEMENTS, qk, softmax, pv)
    _write_heads(o_buf, o_ref, n_blocks)


def _latent_attention(q, kvp, ck, cv, sink, n_batch, seq):
    ns = seq // ATTN_TILE
    nb = seq // BLOCK
    per = ATTN_TILE // BLOCK
    own = lambda b, i: (b * ns + i, 0)
    prev = lambda b, i: (b * nb + jnp.maximum(per * i - 1, 0), 0)
    nxt = lambda b, i: (b * nb + jnp.minimum(per * i + per, nb - 1), 0)
    ctx = lambda b, i: (b, 0, 0)
    past = ck.shape[1]
    win_rows = ATTN_TILE + 2 * BLOCK
    bf = lambda *shape: pltpu.VMEM(shape, BF16)
    return pl.pallas_call(
        functools.partial(_lat_attn_kernel, n_steps=ns, past=past),
        grid=(n_batch, ns),
        in_specs=[
            pl.BlockSpec(memory_space=pltpu.SMEM),
            pl.BlockSpec((ATTN_TILE, ATTN_WIDTH), own),
            pl.BlockSpec((BLOCK, KVP_WIDTH), prev),
            pl.BlockSpec((ATTN_TILE, KVP_WIDTH), own),
            pl.BlockSpec((BLOCK, KVP_WIDTH), nxt),
            pl.BlockSpec((1, past, KV_WIDTH), ctx),
            pl.BlockSpec((1, past, KV_WIDTH), ctx),
        ],
        out_specs=pl.BlockSpec((ATTN_TILE, ATTN_WIDTH), own),
        out_shape=jax.ShapeDtypeStruct((n_batch * seq, ATTN_WIDTH), BF16),
        scratch_shapes=[bf(N_PLACEMENTS, win_rows, LANES), bf(N_PLACEMENTS, win_rows, LANES),
                        bf(N_PLACEMENTS, past, LANES), bf(N_PLACEMENTS, past, LANES)]
        + _attn_scratch(per * N_PLACEMENTS, 3 * BLOCK + past),
        compiler_params=_params("arbitrary", "arbitrary"),
        name="latent_attention",
    )(sink, q, kvp, kvp, kvp, ck, cv)


def _cumsum_rows(x, reverse):
    n = x.shape[0]
    row = lax.broadcasted_iota(jnp.int32, x.shape, 0)
    s = 1
    while s < n:
        if reverse:
            x = x + jnp.where(row < n - s, pltpu.roll(x, n - s, 0), 0.0)
        else:
            x = x + jnp.where(row >= s, pltpu.roll(x, s, 0), 0.0)
        s *= 2
    return x


def _gla_chunk(q, k, v, g, st_ref, d, o_ref, rows, reverse):
    gc = _cumsum_rows(g, reverse)
    gtot = gc[0:1, :] if reverse else gc[CHUNK - 1:CHUNK, :]
    q_dec = q * jnp.exp(gc) * (GLA_DK ** -0.5)
    k_inv = k * jnp.exp(-gc)
    k_end = k * jnp.exp(gtot - gc)
    decay = jnp.exp(gtot)
    pk = 2 * GLA_DK
    pv = 2 * GLA_DV
    lane_k = lax.broadcasted_iota(jnp.int32, (CHUNK, pk), 1)
    lo_k = lane_k < GLA_DK
    row_k = lax.broadcasted_iota(jnp.int32, (CHUNK, pk), 0)
    key = lane_k % CHUNK
    tri = (key >= row_k) if reverse else (key <= row_k)
    lane_v = lax.broadcasted_iota(jnp.int32, (CHUNK, pv), 1)
    lo_v = lane_v < GLA_DV
    srow = lax.broadcasted_iota(jnp.int32, (pv, pk), 0)
    slane = lax.broadcasted_iota(jnp.int32, (pv, pk), 1)
    diag = (srow < GLA_DV) == (slane < GLA_DK)
    for p in range(N_GLA_HEADS // 2):
        qd = q_dec[:, p * pk:(p + 1) * pk].astype(BF16)
        ki = k_inv[:, p * pk:(p + 1) * pk]
        ke = k_end[:, p * pk:(p + 1) * pk].astype(BF16)
        vp = v[:, p * pv:(p + 1) * pv]
        vpf = vp.astype(F32)
        kst = jnp.concatenate([jnp.where(lo_k, ki, 0.0), jnp.where(lo_k, 0.0, ki)], axis=0).astype(BF16)
        a = jnp.where(tri, _dot_nt(qd, kst), 0.0).astype(BF16)
        vbd = jnp.concatenate([jnp.where(lo_v, vpf, 0.0), jnp.where(lo_v, 0.0, vpf)], axis=0).astype(BF16)
        st = st_ref[d, p]
        o = _dot(a, vbd) + _dot_nt(qd, st.astype(BF16))
        o_ref[rows, p * pv:(p + 1) * pv] = o.astype(o_ref.dtype)
        ds = _dot_tn(vp, ke)
        st_ref[d, p] = st * decay[:, p * pk:(p + 1) * pk] + jnp.where(diag, ds, 0.0)


def _gla_kernel(*refs, n_steps, has_init):
    if has_init:
        (qf, kf, vf, gf, qb, kb, vb, gb, s0_ref, of_ref, ob_ref, sout_ref, st_ref) = refs
    else:
        (qf, kf, vf, gf, qb, kb, vb, gb, of_ref, ob_ref, sout_ref, st_ref) = refs
        s0_ref = None
    i = pl.program_id(1)

    def blocks():
        for d in range(2):
            for h in range(N_GLA_HEADS):
                p, e = divmod(h, 2)
                yield d, h, (d, p, slice(e * GLA_DV, (e + 1) * GLA_DV), slice(e * GLA_DK, (e + 1) * GLA_DK))

    @pl.when(i == 0)
    def _():
        st_ref[...] = jnp.zeros_like(st_ref)
        if has_init:
            for d, h, idx in blocks():
                st_ref[idx] = s0_ref[0, d, h].T

    n_chunks = qf.shape[0] // CHUNK
    for c in range(n_chunks):
        rows = slice(c * CHUNK, (c + 1) * CHUNK)
        _gla_chunk(qf[rows, :], kf[rows, :], vf[rows, :], gf[rows, :], st_ref, 0, of_ref, rows, False)
    for c in reversed(range(n_chunks)):
        rows = slice(c * CHUNK, (c + 1) * CHUNK)
        _gla_chunk(qb[rows, :], kb[rows, :], vb[rows, :], gb[rows, :], st_ref, 1, ob_ref, rows, True)

    @pl.when(i == n_steps - 1)
    def _():
        for d, h, idx in blocks():
            sout_ref[0, d, h] = st_ref[idx].T


def _gla(qg, kg, vg, gf, gb, s0, n_batch, seq):
    tm = GLA_TILE
    ns = seq // tm
    fwd = lambda b, i: (b * ns + i, 0)
    bwd = lambda b, i: (b * ns + ns - 1 - i, 0)
    st_map = lambda b, i: (b, 0, 0, 0, 0)
    st_block = (1, 2, N_GLA_HEADS, GLA_DK, GLA_DV)
    in_specs = []
    for m in (fwd, bwd):
        in_specs += [pl.BlockSpec((tm, GLA_K_WIDTH), m), pl.BlockSpec((tm, GLA_K_WIDTH), m),
                     pl.BlockSpec((tm, GLA_V_WIDTH), m), pl.BlockSpec((tm, GLA_K_WIDTH), m)]
    args = [qg, kg, vg, gf, qg, kg, vg, gb]
    if s0 is not None:
        in_specs.append(pl.BlockSpec(st_block, st_map))
        args.append(s0)
    n_tok = n_batch * seq
    return pl.pallas_call(
        functools.partial(_gla_kernel, n_steps=ns, has_init=s0 is not None),
        grid=(n_batch, ns),
        in_specs=in_specs,
        out_specs=[pl.BlockSpec((tm, GLA_V_WIDTH), fwd), pl.BlockSpec((tm, GLA_V_WIDTH), bwd),
                   pl.BlockSpec(st_block, st_map)],
        out_shape=[jax.ShapeDtypeStruct((n_tok, GLA_V_WIDTH), BF16),
                   jax.ShapeDtypeStruct((n_tok, GLA_V_WIDTH), BF16),
                   jax.ShapeDtypeStruct((n_batch,) + st_block[1:], F32)],
        scratch_shapes=[pltpu.VMEM((2, N_GLA_HEADS // 2, 2 * GLA_DV, 2 * GLA_DK), F32)],
        compiler_params=_params("arbitrary", "arbitrary"),
        name="gla_init" if s0 is not None else "gla_zero",
    )(*args)


def _merge_kernel(x_ref, mod_ref, oa_ref, za_ref, of_ref, ob_ref, zg_ref, gng_ref, wo_ref, fng_ref, y_ref):
    ya = oa_ref[...].astype(F32) * _silu(za_ref[...].astype(F32))
    og = of_ref[...].astype(F32) + ob_ref[...].astype(F32)
    zg = zg_ref[...].astype(F32)
    gng = gng_ref[...]
    parts = [ya]
    for h in range(N_GLA_HEADS):
        oh = og[:, h * GLA_DV:(h + 1) * GLA_DV]
        nh = oh * lax.rsqrt(jnp.mean(oh * oh, axis=-1, keepdims=True) + EPS)
        parts.append((nh * gng) * _silu(zg[:, h * GLA_DV:(h + 1) * GLA_DV]))
    yin = jnp.concatenate(parts, axis=-1).astype(BF16)
    gate = mod_ref[0][:, 2 * D_MODEL:3 * D_MODEL]
    y = x_ref[...] + gate * _dot(yin, wo_ref[...])
    y_ref[...] = (y * lax.rsqrt(jnp.mean(y * y, axis=-1, keepdims=True) + EPS)) * fng_ref[...]


def _merge(x2d, mod3, mod_row_of_tile, oa, za, of, ob, zg, gla_norm_g, w_out, final_norm_g, tm):
    n_tok = x2d.shape[0]
    row = lambda i: (i, 0)
    const = lambda i: (0, 0)
    wide = pl.BlockSpec((tm, GLA_V_WIDTH), row)
    return pl.pallas_call(
        _merge_kernel,
        grid=(n_tok // tm,),
        in_specs=[
            pl.BlockSpec((tm, D_MODEL), row),
            pl.BlockSpec((1, 1, 3 * D_MODEL), lambda i: (mod_row_of_tile(i), 0, 0)),
            wide, wide, wide, wide, wide,
            pl.BlockSpec((1, GLA_DV), const),
            pl.BlockSpec((D_MODEL, D_MODEL), const),
            pl.BlockSpec((1, D_MODEL), const),
        ],
        out_specs=pl.BlockSpec((tm, D_MODEL), row),
        out_shape=jax.ShapeDtypeStruct((n_tok, D_MODEL), F32),
        compiler_params=_params("arbitrary"),
        name="merge",
    )(x2d, mod3, oa, za, of, ob, zg, gla_norm_g, w_out, final_norm_g)


def _rope_tables(n_tok):
    t = np.arange(n_tok)
    rowp = (t // GRID_W).astype(np.float64)
    colp = (t % GRID_W).astype(np.float64)
    nf = HEAD_DIM // 4
    freqs = ROPE_BASE ** (-np.arange(nf, dtype=np.float64) / nf)
    lane = np.arange(LANES)
    pos = np.where(((lane % HEAD_DIM) < HEAD_DIM // 2)[None, :], rowp[:, None], colp[:, None])
    ang = pos * freqs[lane % nf][None, :]
    sign = np.where((lane % (2 * nf)) < nf, -1.0, 1.0)
    return jnp.asarray(np.cos(ang), F32), jnp.asarray(np.sin(ang) * sign[None, :], F32)


def kernel(x_prompt, x_sample, c, cache_k, cache_v, state_gla, c_ctx, w_mod, b_mod, norm_g, w_in,
           w_gk_f, b_gk_f, w_gk_b, b_gk_b, sink, gla_norm_g, w_out, final_norm_g):
    depth = w_in.shape[0]
    assert depth == 1, "single-layer step"
    l = 0
    bp, sp, _ = x_prompt.shape
    bs, ss, _ = x_sample.shape

    wl = w_in[l]
    lr0 = _VG0 + GLA_V_WIDTH
    w1 = jnp.concatenate([wl[:, :lr0], wl[:, lr0 + 2 * GATE_RANK:]], axis=1).astype(BF16)
    wlr = jnp.pad(wl[:, lr0:lr0 + 2 * GATE_RANK], ((0, 0), (0, LANES - 2 * GATE_RANK))).astype(BF16)
    wgk = jnp.zeros((LANES, 2 * GLA_K_WIDTH), F32)
    wgk = wgk.at[0:GATE_RANK, 0:GLA_K_WIDTH].set(w_gk_f[l])
    wgk = wgk.at[GATE_RANK:2 * GATE_RANK, GLA_K_WIDTH:].set(w_gk_b[l]).astype(BF16)
    bgk = jnp.concatenate([b_gk_f[l], b_gk_b[l]])[None, :]
    wo = w_out[l].astype(BF16)
    ng = norm_g[l][None, :]
    gng = gla_norm_g[l][None, :]
    fng = final_norm_g[None, :]
    sk = sink[l]

    n_rows = 8
    cvecs = jnp.concatenate([c_ctx[None, :], c, jnp.zeros((n_rows - 1 - bs, D_MODEL), F32)], axis=0)
    mod3 = _modulation(cvecs, w_mod[l], b_mod[l][None, :]).reshape(n_rows, 1, 3 * D_MODEL)
    tm = TOKEN_TILE

    xp2 = x_prompt.reshape(bp * sp, D_MODEL)
    ctx_row = lambda i: 0
    q, kvp, za, qg, kg, vg, zg, gf, gb, k, v = _project(xp2, mod3, ctx_row, ng, w1, wlr, wgk, bgk, None, tm)
    oa = _context_attention(q, kvp, sk, sp)
    of, ob, st = _gla(qg, kg, vg, gf, gb, None, bp, sp)
    y_prompt = _merge(xp2, mod3, ctx_row, oa, za, of, ob, zg, gng, wo, fng, tm).reshape(bp, sp, D_MODEL)
    new_k = k.reshape(bp, 1, sp, N_KV_HEADS, HEAD_DIM)
    new_v = v.reshape(bp, 1, sp, N_KV_HEADS, HEAD_DIM)
    new_state = st[:, None]

    xs2 = x_sample.reshape(bs * ss, D_MODEL)
    tiles_per_seq = ss // tm
    lat_row = lambda i: 1 + i // tiles_per_seq
    q, kvp, za, qg, kg, vg, zg, gf, gb = _project(xs2, mod3, lat_row, ng, w1, wlr, wgk, bgk, _rope_tables(ss), tm)
    past = cache_k.shape[2]
    ck = cache_k[:, l].reshape(bs, past, KV_WIDTH)
    cv = cache_v[:, l].reshape(bs, past, KV_WIDTH)
    oa = _latent_attention(q, kvp, ck, cv, sk, bs, ss)
    of, ob, _ = _gla(qg, kg, vg, gf, gb, state_gla[:, l], bs, ss)
    y_sample = _merge(xs2, mod3, lat_row, oa, za, of, ob, zg, gng, wo, fng, tm).reshape(bs, ss, D_MODEL)

    return (y_prompt, y_sample, new_k, new_v, new_state)
```

```python
import functools
import math

import numpy as np
import jax
import jax.numpy as jnp
from jax import lax
from jax.experimental import pallas as pl
from jax.experimental.pallas import tpu as pltpu

D_MODEL = 1024
GRID_W = 64
HEAD_DIM = 64
N_ATTN_HEADS = 8
N_KV_HEADS = 2
ATTN_WIDTH = 512
KV_WIDTH = 128
BLOCK = 128
N_GLA_HEADS = 4
GLA_DK = 64
GLA_DV = 128
GLA_K_WIDTH = 256
GLA_V_WIDTH = 512
GATE_RANK = 16
GATE_NORMALIZER = 16.0
CHUNK = 64
ROPE_BASE = 10000.0
EPS = 1e-6
NEG_INF = -1e30

LANES = 128
TOKEN_TILE = 512
GLA_TILE = 256
ATTN_TILE = 2048
CTX_ATTN_BATCH = 8
PIPE_SLOTS = 4
KVP_WIDTH = 8 * LANES
N_PLACEMENTS = 2 * N_KV_HEADS
SOFTMAX_ROWS = 32
MOD_TILE = 512
VMEM_LIMIT = 56 * 1024 * 1024
LOG2E = math.log2(math.e)

F32 = jnp.float32
BF16 = jnp.bfloat16


def _dot(a, b):
    return jnp.dot(a, b, preferred_element_type=F32)


def _dot_nt(a, b):
    return lax.dot_general(a, b, (((1,), (1,)), ((), ())), preferred_element_type=F32)


def _dot_tn(a, b):
    return lax.dot_general(a, b, (((0,), (0,)), ((), ())), preferred_element_type=F32)


def _silu(x):
    return x / (1.0 + jnp.exp(-x))


def _log_sigmoid(x):
    return jnp.minimum(x, 0.0) - jnp.log1p(jnp.exp(-jnp.abs(x)))


def _params(*sem):
    return pltpu.CompilerParams(dimension_semantics=sem, vmem_limit_bytes=VMEM_LIMIT)


def _mod_kernel(c_ref, w_ref, b_ref, o_ref):
    c = c_ref[...]
    o_ref[...] = jnp.dot(_silu(c), w_ref[...], preferred_element_type=F32,
                         precision=lax.Precision.HIGHEST) + b_ref[...]


def _modulation(cvecs, w_mod, b_mod):
    rows = cvecs.shape[0]
    n_out = w_mod.shape[1]
    return pl.pallas_call(
        _mod_kernel,
        grid=(n_out // MOD_TILE,),
        in_specs=[
            pl.BlockSpec((rows, D_MODEL), lambda j: (0, 0)),
            pl.BlockSpec((D_MODEL, MOD_TILE), lambda j: (0, j)),
            pl.BlockSpec((1, MOD_TILE), lambda j: (0, j)),
        ],
        out_specs=pl.BlockSpec((rows, MOD_TILE), lambda j: (0, j)),
        out_shape=jax.ShapeDtypeStruct((rows, n_out), F32),
        compiler_params=_params("arbitrary"),
        name="adaln_mod",
    )(cvecs, w_mod, b_mod)


_Q0, _K0, _V0, _ZA0, _QG0, _KG0, _VG0, _ZG0, _W1_COLS = 0, 512, 640, 768, 1280, 1536, 1792, 2304, 2816
_Q_PRESCALE = (HEAD_DIM ** -0.5) * LOG2E


def _rope(x, cos, sin_signed):
    n = x.shape[1] // LANES
    lane = lax.broadcasted_iota(jnp.int32, (x.shape[0], LANES), 1)
    first = (lane % 32) < 16
    outs = []
    for c in range(n):
        xc = x[:, c * LANES:(c + 1) * LANES]
        partner = jnp.where(first, pltpu.roll(xc, LANES - 16, 1), pltpu.roll(xc, 16, 1))
        outs.append(xc * cos + partner * sin_signed)
    return outs[0] if n == 1 else jnp.concatenate(outs, axis=-1)


def _kv_placements(k, v):
    lane = lax.broadcasted_iota(jnp.int32, k.shape, 1)
    lo = lane < HEAD_DIM
    kr = pltpu.roll(k, HEAD_DIM, 1)
    vr = pltpu.roll(v, HEAD_DIM, 1)
    pieces = [jnp.where(lo, k, 0.0), jnp.where(lo, 0.0, kr), jnp.where(lo, kr, 0.0), jnp.where(lo, 0.0, k),
              jnp.where(lo, v, 1.0), jnp.where(lo, 1.0, vr), jnp.where(lo, vr, 1.0), jnp.where(lo, 1.0, v)]
    return jnp.concatenate(pieces, axis=-1).astype(BF16)


def _proj_kernel(*refs, rope):
    if rope:
        (x_ref, mod_ref, ng_ref, w1_ref, wlr_ref, wgk_ref, bgk_ref, cos_ref, sin_ref,
         q_ref, kvp_ref, za_ref, qg_ref, kg_ref, vg_ref, zg_ref, gf_ref, gb_ref) = refs
    else:
        (x_ref, mod_ref, ng_ref, w1_ref, wlr_ref, wgk_ref, bgk_ref,
         q_ref, kvp_ref, za_ref, qg_ref, kg_ref, vg_ref, zg_ref, gf_ref, gb_ref, k_ref, v_ref) = refs
    x = x_ref[...]
    xn = x * lax.rsqrt(jnp.mean(x * x, axis=-1, keepdims=True) + EPS)
    mod = mod_ref[0]
    shift = mod[:, 0:D_MODEL]
    scale = mod[:, D_MODEL:2 * D_MODEL]
    h = ((xn * ng_ref[...]) * (1.0 + scale) + shift).astype(BF16)

    def piece(c0, c1):
        return _dot(h, w1_ref[:, c0:c1])

    q = piece(_Q0, _K0)
    kv = piece(_K0, _ZA0)
    k = kv[:, 0:KV_WIDTH]
    v = kv[:, KV_WIDTH:2 * KV_WIDTH]
    if rope:
        cos = cos_ref[...]
        sin = sin_ref[...]
        q = _rope(q, cos, sin)
        k = _rope(k, cos, sin)
    else:
        seq = k_ref.shape[2]
        for b in range(k_ref.shape[0]):
            k_ref[b] = k[b * seq:(b + 1) * seq, :].T
            v_ref[b] = v[b * seq:(b + 1) * seq, :].T
    q_ref[...] = (q * _Q_PRESCALE).astype(BF16)
    kvp_ref[...] = _kv_placements(k, v)
    za_ref[...] = piece(_ZA0, _QG0).astype(BF16)
    qg_ref[...] = piece(_QG0, _KG0)
    kg_ref[...] = piece(_KG0, _VG0)
    vg_ref[...] = piece(_VG0, _ZG0).astype(BF16)
    zg_ref[...] = piece(_ZG0, _W1_COLS).astype(BF16)
    lr = _dot(h, wlr_ref[...]).astype(BF16)
    g = _log_sigmoid(_dot(lr, wgk_ref[...]) + bgk_ref[...]) * (1.0 / GATE_NORMALIZER)
    gf_ref[...] = g[:, 0:GLA_K_WIDTH]
    gb_ref[...] = g[:, GLA_K_WIDTH:2 * GLA_K_WIDTH]


def _project(x2d, mod3, mod_row_of_tile, norm_g, w1, wlr, wgk, bgk, rope_tabs, tm, seq):
    n_tok = x2d.shape[0]
    rope = rope_tabs is not None
    row = lambda i: (i, 0)
    const = lambda i: (0, 0)
    in_specs = [
        pl.BlockSpec((tm, D_MODEL), row),
        pl.BlockSpec((1, 1, 3 * D_MODEL), lambda i: (mod_row_of_tile(i), 0, 0)),
        pl.BlockSpec((1, D_MODEL), const),
        pl.BlockSpec((D_MODEL, _W1_COLS), const),
        pl.BlockSpec((D_MODEL, LANES), const),
        pl.BlockSpec((LANES, 2 * GLA_K_WIDTH), const),
        pl.BlockSpec((1, 2 * GLA_K_WIDTH), const),
    ]
    args = [x2d, mod3, norm_g, w1, wlr, wgk, bgk]
    if rope:
        tiles_per_seq = rope_tabs[0].shape[0] // tm
        pos = lambda i: (i % tiles_per_seq, 0)
        in_specs += [pl.BlockSpec((tm, LANES), pos), pl.BlockSpec((tm, LANES), pos)]
        args += list(rope_tabs)
    widths = [(ATTN_WIDTH, BF16), (KVP_WIDTH, BF16), (ATTN_WIDTH, BF16),
              (GLA_K_WIDTH, F32), (GLA_K_WIDTH, F32), (GLA_V_WIDTH, BF16), (GLA_V_WIDTH, BF16),
              (GLA_K_WIDTH, F32), (GLA_K_WIDTH, F32)]
    out_specs = [pl.BlockSpec((tm, w), row) for w, _ in widths]
    out_shape = [jax.ShapeDtypeStruct((n_tok, w), dt) for w, dt in widths]
    if not rope:
        for _ in range(2):
            out_specs.append(pl.BlockSpec((tm // seq, KV_WIDTH, seq), lambda i: (i, 0, 0)))
            out_shape.append(jax.ShapeDtypeStruct((n_tok // seq, KV_WIDTH, seq), F32))
    return pl.pallas_call(
        functools.partial(_proj_kernel, rope=rope),
        grid=(n_tok // tm,),
        in_specs=in_specs,
        out_specs=out_specs,
        out_shape=out_shape,
        compiler_params=_params("arbitrary"),
        name="project_rope" if rope else "project",
    )(*args)


def _key_rows(start, size):
    return pl.ds(pl.multiple_of(start, BLOCK), size)


def _softmax_rows(s_buf, p_buf, t_buf, slot, n_rows, n_keys, masks_of, sink_of):
    for r0 in range(0, n_rows, SOFTMAX_ROWS):
        rs = slice(r0, r0 + SOFTMAX_ROWS)
        masks = masks_of(r0)
        cols = []
        for c in range(n_keys // LANES):
            sc = s_buf[slot, rs, c * LANES:(c + 1) * LANES]
            cols.append(jnp.where(masks[c], sc, NEG_INF) if c in masks else sc)
        mx = cols[0]
        for sc in cols[1:]:
            mx = jnp.maximum(mx, sc)
        sink = sink_of(r0)
        m = jnp.maximum(jnp.max(mx, axis=-1, keepdims=True), sink)
        for c, sc in enumerate(cols):
            p_buf[slot, rs, c * LANES:(c + 1) * LANES] = jnp.exp2(sc - m).astype(BF16)
        t_buf[slot, rs, :] = jnp.broadcast_to(jnp.exp2(sink - m), (SOFTMAX_ROWS, LANES))


def _normalize(acc, t_buf, slot):
    return acc / (pltpu.roll(acc, HEAD_DIM, 1) + t_buf[slot])


def _run_pipeline(n_groups, qk, softmax, pv):
    last = n_groups - 1
    n = PIPE_SLOTS

    def body(k, carry):
        for u in range(n):
            pv(jnp.clip(n * k + u - 2 * n, 0, last), u)
        for u in range(n):
            softmax(jnp.clip(n * k + u - n, 0, last), u)
        for u in range(n):
            qk(jnp.minimum(n * k + u, last), u)
        return carry

    lax.fori_loop(0, n_groups // n + 2, body, 0)


def _init_pipeline_scratch(first, s_buf, p_buf, t_buf):
    @pl.when(first)
    def _():
        s_buf[...] = jnp.zeros_like(s_buf)
        p_buf[...] = jnp.zeros_like(p_buf)
        t_buf[...] = jnp.ones_like(t_buf)


def _attn_scratch(n_groups, n_keys):
    rows = 2 * BLOCK
    return [pltpu.VMEM((n_groups // 2, rows, LANES), BF16),
            pltpu.VMEM((PIPE_SLOTS, rows, n_keys), F32), pltpu.VMEM((PIPE_SLOTS, rows, n_keys), BF16),
            pltpu.VMEM((PIPE_SLOTS, rows, LANES), F32), pltpu.VMEM((n_groups, rows, LANES), F32)]


def _stack_queries(q_ref, q2s, n_blocks):
    for blk in range(n_blocks):
        rows = slice(blk * BLOCK, (blk + 1) * BLOCK)
        for j in range(N_KV_HEADS):
            idx = blk * N_KV_HEADS + j
            q2s[idx, 0:BLOCK, :] = q_ref[rows, (2 * j) * LANES:(2 * j + 1) * LANES]
            q2s[idx, BLOCK:2 * BLOCK, :] = q_ref[rows, (2 * j + 1) * LANES:(2 * j + 2) * LANES]


def _write_heads(o_buf, o_ref, n_blocks):
    lo = lax.broadcasted_iota(jnp.int32, (2 * BLOCK, LANES), 1) < HEAD_DIM
    for blk in range(n_blocks):
        rows = slice(blk * BLOCK, (blk + 1) * BLOCK)
        for j in range(N_KV_HEADS):
            g0 = blk * N_PLACEMENTS + 2 * j
            o = jnp.where(lo, o_buf[g0], o_buf[g0 + 1]).astype(o_ref.dtype)
            o_ref[rows, (2 * j) * LANES:(2 * j + 1) * LANES] = o[0:BLOCK]
            o_ref[rows, (2 * j + 1) * LANES:(2 * j + 2) * LANES] = o[BLOCK:2 * BLOCK]


def _group_sinks(sink_ref, g):
    j, e = (g // 2) % N_KV_HEADS, g % 2
    sink_a = sink_ref[4 * j + e] * LOG2E
    sink_b = sink_ref[4 * j + 2 + e] * LOG2E
    return lambda r0: sink_a if r0 < BLOCK else sink_b


def _ctx_attn_kernel(sink_ref, q_ref, kvp_ref, o_ref, kown, vown, q2s, s_buf, p_buf, t_buf, o_buf, *, seq):
    n_blocks = q_ref.shape[0] // BLOCK
    _init_pipeline_scratch(pl.program_id(0) == 0, s_buf, p_buf, t_buf)
    for idx in range(N_PLACEMENTS):
        kown[idx] = kvp_ref[:, idx * LANES:(idx + 1) * LANES]
        vown[idx] = kvp_ref[:, (N_PLACEMENTS + idx) * LANES:(N_PLACEMENTS + idx + 1) * LANES]
    _stack_queries(q_ref, q2s, n_blocks)
    blocks_per_seq = seq // BLOCK

    def keys_of(g):
        return _key_rows(((g // N_PLACEMENTS) // blocks_per_seq) * seq, seq)

    def qk(g, u):
        s_buf[u] = _dot_nt(q2s[g // 2], kown[g % N_PLACEMENTS, keys_of(g), :])

    def softmax(g, u):
        _softmax_rows(s_buf, p_buf, t_buf, u, 2 * BLOCK, seq, lambda r0: {}, _group_sinks(sink_ref, g))

    def pv(g, u):
        o_buf[g] = _normalize(_dot(p_buf[u], vown[g % N_PLACEMENTS, keys_of(g), :]), t_buf, u)

    _run_pipeline(n_blocks * N_PLACEMENTS, qk, softmax, pv)
    _write_heads(o_buf, o_ref, n_blocks)


def _context_attention(q, kvp, sink, seq):
    n_tok = q.shape[0]
    rows = CTX_ATTN_BATCH * seq
    row = lambda b: (b, 0)
    n_groups = (rows // BLOCK) * N_PLACEMENTS
    return pl.pallas_call(
        functools.partial(_ctx_attn_kernel, seq=seq),
        grid=(n_tok // rows,),
        in_specs=[
            pl.BlockSpec(memory_space=pltpu.SMEM),
            pl.BlockSpec((rows, ATTN_WIDTH), row),
            pl.BlockSpec((rows, KVP_WIDTH), row),
        ],
        out_specs=pl.BlockSpec((rows, ATTN_WIDTH), row),
        out_shape=jax.ShapeDtypeStruct((n_tok, ATTN_WIDTH), BF16),
        scratch_shapes=[pltpu.VMEM((N_PLACEMENTS, rows, LANES), BF16)] * 2 + _attn_scratch(n_groups, seq),
        compiler_params=_params("arbitrary"),
        name="context_attention",
    )(sink, q, kvp)


def _lat_attn_kernel(sink_ref, q_ref, kvp_prev, kvp_own, kvp_next, ck_ref, cv_ref, o_ref,
                     kwin, vwin, kctx, vctx, q2s, s_buf, p_buf, t_buf, o_buf, *, n_steps, past):
    b, i = pl.program_id(0), pl.program_id(1)
    n_blocks = q_ref.shape[0] // BLOCK
    own_end = BLOCK + q_ref.shape[0]
    win = 3 * BLOCK
    _init_pipeline_scratch(jnp.logical_and(b == 0, i == 0), s_buf, p_buf, t_buf)

    @pl.when(i == 0)
    def _():
        cpl = _kv_placements(ck_ref[0].T, cv_ref[0].T)
        for idx in range(N_PLACEMENTS):
            kctx[idx] = cpl[:, idx * LANES:(idx + 1) * LANES]
            vctx[idx] = cpl[:, (N_PLACEMENTS + idx) * LANES:(N_PLACEMENTS + idx + 1) * LANES]

    for src, r0, r1 in ((kvp_prev, 0, BLOCK), (kvp_own, BLOCK, own_end), (kvp_next, own_end, own_end + BLOCK)):
        for idx in range(N_PLACEMENTS):
            kwin[idx, r0:r1, :] = src[:, idx * LANES:(idx + 1) * LANES]
            vwin[idx, r0:r1, :] = src[:, (N_PLACEMENTS + idx) * LANES:(N_PLACEMENTS + idx + 1) * LANES]
    _stack_queries(q_ref, q2s, n_blocks)

    row = lax.broadcasted_iota(jnp.int32, (SOFTMAX_ROWS, LANES), 0)
    c = lax.broadcasted_iota(jnp.int32, (SOFTMAX_ROWS, LANES), 1)
    off_prev = jnp.where(i > 0, 0, BLOCK)
    off_next = jnp.where(i < n_steps - 1, 0, BLOCK)

    def window_of(g):
        return _key_rows((g // N_PLACEMENTS) * BLOCK, win)

    def qk(g, u):
        q2 = q2s[g // 2]
        s_buf[u, :, 0:win] = _dot_nt(q2, kwin[g % N_PLACEMENTS, window_of(g), :])
        s_buf[u, :, win:win + past] = _dot_nt(q2, kctx[g % N_PLACEMENTS])

    def softmax(g, u):
        blk = g // N_PLACEMENTS
        off0 = jnp.where(blk == 0, off_prev, 0)
        off2 = jnp.where(blk == n_blocks - 1, off_next, 0)

        def masks_of(r0):
            r = row + r0 % BLOCK
            return {0: c >= r + off0, 2: c <= r - off2}

        _softmax_rows(s_buf, p_buf, t_buf, u, 2 * BLOCK, win + past, masks_of, _group_sinks(sink_ref, g))

    def pv(g, u):
        acc = _dot(p_buf[u, :, 0:win], vwin[g % N_PLACEMENTS, window_of(g), :])
        acc = acc + _dot(p_buf[u, :, win:win + past], vctx[g % N_PLACEMENTS])
        o_buf[g] = _normalize(acc, t_buf, u)

    _run_pipeline(n_blocks * N_PLACEMENTS, qk, softmax, pv)
    _write_heads(o_buf, o_ref, n_blocks)


def _latent_attention(q, kvp, ck, cv, sink, n_batch, seq):
    ns = seq // ATTN_TILE
    nb = seq // BLOCK
    per = ATTN_TILE // BLOCK
    own = lambda b, i: (b * ns + i, 0)
    prev = lambda b, i: (b * nb + jnp.maximum(per * i - 1, 0), 0)
    nxt = lambda b, i: (b * nb + jnp.minimum(per * i + per, nb - 1), 0)
    ctx = lambda b, i: (b, 0, 0)
    past = ck.shape[2]
    win_rows = ATTN_TILE + 2 * BLOCK
    bf = lambda *shape: pltpu.VMEM(shape, BF16)
    return pl.pallas_call(
        functools.partial(_lat_attn_kernel, n_steps=ns, past=past),
        grid=(n_batch, ns),
        in_specs=[
            pl.BlockSpec(memory_space=pltpu.SMEM),
            pl.BlockSpec((ATTN_TILE, ATTN_WIDTH), own),
            pl.BlockSpec((BLOCK, KVP_WIDTH), prev),
            pl.BlockSpec((ATTN_TILE, KVP_WIDTH), own),
            pl.BlockSpec((BLOCK, KVP_WIDTH), nxt),
            pl.BlockSpec((1, KV_WIDTH, past), ctx),
            pl.BlockSpec((1, KV_WIDTH, past), ctx),
        ],
        out_specs=pl.BlockSpec((ATTN_TILE, ATTN_WIDTH), own),
        out_shape=jax.ShapeDtypeStruct((n_batch * seq, ATTN_WIDTH), BF16),
        scratch_shapes=[bf(N_PLACEMENTS, win_rows, LANES), bf(N_PLACEMENTS, win_rows, LANES),
                        bf(N_PLACEMENTS, past, LANES), bf(N_PLACEMENTS, past, LANES)]
        + _attn_scratch(per * N_PLACEMENTS, 3 * BLOCK + past),
        compiler_params=_params("arbitrary", "arbitrary"),
        name="latent_attention",
    )(sink, q, kvp, kvp, kvp, ck, cv)


def _cumsum_rows(x, reverse):
    n = x.shape[0]
    row = lax.broadcasted_iota(jnp.int32, x.shape, 0)
    s = 1
    while s < n:
        if reverse:
            x = x + jnp.where(row < n - s, pltpu.roll(x, n - s, 0), 0.0)
        else:
            x = x + jnp.where(row >= s, pltpu.roll(x, s, 0), 0.0)
        s *= 2
    return x


def _gla_chunk(q, k, v, g, st_ref, d, o_ref, rows, reverse):
    gc = _cumsum_rows(g, reverse)
    gtot = gc[0:1, :] if reverse else gc[CHUNK - 1:CHUNK, :]
    q_dec = q * jnp.exp(gc) * (GLA_DK ** -0.5)
    k_inv = k * jnp.exp(-gc)
    k_end = k * jnp.exp(gtot - gc)
    decay = jnp.exp(gtot)
    pk = 2 * GLA_DK
    pv = 2 * GLA_DV
    lane_k = lax.broadcasted_iota(jnp.int32, (CHUNK, pk), 1)
    lo_k = lane_k < GLA_DK
    row_k = lax.broadcasted_iota(jnp.int32, (CHUNK, pk), 0)
    key = lane_k % CHUNK
    tri = (key >= row_k) if reverse else (key <= row_k)
    lane_v = lax.broadcasted_iota(jnp.int32, (CHUNK, pv), 1)
    lo_v = lane_v < GLA_DV
    srow = lax.broadcasted_iota(jnp.int32, (pv, pk), 0)
    slane = lax.broadcasted_iota(jnp.int32, (pv, pk), 1)
    diag = (srow < GLA_DV) == (slane < GLA_DK)
    for p in range(N_GLA_HEADS // 2):
        qd = q_dec[:, p * pk:(p + 1) * pk].astype(BF16)
        ki = k_inv[:, p * pk:(p + 1) * pk]
        ke = k_end[:, p * pk:(p + 1) * pk].astype(BF16)
        vp = v[:, p * pv:(p + 1) * pv]
        vpf = vp.astype(F32)
        kst = jnp.concatenate([jnp.where(lo_k, ki, 0.0), jnp.where(lo_k, 0.0, ki)], axis=0).astype(BF16)
        a = jnp.where(tri, _dot_nt(qd, kst), 0.0).astype(BF16)
        vbd = jnp.concatenate([jnp.where(lo_v, vpf, 0.0), jnp.where(lo_v, 0.0, vpf)], axis=0).astype(BF16)
        st = st_ref[d, p]
        o = _dot(a, vbd) + _dot_nt(qd, st.astype(BF16))
        o_ref[rows, p * pv:(p + 1) * pv] = o.astype(o_ref.dtype)
        ds = _dot_tn(vp, ke)
        st_ref[d, p] = st * decay[:, p * pk:(p + 1) * pk] + jnp.where(diag, ds, 0.0)


def _gla_kernel(*refs, n_steps, has_init):
    if has_init:
        (qf, kf, vf, gf, qb, kb, vb, gb, s0_ref, of_ref, ob_ref, sout_ref, st_ref) = refs
    else:
        (qf, kf, vf, gf, qb, kb, vb, gb, of_ref, ob_ref, sout_ref, st_ref) = refs
        s0_ref = None
    i = pl.program_id(1)

    def blocks():
        for d in range(2):
            for h in range(N_GLA_HEADS):
                p, e = divmod(h, 2)
                yield d, h, (d, p, slice(e * GLA_DV, (e + 1) * GLA_DV), slice(e * GLA_DK, (e + 1) * GLA_DK))

    @pl.when(i == 0)
    def _():
        st_ref[...] = jnp.zeros_like(st_ref)
        if has_init:
            for d, h, idx in blocks():
                st_ref[idx] = s0_ref[0, d, h].T

    n_chunks = qf.shape[0] // CHUNK
    for c in range(n_chunks):
        rows = slice(c * CHUNK, (c + 1) * CHUNK)
        _gla_chunk(qf[rows, :], kf[rows, :], vf[rows, :], gf[rows, :], st_ref, 0, of_ref, rows, False)
    for c in reversed(range(n_chunks)):
        rows = slice(c * CHUNK, (c + 1) * CHUNK)
        _gla_chunk(qb[rows, :], kb[rows, :], vb[rows, :], gb[rows, :], st_ref, 1, ob_ref, rows, True)

    @pl.when(i == n_steps - 1)
    def _():
        for d, h, idx in blocks():
            sout_ref[0, d, h] = st_ref[idx].T


def _gla(qg, kg, vg, gf, gb, s0, n_batch, seq):
    tm = GLA_TILE
    ns = seq // tm
    fwd = lambda b, i: (b * ns + i, 0)
    bwd = lambda b, i: (b * ns + ns - 1 - i, 0)
    st_map = lambda b, i: (b, 0, 0, 0, 0)
    st_block = (1, 2, N_GLA_HEADS, GLA_DK, GLA_DV)
    in_specs = []
    for m in (fwd, bwd):
        in_specs += [pl.BlockSpec((tm, GLA_K_WIDTH), m), pl.BlockSpec((tm, GLA_K_WIDTH), m),
                     pl.BlockSpec((tm, GLA_V_WIDTH), m), pl.BlockSpec((tm, GLA_K_WIDTH), m)]
    args = [qg, kg, vg, gf, qg, kg, vg, gb]
    if s0 is not None:
        in_specs.append(pl.BlockSpec(st_block, st_map))
        args.append(s0)
    n_tok = n_batch * seq
    return pl.pallas_call(
        functools.partial(_gla_kernel, n_steps=ns, has_init=s0 is not None),
        grid=(n_batch, ns),
        in_specs=in_specs,
        out_specs=[pl.BlockSpec((tm, GLA_V_WIDTH), fwd), pl.BlockSpec((tm, GLA_V_WIDTH), bwd),
                   pl.BlockSpec(st_block, st_map)],
        out_shape=[jax.ShapeDtypeStruct((n_tok, GLA_V_WIDTH), BF16),
                   jax.ShapeDtypeStruct((n_tok, GLA_V_WIDTH), BF16),
                   jax.ShapeDtypeStruct((n_batch,) + st_block[1:], F32)],
        scratch_shapes=[pltpu.VMEM((2, N_GLA_HEADS // 2, 2 * GLA_DV, 2 * GLA_DK), F32)],
        compiler_params=_params("arbitrary", "arbitrary"),
        name="gla_init" if s0 is not None else "gla_zero",
    )(*args)


def _merge_kernel(x_ref, mod_ref, oa_ref, za_ref, of_ref, ob_ref, zg_ref, gng_ref, wo_ref, fng_ref, y_ref):
    ya = oa_ref[...].astype(F32) * _silu(za_ref[...].astype(F32))
    og = of_ref[...].astype(F32) + ob_ref[...].astype(F32)
    zg = zg_ref[...].astype(F32)
    gng = gng_ref[...]
    parts = [ya]
    for h in range(N_GLA_HEADS):
        oh = og[:, h * GLA_DV:(h + 1) * GLA_DV]
        nh = oh * lax.rsqrt(jnp.mean(oh * oh, axis=-1, keepdims=True) + EPS)
        parts.append((nh * gng) * _silu(zg[:, h * GLA_DV:(h + 1) * GLA_DV]))
    yin = jnp.concatenate(parts, axis=-1).astype(BF16)
    gate = mod_ref[0][:, 2 * D_MODEL:3 * D_MODEL]
    y = x_ref[...] + gate * _dot(yin, wo_ref[...])
    y_ref[...] = (y * lax.rsqrt(jnp.mean(y * y, axis=-1, keepdims=True) + EPS)) * fng_ref[...]


def _merge(x2d, mod3, mod_row_of_tile, oa, za, of, ob, zg, gla_norm_g, w_out, final_norm_g, tm):
    n_tok = x2d.shape[0]
    row = lambda i: (i, 0)
    const = lambda i: (0, 0)
    wide = pl.BlockSpec((tm, GLA_V_WIDTH), row)
    return pl.pallas_call(
        _merge_kernel,
        grid=(n_tok // tm,),
        in_specs=[
            pl.BlockSpec((tm, D_MODEL), row),
            pl.BlockSpec((1, 1, 3 * D_MODEL), lambda i: (mod_row_of_tile(i), 0, 0)),
            wide, wide, wide, wide, wide,
            pl.BlockSpec((1, GLA_DV), const),
            pl.BlockSpec((D_MODEL, D_MODEL), const),
            pl.BlockSpec((1, D_MODEL), const),
        ],
        out_specs=pl.BlockSpec((tm, D_MODEL), row),
        out_shape=jax.ShapeDtypeStruct((n_tok, D_MODEL), F32),
        compiler_params=_params("arbitrary"),
        name="merge",
    )(x2d, mod3, oa, za, of, ob, zg, gla_norm_g, w_out, final_norm_g)


def _rope_tables(n_tok):
    t = np.arange(n_tok)
    rowp = (t // GRID_W).astype(np.float64)
    colp = (t % GRID_W).astype(np.float64)
    nf = HEAD_DIM // 4
    freqs = ROPE_BASE ** (-np.arange(nf, dtype=np.float64) / nf)
    lane = np.arange(LANES)
    pos = np.where(((lane % HEAD_DIM) < HEAD_DIM // 2)[None, :], rowp[:, None], colp[:, None])
    ang = pos * freqs[lane % nf][None, :]
    sign = np.where((lane % (2 * nf)) < nf, -1.0, 1.0)
    return jnp.asarray(np.cos(ang), F32), jnp.asarray(np.sin(ang) * sign[None, :], F32)


def kernel(x_prompt, x_sample, c, cache_k, cache_v, state_gla, c_ctx, w_mod, b_mod, norm_g, w_in,
           w_gk_f, b_gk_f, w_gk_b, b_gk_b, sink, gla_norm_g, w_out, final_norm_g):
    depth = w_in.shape[0]
    assert depth == 1, "single-layer step"
    l = 0
    bp, sp, _ = x_prompt.shape
    bs, ss, _ = x_sample.shape

    wl = w_in[l]
    lr0 = _VG0 + GLA_V_WIDTH
    w1 = jnp.concatenate([wl[:, :lr0], wl[:, lr0 + 2 * GATE_RANK:]], axis=1).astype(BF16)
    wlr = jnp.pad(wl[:, lr0:lr0 + 2 * GATE_RANK], ((0, 0), (0, LANES - 2 * GATE_RANK))).astype(BF16)
    wgk = jnp.zeros((LANES, 2 * GLA_K_WIDTH), F32)
    wgk = wgk.at[0:GATE_RANK, 0:GLA_K_WIDTH].set(w_gk_f[l])
    wgk = wgk.at[GATE_RANK:2 * GATE_RANK, GLA_K_WIDTH:].set(w_gk_b[l]).astype(BF16)
    bgk = jnp.concatenate([b_gk_f[l], b_gk_b[l]])[None, :]
    wo = w_out[l].astype(BF16)
    ng = norm_g[l][None, :]
    gng = gla_norm_g[l][None, :]
    fng = final_norm_g[None, :]
    sk = sink[l]

    n_rows = 8
    cvecs = jnp.concatenate([c_ctx[None, :], c, jnp.zeros((n_rows - 1 - bs, D_MODEL), F32)], axis=0)
    mod3 = _modulation(cvecs, w_mod[l], b_mod[l][None, :]).reshape(n_rows, 1, 3 * D_MODEL)
    tm = TOKEN_TILE

    xp2 = x_prompt.reshape(bp * sp, D_MODEL)
    ctx_row = lambda i: 0
    q, kvp, za, qg, kg, vg, zg, gf, gb, kt, vt = _project(xp2, mod3, ctx_row, ng, w1, wlr, wgk, bgk, None, tm, sp)
    oa = _context_attention(q, kvp, sk, sp)
    of, ob, st = _gla(qg, kg, vg, gf, gb, None, bp, sp)
    y_prompt = _merge(xp2, mod3, ctx_row, oa, za, of, ob, zg, gng, wo, fng, tm).reshape(bp, sp, D_MODEL)
    to_cache = lambda t: t.reshape(bp, 1, N_KV_HEADS, HEAD_DIM, sp).transpose(0, 1, 4, 2, 3)
    new_k, new_v = to_cache(kt), to_cache(vt)
    new_state = st[:, None]

    xs2 = x_sample.reshape(bs * ss, D_MODEL)
    tiles_per_seq = ss // tm
    lat_row = lambda i: 1 + i // tiles_per_seq
    q, kvp, za, qg, kg, vg, zg, gf, gb = _project(xs2, mod3, lat_row, ng, w1, wlr, wgk, bgk, _rope_tables(ss), tm, ss)
    past = cache_k.shape[2]
    from_cache = lambda t: t[:, l].transpose(0, 2, 3, 1).reshape(bs, KV_WIDTH, past)
    ck, cv = from_cache(cache_k), from_cache(cache_v)
    oa = _latent_attention(q, kvp, ck, cv, sk, bs, ss)
    of, ob, _ = _gla(qg, kg, vg, gf, gb, state_gla[:, l], bs, ss)
    y_sample = _merge(xs2, mod3, lat_row, oa, za, of, ob, zg, gng, wo, fng, tm).reshape(bs, ss, D_MODEL)

    return (y_prompt, y_sample, new_k, new_v, new_state)
```

```python
import functools
import math

import numpy as np
import jax
import jax.numpy as jnp
from jax import lax
from jax.experimental import pallas as pl
from jax.experimental.pallas import tpu as pltpu

D_MODEL = 1024
GRID_W = 64
HEAD_DIM = 64
N_ATTN_HEADS = 8
N_KV_HEADS = 2
ATTN_WIDTH = 512
KV_WIDTH = 128
BLOCK = 128
N_GLA_HEADS = 4
GLA_DK = 64
GLA_DV = 128
GLA_K_WIDTH = 256
GLA_V_WIDTH = 512
GATE_RANK = 16
GATE_NORMALIZER = 16.0
CHUNK = 64
ROPE_BASE = 10000.0
EPS = 1e-6
NEG_INF = -1e30

LANES = 128
TOKEN_TILE = 512
GLA_TILE = 2048
ATTN_TILE = 2048
CTX_ATTN_BATCH = 8
PIPE_SLOTS = 4
KVP_WIDTH = 8 * LANES
N_PLACEMENTS = 2 * N_KV_HEADS
SOFTMAX_ROWS = 32
MOD_TILE = 512
VMEM_LIMIT = 56 * 1024 * 1024
LOG2E = math.log2(math.e)

F32 = jnp.float32
BF16 = jnp.bfloat16


def _dot(a, b):
    return jnp.dot(a, b, preferred_element_type=F32)


def _dot_nt(a, b):
    return lax.dot_general(a, b, (((1,), (1,)), ((), ())), preferred_element_type=F32)


def _dot_tn(a, b):
    return lax.dot_general(a, b, (((0,), (0,)), ((), ())), preferred_element_type=F32)


def _silu(x):
    return x / (1.0 + jnp.exp(-x))


def _log_sigmoid(x):
    return jnp.minimum(x, 0.0) - jnp.log1p(jnp.exp(-jnp.abs(x)))


def _params(*sem):
    return pltpu.CompilerParams(dimension_semantics=sem, vmem_limit_bytes=VMEM_LIMIT)


def _mod_kernel(c_ref, w_ref, b_ref, o_ref):
    c = c_ref[...]
    o_ref[...] = jnp.dot(_silu(c), w_ref[...], preferred_element_type=F32,
                         precision=lax.Precision.HIGHEST) + b_ref[...]


def _modulation(cvecs, w_mod, b_mod):
    rows = cvecs.shape[0]
    n_out = w_mod.shape[1]
    return pl.pallas_call(
        _mod_kernel,
        grid=(n_out // MOD_TILE,),
        in_specs=[
            pl.BlockSpec((rows, D_MODEL), lambda j: (0, 0)),
            pl.BlockSpec((D_MODEL, MOD_TILE), lambda j: (0, j)),
            pl.BlockSpec((1, MOD_TILE), lambda j: (0, j)),
        ],
        out_specs=pl.BlockSpec((rows, MOD_TILE), lambda j: (0, j)),
        out_shape=jax.ShapeDtypeStruct((rows, n_out), F32),
        compiler_params=_params("arbitrary"),
        name="adaln_mod",
    )(cvecs, w_mod, b_mod)


_Q0, _K0, _V0, _ZA0, _QG0, _KG0, _VG0, _ZG0, _W1_COLS = 0, 512, 640, 768, 1280, 1536, 1792, 2304, 2816
_Q_PRESCALE = (HEAD_DIM ** -0.5) * LOG2E


def _rope(x, cos, sin_signed):
    n = x.shape[1] // LANES
    lane = lax.broadcasted_iota(jnp.int32, (x.shape[0], LANES), 1)
    first = (lane % 32) < 16
    outs = []
    for c in range(n):
        xc = x[:, c * LANES:(c + 1) * LANES]
        partner = jnp.where(first, pltpu.roll(xc, LANES - 16, 1), pltpu.roll(xc, 16, 1))
        outs.append(xc * cos + partner * sin_signed)
    return outs[0] if n == 1 else jnp.concatenate(outs, axis=-1)


def _kv_placements(k, v):
    lane = lax.broadcasted_iota(jnp.int32, k.shape, 1)
    lo = lane < HEAD_DIM
    kr = pltpu.roll(k, HEAD_DIM, 1)
    vr = pltpu.roll(v, HEAD_DIM, 1)
    pieces = [jnp.where(lo, k, 0.0), jnp.where(lo, 0.0, kr), jnp.where(lo, kr, 0.0), jnp.where(lo, 0.0, k),
              jnp.where(lo, v, 1.0), jnp.where(lo, 1.0, vr), jnp.where(lo, vr, 1.0), jnp.where(lo, 1.0, v)]
    return jnp.concatenate(pieces, axis=-1).astype(BF16)


def _cumsum_rows(x, reverse):
    n = x.shape[0]
    row = lax.broadcasted_iota(jnp.int32, x.shape, 0)
    s = 1
    while s < n:
        if reverse:
            x = x + jnp.where(row < n - s, pltpu.roll(x, n - s, 0), 0.0)
        else:
            x = x + jnp.where(row >= s, pltpu.roll(x, s, 0), 0.0)
        s *= 2
    return x


def _gla_operands(qg, kg, g, reverse, out_ref, dec_ref, d):
    for ch in range(qg.shape[0] // CHUNK):
        rows = slice(ch * CHUNK, (ch + 1) * CHUNK)
        gc = _cumsum_rows(g[rows, :], reverse)
        gtot = gc[0:1, :] if reverse else gc[CHUNK - 1:CHUNK, :]
        q, k = qg[rows, :], kg[rows, :]
        out_ref[rows, 0:GLA_K_WIDTH] = (q * jnp.exp(gc) * (GLA_DK ** -0.5)).astype(BF16)
        out_ref[rows, GLA_K_WIDTH:2 * GLA_K_WIDTH] = (k * jnp.exp(-gc)).astype(BF16)
        out_ref[rows, 2 * GLA_K_WIDTH:3 * GLA_K_WIDTH] = (k * jnp.exp(gtot - gc)).astype(BF16)
        dec_ref[ch, d:d + 1, :] = jnp.exp(gtot)


def _proj_kernel(*refs, rope):
    if rope:
        (x_ref, mod_ref, ng_ref, w1_ref, wlr_ref, wgk_ref, bgk_ref, cos_ref, sin_ref,
         q_ref, kvp_ref, za_ref, vg_ref, zg_ref, glaf_ref, glab_ref, dec_ref) = refs
    else:
        (x_ref, mod_ref, ng_ref, w1_ref, wlr_ref, wgk_ref, bgk_ref,
         q_ref, kvp_ref, za_ref, vg_ref, zg_ref, glaf_ref, glab_ref, dec_ref, k_ref, v_ref) = refs
    x = x_ref[...]
    xn = x * lax.rsqrt(jnp.mean(x * x, axis=-1, keepdims=True) + EPS)
    mod = mod_ref[0]
    shift = mod[:, 0:D_MODEL]
    scale = mod[:, D_MODEL:2 * D_MODEL]
    h = ((xn * ng_ref[...]) * (1.0 + scale) + shift).astype(BF16)

    def piece(c0, c1):
        return _dot(h, w1_ref[:, c0:c1])

    q = piece(_Q0, _K0)
    kv = piece(_K0, _ZA0)
    k = kv[:, 0:KV_WIDTH]
    v = kv[:, KV_WIDTH:2 * KV_WIDTH]
    if rope:
        cos = cos_ref[...]
        sin = sin_ref[...]
        q = _rope(q, cos, sin)
        k = _rope(k, cos, sin)
    else:
        seq = k_ref.shape[2]
        for b in range(k_ref.shape[0]):
            k_ref[b] = k[b * seq:(b + 1) * seq, :].T
            v_ref[b] = v[b * seq:(b + 1) * seq, :].T
    q_ref[...] = (q * _Q_PRESCALE).astype(BF16)
    kvp_ref[...] = _kv_placements(k, v)
    za_ref[...] = piece(_ZA0, _QG0).astype(BF16)
    vg_ref[...] = piece(_VG0, _ZG0).astype(BF16)
    zg_ref[...] = piece(_ZG0, _W1_COLS).astype(BF16)
    qkg = piece(_QG0, _VG0)
    qg = qkg[:, 0:GLA_K_WIDTH]
    kg = qkg[:, GLA_K_WIDTH:2 * GLA_K_WIDTH]
    lr = _dot(h, wlr_ref[...]).astype(BF16)
    g = _log_sigmoid(_dot(lr, wgk_ref[...]) + bgk_ref[...]) * (1.0 / GATE_NORMALIZER)
    _gla_operands(qg, kg, g[:, 0:GLA_K_WIDTH], False, glaf_ref, dec_ref, 0)
    _gla_operands(qg, kg, g[:, GLA_K_WIDTH:2 * GLA_K_WIDTH], True, glab_ref, dec_ref, 1)


def _project(x2d, mod3, mod_row_of_tile, norm_g, w1, wlr, wgk, bgk, rope_tabs, tm, seq):
    n_tok = x2d.shape[0]
    rope = rope_tabs is not None
    row = lambda i: (i, 0)
    const = lambda i: (0, 0)
    in_specs = [
        pl.BlockSpec((tm, D_MODEL), row),
        pl.BlockSpec((1, 1, 3 * D_MODEL), lambda i: (mod_row_of_tile(i), 0, 0)),
        pl.BlockSpec((1, D_MODEL), const),
        pl.BlockSpec((D_MODEL, _W1_COLS), const),
        pl.BlockSpec((D_MODEL, LANES), const),
        pl.BlockSpec((LANES, 2 * GLA_K_WIDTH), const),
        pl.BlockSpec((1, 2 * GLA_K_WIDTH), const),
    ]
    args = [x2d, mod3, norm_g, w1, wlr, wgk, bgk]
    if rope:
        tiles_per_seq = rope_tabs[0].shape[0] // tm
        pos = lambda i: (i % tiles_per_seq, 0)
        in_specs += [pl.BlockSpec((tm, LANES), pos), pl.BlockSpec((tm, LANES), pos)]
        args += list(rope_tabs)
    widths = [(ATTN_WIDTH, BF16), (KVP_WIDTH, BF16), (ATTN_WIDTH, BF16), (GLA_V_WIDTH, BF16), (GLA_V_WIDTH, BF16),
              (3 * GLA_K_WIDTH, BF16), (3 * GLA_K_WIDTH, BF16)]
    out_specs = [pl.BlockSpec((tm, w), row) for w, _ in widths]
    out_shape = [jax.ShapeDtypeStruct((n_tok, w), dt) for w, dt in widths]
    out_specs.append(pl.BlockSpec((tm // CHUNK, 2, GLA_K_WIDTH), lambda i: (i, 0, 0)))
    out_shape.append(jax.ShapeDtypeStruct((n_tok // CHUNK, 2, GLA_K_WIDTH), F32))
    if not rope:
        for _ in range(2):
            out_specs.append(pl.BlockSpec((tm // seq, KV_WIDTH, seq), lambda i: (i, 0, 0)))
            out_shape.append(jax.ShapeDtypeStruct((n_tok // seq, KV_WIDTH, seq), F32))
    return pl.pallas_call(
        functools.partial(_proj_kernel, rope=rope),
        grid=(n_tok // tm,),
        in_specs=in_specs,
        out_specs=out_specs,
        out_shape=out_shape,
        compiler_params=_params("arbitrary"),
        name="project_rope" if rope else "project",
    )(*args)


def _key_rows(start, size):
    return pl.ds(pl.multiple_of(start, BLOCK), size)


def _softmax_rows(s_buf, p_buf, t_buf, slot, n_rows, n_keys, masks_of, sink_of):
    for r0 in range(0, n_rows, SOFTMAX_ROWS):
        rs = slice(r0, r0 + SOFTMAX_ROWS)
        masks = masks_of(r0)
        cols = []
        for c in range(n_keys // LANES):
            sc = s_buf[slot, rs, c * LANES:(c + 1) * LANES]
            cols.append(jnp.where(masks[c], sc, NEG_INF) if c in masks else sc)
        mx = cols[0]
        for sc in cols[1:]:
            mx = jnp.maximum(mx, sc)
        sink = sink_of(r0)
        m = jnp.maximum(jnp.max(mx, axis=-1, keepdims=True), sink)
        for c, sc in enumerate(cols):
            p_buf[slot, rs, c * LANES:(c + 1) * LANES] = jnp.exp2(sc - m).astype(BF16)
        t_buf[slot, rs, :] = jnp.broadcast_to(jnp.exp2(sink - m), (SOFTMAX_ROWS, LANES))


def _normalize(acc, t_buf, slot):
    return acc / (pltpu.roll(acc, HEAD_DIM, 1) + t_buf[slot])


def _run_pipeline(n_groups, qk, softmax, pv):
    last = n_groups - 1
    n = PIPE_SLOTS

    def body(k, carry):
        for u in range(n):
            pv(jnp.clip(n * k + u - 2 * n, 0, last), u)
        for u in range(n):
            softmax(jnp.clip(n * k + u - n, 0, last), u)
        for u in range(n):
            qk(jnp.minimum(n * k + u, last), u)
        return carry

    lax.fori_loop(0, n_groups // n + 2, body, 0)


def _init_pipeline_scratch(first, s_buf, p_buf, t_buf):
    @pl.when(first)
    def _():
        s_buf[...] = jnp.zeros_like(s_buf)
        p_buf[...] = jnp.zeros_like(p_buf)
        t_buf[...] = jnp.ones_like(t_buf)


def _attn_scratch(n_groups, n_keys):
    rows = 2 * BLOCK
    return [pltpu.VMEM((n_groups // 2, rows, LANES), BF16),
            pltpu.VMEM((PIPE_SLOTS, rows, n_keys), F32), pltpu.VMEM((PIPE_SLOTS, rows, n_keys), BF16),
            pltpu.VMEM((PIPE_SLOTS, rows, LANES), F32), pltpu.VMEM((n_groups, rows, LANES), F32)]


def _stack_queries(q_ref, q2s, n_blocks):
    for blk in range(n_blocks):
        rows = slice(blk * BLOCK, (blk + 1) * BLOCK)
        for j in range(N_KV_HEADS):
            idx = blk * N_KV_HEADS + j
            q2s[idx, 0:BLOCK, :] = q_ref[rows, (2 * j) * LANES:(2 * j + 1) * LANES]
            q2s[idx, BLOCK:2 * BLOCK, :] = q_ref[rows, (2 * j + 1) * LANES:(2 * j + 2) * LANES]


def _write_heads(o_buf, o_ref, n_blocks):
    lo = lax.broadcasted_iota(jnp.int32, (2 * BLOCK, LANES), 1) < HEAD_DIM
    for blk in range(n_blocks):
        rows = slice(blk * BLOCK, (blk + 1) * BLOCK)
        for j in range(N_KV_HEADS):
            g0 = blk * N_PLACEMENTS + 2 * j
            o = jnp.where(lo, o_buf[g0], o_buf[g0 + 1]).astype(o_ref.dtype)
            o_ref[rows, (2 * j) * LANES:(2 * j + 1) * LANES] = o[0:BLOCK]
            o_ref[rows, (2 * j + 1) * LANES:(2 * j + 2) * LANES] = o[BLOCK:2 * BLOCK]


def _group_sinks(sink_ref, g):
    j, e = (g // 2) % N_KV_HEADS, g % 2
    sink_a = sink_ref[4 * j + e] * LOG2E
    sink_b = sink_ref[4 * j + 2 + e] * LOG2E
    return lambda r0: sink_a if r0 < BLOCK else sink_b


def _ctx_attn_kernel(sink_ref, q_ref, kvp_ref, o_ref, kown, vown, q2s, s_buf, p_buf, t_buf, o_buf, *, seq):
    n_blocks = q_ref.shape[0] // BLOCK
    _init_pipeline_scratch(pl.program_id(0) == 0, s_buf, p_buf, t_buf)
    for idx in range(N_PLACEMENTS):
        kown[idx] = kvp_ref[:, idx * LANES:(idx + 1) * LANES]
        vown[idx] = kvp_ref[:, (N_PLACEMENTS + idx) * LANES:(N_PLACEMENTS + idx + 1) * LANES]
    _stack_queries(q_ref, q2s, n_blocks)
    blocks_per_seq = seq // BLOCK

    def keys_of(g):
        return _key_rows(((g // N_PLACEMENTS) // blocks_per_seq) * seq, seq)

    def qk(g, u):
        s_buf[u] = _dot_nt(q2s[g // 2], kown[g % N_PLACEMENTS, keys_of(g), :])

    def softmax(g, u):
        _softmax_rows(s_buf, p_buf, t_buf, u, 2 * BLOCK, seq, lambda r0: {}, _group_sinks(sink_ref, g))

    def pv(g, u):
        o_buf[g] = _normalize(_dot(p_buf[u], vown[g % N_PLACEMENTS, keys_of(g), :]), t_buf, u)

    _run_pipeline(n_blocks * N_PLACEMENTS, qk, softmax, pv)
    _write_heads(o_buf, o_ref, n_blocks)


def _context_attention(q, kvp, sink, seq):
    n_tok = q.shape[0]
    rows = CTX_ATTN_BATCH * seq
    row = lambda b: (b, 0)
    n_groups = (rows // BLOCK) * N_PLACEMENTS
    return pl.pallas_call(
        functools.partial(_ctx_attn_kernel, seq=seq),
        grid=(n_tok // rows,),
        in_specs=[
            pl.BlockSpec(memory_space=pltpu.SMEM),
            pl.BlockSpec((rows, ATTN_WIDTH), row),
            pl.BlockSpec((rows, KVP_WIDTH), row),
        ],
        out_specs=pl.BlockSpec((rows, ATTN_WIDTH), row),
        out_shape=jax.ShapeDtypeStruct((n_tok, ATTN_WIDTH), BF16),
        scratch_shapes=[pltpu.VMEM((N_PLACEMENTS, rows, LANES), BF16)] * 2 + _attn_scratch(n_groups, seq),
        compiler_params=_params("arbitrary"),
        name="context_attention",
    )(sink, q, kvp)


def _lat_attn_kernel(sink_ref, q_ref, kvp_prev, kvp_own, kvp_next, ck_ref, cv_ref, o_ref,
                     kwin, vwin, kctx, vctx, q2s, s_buf, p_buf, t_buf, o_buf, *, n_steps, past):
    b, i = pl.program_id(0), pl.program_id(1)
    n_blocks = q_ref.shape[0] // BLOCK
    own_end = BLOCK + q_ref.shape[0]
    win = 3 * BLOCK
    _init_pipeline_scratch(jnp.logical_and(b == 0, i == 0), s_buf, p_buf, t_buf)

    @pl.when(i == 0)
    def _():
        cpl = _kv_placements(ck_ref[0].T, cv_ref[0].T)
        for idx in range(N_PLACEMENTS):
            kctx[idx] = cpl[:, idx * LANES:(idx + 1) * LANES]
            vctx[idx] = cpl[:, (N_PLACEMENTS + idx) * LANES:(N_PLACEMENTS + idx + 1) * LANES]

    for src, r0, r1 in ((kvp_prev, 0, BLOCK), (kvp_own, BLOCK, own_end), (kvp_next, own_end, own_end + BLOCK)):
        for idx in range(N_PLACEMENTS):
            kwin[idx, r0:r1, :] = src[:, idx * LANES:(idx + 1) * LANES]
            vwin[idx, r0:r1, :] = src[:, (N_PLACEMENTS + idx) * LANES:(N_PLACEMENTS + idx + 1) * LANES]
    _stack_queries(q_ref, q2s, n_blocks)

    row = lax.broadcasted_iota(jnp.int32, (SOFTMAX_ROWS, LANES), 0)
    c = lax.broadcasted_iota(jnp.int32, (SOFTMAX_ROWS, LANES), 1)
    off_prev = jnp.where(i > 0, 0, BLOCK)
    off_next = jnp.where(i < n_steps - 1, 0, BLOCK)

    def window_of(g):
        return _key_rows((g // N_PLACEMENTS) * BLOCK, win)

    def qk(g, u):
        q2 = q2s[g // 2]
        s_buf[u, :, 0:win] = _dot_nt(q2, kwin[g % N_PLACEMENTS, window_of(g), :])
        s_buf[u, :, win:win + past] = _dot_nt(q2, kctx[g % N_PLACEMENTS])

    def softmax(g, u):
        blk = g // N_PLACEMENTS
        off0 = jnp.where(blk == 0, off_prev, 0)
        off2 = jnp.where(blk == n_blocks - 1, off_next, 0)

        def masks_of(r0):
            r = row + r0 % BLOCK
            return {0: c >= r + off0, 2: c <= r - off2}

        _softmax_rows(s_buf, p_buf, t_buf, u, 2 * BLOCK, win + past, masks_of, _group_sinks(sink_ref, g))

    def pv(g, u):
        acc = _dot(p_buf[u, :, 0:win], vwin[g % N_PLACEMENTS, window_of(g), :])
        acc = acc + _dot(p_buf[u, :, win:win + past], vctx[g % N_PLACEMENTS])
        o_buf[g] = _normalize(acc, t_buf, u)

    _run_pipeline(n_blocks * N_PLACEMENTS, qk, softmax, pv)
    _write_heads(o_buf, o_ref, n_blocks)


def _latent_attention(q, kvp, ck, cv, sink, n_batch, seq):
    ns = seq // ATTN_TILE
    nb = seq // BLOCK
    per = ATTN_TILE // BLOCK
    own = lambda b, i: (b * ns + i, 0)
    prev = lambda b, i: (b * nb + jnp.maximum(per * i - 1, 0), 0)
    nxt = lambda b, i: (b * nb + jnp.minimum(per * i + per, nb - 1), 0)
    ctx = lambda b, i: (b, 0, 0)
    past = ck.shape[2]
    win_rows = ATTN_TILE + 2 * BLOCK
    bf = lambda *shape: pltpu.VMEM(shape, BF16)
    return pl.pallas_call(
        functools.partial(_lat_attn_kernel, n_steps=ns, past=past),
        grid=(n_batch, ns),
        in_specs=[
            pl.BlockSpec(memory_space=pltpu.SMEM),
            pl.BlockSpec((ATTN_TILE, ATTN_WIDTH), own),
            pl.BlockSpec((BLOCK, KVP_WIDTH), prev),
            pl.BlockSpec((ATTN_TILE, KVP_WIDTH), own),
            pl.BlockSpec((BLOCK, KVP_WIDTH), nxt),
            pl.BlockSpec((1, KV_WIDTH, past), ctx),
            pl.BlockSpec((1, KV_WIDTH, past), ctx),
        ],
        out_specs=pl.BlockSpec((ATTN_TILE, ATTN_WIDTH), own),
        out_shape=jax.ShapeDtypeStruct((n_batch * seq, ATTN_WIDTH), BF16),
        scratch_shapes=[bf(N_PLACEMENTS, win_rows, LANES), bf(N_PLACEMENTS, win_rows, LANES),
                        bf(N_PLACEMENTS, past, LANES), bf(N_PLACEMENTS, past, LANES)]
        + _attn_scratch(per * N_PLACEMENTS, 3 * BLOCK + past),
        compiler_params=_params("arbitrary", "arbitrary"),
        name="latent_attention",
    )(sink, q, kvp, kvp, kvp, ck, cv)


_PK = 2 * GLA_DK
_PV = 2 * GLA_DV


def _gla_kernel(*refs, seq_chunks, has_init, emit_state):
    it = iter(refs)
    views = [tuple(next(it) for _ in range(3)) for _ in range(2)]
    s0_ref = next(it) if has_init else None
    out_refs = (next(it), next(it))
    sout_ref = next(it) if emit_state else None
    st_ref, a_buf, a2_buf, ds_buf, sb_buf, dc_buf = (next(it) for _ in range(6))
    n_chunks = views[0][0].shape[0] // CHUNK
    n_groups = 2 * n_chunks
    n_pairs = N_GLA_HEADS // 2
    tile = pl.program_id(1)

    @pl.when(jnp.logical_and(pl.program_id(0) == 0, tile == 0))
    def _():
        for ref in (st_ref, a_buf, a2_buf, ds_buf, sb_buf, dc_buf):
            ref[...] = jnp.zeros_like(ref)

    if has_init:
        @pl.when(tile == 0)
        def _():
            st_ref[...] = jnp.zeros_like(st_ref)
            for d in range(2):
                for h in range(N_GLA_HEADS):
                    p, e = divmod(h, 2)
                    st_ref[d, p, e * GLA_DK:(e + 1) * GLA_DK, e * GLA_DV:(e + 1) * GLA_DV] = s0_ref[0, d, h]

    lane_k = lax.broadcasted_iota(jnp.int32, (CHUNK, _PK), 1)
    row_k = lax.broadcasted_iota(jnp.int32, (CHUNK, _PK), 0)
    lo_k = lane_k < GLA_DK
    tri = (lane_k % CHUNK <= row_k, lane_k % CHUNK >= row_k)
    lo_v = lax.broadcasted_iota(jnp.int32, (CHUNK, _PV), 1) < GLA_DV
    diag = ((lax.broadcasted_iota(jnp.int32, (_PK, _PV), 0) < GLA_DK)
            == (lax.broadcasted_iota(jnp.int32, (_PK, _PV), 1) < GLA_DV))

    def place(g, d):
        c = g // 2
        if d == 0:
            return c, c
        seq = c // seq_chunks
        return c, seq * seq_chunks + (seq_chunks - 1 - (c - seq * seq_chunks))

    def rows_of(pos):
        return pl.ds(pl.multiple_of(pos * CHUNK, CHUNK), CHUNK)

    def prep(g, u):
        d = u % 2
        ops_ref, v_ref, dec_ref = views[d]
        _, pos = place(g, d)
        rows = rows_of(pos)
        dec = dec_ref[pos]
        for p in range(n_pairs):
            qd = ops_ref[rows, p * _PK:(p + 1) * _PK]
            ki = ops_ref[rows, GLA_K_WIDTH + p * _PK:GLA_K_WIDTH + (p + 1) * _PK].astype(F32)
            ke = ops_ref[rows, 2 * GLA_K_WIDTH + p * _PK:2 * GLA_K_WIDTH + (p + 1) * _PK]
            kst = jnp.concatenate([jnp.where(lo_k, ki, 0.0), jnp.where(lo_k, 0.0, ki)], axis=0).astype(BF16)
            a_buf[u, p] = jnp.where(tri[d], _dot_nt(qd, kst), 0.0).astype(BF16)
            ds_buf[u, p] = jnp.where(diag, _dot_tn(ke, v_ref[rows, p * _PV:(p + 1) * _PV]), 0.0)
            drow = dec[d:d + 1, p * _PK:(p + 1) * _PK]
            dc_buf[u, p] = jnp.broadcast_to(drow, (_PK, _PK)).T

    def state(g_raw, u):
        d = u % 2
        valid = jnp.logical_and(g_raw >= 0, g_raw < n_groups)
        c, _ = place(jnp.clip(g_raw, 0, n_groups - 1), d)
        for p in range(n_pairs):
            s_old = st_ref[d, p]
            fresh = (c % seq_chunks == 0) if emit_state else (c < 0)
            s_in = jnp.where(fresh, 0.0, s_old)
            sb_buf[u, p] = s_in.astype(BF16)
            a2_buf[u, p] = a_buf[u, p]
            dc = dc_buf[u, p]
            s_new = jnp.where(valid, s_in * jnp.concatenate([dc, dc], axis=1) + ds_buf[u, p], s_old)
            st_ref[d, p] = s_new
            if emit_state:
                for e in range(2):
                    sout_ref[c // seq_chunks, d, 2 * p + e] = s_new[e * GLA_DK:(e + 1) * GLA_DK,
                                                                    e * GLA_DV:(e + 1) * GLA_DV]

    def out(g, u):
        d = u % 2
        ops_ref, v_ref, _ = views[d]
        _, pos = place(g, d)
        rows = rows_of(pos)
        for p in range(n_pairs):
            vpf = v_ref[rows, p * _PV:(p + 1) * _PV].astype(F32)
            vbd = jnp.concatenate([jnp.where(lo_v, vpf, 0.0), jnp.where(lo_v, 0.0, vpf)], axis=0).astype(BF16)
            x = jnp.concatenate([a2_buf[u, p], ops_ref[rows, p * _PK:(p + 1) * _PK]], axis=1)
            w = jnp.concatenate([vbd, sb_buf[u, p]], axis=0)
            out_refs[d][rows, p * _PV:(p + 1) * _PV] = _dot(x, w).astype(out_refs[d].dtype)

    n = PIPE_SLOTS
    last = n_groups - 1

    def body(k, carry):
        for u in range(n):
            out(jnp.clip(n * k + u - 2 * n, 0, last), u)
        for u in range(n):
            state(n * k + u - n, u)
        for u in range(n):
            prep(jnp.minimum(n * k + u, last), u)
        return carry

    lax.fori_loop(0, n_groups // n + 2, body, 0)


def _gla(ops_f, ops_b, vg, dec, s0, n_batch, seq):
    tm = GLA_TILE
    long_seq = s0 is not None
    if long_seq:
        nt = seq // tm
        grid = (n_batch, nt)
        fwd = lambda b, i: (b * nt + i, 0)
        bwd = lambda b, i: (b * nt + nt - 1 - i, 0)
        seq_chunks = tm // CHUNK
    else:
        per_step = tm // seq
        grid = (1, n_batch // per_step)
        fwd = bwd = lambda b, i: (i, 0)
        seq_chunks = seq // CHUNK
    in_specs = []
    args = []
    for m, ops in ((fwd, ops_f), (bwd, ops_b)):
        in_specs += [pl.BlockSpec((tm, 3 * GLA_K_WIDTH), m), pl.BlockSpec((tm, GLA_V_WIDTH), m),
                     pl.BlockSpec((tm // CHUNK, 2, GLA_K_WIDTH), lambda b, i, m=m: m(b, i) + (0,))]
        args += [ops, vg, dec]
    n_tok = n_batch * seq
    out_specs = [pl.BlockSpec((tm, GLA_V_WIDTH), fwd), pl.BlockSpec((tm, GLA_V_WIDTH), bwd)]
    out_shape = [jax.ShapeDtypeStruct((n_tok, GLA_V_WIDTH), BF16)] * 2
    st_dims = (2, N_GLA_HEADS, GLA_DK, GLA_DV)
    if long_seq:
        in_specs.append(pl.BlockSpec((1,) + st_dims, lambda b, i: (b, 0, 0, 0, 0)))
        args.append(s0)
    else:
        out_specs.append(pl.BlockSpec((per_step,) + st_dims, lambda b, i: (i, 0, 0, 0, 0)))
        out_shape.append(jax.ShapeDtypeStruct((n_batch,) + st_dims, F32))
    n = PIPE_SLOTS
    n_pairs = N_GLA_HEADS // 2
    return pl.pallas_call(
        functools.partial(_gla_kernel, seq_chunks=seq_chunks, has_init=long_seq, emit_state=not long_seq),
        grid=grid,
        in_specs=in_specs,
        out_specs=out_specs,
        out_shape=out_shape,
        scratch_shapes=[pltpu.VMEM((2, n_pairs, _PK, _PV), F32),
                        pltpu.VMEM((n, n_pairs, CHUNK, _PK), BF16),
                        pltpu.VMEM((n, n_pairs, CHUNK, _PK), BF16),
                        pltpu.VMEM((n, n_pairs, _PK, _PV), F32),
                        pltpu.VMEM((n, n_pairs, _PK, _PV), BF16),
                        pltpu.VMEM((n, n_pairs, _PK, _PK), F32)],
        compiler_params=_params("arbitrary", "arbitrary"),
        name="gla_long" if long_seq else "gla_short",
    )(*args)


def _merge_kernel(x_ref, mod_ref, oa_ref, za_ref, of_ref, ob_ref, zg_ref, gng_ref, wo_ref, fng_ref, y_ref):
    ya = oa_ref[...].astype(F32) * _silu(za_ref[...].astype(F32))
    og = of_ref[...].astype(F32) + ob_ref[...].astype(F32)
    zg = zg_ref[...].astype(F32)
    gng = gng_ref[...]
    parts = [ya]
    for h in range(N_GLA_HEADS):
        oh = og[:, h * GLA_DV:(h + 1) * GLA_DV]
        nh = oh * lax.rsqrt(jnp.mean(oh * oh, axis=-1, keepdims=True) + EPS)
        parts.append((nh * gng) * _silu(zg[:, h * GLA_DV:(h + 1) * GLA_DV]))
    yin = jnp.concatenate(parts, axis=-1).astype(BF16)
    gate = mod_ref[0][:, 2 * D_MODEL:3 * D_MODEL]
    y = x_ref[...] + gate * _dot(yin, wo_ref[...])
    y_ref[...] = (y * lax.rsqrt(jnp.mean(y * y, axis=-1, keepdims=True) + EPS)) * fng_ref[...]


def _merge(x2d, mod3, mod_row_of_tile, oa, za, of, ob, zg, gla_norm_g, w_out, final_norm_g, tm):
    n_tok = x2d.shape[0]
    row = lambda i: (i, 0)
    const = lambda i: (0, 0)
    wide = pl.BlockSpec((tm, GLA_V_WIDTH), row)
    return pl.pallas_call(
        _merge_kernel,
        grid=(n_tok // tm,),
        in_specs=[
            pl.BlockSpec((tm, D_MODEL), row),
            pl.BlockSpec((1, 1, 3 * D_MODEL), lambda i: (mod_row_of_tile(i), 0, 0)),
            wide, wide, wide, wide, wide,
            pl.BlockSpec((1, GLA_DV), const),
            pl.BlockSpec((D_MODEL, D_MODEL), const),
            pl.BlockSpec((1, D_MODEL), const),
        ],
        out_specs=pl.BlockSpec((tm, D_MODEL), row),
        out_shape=jax.ShapeDtypeStruct((n_tok, D_MODEL), F32),
        compiler_params=_params("arbitrary"),
        name="merge",
    )(x2d, mod3, oa, za, of, ob, zg, gla_norm_g, w_out, final_norm_g)


def _rope_tables(n_tok):
    t = np.arange(n_tok)
    rowp = (t // GRID_W).astype(np.float64)
    colp = (t % GRID_W).astype(np.float64)
    nf = HEAD_DIM // 4
    freqs = ROPE_BASE ** (-np.arange(nf, dtype=np.float64) / nf)
    lane = np.arange(LANES)
    pos = np.where(((lane % HEAD_DIM) < HEAD_DIM // 2)[None, :], rowp[:, None], colp[:, None])
    ang = pos * freqs[lane % nf][None, :]
    sign = np.where((lane % (2 * nf)) < nf, -1.0, 1.0)
    return jnp.asarray(np.cos(ang), F32), jnp.asarray(np.sin(ang) * sign[None, :], F32)


def kernel(x_prompt, x_sample, c, cache_k, cache_v, state_gla, c_ctx, w_mod, b_mod, norm_g, w_in,
           w_gk_f, b_gk_f, w_gk_b, b_gk_b, sink, gla_norm_g, w_out, final_norm_g):
    depth = w_in.shape[0]
    assert depth == 1, "single-layer step"
    l = 0
    bp, sp, _ = x_prompt.shape
    bs, ss, _ = x_sample.shape

    wl = w_in[l]
    lr0 = _VG0 + GLA_V_WIDTH
    w1 = jnp.concatenate([wl[:, :lr0], wl[:, lr0 + 2 * GATE_RANK:]], axis=1).astype(BF16)
    wlr = jnp.pad(wl[:, lr0:lr0 + 2 * GATE_RANK], ((0, 0), (0, LANES - 2 * GATE_RANK))).astype(BF16)
    wgk = jnp.zeros((LANES, 2 * GLA_K_WIDTH), F32)
    wgk = wgk.at[0:GATE_RANK, 0:GLA_K_WIDTH].set(w_gk_f[l])
    wgk = wgk.at[GATE_RANK:2 * GATE_RANK, GLA_K_WIDTH:].set(w_gk_b[l]).astype(BF16)
    bgk = jnp.concatenate([b_gk_f[l], b_gk_b[l]])[None, :]
    wo = w_out[l].astype(BF16)
    ng = norm_g[l][None, :]
    gng = gla_norm_g[l][None, :]
    fng = final_norm_g[None, :]
    sk = sink[l]

    n_rows = 8
    cvecs = jnp.concatenate([c_ctx[None, :], c, jnp.zeros((n_rows - 1 - bs, D_MODEL), F32)], axis=0)
    mod3 = _modulation(cvecs, w_mod[l], b_mod[l][None, :]).reshape(n_rows, 1, 3 * D_MODEL)
    tm = TOKEN_TILE

    xp2 = x_prompt.reshape(bp * sp, D_MODEL)
    ctx_row = lambda i: 0
    q, kvp, za, vg, zg, glaf, glab, dec, kt, vt = _project(xp2, mod3, ctx_row, ng, w1, wlr, wgk, bgk, None, tm, sp)
    oa = _context_attention(q, kvp, sk, sp)
    of, ob, st = _gla(glaf, glab, vg, dec, None, bp, sp)
    y_prompt = _merge(xp2, mod3, ctx_row, oa, za, of, ob, zg, gng, wo, fng, tm).reshape(bp, sp, D_MODEL)
    to_cache = lambda t: t.reshape(bp, 1, N_KV_HEADS, HEAD_DIM, sp).transpose(0, 1, 4, 2, 3)
    new_k, new_v = to_cache(kt), to_cache(vt)
    new_state = st[:, None]

    xs2 = x_sample.reshape(bs * ss, D_MODEL)
    tiles_per_seq = ss // tm
    lat_row = lambda i: 1 + i // tiles_per_seq
    q, kvp, za, vg, zg, glaf, glab, dec = _project(xs2, mod3, lat_row, ng, w1, wlr, wgk, bgk, _rope_tables(ss), tm, ss)
    past = cache_k.shape[2]
    from_cache = lambda t: t[:, l].transpose(0, 2, 3, 1).reshape(bs, KV_WIDTH, past)
    ck, cv = from_cache(cache_k), from_cache(cache_v)
    oa = _latent_attention(q, kvp, ck, cv, sk, bs, ss)
    of, ob = _gla(glaf, glab, vg, dec, state_gla[:, l], bs, ss)
    y_sample = _merge(xs2, mod3, lat_row, oa, za, of, ob, zg, gng, wo, fng, tm).reshape(bs, ss, D_MODEL)

    return (y_prompt, y_sample, new_k, new_v, new_state)
```

```python
import functools
import math

import numpy as np
import jax
import jax.numpy as jnp
from jax import lax
from jax.experimental import pallas as pl
from jax.experimental.pallas import tpu as pltpu

D_MODEL = 1024
GRID_W = 64
HEAD_DIM = 64
N_ATTN_HEADS = 8
N_KV_HEADS = 2
ATTN_WIDTH = 512
KV_WIDTH = 128
BLOCK = 128
N_GLA_HEADS = 4
GLA_DK = 64
GLA_DV = 128
GLA_K_WIDTH = 256
GLA_V_WIDTH = 512
GATE_RANK = 16
GATE_NORMALIZER = 16.0
CHUNK = 64
ROPE_BASE = 10000.0
EPS = 1e-6
NEG_INF = -1e30

LANES = 128
TOKEN_TILE = 512
GLA_TILE = 2048
ATTN_TILE = 2048
CTX_ATTN_BATCH = 8
PIPE_SLOTS = 4
GLA_PIPE_SLOTS = 4
KVP_WIDTH = 8 * LANES
N_PLACEMENTS = 2 * N_KV_HEADS
SOFTMAX_ROWS = 32
MOD_TILE = 512
VMEM_LIMIT = 56 * 1024 * 1024
LOG2E = math.log2(math.e)

F32 = jnp.float32
BF16 = jnp.bfloat16


def _dot(a, b):
    return jnp.dot(a, b, preferred_element_type=F32)


def _dot_nt(a, b):
    return lax.dot_general(a, b, (((1,), (1,)), ((), ())), preferred_element_type=F32)


def _dot_tn(a, b):
    return lax.dot_general(a, b, (((0,), (0,)), ((), ())), preferred_element_type=F32)


def _silu(x):
    return x / (1.0 + jnp.exp(-x))


def _log_sigmoid(x):
    return jnp.minimum(x, 0.0) - jnp.log1p(jnp.exp(-jnp.abs(x)))


def _params(*sem):
    return pltpu.CompilerParams(dimension_semantics=sem, vmem_limit_bytes=VMEM_LIMIT)


def _mod_kernel(c_ref, w_ref, b_ref, o_ref):
    c = c_ref[...]
    o_ref[...] = jnp.dot(_silu(c), w_ref[...], preferred_element_type=F32,
                         precision=lax.Precision.HIGHEST) + b_ref[...]


def _modulation(cvecs, w_mod, b_mod):
    rows = cvecs.shape[0]
    n_out = w_mod.shape[1]
    return pl.pallas_call(
        _mod_kernel,
        grid=(n_out // MOD_TILE,),
        in_specs=[
            pl.BlockSpec((rows, D_MODEL), lambda j: (0, 0)),
            pl.BlockSpec((D_MODEL, MOD_TILE), lambda j: (0, j)),
            pl.BlockSpec((1, MOD_TILE), lambda j: (0, j)),
        ],
        out_specs=pl.BlockSpec((rows, MOD_TILE), lambda j: (0, j)),
        out_shape=jax.ShapeDtypeStruct((rows, n_out), F32),
        compiler_params=_params("arbitrary"),
        name="adaln_mod",
    )(cvecs, w_mod, b_mod)


_Q0, _K0, _V0, _ZA0, _QG0, _KG0, _VG0, _ZG0, _W1_COLS = 0, 512, 640, 768, 1280, 1536, 1792, 2304, 2816
_Q_PRESCALE = (HEAD_DIM ** -0.5) * LOG2E


def _rope(x, cos, sin_signed):
    n = x.shape[1] // LANES
    lane = lax.broadcasted_iota(jnp.int32, (x.shape[0], LANES), 1)
    first = (lane % 32) < 16
    outs = []
    for c in range(n):
        xc = x[:, c * LANES:(c + 1) * LANES]
        partner = jnp.where(first, pltpu.roll(xc, LANES - 16, 1), pltpu.roll(xc, 16, 1))
        outs.append(xc * cos + partner * sin_signed)
    return outs[0] if n == 1 else jnp.concatenate(outs, axis=-1)


def _kv_placements(k, v):
    lane = lax.broadcasted_iota(jnp.int32, k.shape, 1)
    lo = lane < HEAD_DIM
    kr = pltpu.roll(k, HEAD_DIM, 1)
    vr = pltpu.roll(v, HEAD_DIM, 1)
    pieces = [jnp.where(lo, k, 0.0), jnp.where(lo, 0.0, kr), jnp.where(lo, kr, 0.0), jnp.where(lo, 0.0, k),
              jnp.where(lo, v, 1.0), jnp.where(lo, 1.0, vr), jnp.where(lo, vr, 1.0), jnp.where(lo, 1.0, v)]
    return jnp.concatenate(pieces, axis=-1).astype(BF16)


def _proj_kernel(*refs, rope):
    if rope:
        (x_ref, mod_ref, ng_ref, w1_ref, wlr_ref, wgk_ref, bgk_ref, cos_ref, sin_ref,
         q_ref, kvp_ref, za_ref, vg_ref, zg_ref, qkg_ref, gf_ref, gb_ref) = refs
    else:
        (x_ref, mod_ref, ng_ref, w1_ref, wlr_ref, wgk_ref, bgk_ref,
         q_ref, kvp_ref, za_ref, vg_ref, zg_ref, qkg_ref, gf_ref, gb_ref, k_ref, v_ref) = refs
    x = x_ref[...]
    xn = x * lax.rsqrt(jnp.mean(x * x, axis=-1, keepdims=True) + EPS)
    mod = mod_ref[0]
    shift = mod[:, 0:D_MODEL]
    scale = mod[:, D_MODEL:2 * D_MODEL]
    h = ((xn * ng_ref[...]) * (1.0 + scale) + shift).astype(BF16)

    def piece(c0, c1):
        return _dot(h, w1_ref[:, c0:c1])

    q = piece(_Q0, _K0)
    kv = piece(_K0, _ZA0)
    k = kv[:, 0:KV_WIDTH]
    v = kv[:, KV_WIDTH:2 * KV_WIDTH]
    if rope:
        cos = cos_ref[...]
        sin = sin_ref[...]
        q = _rope(q, cos, sin)
        k = _rope(k, cos, sin)
    else:
        seq = k_ref.shape[2]
        for b in range(k_ref.shape[0]):
            k_ref[b] = k[b * seq:(b + 1) * seq, :].T
            v_ref[b] = v[b * seq:(b + 1) * seq, :].T
    q_ref[...] = (q * _Q_PRESCALE).astype(BF16)
    kvp_ref[...] = _kv_placements(k, v)
    za_ref[...] = piece(_ZA0, _QG0).astype(BF16)
    vg_ref[...] = piece(_VG0, _ZG0).astype(BF16)
    zg_ref[...] = piece(_ZG0, _W1_COLS).astype(BF16)
    qkg_ref[...] = piece(_QG0, _VG0)
    lr = _dot(h, wlr_ref[...]).astype(BF16)
    g = _log_sigmoid(_dot(lr, wgk_ref[...]) + bgk_ref[...]) * (1.0 / GATE_NORMALIZER)
    gf_ref[...] = g[:, 0:GLA_K_WIDTH]
    gb_ref[...] = g[:, GLA_K_WIDTH:2 * GLA_K_WIDTH]


def _project(x2d, mod3, mod_row_of_tile, norm_g, w1, wlr, wgk, bgk, rope_tabs, tm, seq):
    n_tok = x2d.shape[0]
    rope = rope_tabs is not None
    row = lambda i: (i, 0)
    const = lambda i: (0, 0)
    in_specs = [
        pl.BlockSpec((tm, D_MODEL), row),
        pl.BlockSpec((1, 1, 3 * D_MODEL), lambda i: (mod_row_of_tile(i), 0, 0)),
        pl.BlockSpec((1, D_MODEL), const),
        pl.BlockSpec((D_MODEL, _W1_COLS), const),
        pl.BlockSpec((D_MODEL, LANES), const),
        pl.BlockSpec((LANES, 2 * GLA_K_WIDTH), const),
        pl.BlockSpec((1, 2 * GLA_K_WIDTH), const),
    ]
    args = [x2d, mod3, norm_g, w1, wlr, wgk, bgk]
    if rope:
        tiles_per_seq = rope_tabs[0].shape[0] // tm
        pos = lambda i: (i % tiles_per_seq, 0)
        in_specs += [pl.BlockSpec((tm, LANES), pos), pl.BlockSpec((tm, LANES), pos)]
        args += list(rope_tabs)
    widths = [(ATTN_WIDTH, BF16), (KVP_WIDTH, BF16), (ATTN_WIDTH, BF16), (GLA_V_WIDTH, BF16), (GLA_V_WIDTH, BF16),
              (2 * GLA_K_WIDTH, F32), (GLA_K_WIDTH, F32), (GLA_K_WIDTH, F32)]
    out_specs = [pl.BlockSpec((tm, w), row) for w, _ in widths]
    out_shape = [jax.ShapeDtypeStruct((n_tok, w), dt) for w, dt in widths]
    if not rope:
        for _ in range(2):
            out_specs.append(pl.BlockSpec((tm // seq, KV_WIDTH, seq), lambda i: (i, 0, 0)))
            out_shape.append(jax.ShapeDtypeStruct((n_tok // seq, KV_WIDTH, seq), F32))
    return pl.pallas_call(
        functools.partial(_proj_kernel, rope=rope),
        grid=(n_tok // tm,),
        in_specs=in_specs,
        out_specs=out_specs,
        out_shape=out_shape,
        compiler_params=_params("arbitrary"),
        name="project_rope" if rope else "project",
    )(*args)


def _software_pipeline(n_groups, n_slots, stages):
    last_g = n_groups - 1

    def body(k, carry):
        for age in (2, 1, 0):
            for u in range(n_slots):
                g = n_slots * (k - age) + u
                stages[age](jnp.clip(g, 0, last_g), u, jnp.logical_and(g >= 0, g <= last_g))
        return carry

    lax.fori_loop(0, n_groups // n_slots + 2, body, 0)


def _init_pipeline_scratch(first, zero_refs, one_refs=()):
    @pl.when(first)
    def _():
        for ref in zero_refs:
            ref[...] = jnp.zeros_like(ref)
        for ref in one_refs:
            ref[...] = jnp.ones_like(ref)


def _key_rows(start, size):
    return pl.ds(pl.multiple_of(start, BLOCK), size)


def _softmax_rows(s_buf, p_buf, t_buf, slot, n_rows, n_keys, masks_of, sink_of):
    for r0 in range(0, n_rows, SOFTMAX_ROWS):
        rs = slice(r0, r0 + SOFTMAX_ROWS)
        masks = masks_of(r0)
        cols = []
        for c in range(n_keys // LANES):
            sc = s_buf[slot, rs, c * LANES:(c + 1) * LANES]
            cols.append(jnp.where(masks[c], sc, NEG_INF) if c in masks else sc)
        mx = cols[0]
        for sc in cols[1:]:
            mx = jnp.maximum(mx, sc)
        sink = sink_of(r0)
        m = jnp.maximum(jnp.max(mx, axis=-1, keepdims=True), sink)
        for c, sc in enumerate(cols):
            p_buf[slot, rs, c * LANES:(c + 1) * LANES] = jnp.exp2(sc - m).astype(BF16)
        t_buf[slot, rs, :] = jnp.broadcast_to(jnp.exp2(sink - m), (SOFTMAX_ROWS, LANES))


def _normalize(acc, t_buf, slot):
    return acc / (pltpu.roll(acc, HEAD_DIM, 1) + t_buf[slot])


def _attn_scratch(n_groups, n_keys):
    rows = 2 * BLOCK
    return [pltpu.VMEM((n_groups // 2, rows, LANES), BF16),
            pltpu.VMEM((PIPE_SLOTS, rows, n_keys), F32), pltpu.VMEM((PIPE_SLOTS, rows, n_keys), BF16),
            pltpu.VMEM((PIPE_SLOTS, rows, LANES), F32), pltpu.VMEM((n_groups, rows, LANES), F32)]


def _stack_queries(q_ref, q2s, n_blocks):
    for blk in range(n_blocks):
        rows = slice(blk * BLOCK, (blk + 1) * BLOCK)
        for j in range(N_KV_HEADS):
            idx = blk * N_KV_HEADS + j
            q2s[idx, 0:BLOCK, :] = q_ref[rows, (2 * j) * LANES:(2 * j + 1) * LANES]
            q2s[idx, BLOCK:2 * BLOCK, :] = q_ref[rows, (2 * j + 1) * LANES:(2 * j + 2) * LANES]


def _write_heads(o_buf, o_ref, n_blocks):
    lo = lax.broadcasted_iota(jnp.int32, (2 * BLOCK, LANES), 1) < HEAD_DIM
    for blk in range(n_blocks):
        rows = slice(blk * BLOCK, (blk + 1) * BLOCK)
        for j in range(N_KV_HEADS):
            g0 = blk * N_PLACEMENTS + 2 * j
            o = jnp.where(lo, o_buf[g0], o_buf[g0 + 1]).astype(o_ref.dtype)
            o_ref[rows, (2 * j) * LANES:(2 * j + 1) * LANES] = o[0:BLOCK]
            o_ref[rows, (2 * j + 1) * LANES:(2 * j + 2) * LANES] = o[BLOCK:2 * BLOCK]


def _group_sinks(sink_ref, g):
    j, e = (g // 2) % N_KV_HEADS, g % 2
    sink_a = sink_ref[4 * j + e] * LOG2E
    sink_b = sink_ref[4 * j + 2 + e] * LOG2E
    return lambda r0: sink_a if r0 < BLOCK else sink_b


def _ctx_attn_kernel(sink_ref, q_ref, kvp_ref, o_ref, kown, vown, q2s, s_buf, p_buf, t_buf, o_buf, *, seq):
    n_blocks = q_ref.shape[0] // BLOCK
    _init_pipeline_scratch(pl.program_id(0) == 0, (s_buf, p_buf), (t_buf,))
    for idx in range(N_PLACEMENTS):
        kown[idx] = kvp_ref[:, idx * LANES:(idx + 1) * LANES]
        vown[idx] = kvp_ref[:, (N_PLACEMENTS + idx) * LANES:(N_PLACEMENTS + idx + 1) * LANES]
    _stack_queries(q_ref, q2s, n_blocks)
    blocks_per_seq = seq // BLOCK

    def keys_of(g):
        return _key_rows(((g // N_PLACEMENTS) // blocks_per_seq) * seq, seq)

    def qk(g, u, valid):
        s_buf[u] = _dot_nt(q2s[g // 2], kown[g % N_PLACEMENTS, keys_of(g), :])

    def softmax(g, u, valid):
        _softmax_rows(s_buf, p_buf, t_buf, u, 2 * BLOCK, seq, lambda r0: {}, _group_sinks(sink_ref, g))

    def pv(g, u, valid):
        o_buf[g] = _normalize(_dot(p_buf[u], vown[g % N_PLACEMENTS, keys_of(g), :]), t_buf, u)

    _software_pipeline(n_blocks * N_PLACEMENTS, PIPE_SLOTS, (qk, softmax, pv))
    _write_heads(o_buf, o_ref, n_blocks)


def _context_attention(q, kvp, sink, seq):
    n_tok = q.shape[0]
    rows = CTX_ATTN_BATCH * seq
    row = lambda b: (b, 0)
    n_groups = (rows // BLOCK) * N_PLACEMENTS
    return pl.pallas_call(
        functools.partial(_ctx_attn_kernel, seq=seq),
        grid=(n_tok // rows,),
        in_specs=[
            pl.BlockSpec(memory_space=pltpu.SMEM),
            pl.BlockSpec((rows, ATTN_WIDTH), row),
            pl.BlockSpec((rows, KVP_WIDTH), row),
        ],
        out_specs=pl.BlockSpec((rows, ATTN_WIDTH), row),
        out_shape=jax.ShapeDtypeStruct((n_tok, ATTN_WIDTH), BF16),
        scratch_shapes=[pltpu.VMEM((N_PLACEMENTS, rows, LANES), BF16)] * 2 + _attn_scratch(n_groups, seq),
        compiler_params=_params("arbitrary"),
        name="context_attention",
    )(sink, q, kvp)


def _lat_attn_kernel(sink_ref, q_ref, kvp_prev, kvp_own, kvp_next, ck_ref, cv_ref, o_ref,
                     kwin, vwin, kctx, vctx, q2s, s_buf, p_buf, t_buf, o_buf, *, n_steps, past):
    i = pl.program_id(1)
    n_blocks = q_ref.shape[0] // BLOCK
    own_end = BLOCK + q_ref.shape[0]
    win = 3 * BLOCK
    _init_pipeline_scratch(jnp.logical_and(pl.program_id(0) == 0, i == 0), (s_buf, p_buf), (t_buf,))

    @pl.when(i == 0)
    def _():
        cpl = _kv_placements(ck_ref[0].T, cv_ref[0].T)
        for idx in range(N_PLACEMENTS):
            kctx[idx] = cpl[:, idx * LANES:(idx + 1) * LANES]
            vctx[idx] = cpl[:, (N_PLACEMENTS + idx) * LANES:(N_PLACEMENTS + idx + 1) * LANES]

    for src, r0, r1 in ((kvp_prev, 0, BLOCK), (kvp_own, BLOCK, own_end), (kvp_next, own_end, own_end + BLOCK)):
        for idx in range(N_PLACEMENTS):
            kwin[idx, r0:r1, :] = src[:, idx * LANES:(idx + 1) * LANES]
            vwin[idx, r0:r1, :] = src[:, (N_PLACEMENTS + idx) * LANES:(N_PLACEMENTS + idx + 1) * LANES]
    _stack_queries(q_ref, q2s, n_blocks)

    row = lax.broadcasted_iota(jnp.int32, (SOFTMAX_ROWS, LANES), 0)
    c = lax.broadcasted_iota(jnp.int32, (SOFTMAX_ROWS, LANES), 1)
    off_prev = jnp.where(i > 0, 0, BLOCK)
    off_next = jnp.where(i < n_steps - 1, 0, BLOCK)

    def window_of(g):
        return _key_rows((g // N_PLACEMENTS) * BLOCK, win)

    def qk(g, u, valid):
        q2 = q2s[g // 2]
        s_buf[u, :, 0:win] = _dot_nt(q2, kwin[g % N_PLACEMENTS, window_of(g), :])
        s_buf[u, :, win:win + past] = _dot_nt(q2, kctx[g % N_PLACEMENTS])

    def softmax(g, u, valid):
        blk = g // N_PLACEMENTS
        off0 = jnp.where(blk == 0, off_prev, 0)
        off2 = jnp.where(blk == n_blocks - 1, off_next, 0)

        def masks_of(r0):
            r = row + r0 % BLOCK
            return {0: c >= r + off0, 2: c <= r - off2}

        _softmax_rows(s_buf, p_buf, t_buf, u, 2 * BLOCK, win + past, masks_of, _group_sinks(sink_ref, g))

    def pv(g, u, valid):
        acc = _dot(p_buf[u, :, 0:win], vwin[g % N_PLACEMENTS, window_of(g), :])
        acc = acc + _dot(p_buf[u, :, win:win + past], vctx[g % N_PLACEMENTS])
        o_buf[g] = _normalize(acc, t_buf, u)

    _software_pipeline(n_blocks * N_PLACEMENTS, PIPE_SLOTS, (qk, softmax, pv))
    _write_heads(o_buf, o_ref, n_blocks)


def _latent_attention(q, kvp, ck, cv, sink, n_batch, seq):
    ns = seq // ATTN_TILE
    nb = seq // BLOCK
    per = ATTN_TILE // BLOCK
    own = lambda b, i: (b * ns + i, 0)
    prev = lambda b, i: (b * nb + jnp.maximum(per * i - 1, 0), 0)
    nxt = lambda b, i: (b * nb + jnp.minimum(per * i + per, nb - 1), 0)
    ctx = lambda b, i: (b, 0, 0)
    past = ck.shape[2]
    win_rows = ATTN_TILE + 2 * BLOCK
    bf = lambda *shape: pltpu.VMEM(shape, BF16)
    return pl.pallas_call(
        functools.partial(_lat_attn_kernel, n_steps=ns, past=past),
        grid=(n_batch, ns),
        in_specs=[
            pl.BlockSpec(memory_space=pltpu.SMEM),
            pl.BlockSpec((ATTN_TILE, ATTN_WIDTH), own),
            pl.BlockSpec((BLOCK, KVP_WIDTH), prev),
            pl.BlockSpec((ATTN_TILE, KVP_WIDTH), own),
            pl.BlockSpec((BLOCK, KVP_WIDTH), nxt),
            pl.BlockSpec((1, KV_WIDTH, past), ctx),
            pl.BlockSpec((1, KV_WIDTH, past), ctx),
        ],
        out_specs=pl.BlockSpec((ATTN_TILE, ATTN_WIDTH), own),
        out_shape=jax.ShapeDtypeStruct((n_batch * seq, ATTN_WIDTH), BF16),
        scratch_shapes=[bf(N_PLACEMENTS, win_rows, LANES), bf(N_PLACEMENTS, win_rows, LANES),
                        bf(N_PLACEMENTS, past, LANES), bf(N_PLACEMENTS, past, LANES)]
        + _attn_scratch(per * N_PLACEMENTS, 3 * BLOCK + past),
        compiler_params=_params("arbitrary", "arbitrary"),
        name="latent_attention",
    )(sink, q, kvp, kvp, kvp, ck, cv)


_PK = 2 * GLA_DK
_PV = 2 * GLA_DV


def _cumsum_rows(x, reverse):
    n = x.shape[0]
    row = lax.broadcasted_iota(jnp.int32, x.shape, 0)
    s = 1
    while s < n:
        if reverse:
            x = x + jnp.where(row < n - s, pltpu.roll(x, n - s, 0), 0.0)
        else:
            x = x + jnp.where(row >= s, pltpu.roll(x, s, 0), 0.0)
        s *= 2
    return x


def _gla_kernel(*refs, seq_chunks, has_init, emit_state):
    it = iter(refs)
    views = [tuple(next(it) for _ in range(3)) for _ in range(2)]
    s0_ref = next(it) if has_init else None
    out_refs = (next(it), next(it))
    sout_ref = next(it) if emit_state else None
    st_ref, a_buf, a2_buf, qe_buf, qd_buf, ds_buf, sb_buf, dc_buf = (next(it) for _ in range(8))
    n_chunks = views[0][0].shape[0] // CHUNK
    n_groups = 2 * n_chunks
    n_pairs = N_GLA_HEADS // 2
    tile = pl.program_id(1)
    q_scale = GLA_DK ** -0.5
    _init_pipeline_scratch(jnp.logical_and(pl.program_id(0) == 0, tile == 0),
                           (st_ref, a_buf, a2_buf, qe_buf, qd_buf, ds_buf, sb_buf, dc_buf))

    if has_init:
        @pl.when(tile == 0)
        def _():
            st_ref[...] = jnp.zeros_like(st_ref)
            for d in range(2):
                for h in range(N_GLA_HEADS):
                    p, e = divmod(h, 2)
                    st_ref[d, p, e * GLA_DK:(e + 1) * GLA_DK, e * GLA_DV:(e + 1) * GLA_DV] = s0_ref[0, d, h]

    lane_k = lax.broadcasted_iota(jnp.int32, (CHUNK, _PK), 1)
    row_k = lax.broadcasted_iota(jnp.int32, (CHUNK, _PK), 0)
    lo_k = lane_k < GLA_DK
    tri = (lane_k % CHUNK <= row_k, lane_k % CHUNK >= row_k)
    lo_v = lax.broadcasted_iota(jnp.int32, (CHUNK, _PV), 1) < GLA_DV
    diag = ((lax.broadcasted_iota(jnp.int32, (_PK, _PV), 0) < GLA_DK)
            == (lax.broadcasted_iota(jnp.int32, (_PK, _PV), 1) < GLA_DV))

    def place(g, d):
        c = g // 2
        if d == 0:
            return c, c
        seq = c // seq_chunks
        return c, seq * seq_chunks + (seq_chunks - 1 - (c - seq * seq_chunks))

    def rows_of(pos):
        return pl.ds(pl.multiple_of(pos * CHUNK, CHUNK), CHUNK)

    def prep(g, u, valid):
        d = u % 2
        qk_ref, g_ref, v_ref = views[d]
        _, pos = place(g, d)
        rows = rows_of(pos)
        gc = _cumsum_rows(g_ref[rows, :], reverse=(d == 1))
        gtot = gc[0:1, :] if d == 1 else gc[CHUNK - 1:CHUNK, :]
        k = qk_ref[rows, GLA_K_WIDTH:2 * GLA_K_WIDTH]
        qe = qk_ref[rows, 0:GLA_K_WIDTH] * jnp.exp(gc)
        qe_buf[u] = qe
        q_dec = (qe * q_scale).astype(BF16)
        k_inv = k * jnp.exp(-gc)
        k_end = (k * jnp.exp(gtot - gc)).astype(BF16)
        decay = jnp.exp(gtot)
        for p in range(n_pairs):
            lanes = slice(p * _PK, (p + 1) * _PK)
            ki = k_inv[:, lanes]
            kst = jnp.concatenate([jnp.where(lo_k, ki, 0.0), jnp.where(lo_k, 0.0, ki)], axis=0).astype(BF16)
            a_buf[u, p] = _dot_nt(q_dec[:, lanes], kst)
            ds_buf[u, p] = jnp.where(diag, _dot_tn(k_end[:, lanes], v_ref[rows, p * _PV:(p + 1) * _PV]), 0.0)
            dc_buf[u, p] = jnp.broadcast_to(decay[:, lanes], (_PK, _PK)).T

    def state(g, u, valid):
        d = u % 2
        c, _ = place(g, d)
        qd_buf[u] = (qe_buf[u] * q_scale).astype(BF16)
        for p in range(n_pairs):
            s_old = st_ref[d, p]
            fresh = (c % seq_chunks == 0) if emit_state else (c < 0)
            s_in = jnp.where(fresh, 0.0, s_old)
            sb_buf[u, p] = s_in.astype(BF16)
            a2_buf[u, p] = jnp.where(tri[d], a_buf[u, p], 0.0).astype(BF16)
            dc = dc_buf[u, p]
            s_new = jnp.where(valid, s_in * jnp.concatenate([dc, dc], axis=1) + ds_buf[u, p], s_old)
            st_ref[d, p] = s_new
            if emit_state:
                for e in range(2):
                    sout_ref[c // seq_chunks, d, 2 * p + e] = s_new[e * GLA_DK:(e + 1) * GLA_DK,
                                                                    e * GLA_DV:(e + 1) * GLA_DV]

    def out(g, u, valid):
        d = u % 2
        _, _, v_ref = views[d]
        _, pos = place(g, d)
        rows = rows_of(pos)
        for p in range(n_pairs):
            vpf = v_ref[rows, p * _PV:(p + 1) * _PV].astype(F32)
            vbd = jnp.concatenate([jnp.where(lo_v, vpf, 0.0), jnp.where(lo_v, 0.0, vpf)], axis=0).astype(BF16)
            x = jnp.concatenate([a2_buf[u, p], qd_buf[u, :, p * _PK:(p + 1) * _PK]], axis=1)
            w = jnp.concatenate([vbd, sb_buf[u, p]], axis=0)
            out_refs[d][rows, p * _PV:(p + 1) * _PV] = _dot(x, w).astype(out_refs[d].dtype)

    _software_pipeline(n_groups, GLA_PIPE_SLOTS, (prep, state, out))


def _gla(qkg, gf, gb, vg, s0, n_batch, seq):
    tm = GLA_TILE
    long_seq = s0 is not None
    if long_seq:
        nt = seq // tm
        grid = (n_batch, nt)
        fwd = lambda b, i: (b * nt + i, 0)
        bwd = lambda b, i: (b * nt + nt - 1 - i, 0)
        seq_chunks = tm // CHUNK
    else:
        per_step = tm // seq
        grid = (1, n_batch // per_step)
        fwd = bwd = lambda b, i: (i, 0)
        seq_chunks = seq // CHUNK
    in_specs = []
    args = []
    for m, gates in ((fwd, gf), (bwd, gb)):
        in_specs += [pl.BlockSpec((tm, 2 * GLA_K_WIDTH), m), pl.BlockSpec((tm, GLA_K_WIDTH), m),
                     pl.BlockSpec((tm, GLA_V_WIDTH), m)]
        args += [qkg, gates, vg]
    n_tok = n_batch * seq
    out_specs = [pl.BlockSpec((tm, GLA_V_WIDTH), fwd), pl.BlockSpec((tm, GLA_V_WIDTH), bwd)]
    out_shape = [jax.ShapeDtypeStruct((n_tok, GLA_V_WIDTH), BF16)] * 2
    st_dims = (2, N_GLA_HEADS, GLA_DK, GLA_DV)
    if long_seq:
        in_specs.append(pl.BlockSpec((1,) + st_dims, lambda b, i: (b, 0, 0, 0, 0)))
        args.append(s0)
    else:
        out_specs.append(pl.BlockSpec((per_step,) + st_dims, lambda b, i: (i, 0, 0, 0, 0)))
        out_shape.append(jax.ShapeDtypeStruct((n_batch,) + st_dims, F32))
    n = GLA_PIPE_SLOTS
    n_pairs = N_GLA_HEADS // 2
    return pl.pallas_call(
        functools.partial(_gla_kernel, seq_chunks=seq_chunks, has_init=long_seq, emit_state=not long_seq),
        grid=grid,
        in_specs=in_specs,
        out_specs=out_specs,
        out_shape=out_shape,
        scratch_shapes=[pltpu.VMEM((2, n_pairs, _PK, _PV), F32),
                        pltpu.VMEM((n, n_pairs, CHUNK, _PK), F32),
                        pltpu.VMEM((n, n_pairs, CHUNK, _PK), BF16),
                        pltpu.VMEM((n, CHUNK, GLA_K_WIDTH), F32),
                        pltpu.VMEM((n, CHUNK, GLA_K_WIDTH), BF16),
                        pltpu.VMEM((n, n_pairs, _PK, _PV), F32),
                        pltpu.VMEM((n, n_pairs, _PK, _PV), BF16),
                        pltpu.VMEM((n, n_pairs, _PK, _PK), F32)],
        compiler_params=_params("arbitrary", "arbitrary"),
        name="gla_long" if long_seq else "gla_short",
    )(*args)


def _merge_kernel(x_ref, mod_ref, oa_ref, za_ref, of_ref, ob_ref, zg_ref, gng_ref, wo_ref, fng_ref, y_ref):
    ya = oa_ref[...].astype(F32) * _silu(za_ref[...].astype(F32))
    og = of_ref[...].astype(F32) + ob_ref[...].astype(F32)
    zg = zg_ref[...].astype(F32)
    gng = gng_ref[...]
    parts = [ya]
    for h in range(N_GLA_HEADS):
        oh = og[:, h * GLA_DV:(h + 1) * GLA_DV]
        nh = oh * lax.rsqrt(jnp.mean(oh * oh, axis=-1, keepdims=True) + EPS)
        parts.append((nh * gng) * _silu(zg[:, h * GLA_DV:(h + 1) * GLA_DV]))
    yin = jnp.concatenate(parts, axis=-1).astype(BF16)
    gate = mod_ref[0][:, 2 * D_MODEL:3 * D_MODEL]
    y = x_ref[...] + gate * _dot(yin, wo_ref[...])
    y_ref[...] = (y * lax.rsqrt(jnp.mean(y * y, axis=-1, keepdims=True) + EPS)) * fng_ref[...]


def _merge(x2d, mod3, mod_row_of_tile, oa, za, of, ob, zg, gla_norm_g, w_out, final_norm_g, tm):
    n_tok = x2d.shape[0]
    row = lambda i: (i, 0)
    const = lambda i: (0, 0)
    wide = pl.BlockSpec((tm, GLA_V_WIDTH), row)
    return pl.pallas_call(
        _merge_kernel,
        grid=(n_tok // tm,),
        in_specs=[
            pl.BlockSpec((tm, D_MODEL), row),
            pl.BlockSpec((1, 1, 3 * D_MODEL), lambda i: (mod_row_of_tile(i), 0, 0)),
            wide, wide, wide, wide, wide,
            pl.BlockSpec((1, GLA_DV), const),
            pl.BlockSpec((D_MODEL, D_MODEL), const),
            pl.BlockSpec((1, D_MODEL), const),
        ],
        out_specs=pl.BlockSpec((tm, D_MODEL), row),
        out_shape=jax.ShapeDtypeStruct((n_tok, D_MODEL), F32),
        compiler_params=_params("arbitrary"),
        name="merge",
    )(x2d, mod3, oa, za, of, ob, zg, gla_norm_g, w_out, final_norm_g)


def _rope_tables(n_tok):
    t = np.arange(n_tok)
    rowp = (t // GRID_W).astype(np.float64)
    colp = (t % GRID_W).astype(np.float64)
    nf = HEAD_DIM // 4
    freqs = ROPE_BASE ** (-np.arange(nf, dtype=np.float64) / nf)
    lane = np.arange(LANES)
    pos = np.where(((lane % HEAD_DIM) < HEAD_DIM // 2)[None, :], rowp[:, None], colp[:, None])
    ang = pos * freqs[lane % nf][None, :]
    sign = np.where((lane % (2 * nf)) < nf, -1.0, 1.0)
    return jnp.asarray(np.cos(ang), F32), jnp.asarray(np.sin(ang) * sign[None, :], F32)


def kernel(x_prompt, x_sample, c, cache_k, cache_v, state_gla, c_ctx, w_mod, b_mod, norm_g, w_in,
           w_gk_f, b_gk_f, w_gk_b, b_gk_b, sink, gla_norm_g, w_out, final_norm_g):
    depth = w_in.shape[0]
    assert depth == 1, "single-layer step"
    l = 0
    bp, sp, _ = x_prompt.shape
    bs, ss, _ = x_sample.shape

    wl = w_in[l]
    lr0 = _VG0 + GLA_V_WIDTH
    w1 = jnp.concatenate([wl[:, :lr0], wl[:, lr0 + 2 * GATE_RANK:]], axis=1).astype(BF16)
    wlr = jnp.pad(wl[:, lr0:lr0 + 2 * GATE_RANK], ((0, 0), (0, LANES - 2 * GATE_RANK))).astype(BF16)
    wgk = jnp.zeros((LANES, 2 * GLA_K_WIDTH), F32)
    wgk = wgk.at[0:GATE_RANK, 0:GLA_K_WIDTH].set(w_gk_f[l])
    wgk = wgk.at[GATE_RANK:2 * GATE_RANK, GLA_K_WIDTH:].set(w_gk_b[l]).astype(BF16)
    bgk = jnp.concatenate([b_gk_f[l], b_gk_b[l]])[None, :]
    wo = w_out[l].astype(BF16)
    ng = norm_g[l][None, :]
    gng = gla_norm_g[l][None, :]
    fng = final_norm_g[None, :]
    sk = sink[l]

    n_rows = 8
    cvecs = jnp.concatenate([c_ctx[None, :], c, jnp.zeros((n_rows - 1 - bs, D_MODEL), F32)], axis=0)
    mod3 = _modulation(cvecs, w_mod[l], b_mod[l][None, :]).reshape(n_rows, 1, 3 * D_MODEL)
    tm = TOKEN_TILE

    xp2 = x_prompt.reshape(bp * sp, D_MODEL)
    ctx_row = lambda i: 0
    q, kvp, za, vg, zg, qkg, gf, gb, kt, vt = _project(xp2, mod3, ctx_row, ng, w1, wlr, wgk, bgk, None, tm, sp)
    oa = _context_attention(q, kvp, sk, sp)
    of, ob, st = _gla(qkg, gf, gb, vg, None, bp, sp)
    y_prompt = _merge(xp2, mod3, ctx_row, oa, za, of, ob, zg, gng, wo, fng, tm).reshape(bp, sp, D_MODEL)
    to_cache = lambda t: t.reshape(bp, 1, N_KV_HEADS, HEAD_DIM, sp).transpose(0, 1, 4, 2, 3)
    new_k, new_v = to_cache(kt), to_cache(vt)
    new_state = st[:, None]

    xs2 = x_sample.reshape(bs * ss, D_MODEL)
    tiles_per_seq = ss // tm
    lat_row = lambda i: 1 + i // tiles_per_seq
    q, kvp, za, vg, zg, qkg, gf, gb = _project(xs2, mod3, lat_row, ng, w1, wlr, wgk, bgk, _rope_tables(ss), tm, ss)
    past = cache_k.shape[2]
    from_cache = lambda t: t[:, l].transpose(0, 2, 3, 1).reshape(bs, KV_WIDTH, past)
    ck, cv = from_cache(cache_k), from_cache(cache_v)
    oa = _latent_attention(q, kvp, ck, cv, sk, bs, ss)
    of, ob = _gla(qkg, gf, gb, vg, state_gla[:, l], bs, ss)
    y_sample = _merge(xs2, mod3, lat_row, oa, za, of, ob, zg, gng, wo, fng, tm).reshape(bs, ss, D_MODEL)

    return (y_prompt, y_sample, new_k, new_v, new_state)
```

```python
import functools
import math

import numpy as np
import jax
import jax.numpy as jnp
from jax import lax
from jax.experimental import pallas as pl
from jax.experimental.pallas import tpu as pltpu

D_MODEL = 1024
GRID_W = 64
HEAD_DIM = 64
N_ATTN_HEADS = 8
N_KV_HEADS = 2
ATTN_WIDTH = 512
KV_WIDTH = 128
BLOCK = 128
N_GLA_HEADS = 4
GLA_DK = 64
GLA_DV = 128
GLA_K_WIDTH = 256
GLA_V_WIDTH = 512
GATE_RANK = 16
GATE_NORMALIZER = 16.0
CHUNK = 64
ROPE_BASE = 10000.0
EPS = 1e-6
NEG_INF = -1e30

LANES = 128
TOKEN_TILE = 512
MERGE_TILE = 1024
GLA_TILE = 2048
ATTN_TILE = 2048
CTX_ATTN_BATCH = 8
PIPE_SLOTS = 4
GLA_PIPE_SLOTS = 4
KVP_WIDTH = 8 * LANES
N_PLACEMENTS = 2 * N_KV_HEADS
SOFTMAX_ROWS = 32
MOD_TILE = 512
VMEM_LIMIT = 56 * 1024 * 1024
LOG2E = math.log2(math.e)

F32 = jnp.float32
BF16 = jnp.bfloat16


def _dot(a, b):
    return jnp.dot(a, b, preferred_element_type=F32)


def _dot_nt(a, b):
    return lax.dot_general(a, b, (((1,), (1,)), ((), ())), preferred_element_type=F32)


def _dot_tn(a, b):
    return lax.dot_general(a, b, (((0,), (0,)), ((), ())), preferred_element_type=F32)


def _silu(x):
    h = 0.5 * x
    return h + h * jnp.tanh(h)


def _log_sigmoid(x):
    return jnp.minimum(x, 0.0) - jnp.log1p(jnp.exp(-jnp.abs(x)))


def _params(*sem):
    return pltpu.CompilerParams(dimension_semantics=sem, vmem_limit_bytes=VMEM_LIMIT)


def _split_bf16(x):
    hi = x.astype(BF16)
    return hi, (x - hi.astype(F32)).astype(BF16)


def _mod_kernel(c_ref, w_ref, b_ref, o_ref):
    rows = c_ref.shape[0]
    s_hi, s_lo = _split_bf16(_silu(c_ref[...]))
    w_hi, w_lo = _split_bf16(w_ref[...])
    by_hi = _dot(jnp.concatenate([s_hi, s_lo], axis=0), w_hi)
    o_ref[...] = by_hi[0:rows] + by_hi[rows:2 * rows] + _dot(s_hi, w_lo) + b_ref[...]


def _modulation(cvecs, w_mod, b_mod):
    rows = cvecs.shape[0]
    n_out = w_mod.shape[1]
    return pl.pallas_call(
        _mod_kernel,
        grid=(n_out // MOD_TILE,),
        in_specs=[
            pl.BlockSpec((rows, D_MODEL), lambda j: (0, 0)),
            pl.BlockSpec((D_MODEL, MOD_TILE), lambda j: (0, j)),
            pl.BlockSpec((1, MOD_TILE), lambda j: (0, j)),
        ],
        out_specs=pl.BlockSpec((rows, MOD_TILE), lambda j: (0, j)),
        out_shape=jax.ShapeDtypeStruct((rows, n_out), F32),
        compiler_params=_params("arbitrary"),
        name="adaln_mod",
    )(cvecs, w_mod, b_mod)


_Q0, _K0, _V0, _ZA0, _QG0, _KG0, _VG0, _ZG0, _W1_COLS = 0, 512, 640, 768, 1280, 1536, 1792, 2304, 2816
_Q_PRESCALE = (HEAD_DIM ** -0.5) * LOG2E


def _rope(x, cos, sin_signed):
    n = x.shape[1] // LANES
    lane = lax.broadcasted_iota(jnp.int32, (x.shape[0], LANES), 1)
    first = (lane % 32) < 16
    outs = []
    for c in range(n):
        xc = x[:, c * LANES:(c + 1) * LANES]
        partner = jnp.where(first, pltpu.roll(xc, LANES - 16, 1), pltpu.roll(xc, 16, 1))
        outs.append(xc * cos + partner * sin_signed)
    return outs[0] if n == 1 else jnp.concatenate(outs, axis=-1)


def _kv_placements(k, v):
    lane = lax.broadcasted_iota(jnp.int32, k.shape, 1)
    lo = lane < HEAD_DIM
    kr = pltpu.roll(k, HEAD_DIM, 1)
    vr = pltpu.roll(v, HEAD_DIM, 1)
    pieces = [jnp.where(lo, k, 0.0), jnp.where(lo, 0.0, kr), jnp.where(lo, kr, 0.0), jnp.where(lo, 0.0, k),
              jnp.where(lo, v, 1.0), jnp.where(lo, 1.0, vr), jnp.where(lo, vr, 1.0), jnp.where(lo, 1.0, v)]
    return jnp.concatenate(pieces, axis=-1).astype(BF16)


def _proj_kernel(*refs, rope):
    if rope:
        (x_ref, mod_ref, ng_ref, w1_ref, wlr_ref, wgk_ref, bgk_ref, cos_ref, sin_ref,
         q_ref, kvp_ref, za_ref, vg_ref, zg_ref, qkg_ref, gf_ref, gb_ref) = refs
    else:
        (x_ref, mod_ref, ng_ref, w1_ref, wlr_ref, wgk_ref, bgk_ref,
         q_ref, kvp_ref, za_ref, vg_ref, zg_ref, qkg_ref, gf_ref, gb_ref, k_ref, v_ref) = refs
    x = x_ref[...]
    xn = x * lax.rsqrt(jnp.mean(x * x, axis=-1, keepdims=True) + EPS)
    mod = mod_ref[0]
    shift = mod[:, 0:D_MODEL]
    scale = mod[:, D_MODEL:2 * D_MODEL]
    h = ((xn * ng_ref[...]) * (1.0 + scale) + shift).astype(BF16)

    def piece(c0, c1):
        return _dot(h, w1_ref[:, c0:c1])

    q = piece(_Q0, _K0)
    kv = piece(_K0, _ZA0)
    k = kv[:, 0:KV_WIDTH]
    v = kv[:, KV_WIDTH:2 * KV_WIDTH]
    if rope:
        cos = cos_ref[...]
        sin = sin_ref[...]
        q = _rope(q, cos, sin)
        k = _rope(k, cos, sin)
    else:
        seq = k_ref.shape[2]
        for b in range(k_ref.shape[0]):
            k_ref[b] = k[b * seq:(b + 1) * seq, :].T
            v_ref[b] = v[b * seq:(b + 1) * seq, :].T
    q_ref[...] = (q * _Q_PRESCALE).astype(BF16)
    kvp_ref[...] = _kv_placements(k, v)
    za_ref[...] = piece(_ZA0, _QG0).astype(BF16)
    vg_ref[...] = piece(_VG0, _ZG0).astype(BF16)
    zg_ref[...] = piece(_ZG0, _W1_COLS).astype(BF16)
    qkg_ref[...] = piece(_QG0, _VG0)
    lr = _dot(h, wlr_ref[...]).astype(BF16)
    g = _log_sigmoid(_dot(lr, wgk_ref[...]) + bgk_ref[...]) * (1.0 / GATE_NORMALIZER)
    gf_ref[...] = g[:, 0:GLA_K_WIDTH]
    gb_ref[...] = g[:, GLA_K_WIDTH:2 * GLA_K_WIDTH]


def _project(x2d, mod3, mod_row_of_tile, norm_g, w1, wlr, wgk, bgk, rope_tabs, tm, seq):
    n_tok = x2d.shape[0]
    rope = rope_tabs is not None
    row = lambda i: (i, 0)
    const = lambda i: (0, 0)
    in_specs = [
        pl.BlockSpec((tm, D_MODEL), row),
        pl.BlockSpec((1, 1, 3 * D_MODEL), lambda i: (mod_row_of_tile(i), 0, 0)),
        pl.BlockSpec((1, D_MODEL), const),
        pl.BlockSpec((D_MODEL, _W1_COLS), const),
        pl.BlockSpec((D_MODEL, LANES), const),
        pl.BlockSpec((LANES, 2 * GLA_K_WIDTH), const),
        pl.BlockSpec((1, 2 * GLA_K_WIDTH), const),
    ]
    args = [x2d, mod3, norm_g, w1, wlr, wgk, bgk]
    if rope:
        tiles_per_seq = rope_tabs[0].shape[0] // tm
        pos = lambda i: (i % tiles_per_seq, 0)
        in_specs += [pl.BlockSpec((tm, LANES), pos), pl.BlockSpec((tm, LANES), pos)]
        args += list(rope_tabs)
    widths = [(ATTN_WIDTH, BF16), (KVP_WIDTH, BF16), (ATTN_WIDTH, BF16), (GLA_V_WIDTH, BF16), (GLA_V_WIDTH, BF16),
              (2 * GLA_K_WIDTH, F32), (GLA_K_WIDTH, F32), (GLA_K_WIDTH, F32)]
    out_specs = [pl.BlockSpec((tm, w), row) for w, _ in widths]
    out_shape = [jax.ShapeDtypeStruct((n_tok, w), dt) for w, dt in widths]
    if not rope:
        for _ in range(2):
            out_specs.append(pl.BlockSpec((tm // seq, KV_WIDTH, seq), lambda i: (i, 0, 0)))
            out_shape.append(jax.ShapeDtypeStruct((n_tok // seq, KV_WIDTH, seq), F32))
    return pl.pallas_call(
        functools.partial(_proj_kernel, rope=rope),
        grid=(n_tok // tm,),
        in_specs=in_specs,
        out_specs=out_specs,
        out_shape=out_shape,
        compiler_params=_params("arbitrary"),
        name="project_rope" if rope else "project",
    )(*args)


def _software_pipeline(n_groups, n_slots, stages):
    last_g = n_groups - 1

    def body(k, carry):
        for age in (2, 1, 0):
            for u in range(n_slots):
                g = n_slots * (k - age) + u
                stages[age](jnp.clip(g, 0, last_g), u, jnp.logical_and(g >= 0, g <= last_g))
        return carry

    lax.fori_loop(0, n_groups // n_slots + 2, body, 0)


def _init_pipeline_scratch(first, zero_refs, one_refs=()):
    @pl.when(first)
    def _():
        for ref in zero_refs:
            ref[...] = jnp.zeros_like(ref)
        for ref in one_refs:
            ref[...] = jnp.ones_like(ref)


def _key_rows(start, size):
    return pl.ds(pl.multiple_of(start, BLOCK), size)


def _softmax_rows(s_buf, p_buf, t_buf, slot, n_rows, n_keys, masks_of, sink_of):
    for r0 in range(0, n_rows, SOFTMAX_ROWS):
        rs = slice(r0, r0 + SOFTMAX_ROWS)
        masks = masks_of(r0)
        cols = []
        for c in range(n_keys // LANES):
            sc = s_buf[slot, rs, c * LANES:(c + 1) * LANES]
            cols.append(jnp.where(masks[c], sc, NEG_INF) if c in masks else sc)
        mx = cols[0]
        for sc in cols[1:]:
            mx = jnp.maximum(mx, sc)
        sink = sink_of(r0)
        m = jnp.maximum(jnp.max(mx, axis=-1, keepdims=True), sink)
        for c, sc in enumerate(cols):
            p_buf[slot, rs, c * LANES:(c + 1) * LANES] = jnp.exp2(sc - m).astype(BF16)
        t_buf[slot, rs, :] = jnp.broadcast_to(jnp.exp2(sink - m), (SOFTMAX_ROWS, LANES))


def _normalize(acc, t_buf, slot):
    return acc / (pltpu.roll(acc, HEAD_DIM, 1) + t_buf[slot])


def _attn_scratch(n_groups, n_keys):
    rows = 2 * BLOCK
    return [pltpu.VMEM((n_groups // 2, rows, LANES), BF16),
            pltpu.VMEM((PIPE_SLOTS, rows, n_keys), F32), pltpu.VMEM((PIPE_SLOTS, rows, n_keys), BF16),
            pltpu.VMEM((PIPE_SLOTS, rows, LANES), F32), pltpu.VMEM((n_groups, rows, LANES), F32)]


def _stack_queries(q_ref, q2s, n_blocks):
    for blk in range(n_blocks):
        rows = slice(blk * BLOCK, (blk + 1) * BLOCK)
        for j in range(N_KV_HEADS):
            idx = blk * N_KV_HEADS + j
            q2s[idx, 0:BLOCK, :] = q_ref[rows, (2 * j) * LANES:(2 * j + 1) * LANES]
            q2s[idx, BLOCK:2 * BLOCK, :] = q_ref[rows, (2 * j + 1) * LANES:(2 * j + 2) * LANES]


def _write_heads(o_buf, o_ref, n_blocks):
    lo = lax.broadcasted_iota(jnp.int32, (2 * BLOCK, LANES), 1) < HEAD_DIM
    for blk in range(n_blocks):
        rows = slice(blk * BLOCK, (blk + 1) * BLOCK)
        for j in range(N_KV_HEADS):
            g0 = blk * N_PLACEMENTS + 2 * j
            o = jnp.where(lo, o_buf[g0], o_buf[g0 + 1]).astype(o_ref.dtype)
            o_ref[rows, (2 * j) * LANES:(2 * j + 1) * LANES] = o[0:BLOCK]
            o_ref[rows, (2 * j + 1) * LANES:(2 * j + 2) * LANES] = o[BLOCK:2 * BLOCK]


def _group_sinks(sink_ref, g):
    j, e = (g // 2) % N_KV_HEADS, g % 2
    sink_a = sink_ref[4 * j + e] * LOG2E
    sink_b = sink_ref[4 * j + 2 + e] * LOG2E
    return lambda r0: sink_a if r0 < BLOCK else sink_b


def _ctx_attn_kernel(sink_ref, q_ref, kvp_ref, o_ref, kown, vown, q2s, s_buf, p_buf, t_buf, o_buf, *, seq):
    n_blocks = q_ref.shape[0] // BLOCK
    _init_pipeline_scratch(pl.program_id(0) == 0, (s_buf, p_buf), (t_buf,))
    for idx in range(N_PLACEMENTS):
        kown[idx] = kvp_ref[:, idx * LANES:(idx + 1) * LANES]
        vown[idx] = kvp_ref[:, (N_PLACEMENTS + idx) * LANES:(N_PLACEMENTS + idx + 1) * LANES]
    _stack_queries(q_ref, q2s, n_blocks)
    blocks_per_seq = seq // BLOCK

    def keys_of(g):
        return _key_rows(((g // N_PLACEMENTS) // blocks_per_seq) * seq, seq)

    def qk(g, u, valid):
        s_buf[u] = _dot_nt(q2s[g // 2], kown[g % N_PLACEMENTS, keys_of(g), :])

    def softmax(g, u, valid):
        _softmax_rows(s_buf, p_buf, t_buf, u, 2 * BLOCK, seq, lambda r0: {}, _group_sinks(sink_ref, g))

    def pv(g, u, valid):
        o_buf[g] = _normalize(_dot(p_buf[u], vown[g % N_PLACEMENTS, keys_of(g), :]), t_buf, u)

    _software_pipeline(n_blocks * N_PLACEMENTS, PIPE_SLOTS, (qk, softmax, pv))
    _write_heads(o_buf, o_ref, n_blocks)


def _context_attention(q, kvp, sink, seq):
    n_tok = q.shape[0]
    rows = CTX_ATTN_BATCH * seq
    row = lambda b: (b, 0)
    n_groups = (rows // BLOCK) * N_PLACEMENTS
    return pl.pallas_call(
        functools.partial(_ctx_attn_kernel, seq=seq),
        grid=(n_tok // rows,),
        in_specs=[
            pl.BlockSpec(memory_space=pltpu.SMEM),
            pl.BlockSpec((rows, ATTN_WIDTH), row),
            pl.BlockSpec((rows, KVP_WIDTH), row),
        ],
        out_specs=pl.BlockSpec((rows, ATTN_WIDTH), row),
        out_shape=jax.ShapeDtypeStruct((n_tok, ATTN_WIDTH), BF16),
        scratch_shapes=[pltpu.VMEM((N_PLACEMENTS, rows, LANES), BF16)] * 2 + _attn_scratch(n_groups, seq),
        compiler_params=_params("arbitrary"),
        name="context_attention",
    )(sink, q, kvp)


def _lat_attn_kernel(sink_ref, q_ref, kvp_prev, kvp_own, kvp_next, ck_ref, cv_ref, o_ref,
                     kwin, vwin, kctx, vctx, q2s, s_buf, p_buf, t_buf, o_buf, *, n_steps, past):
    i = pl.program_id(1)
    n_blocks = q_ref.shape[0] // BLOCK
    own_end = BLOCK + q_ref.shape[0]
    win = 3 * BLOCK
    _init_pipeline_scratch(jnp.logical_and(pl.program_id(0) == 0, i == 0), (s_buf, p_buf), (t_buf,))

    @pl.when(i == 0)
    def _():
        cpl = _kv_placements(ck_ref[0].T, cv_ref[0].T)
        for idx in range(N_PLACEMENTS):
            kctx[idx] = cpl[:, idx * LANES:(idx + 1) * LANES]
            vctx[idx] = cpl[:, (N_PLACEMENTS + idx) * LANES:(N_PLACEMENTS + idx + 1) * LANES]

    for src, r0, r1 in ((kvp_prev, 0, BLOCK), (kvp_own, BLOCK, own_end), (kvp_next, own_end, own_end + BLOCK)):
        for idx in range(N_PLACEMENTS):
            kwin[idx, r0:r1, :] = src[:, idx * LANES:(idx + 1) * LANES]
            vwin[idx, r0:r1, :] = src[:, (N_PLACEMENTS + idx) * LANES:(N_PLACEMENTS + idx + 1) * LANES]
    _stack_queries(q_ref, q2s, n_blocks)

    row = lax.broadcasted_iota(jnp.int32, (SOFTMAX_ROWS, LANES), 0)
    c = lax.broadcasted_iota(jnp.int32, (SOFTMAX_ROWS, LANES), 1)
    off_prev = jnp.where(i > 0, 0, BLOCK)
    off_next = jnp.where(i < n_steps - 1, 0, BLOCK)

    def window_of(g):
        return _key_rows((g // N_PLACEMENTS) * BLOCK, win)

    def qk(g, u, valid):
        q2 = q2s[g // 2]
        s_buf[u, :, 0:win] = _dot_nt(q2, kwin[g % N_PLACEMENTS, window_of(g), :])
        s_buf[u, :, win:win + past] = _dot_nt(q2, kctx[g % N_PLACEMENTS])

    def softmax(g, u, valid):
        blk = g // N_PLACEMENTS
        off0 = jnp.where(blk == 0, off_prev, 0)
        off2 = jnp.where(blk == n_blocks - 1, off_next, 0)

        def masks_of(r0):
            r = row + r0 % BLOCK
            return {0: c >= r + off0, 2: c <= r - off2}

        _softmax_rows(s_buf, p_buf, t_buf, u, 2 * BLOCK, win + past, masks_of, _group_sinks(sink_ref, g))

    def pv(g, u, valid):
        acc = _dot(p_buf[u, :, 0:win], vwin[g % N_PLACEMENTS, window_of(g), :])
        acc = acc + _dot(p_buf[u, :, win:win + past], vctx[g % N_PLACEMENTS])
        o_buf[g] = _normalize(acc, t_buf, u)

    _software_pipeline(n_blocks * N_PLACEMENTS, PIPE_SLOTS, (qk, softmax, pv))
    _write_heads(o_buf, o_ref, n_blocks)


def _latent_attention(q, kvp, ck, cv, sink, n_batch, seq):
    ns = seq // ATTN_TILE
    nb = seq // BLOCK
    per = ATTN_TILE // BLOCK
    own = lambda b, i: (b * ns + i, 0)
    prev = lambda b, i: (b * nb + jnp.maximum(per * i - 1, 0), 0)
    nxt = lambda b, i: (b * nb + jnp.minimum(per * i + per, nb - 1), 0)
    ctx = lambda b, i: (b, 0, 0)
    past = ck.shape[2]
    win_rows = ATTN_TILE + 2 * BLOCK
    bf = lambda *shape: pltpu.VMEM(shape, BF16)
    return pl.pallas_call(
        functools.partial(_lat_attn_kernel, n_steps=ns, past=past),
        grid=(n_batch, ns),
        in_specs=[
            pl.BlockSpec(memory_space=pltpu.SMEM),
            pl.BlockSpec((ATTN_TILE, ATTN_WIDTH), own),
            pl.BlockSpec((BLOCK, KVP_WIDTH), prev),
            pl.BlockSpec((ATTN_TILE, KVP_WIDTH), own),
            pl.BlockSpec((BLOCK, KVP_WIDTH), nxt),
            pl.BlockSpec((1, KV_WIDTH, past), ctx),
            pl.BlockSpec((1, KV_WIDTH, past), ctx),
        ],
        out_specs=pl.BlockSpec((ATTN_TILE, ATTN_WIDTH), own),
        out_shape=jax.ShapeDtypeStruct((n_batch * seq, ATTN_WIDTH), BF16),
        scratch_shapes=[bf(N_PLACEMENTS, win_rows, LANES), bf(N_PLACEMENTS, win_rows, LANES),
                        bf(N_PLACEMENTS, past, LANES), bf(N_PLACEMENTS, past, LANES)]
        + _attn_scratch(per * N_PLACEMENTS, 3 * BLOCK + past),
        compiler_params=_params("arbitrary", "arbitrary"),
        name="latent_attention",
    )(sink, q, kvp, kvp, kvp, ck, cv)


_PK = 2 * GLA_DK
_PV = 2 * GLA_DV


def _cumsum_rows(x, reverse):
    n = x.shape[0]
    row = lax.broadcasted_iota(jnp.int32, x.shape, 0)
    s = 1
    while s < n:
        if reverse:
            x = x + jnp.where(row < n - s, pltpu.roll(x, n - s, 0), 0.0)
        else:
            x = x + jnp.where(row >= s, pltpu.roll(x, s, 0), 0.0)
        s *= 2
    return x


def _gla_kernel(*refs, seq_chunks, has_init, emit_state):
    it = iter(refs)
    views = [tuple(next(it) for _ in range(3)) for _ in range(2)]
    s0_ref = next(it) if has_init else None
    out_refs = (next(it), next(it))
    sout_ref = next(it) if emit_state else None
    st_ref, a_buf, a2_buf, qe_buf, qd_buf, ds_buf, sb_buf, dc_buf = (next(it) for _ in range(8))
    n_chunks = views[0][0].shape[0] // CHUNK
    n_groups = 2 * n_chunks
    n_pairs = N_GLA_HEADS // 2
    tile = pl.program_id(1)
    q_scale = GLA_DK ** -0.5
    _init_pipeline_scratch(jnp.logical_and(pl.program_id(0) == 0, tile == 0),
                           (st_ref, a_buf, a2_buf, qe_buf, qd_buf, ds_buf, sb_buf, dc_buf))

    if has_init:
        @pl.when(tile == 0)
        def _():
            st_ref[...] = jnp.zeros_like(st_ref)
            for d in range(2):
                for h in range(N_GLA_HEADS):
                    p, e = divmod(h, 2)
                    st_ref[d, p, e * GLA_DK:(e + 1) * GLA_DK, e * GLA_DV:(e + 1) * GLA_DV] = s0_ref[0, d, h]

    lane_k = lax.broadcasted_iota(jnp.int32, (CHUNK, _PK), 1)
    row_k = lax.broadcasted_iota(jnp.int32, (CHUNK, _PK), 0)
    lo_k = lane_k < GLA_DK
    tri = (lane_k % CHUNK <= row_k, lane_k % CHUNK >= row_k)
    lo_v = lax.broadcasted_iota(jnp.int32, (CHUNK, _PV), 1) < GLA_DV
    diag = ((lax.broadcasted_iota(jnp.int32, (_PK, _PV), 0) < GLA_DK)
            == (lax.broadcasted_iota(jnp.int32, (_PK, _PV), 1) < GLA_DV))

    def place(g, d):
        c = g // 2
        if d == 0:
            return c, c
        seq = c // seq_chunks
        return c, seq * seq_chunks + (seq_chunks - 1 - (c - seq * seq_chunks))

    def rows_of(pos):
        return pl.ds(pl.multiple_of(pos * CHUNK, CHUNK), CHUNK)

    def prep(g, u, valid):
        d = u % 2
        qk_ref, g_ref, v_ref = views[d]
        _, pos = place(g, d)
        rows = rows_of(pos)
        gc = _cumsum_rows(g_ref[rows, :], reverse=(d == 1))
        gtot = gc[0:1, :] if d == 1 else gc[CHUNK - 1:CHUNK, :]
        k = qk_ref[rows, GLA_K_WIDTH:2 * GLA_K_WIDTH]
        qe = qk_ref[rows, 0:GLA_K_WIDTH] * jnp.exp(gc)
        qe_buf[u] = qe
        q_dec = (qe * q_scale).astype(BF16)
        k_inv = k * jnp.exp(-gc)
        k_end = (k * jnp.exp(gtot - gc)).astype(BF16)
        decay = jnp.exp(gtot)
        for p in range(n_pairs):
            lanes = slice(p * _PK, (p + 1) * _PK)
            ki = k_inv[:, lanes]
            kst = jnp.concatenate([jnp.where(lo_k, ki, 0.0), jnp.where(lo_k, 0.0, ki)], axis=0).astype(BF16)
            a_buf[u, p] = _dot_nt(q_dec[:, lanes], kst)
            ds_buf[u, p] = jnp.where(diag, _dot_tn(k_end[:, lanes], v_ref[rows, p * _PV:(p + 1) * _PV]), 0.0)
            dc_buf[u, p] = jnp.broadcast_to(decay[:, lanes], (_PK, _PK)).T

    def state(g, u, valid):
        d = u % 2
        c, _ = place(g, d)
        qd_buf[u] = (qe_buf[u] * q_scale).astype(BF16)
        for p in range(n_pairs):
            s_old = st_ref[d, p]
            fresh = (c % seq_chunks == 0) if emit_state else (c < 0)
            s_in = jnp.where(fresh, 0.0, s_old)
            sb_buf[u, p] = s_in.astype(BF16)
            a2_buf[u, p] = jnp.where(tri[d], a_buf[u, p], 0.0).astype(BF16)
            dc = dc_buf[u, p]
            s_new = jnp.where(valid, s_in * jnp.concatenate([dc, dc], axis=1) + ds_buf[u, p], s_old)
            st_ref[d, p] = s_new
            if emit_state:
                for e in range(2):
                    sout_ref[c // seq_chunks, d, 2 * p + e] = s_new[e * GLA_DK:(e + 1) * GLA_DK,
                                                                    e * GLA_DV:(e + 1) * GLA_DV]

    def out(g, u, valid):
        d = u % 2
        _, _, v_ref = views[d]
        _, pos = place(g, d)
        rows = rows_of(pos)
        for p in range(n_pairs):
            vpf = v_ref[rows, p * _PV:(p + 1) * _PV].astype(F32)
            vbd = jnp.concatenate([jnp.where(lo_v, vpf, 0.0), jnp.where(lo_v, 0.0, vpf)], axis=0).astype(BF16)
            x = jnp.concatenate([a2_buf[u, p], qd_buf[u, :, p * _PK:(p + 1) * _PK]], axis=1)
            w = jnp.concatenate([vbd, sb_buf[u, p]], axis=0)
            out_refs[d][rows, p * _PV:(p + 1) * _PV] = _dot(x, w).astype(out_refs[d].dtype)

    _software_pipeline(n_groups, GLA_PIPE_SLOTS, (prep, state, out))


def _gla(qkg, gf, gb, vg, s0, n_batch, seq):
    tm = GLA_TILE
    long_seq = s0 is not None
    if long_seq:
        nt = seq // tm
        grid = (n_batch, nt)
        fwd = lambda b, i: (b * nt + i, 0)
        bwd = lambda b, i: (b * nt + nt - 1 - i, 0)
        seq_chunks = tm // CHUNK
    else:
        per_step = tm // seq
        grid = (1, n_batch // per_step)
        fwd = bwd = lambda b, i: (i, 0)
        seq_chunks = seq // CHUNK
    in_specs = []
    args = []
    for m, gates in ((fwd, gf), (bwd, gb)):
        in_specs += [pl.BlockSpec((tm, 2 * GLA_K_WIDTH), m), pl.BlockSpec((tm, GLA_K_WIDTH), m),
                     pl.BlockSpec((tm, GLA_V_WIDTH), m)]
        args += [qkg, gates, vg]
    n_tok = n_batch * seq
    out_specs = [pl.BlockSpec((tm, GLA_V_WIDTH), fwd), pl.BlockSpec((tm, GLA_V_WIDTH), bwd)]
    out_shape = [jax.ShapeDtypeStruct((n_tok, GLA_V_WIDTH), BF16)] * 2
    st_dims = (2, N_GLA_HEADS, GLA_DK, GLA_DV)
    if long_seq:
        in_specs.append(pl.BlockSpec((1,) + st_dims, lambda b, i: (b, 0, 0, 0, 0)))
        args.append(s0)
    else:
        out_specs.append(pl.BlockSpec((per_step,) + st_dims, lambda b, i: (i, 0, 0, 0, 0)))
        out_shape.append(jax.ShapeDtypeStruct((n_batch,) + st_dims, F32))
    n = GLA_PIPE_SLOTS
    n_pairs = N_GLA_HEADS // 2
    return pl.pallas_call(
        functools.partial(_gla_kernel, seq_chunks=seq_chunks, has_init=long_seq, emit_state=not long_seq),
        grid=grid,
        in_specs=in_specs,
        out_specs=out_specs,
        out_shape=out_shape,
        scratch_shapes=[pltpu.VMEM((2, n_pairs, _PK, _PV), F32),
                        pltpu.VMEM((n, n_pairs, CHUNK, _PK), F32),
                        pltpu.VMEM((n, n_pairs, CHUNK, _PK), BF16),
                        pltpu.VMEM((n, CHUNK, GLA_K_WIDTH), F32),
                        pltpu.VMEM((n, CHUNK, GLA_K_WIDTH), BF16),
                        pltpu.VMEM((n, n_pairs, _PK, _PV), F32),
                        pltpu.VMEM((n, n_pairs, _PK, _PV), BF16),
                        pltpu.VMEM((n, n_pairs, _PK, _PK), F32)],
        compiler_params=_params("arbitrary", "arbitrary"),
        name="gla_long" if long_seq else "gla_short",
    )(*args)


def _merge_kernel(x_ref, mod_ref, oa_ref, za_ref, of_ref, ob_ref, zg_ref, gng_ref, wo_ref, fng_ref, y_ref):
    ya = oa_ref[...].astype(F32) * _silu(za_ref[...].astype(F32))
    og = of_ref[...].astype(F32) + ob_ref[...].astype(F32)
    zg = zg_ref[...].astype(F32)
    gng = gng_ref[...]
    parts = [ya]
    for h in range(N_GLA_HEADS):
        oh = og[:, h * GLA_DV:(h + 1) * GLA_DV]
        nh = oh * lax.rsqrt(jnp.mean(oh * oh, axis=-1, keepdims=True) + EPS)
        parts.append((nh * gng) * _silu(zg[:, h * GLA_DV:(h + 1) * GLA_DV]))
    yin = jnp.concatenate(parts, axis=-1).astype(BF16)
    gate = mod_ref[0][:, 2 * D_MODEL:3 * D_MODEL]
    y = x_ref[...] + gate * _dot(yin, wo_ref[...])
    y_ref[...] = (y * lax.rsqrt(jnp.mean(y * y, axis=-1, keepdims=True) + EPS)) * fng_ref[...]


def _merge(x2d, mod3, mod_row_of_tile, oa, za, of, ob, zg, gla_norm_g, w_out, final_norm_g, tm):
    n_tok = x2d.shape[0]
    row = lambda i: (i, 0)
    const = lambda i: (0, 0)
    wide = pl.BlockSpec((tm, GLA_V_WIDTH), row)
    return pl.pallas_call(
        _merge_kernel,
        grid=(n_tok // tm,),
        in_specs=[
            pl.BlockSpec((tm, D_MODEL), row),
            pl.BlockSpec((1, 1, 3 * D_MODEL), lambda i: (mod_row_of_tile(i), 0, 0)),
            wide, wide, wide, wide, wide,
            pl.BlockSpec((1, GLA_DV), const),
            pl.BlockSpec((D_MODEL, D_MODEL), const),
            pl.BlockSpec((1, D_MODEL), const),
        ],
        out_specs=pl.BlockSpec((tm, D_MODEL), row),
        out_shape=jax.ShapeDtypeStruct((n_tok, D_MODEL), F32),
        compiler_params=_params("arbitrary"),
        name="merge",
    )(x2d, mod3, oa, za, of, ob, zg, gla_norm_g, w_out, final_norm_g)


def _rope_tables(n_tok):
    t = np.arange(n_tok)
    rowp = (t // GRID_W).astype(np.float64)
    colp = (t % GRID_W).astype(np.float64)
    nf = HEAD_DIM // 4
    freqs = ROPE_BASE ** (-np.arange(nf, dtype=np.float64) / nf)
    lane = np.arange(LANES)
    pos = np.where(((lane % HEAD_DIM) < HEAD_DIM // 2)[None, :], rowp[:, None], colp[:, None])
    ang = pos * freqs[lane % nf][None, :]
    sign = np.where((lane % (2 * nf)) < nf, -1.0, 1.0)
    return jnp.asarray(np.cos(ang), F32), jnp.asarray(np.sin(ang) * sign[None, :], F32)


def kernel(x_prompt, x_sample, c, cache_k, cache_v, state_gla, c_ctx, w_mod, b_mod, norm_g, w_in,
           w_gk_f, b_gk_f, w_gk_b, b_gk_b, sink, gla_norm_g, w_out, final_norm_g):
    depth = w_in.shape[0]
    assert depth == 1, "single-layer step"
    l = 0
    bp, sp, _ = x_prompt.shape
    bs, ss, _ = x_sample.shape

    wl = w_in[l]
    lr0 = _VG0 + GLA_V_WIDTH
    w1 = jnp.concatenate([wl[:, :lr0], wl[:, lr0 + 2 * GATE_RANK:]], axis=1).astype(BF16)
    wlr = jnp.pad(wl[:, lr0:lr0 + 2 * GATE_RANK], ((0, 0), (0, LANES - 2 * GATE_RANK))).astype(BF16)
    wgk = jnp.zeros((LANES, 2 * GLA_K_WIDTH), F32)
    wgk = wgk.at[0:GATE_RANK, 0:GLA_K_WIDTH].set(w_gk_f[l])
    wgk = wgk.at[GATE_RANK:2 * GATE_RANK, GLA_K_WIDTH:].set(w_gk_b[l]).astype(BF16)
    bgk = jnp.concatenate([b_gk_f[l], b_gk_b[l]])[None, :]
    wo = w_out[l].astype(BF16)
    ng = norm_g[l][None, :]
    gng = gla_norm_g[l][None, :]
    fng = final_norm_g[None, :]
    sk = sink[l]

    n_rows = 8
    cvecs = jnp.concatenate([c_ctx[None, :], c, jnp.zeros((n_rows - 1 - bs, D_MODEL), F32)], axis=0)
    mod3 = _modulation(cvecs, w_mod[l], b_mod[l][None, :]).reshape(n_rows, 1, 3 * D_MODEL)
    tm = TOKEN_TILE

    xp2 = x_prompt.reshape(bp * sp, D_MODEL)
    ctx_row = lambda i: 0
    q, kvp, za, vg, zg, qkg, gf, gb, kt, vt = _project(xp2, mod3, ctx_row, ng, w1, wlr, wgk, bgk, None, tm, sp)
    oa = _context_attention(q, kvp, sk, sp)
    of, ob, st = _gla(qkg, gf, gb, vg, None, bp, sp)
    y_prompt = _merge(xp2, mod3, ctx_row, oa, za, of, ob, zg, gng, wo, fng, MERGE_TILE).reshape(bp, sp, D_MODEL)
    to_cache = lambda t: t.reshape(bp, 1, N_KV_HEADS, HEAD_DIM, sp).transpose(0, 1, 4, 2, 3)
    new_k, new_v = to_cache(kt), to_cache(vt)
    new_state = st[:, None]

    xs2 = x_sample.reshape(bs * ss, D_MODEL)
    tiles_per_seq = ss // tm
    lat_row = lambda i: 1 + i // tiles_per_seq
    q, kvp, za, vg, zg, qkg, gf, gb = _project(xs2, mod3, lat_row, ng, w1, wlr, wgk, bgk, _rope_tables(ss), tm, ss)
    past = cache_k.shape[2]
    from_cache = lambda t: t[:, l].transpose(0, 2, 3, 1).reshape(bs, KV_WIDTH, past)
    ck, cv = from_cache(cache_k), from_cache(cache_v)
    oa = _latent_attention(q, kvp, ck, cv, sk, bs, ss)
    of, ob = _gla(qkg, gf, gb, vg, state_gla[:, l], bs, ss)
    merge_row = lambda i: 1 + i // (ss // MERGE_TILE)
    y_sample = _merge(xs2, mod3, merge_row, oa, za, of, ob, zg, gng, wo, fng, MERGE_TILE).reshape(bs, ss, D_MODEL)

    return (y_prompt, y_sample, new_k, new_v, new_state)
```

```python
import functools
import math

import numpy as np
import jax
import jax.numpy as jnp
from jax import lax
from jax.experimental import pallas as pl
from jax.experimental.pallas import tpu as pltpu

D_MODEL = 1024
GRID_W = 64
HEAD_DIM = 64
N_ATTN_HEADS = 8
N_KV_HEADS = 2
ATTN_WIDTH = 512
KV_WIDTH = 128
BLOCK = 128
N_GLA_HEADS = 4
GLA_DK = 64
GLA_DV = 128
GLA_K_WIDTH = 256
GLA_V_WIDTH = 512
GATE_RANK = 16
GATE_NORMALIZER = 16.0
CHUNK = 64
ROPE_BASE = 10000.0
EPS = 1e-6
NEG_INF = -1e30

LANES = 128
TOKEN_TILE = 512
MERGE_TILE = 1024
GLA_TILE = 2048
ATTN_TILE = 2048
CTX_ATTN_BATCH = 8
PIPE_SLOTS = 4
GLA_PIPE_SLOTS = 4
KVP_WIDTH = 8 * LANES
N_PLACEMENTS = 2 * N_KV_HEADS
SOFTMAX_ROWS = 32
MOD_TILE = 512
VMEM_LIMIT = 56 * 1024 * 1024
LOG2E = math.log2(math.e)

F32 = jnp.float32
BF16 = jnp.bfloat16


def _dot(a, b):
    return jnp.dot(a, b, preferred_element_type=F32)


def _dot_nt(a, b):
    return lax.dot_general(a, b, (((1,), (1,)), ((), ())), preferred_element_type=F32)


def _dot_tn(a, b):
    return lax.dot_general(a, b, (((0,), (0,)), ((), ())), preferred_element_type=F32)


def _silu(x):
    h = 0.5 * x
    return h + h * jnp.tanh(h)


def _log_sigmoid(x):
    return jnp.minimum(x, 0.0) - jnp.log1p(jnp.exp(-jnp.abs(x)))


def _params(*sem):
    return pltpu.CompilerParams(dimension_semantics=sem, vmem_limit_bytes=VMEM_LIMIT)


def _split_bf16(x):
    hi = x.astype(BF16)
    return hi, (x - hi.astype(F32)).astype(BF16)


def _mod_kernel(c_ref, w_ref, b_ref, o_ref):
    rows = c_ref.shape[0]
    s_hi, s_lo = _split_bf16(_silu(c_ref[...]))
    w_hi, w_lo = _split_bf16(w_ref[...])
    by_hi = _dot(jnp.concatenate([s_hi, s_lo], axis=0), w_hi)
    o_ref[...] = by_hi[0:rows] + by_hi[rows:2 * rows] + _dot(s_hi, w_lo) + b_ref[...]


def _modulation(cvecs, w_mod, b_mod):
    rows = cvecs.shape[0]
    n_out = w_mod.shape[1]
    return pl.pallas_call(
        _mod_kernel,
        grid=(n_out // MOD_TILE,),
        in_specs=[
            pl.BlockSpec((rows, D_MODEL), lambda j: (0, 0)),
            pl.BlockSpec((D_MODEL, MOD_TILE), lambda j: (0, j)),
            pl.BlockSpec((1, MOD_TILE), lambda j: (0, j)),
        ],
        out_specs=pl.BlockSpec((rows, MOD_TILE), lambda j: (0, j)),
        out_shape=jax.ShapeDtypeStruct((rows, n_out), F32),
        compiler_params=_params("arbitrary"),
        name="adaln_mod",
    )(cvecs, w_mod, b_mod)


_Q0, _K0, _V0, _ZA0, _QG0, _KG0, _VG0, _ZG0, _W1_COLS = 0, 512, 640, 768, 1280, 1536, 1792, 2304, 2816
_LR0 = _W1_COLS
_W_COLS = _W1_COLS + LANES
WEIGHT_ROWS = 128


def _reorder_w_kernel(w_ref, o_ref):
    lr_src = _ZG0
    zg_src = _ZG0 + 2 * GATE_RANK
    o_ref[:, 0:_ZG0] = w_ref[:, 0:_ZG0].astype(BF16)
    o_ref[:, _ZG0:_W1_COLS] = w_ref[:, zg_src:zg_src + GLA_V_WIDTH].astype(BF16)
    lr = w_ref[:, lr_src:lr_src + 2 * GATE_RANK].astype(BF16)
    o_ref[:, _LR0:_W_COLS] = jnp.concatenate(
        [lr, jnp.zeros((lr.shape[0], LANES - 2 * GATE_RANK), BF16)], axis=1)


def _reorder_w_in(w):
    rows, cols = w.shape
    return pl.pallas_call(
        _reorder_w_kernel,
        grid=(rows // WEIGHT_ROWS,),
        in_specs=[pl.BlockSpec((WEIGHT_ROWS, cols), lambda i: (i, 0))],
        out_specs=pl.BlockSpec((WEIGHT_ROWS, _W_COLS), lambda i: (i, 0)),
        out_shape=jax.ShapeDtypeStruct((rows, _W_COLS), BF16),
        compiler_params=_params("arbitrary"),
        name="reorder_w_in",
    )(w)
_Q_PRESCALE = (HEAD_DIM ** -0.5) * LOG2E


def _rope(x, cos, sin_signed):
    n = x.shape[1] // LANES
    lane = lax.broadcasted_iota(jnp.int32, (x.shape[0], LANES), 1)
    first = (lane % 32) < 16
    outs = []
    for c in range(n):
        xc = x[:, c * LANES:(c + 1) * LANES]
        partner = jnp.where(first, pltpu.roll(xc, LANES - 16, 1), pltpu.roll(xc, 16, 1))
        outs.append(xc * cos + partner * sin_signed)
    return outs[0] if n == 1 else jnp.concatenate(outs, axis=-1)


def _kv_placements(k, v):
    lane = lax.broadcasted_iota(jnp.int32, k.shape, 1)
    lo = lane < HEAD_DIM
    kr = pltpu.roll(k, HEAD_DIM, 1)
    vr = pltpu.roll(v, HEAD_DIM, 1)
    pieces = [jnp.where(lo, k, 0.0), jnp.where(lo, 0.0, kr), jnp.where(lo, kr, 0.0), jnp.where(lo, 0.0, k),
              jnp.where(lo, v, 1.0), jnp.where(lo, 1.0, vr), jnp.where(lo, vr, 1.0), jnp.where(lo, 1.0, v)]
    return jnp.concatenate(pieces, axis=-1).astype(BF16)


def _proj_kernel(*refs, rope):
    if rope:
        (x_ref, mod_ref, ng_ref, w1_ref, wgk_ref, bgk_ref, cos_ref, sin_ref,
         q_ref, kvp_ref, za_ref, vg_ref, zg_ref, qkg_ref, gf_ref, gb_ref) = refs
    else:
        (x_ref, mod_ref, ng_ref, w1_ref, wgk_ref, bgk_ref,
         q_ref, kvp_ref, za_ref, vg_ref, zg_ref, qkg_ref, gf_ref, gb_ref, k_ref, v_ref) = refs
    x = x_ref[...]
    xn = x * lax.rsqrt(jnp.mean(x * x, axis=-1, keepdims=True) + EPS)
    mod = mod_ref[0]
    shift = mod[:, 0:D_MODEL]
    scale = mod[:, D_MODEL:2 * D_MODEL]
    h = ((xn * ng_ref[...]) * (1.0 + scale) + shift).astype(BF16)

    def piece(c0, c1):
        return _dot(h, w1_ref[:, c0:c1])

    q = piece(_Q0, _K0)
    kv = piece(_K0, _ZA0)
    k = kv[:, 0:KV_WIDTH]
    v = kv[:, KV_WIDTH:2 * KV_WIDTH]
    if rope:
        cos = cos_ref[...]
        sin = sin_ref[...]
        q = _rope(q, cos, sin)
        k = _rope(k, cos, sin)
    else:
        seq = k_ref.shape[2]
        for b in range(k_ref.shape[0]):
            k_ref[b] = k[b * seq:(b + 1) * seq, :].T
            v_ref[b] = v[b * seq:(b + 1) * seq, :].T
    q_ref[...] = (q * _Q_PRESCALE).astype(BF16)
    kvp_ref[...] = _kv_placements(k, v)
    za_ref[...] = piece(_ZA0, _QG0).astype(BF16)
    vg_ref[...] = piece(_VG0, _ZG0).astype(BF16)
    zg_ref[...] = piece(_ZG0, _W1_COLS).astype(BF16)
    qkg_ref[...] = piece(_QG0, _VG0)
    lr = piece(_LR0, _W_COLS).astype(BF16)
    g = _log_sigmoid(_dot(lr, wgk_ref[...]) + bgk_ref[...]) * (1.0 / GATE_NORMALIZER)
    gf_ref[...] = g[:, 0:GLA_K_WIDTH]
    gb_ref[...] = g[:, GLA_K_WIDTH:2 * GLA_K_WIDTH]


def _project(x2d, mod3, mod_row_of_tile, norm_g, w1, wgk, bgk, rope_tabs, tm, seq):
    n_tok = x2d.shape[0]
    rope = rope_tabs is not None
    row = lambda i: (i, 0)
    const = lambda i: (0, 0)
    in_specs = [
        pl.BlockSpec((tm, D_MODEL), row),
        pl.BlockSpec((1, 1, 3 * D_MODEL), lambda i: (mod_row_of_tile(i), 0, 0)),
        pl.BlockSpec((1, D_MODEL), const),
        pl.BlockSpec((D_MODEL, _W_COLS), const),
        pl.BlockSpec((LANES, 2 * GLA_K_WIDTH), const),
        pl.BlockSpec((1, 2 * GLA_K_WIDTH), const),
    ]
    args = [x2d, mod3, norm_g, w1, wgk, bgk]
    if rope:
        tiles_per_seq = rope_tabs[0].shape[0] // tm
        pos = lambda i: (i % tiles_per_seq, 0)
        in_specs += [pl.BlockSpec((tm, LANES), pos), pl.BlockSpec((tm, LANES), pos)]
        args += list(rope_tabs)
    widths = [(ATTN_WIDTH, BF16), (KVP_WIDTH, BF16), (ATTN_WIDTH, BF16), (GLA_V_WIDTH, BF16), (GLA_V_WIDTH, BF16),
              (2 * GLA_K_WIDTH, F32), (GLA_K_WIDTH, F32), (GLA_K_WIDTH, F32)]
    out_specs = [pl.BlockSpec((tm, w), row) for w, _ in widths]
    out_shape = [jax.ShapeDtypeStruct((n_tok, w), dt) for w, dt in widths]
    if not rope:
        for _ in range(2):
            out_specs.append(pl.BlockSpec((tm // seq, KV_WIDTH, seq), lambda i: (i, 0, 0)))
            out_shape.append(jax.ShapeDtypeStruct((n_tok // seq, KV_WIDTH, seq), F32))
    return pl.pallas_call(
        functools.partial(_proj_kernel, rope=rope),
        grid=(n_tok // tm,),
        in_specs=in_specs,
        out_specs=out_specs,
        out_shape=out_shape,
        compiler_params=_params("arbitrary"),
        name="project_rope" if rope else "project",
    )(*args)


def _software_pipeline(n_groups, n_slots, stages):
    last_g = n_groups - 1

    def body(k, carry):
        for age in (2, 1, 0):
            for u in range(n_slots):
                g = n_slots * (k - age) + u
                stages[age](jnp.clip(g, 0, last_g), u, jnp.logical_and(g >= 0, g <= last_g))
        return carry

    lax.fori_loop(0, n_groups // n_slots + 2, body, 0)


def _init_pipeline_scratch(first, zero_refs, one_refs=()):
    @pl.when(first)
    def _():
        for ref in zero_refs:
            ref[...] = jnp.zeros_like(ref)
        for ref in one_refs:
            ref[...] = jnp.ones_like(ref)


def _key_rows(start, size):
    return pl.ds(pl.multiple_of(start, BLOCK), size)


def _softmax_rows(s_buf, p_buf, t_buf, slot, n_rows, n_keys, masks_of, sink_of):
    for r0 in range(0, n_rows, SOFTMAX_ROWS):
        rs = slice(r0, r0 + SOFTMAX_ROWS)
        masks = masks_of(r0)
        cols = []
        for c in range(n_keys // LANES):
            sc = s_buf[slot, rs, c * LANES:(c + 1) * LANES]
            cols.append(jnp.where(masks[c], sc, NEG_INF) if c in masks else sc)
        mx = cols[0]
        for sc in cols[1:]:
            mx = jnp.maximum(mx, sc)
        sink = sink_of(r0)
        m = jnp.maximum(jnp.max(mx, axis=-1, keepdims=True), sink)
        for c, sc in enumerate(cols):
            p_buf[slot, rs, c * LANES:(c + 1) * LANES] = jnp.exp2(sc - m).astype(BF16)
        t_buf[slot, rs, :] = jnp.broadcast_to(jnp.exp2(sink - m), (SOFTMAX_ROWS, LANES))


def _normalize(acc, t_buf, slot):
    return acc / (pltpu.roll(acc, HEAD_DIM, 1) + t_buf[slot])


def _attn_scratch(n_groups, n_keys):
    rows = 2 * BLOCK
    return [pltpu.VMEM((n_groups // 2, rows, LANES), BF16),
            pltpu.VMEM((PIPE_SLOTS, rows, n_keys), F32), pltpu.VMEM((PIPE_SLOTS, rows, n_keys), BF16),
            pltpu.VMEM((PIPE_SLOTS, rows, LANES), F32), pltpu.VMEM((n_groups, rows, LANES), F32)]


def _stack_queries(q_ref, q2s, n_blocks):
    for blk in range(n_blocks):
        rows = slice(blk * BLOCK, (blk + 1) * BLOCK)
        for j in range(N_KV_HEADS):
            idx = blk * N_KV_HEADS + j
            q2s[idx, 0:BLOCK, :] = q_ref[rows, (2 * j) * LANES:(2 * j + 1) * LANES]
            q2s[idx, BLOCK:2 * BLOCK, :] = q_ref[rows, (2 * j + 1) * LANES:(2 * j + 2) * LANES]


def _write_heads(o_buf, o_ref, n_blocks):
    lo = lax.broadcasted_iota(jnp.int32, (2 * BLOCK, LANES), 1) < HEAD_DIM
    for blk in range(n_blocks):
        rows = slice(blk * BLOCK, (blk + 1) * BLOCK)
        for j in range(N_KV_HEADS):
            g0 = blk * N_PLACEMENTS + 2 * j
            o = jnp.where(lo, o_buf[g0], o_buf[g0 + 1]).astype(o_ref.dtype)
            o_ref[rows, (2 * j) * LANES:(2 * j + 1) * LANES] = o[0:BLOCK]
            o_ref[rows, (2 * j + 1) * LANES:(2 * j + 2) * LANES] = o[BLOCK:2 * BLOCK]


def _group_sinks(sink_ref, g):
    j, e = (g // 2) % N_KV_HEADS, g % 2
    sink_a = sink_ref[4 * j + e] * LOG2E
    sink_b = sink_ref[4 * j + 2 + e] * LOG2E
    return lambda r0: sink_a if r0 < BLOCK else sink_b


def _ctx_attn_kernel(sink_ref, q_ref, kvp_ref, o_ref, kown, vown, q2s, s_buf, p_buf, t_buf, o_buf, *, seq):
    n_blocks = q_ref.shape[0] // BLOCK
    _init_pipeline_scratch(pl.program_id(0) == 0, (s_buf, p_buf), (t_buf,))
    for idx in range(N_PLACEMENTS):
        kown[idx] = kvp_ref[:, idx * LANES:(idx + 1) * LANES]
        vown[idx] = kvp_ref[:, (N_PLACEMENTS + idx) * LANES:(N_PLACEMENTS + idx + 1) * LANES]
    _stack_queries(q_ref, q2s, n_blocks)
    blocks_per_seq = seq // BLOCK

    def keys_of(g):
        return _key_rows(((g // N_PLACEMENTS) // blocks_per_seq) * seq, seq)

    def qk(g, u, valid):
        s_buf[u] = _dot_nt(q2s[g // 2], kown[g % N_PLACEMENTS, keys_of(g), :])

    def softmax(g, u, valid):
        _softmax_rows(s_buf, p_buf, t_buf, u, 2 * BLOCK, seq, lambda r0: {}, _group_sinks(sink_ref, g))

    def pv(g, u, valid):
        o_buf[g] = _normalize(_dot(p_buf[u], vown[g % N_PLACEMENTS, keys_of(g), :]), t_buf, u)

    _software_pipeline(n_blocks * N_PLACEMENTS, PIPE_SLOTS, (qk, softmax, pv))
    _write_heads(o_buf, o_ref, n_blocks)


def _context_attention(q, kvp, sink, seq):
    n_tok = q.shape[0]
    rows = CTX_ATTN_BATCH * seq
    row = lambda b: (b, 0)
    n_groups = (rows // BLOCK) * N_PLACEMENTS
    return pl.pallas_call(
        functools.partial(_ctx_attn_kernel, seq=seq),
        grid=(n_tok // rows,),
        in_specs=[
            pl.BlockSpec(memory_space=pltpu.SMEM),
            pl.BlockSpec((rows, ATTN_WIDTH), row),
            pl.BlockSpec((rows, KVP_WIDTH), row),
        ],
        out_specs=pl.BlockSpec((rows, ATTN_WIDTH), row),
        out_shape=jax.ShapeDtypeStruct((n_tok, ATTN_WIDTH), BF16),
        scratch_shapes=[pltpu.VMEM((N_PLACEMENTS, rows, LANES), BF16)] * 2 + _attn_scratch(n_groups, seq),
        compiler_params=_params("arbitrary"),
        name="context_attention",
    )(sink, q, kvp)


def _lat_attn_kernel(sink_ref, q_ref, kvp_prev, kvp_own, kvp_next, ck_ref, cv_ref, o_ref,
                     kwin, vwin, kctx, vctx, q2s, s_buf, p_buf, t_buf, o_buf, *, n_steps, past):
    i = pl.program_id(1)
    n_blocks = q_ref.shape[0] // BLOCK
    own_end = BLOCK + q_ref.shape[0]
    win = 3 * BLOCK
    _init_pipeline_scratch(jnp.logical_and(pl.program_id(0) == 0, i == 0), (s_buf, p_buf), (t_buf,))

    @pl.when(i == 0)
    def _():
        cpl = _kv_placements(ck_ref[0].T, cv_ref[0].T)
        for idx in range(N_PLACEMENTS):
            kctx[idx] = cpl[:, idx * LANES:(idx + 1) * LANES]
            vctx[idx] = cpl[:, (N_PLACEMENTS + idx) * LANES:(N_PLACEMENTS + idx + 1) * LANES]

    for src, r0, r1 in ((kvp_prev, 0, BLOCK), (kvp_own, BLOCK, own_end), (kvp_next, own_end, own_end + BLOCK)):
        for idx in range(N_PLACEMENTS):
            kwin[idx, r0:r1, :] = src[:, idx * LANES:(idx + 1) * LANES]
            vwin[idx, r0:r1, :] = src[:, (N_PLACEMENTS + idx) * LANES:(N_PLACEMENTS + idx + 1) * LANES]
    _stack_queries(q_ref, q2s, n_blocks)

    row = lax.broadcasted_iota(jnp.int32, (SOFTMAX_ROWS, LANES), 0)
    c = lax.broadcasted_iota(jnp.int32, (SOFTMAX_ROWS, LANES), 1)
    off_prev = jnp.where(i > 0, 0, BLOCK)
    off_next = jnp.where(i < n_steps - 1, 0, BLOCK)

    def window_of(g):
        return _key_rows((g // N_PLACEMENTS) * BLOCK, win)

    def qk(g, u, valid):
        q2 = q2s[g // 2]
        s_buf[u, :, 0:win] = _dot_nt(q2, kwin[g % N_PLACEMENTS, window_of(g), :])
        s_buf[u, :, win:win + past] = _dot_nt(q2, kctx[g % N_PLACEMENTS])

    def softmax(g, u, valid):
        blk = g // N_PLACEMENTS
        off0 = jnp.where(blk == 0, off_prev, 0)
        off2 = jnp.where(blk == n_blocks - 1, off_next, 0)

        def masks_of(r0):
            r = row + r0 % BLOCK
            return {0: c >= r + off0, 2: c <= r - off2}

        _softmax_rows(s_buf, p_buf, t_buf, u, 2 * BLOCK, win + past, masks_of, _group_sinks(sink_ref, g))

    def pv(g, u, valid):
        acc = _dot(p_buf[u, :, 0:win], vwin[g % N_PLACEMENTS, window_of(g), :])
        acc = acc + _dot(p_buf[u, :, win:win + past], vctx[g % N_PLACEMENTS])
        o_buf[g] = _normalize(acc, t_buf, u)

    _software_pipeline(n_blocks * N_PLACEMENTS, PIPE_SLOTS, (qk, softmax, pv))
    _write_heads(o_buf, o_ref, n_blocks)


def _latent_attention(q, kvp, ck, cv, sink, n_batch, seq):
    ns = seq // ATTN_TILE
    nb = seq // BLOCK
    per = ATTN_TILE // BLOCK
    own = lambda b, i: (b * ns + i, 0)
    prev = lambda b, i: (b * nb + jnp.maximum(per * i - 1, 0), 0)
    nxt = lambda b, i: (b * nb + jnp.minimum(per * i + per, nb - 1), 0)
    ctx = lambda b, i: (b, 0, 0)
    past = ck.shape[2]
    win_rows = ATTN_TILE + 2 * BLOCK
    bf = lambda *shape: pltpu.VMEM(shape, BF16)
    return pl.pallas_call(
        functools.partial(_lat_attn_kernel, n_steps=ns, past=past),
        grid=(n_batch, ns),
        in_specs=[
            pl.BlockSpec(memory_space=pltpu.SMEM),
            pl.BlockSpec((ATTN_TILE, ATTN_WIDTH), own),
            pl.BlockSpec((BLOCK, KVP_WIDTH), prev),
            pl.BlockSpec((ATTN_TILE, KVP_WIDTH), own),
            pl.BlockSpec((BLOCK, KVP_WIDTH), nxt),
            pl.BlockSpec((1, KV_WIDTH, past), ctx),
            pl.BlockSpec((1, KV_WIDTH, past), ctx),
        ],
        out_specs=pl.BlockSpec((ATTN_TILE, ATTN_WIDTH), own),
        out_shape=jax.ShapeDtypeStruct((n_batch * seq, ATTN_WIDTH), BF16),
        scratch_shapes=[bf(N_PLACEMENTS, win_rows, LANES), bf(N_PLACEMENTS, win_rows, LANES),
                        bf(N_PLACEMENTS, past, LANES), bf(N_PLACEMENTS, past, LANES)]
        + _attn_scratch(per * N_PLACEMENTS, 3 * BLOCK + past),
        compiler_params=_params("arbitrary", "arbitrary"),
        name="latent_attention",
    )(sink, q, kvp, kvp, kvp, ck, cv)


_PK = 2 * GLA_DK
_PV = 2 * GLA_DV


def _cumsum_rows(x, reverse):
    n = x.shape[0]
    row = lax.broadcasted_iota(jnp.int32, x.shape, 0)
    s = 1
    while s < n:
        if reverse:
            x = x + jnp.where(row < n - s, pltpu.roll(x, n - s, 0), 0.0)
        else:
            x = x + jnp.where(row >= s, pltpu.roll(x, s, 0), 0.0)
        s *= 2
    return x


def _gla_kernel(*refs, seq_chunks, has_init, emit_state):
    it = iter(refs)
    views = [tuple(next(it) for _ in range(3)) for _ in range(2)]
    s0_ref = next(it) if has_init else None
    out_refs = (next(it), next(it))
    sout_ref = next(it) if emit_state else None
    st_ref, a_buf, a2_buf, qe_buf, qd_buf, ds_buf, sb_buf, dc_buf = (next(it) for _ in range(8))
    n_chunks = views[0][0].shape[0] // CHUNK
    n_groups = 2 * n_chunks
    n_pairs = N_GLA_HEADS // 2
    tile = pl.program_id(1)
    q_scale = GLA_DK ** -0.5
    _init_pipeline_scratch(jnp.logical_and(pl.program_id(0) == 0, tile == 0),
                           (st_ref, a_buf, a2_buf, qe_buf, qd_buf, ds_buf, sb_buf), (dc_buf,))

    if has_init:
        @pl.when(tile == 0)
        def _():
            st_ref[...] = jnp.zeros_like(st_ref)
            for d in range(2):
                for h in range(N_GLA_HEADS):
                    p, e = divmod(h, 2)
                    st_ref[d, p, e * GLA_DK:(e + 1) * GLA_DK, e * GLA_DV:(e + 1) * GLA_DV] = s0_ref[0, d, h]

    lane_k = lax.broadcasted_iota(jnp.int32, (CHUNK, _PK), 1)
    row_k = lax.broadcasted_iota(jnp.int32, (CHUNK, _PK), 0)
    lo_k = lane_k < GLA_DK
    tri = (lane_k % CHUNK <= row_k, lane_k % CHUNK >= row_k)
    lo_v = lax.broadcasted_iota(jnp.int32, (CHUNK, _PV), 1) < GLA_DV
    diag = ((lax.broadcasted_iota(jnp.int32, (_PK, _PV), 0) < GLA_DK)
            == (lax.broadcasted_iota(jnp.int32, (_PK, _PV), 1) < GLA_DV))

    def place(g, d):
        c = g // 2
        if d == 0:
            return c, c
        seq = c // seq_chunks
        return c, seq * seq_chunks + (seq_chunks - 1 - (c - seq * seq_chunks))

    def rows_of(pos):
        return pl.ds(pl.multiple_of(pos * CHUNK, CHUNK), CHUNK)

    def prep(g, u, valid):
        d = u % 2
        qk_ref, g_ref, v_ref = views[d]
        _, pos = place(g, d)
        rows = rows_of(pos)
        gc = _cumsum_rows(g_ref[rows, :], reverse=(d == 1))
        gtot = gc[0:1, :] if d == 1 else gc[CHUNK - 1:CHUNK, :]
        k = qk_ref[rows, GLA_K_WIDTH:2 * GLA_K_WIDTH]
        qe = qk_ref[rows, 0:GLA_K_WIDTH] * jnp.exp(gc)
        qe_buf[u] = qe
        q_dec = (qe * q_scale).astype(BF16)
        k_inv = k * jnp.exp(-gc)
        k_end = jnp.where(valid, k * jnp.exp(gtot - gc), 0.0).astype(BF16)
        decay = jnp.where(valid, jnp.exp(gtot), 1.0)
        for p in range(n_pairs):
            lanes = slice(p * _PK, (p + 1) * _PK)
            ki = k_inv[:, lanes]
            kst = jnp.concatenate([jnp.where(lo_k, ki, 0.0), jnp.where(lo_k, 0.0, ki)], axis=0).astype(BF16)
            a_buf[u, p] = _dot_nt(q_dec[:, lanes], kst)
            ds_buf[u, p] = jnp.where(diag, _dot_tn(k_end[:, lanes], v_ref[rows, p * _PV:(p + 1) * _PV]), 0.0)
            dc_buf[u, p] = jnp.broadcast_to(decay[:, lanes], (_PK, _PK)).T

    def state(g, u, valid):
        d = u % 2
        c, _ = place(g, d)
        qd_buf[u] = (qe_buf[u] * q_scale).astype(BF16)
        for p in range(n_pairs):
            fresh = (c % seq_chunks == 0) if emit_state else (c < 0)
            s_in = jnp.where(fresh, 0.0, st_ref[d, p])
            sb_buf[u, p] = s_in.astype(BF16)
            a2_buf[u, p] = jnp.where(tri[d], a_buf[u, p], 0.0).astype(BF16)
            dc = dc_buf[u, p]
            s_new = s_in * jnp.concatenate([dc, dc], axis=1) + ds_buf[u, p]
            st_ref[d, p] = s_new
            if emit_state:
                for e in range(2):
                    sout_ref[c // seq_chunks, d, 2 * p + e] = s_new[e * GLA_DK:(e + 1) * GLA_DK,
                                                                    e * GLA_DV:(e + 1) * GLA_DV]

    def out(g, u, valid):
        d = u % 2
        _, _, v_ref = views[d]
        _, pos = place(g, d)
        rows = rows_of(pos)
        for p in range(n_pairs):
            vpf = v_ref[rows, p * _PV:(p + 1) * _PV].astype(F32)
            vbd = jnp.concatenate([jnp.where(lo_v, vpf, 0.0), jnp.where(lo_v, 0.0, vpf)], axis=0).astype(BF16)
            x = jnp.concatenate([a2_buf[u, p], qd_buf[u, :, p * _PK:(p + 1) * _PK]], axis=1)
            w = jnp.concatenate([vbd, sb_buf[u, p]], axis=0)
            out_refs[d][rows, p * _PV:(p + 1) * _PV] = _dot(x, w).astype(out_refs[d].dtype)

    _software_pipeline(n_groups, GLA_PIPE_SLOTS, (prep, state, out))


def _gla(qkg, gf, gb, vg, s0, n_batch, seq):
    tm = GLA_TILE
    long_seq = s0 is not None
    if long_seq:
        nt = seq // tm
        grid = (n_batch, nt)
        fwd = lambda b, i: (b * nt + i, 0)
        bwd = lambda b, i: (b * nt + nt - 1 - i, 0)
        seq_chunks = tm // CHUNK
    else:
        per_step = tm // seq
        grid = (1, n_batch // per_step)
        fwd = bwd = lambda b, i: (i, 0)
        seq_chunks = seq // CHUNK
    in_specs = []
    args = []
    for m, gates in ((fwd, gf), (bwd, gb)):
        in_specs += [pl.BlockSpec((tm, 2 * GLA_K_WIDTH), m), pl.BlockSpec((tm, GLA_K_WIDTH), m),
                     pl.BlockSpec((tm, GLA_V_WIDTH), m)]
        args += [qkg, gates, vg]
    n_tok = n_batch * seq
    out_specs = [pl.BlockSpec((tm, GLA_V_WIDTH), fwd), pl.BlockSpec((tm, GLA_V_WIDTH), bwd)]
    out_shape = [jax.ShapeDtypeStruct((n_tok, GLA_V_WIDTH), BF16)] * 2
    st_dims = (2, N_GLA_HEADS, GLA_DK, GLA_DV)
    if long_seq:
        in_specs.append(pl.BlockSpec((1,) + st_dims, lambda b, i: (b, 0, 0, 0, 0)))
        args.append(s0)
    else:
        out_specs.append(pl.BlockSpec((per_step,) + st_dims, lambda b, i: (i, 0, 0, 0, 0)))
        out_shape.append(jax.ShapeDtypeStruct((n_batch,) + st_dims, F32))
    n = GLA_PIPE_SLOTS
    n_pairs = N_GLA_HEADS // 2
    return pl.pallas_call(
        functools.partial(_gla_kernel, seq_chunks=seq_chunks, has_init=long_seq, emit_state=not long_seq),
        grid=grid,
        in_specs=in_specs,
        out_specs=out_specs,
        out_shape=out_shape,
        scratch_shapes=[pltpu.VMEM((2, n_pairs, _PK, _PV), F32),
                        pltpu.VMEM((n, n_pairs, CHUNK, _PK), F32),
                        pltpu.VMEM((n, n_pairs, CHUNK, _PK), BF16),
                        pltpu.VMEM((n, CHUNK, GLA_K_WIDTH), F32),
                        pltpu.VMEM((n, CHUNK, GLA_K_WIDTH), BF16),
                        pltpu.VMEM((n, n_pairs, _PK, _PV), F32),
                        pltpu.VMEM((n, n_pairs, _PK, _PV), BF16),
                        pltpu.VMEM((n, n_pairs, _PK, _PK), F32)],
        compiler_params=_params("arbitrary", "arbitrary"),
        name="gla_long" if long_seq else "gla_short",
    )(*args)


def _merge_kernel(x_ref, mod_ref, oa_ref, za_ref, of_ref, ob_ref, zg_ref, gng_ref, wo_ref, fng_ref, y_ref):
    ya = oa_ref[...].astype(F32) * _silu(za_ref[...].astype(F32))
    og = of_ref[...].astype(F32) + ob_ref[...].astype(F32)
    zg = zg_ref[...].astype(F32)
    gng = gng_ref[...]
    parts = [ya]
    for h in range(N_GLA_HEADS):
        oh = og[:, h * GLA_DV:(h + 1) * GLA_DV]
        nh = oh * lax.rsqrt(jnp.mean(oh * oh, axis=-1, keepdims=True) + EPS)
        parts.append((nh * gng) * _silu(zg[:, h * GLA_DV:(h + 1) * GLA_DV]))
    yin = jnp.concatenate(parts, axis=-1).astype(BF16)
    gate = mod_ref[0][:, 2 * D_MODEL:3 * D_MODEL]
    y = x_ref[...] + gate * _dot(yin, wo_ref[...])
    y_ref[...] = (y * lax.rsqrt(jnp.mean(y * y, axis=-1, keepdims=True) + EPS)) * fng_ref[...]


def _merge(x2d, mod3, mod_row_of_tile, oa, za, of, ob, zg, gla_norm_g, w_out, final_norm_g, tm):
    n_tok = x2d.shape[0]
    row = lambda i: (i, 0)
    const = lambda i: (0, 0)
    wide = pl.BlockSpec((tm, GLA_V_WIDTH), row)
    return pl.pallas_call(
        _merge_kernel,
        grid=(n_tok // tm,),
        in_specs=[
            pl.BlockSpec((tm, D_MODEL), row),
            pl.BlockSpec((1, 1, 3 * D_MODEL), lambda i: (mod_row_of_tile(i), 0, 0)),
            wide, wide, wide, wide, wide,
            pl.BlockSpec((1, GLA_DV), const),
            pl.BlockSpec((D_MODEL, D_MODEL), const),
            pl.BlockSpec((1, D_MODEL), const),
        ],
        out_specs=pl.BlockSpec((tm, D_MODEL), row),
        out_shape=jax.ShapeDtypeStruct((n_tok, D_MODEL), F32),
        compiler_params=_params("arbitrary"),
        name="merge",
    )(x2d, mod3, oa, za, of, ob, zg, gla_norm_g, w_out, final_norm_g)


def _rope_tables(n_tok):
    t = np.arange(n_tok)
    rowp = (t // GRID_W).astype(np.float64)
    colp = (t % GRID_W).astype(np.float64)
    nf = HEAD_DIM // 4
    freqs = ROPE_BASE ** (-np.arange(nf, dtype=np.float64) / nf)
    lane = np.arange(LANES)
    pos = np.where(((lane % HEAD_DIM) < HEAD_DIM // 2)[None, :], rowp[:, None], colp[:, None])
    ang = pos * freqs[lane % nf][None, :]
    sign = np.where((lane % (2 * nf)) < nf, -1.0, 1.0)
    return jnp.asarray(np.cos(ang), F32), jnp.asarray(np.sin(ang) * sign[None, :], F32)


def kernel(x_prompt, x_sample, c, cache_k, cache_v, state_gla, c_ctx, w_mod, b_mod, norm_g, w_in,
           w_gk_f, b_gk_f, w_gk_b, b_gk_b, sink, gla_norm_g, w_out, final_norm_g):
    depth = w_in.shape[0]
    assert depth == 1, "single-layer step"
    l = 0
    bp, sp, _ = x_prompt.shape
    bs, ss, _ = x_sample.shape

    w1 = _reorder_w_in(w_in[l])
    wgk = jnp.zeros((LANES, 2 * GLA_K_WIDTH), F32)
    wgk = wgk.at[0:GATE_RANK, 0:GLA_K_WIDTH].set(w_gk_f[l])
    wgk = wgk.at[GATE_RANK:2 * GATE_RANK, GLA_K_WIDTH:].set(w_gk_b[l]).astype(BF16)
    bgk = jnp.concatenate([b_gk_f[l], b_gk_b[l]])[None, :]
    wo = w_out[l].astype(BF16)
    ng = norm_g[l][None, :]
    gng = gla_norm_g[l][None, :]
    fng = final_norm_g[None, :]
    sk = sink[l]

    n_rows = 8
    cvecs = jnp.concatenate([c_ctx[None, :], c, jnp.zeros((n_rows - 1 - bs, D_MODEL), F32)], axis=0)
    mod3 = _modulation(cvecs, w_mod[l], b_mod[l][None, :]).reshape(n_rows, 1, 3 * D_MODEL)
    tm = TOKEN_TILE

    xp2 = x_prompt.reshape(bp * sp, D_MODEL)
    ctx_row = lambda i: 0
    q, kvp, za, vg, zg, qkg, gf, gb, kt, vt = _project(xp2, mod3, ctx_row, ng, w1, wgk, bgk, None, tm, sp)
    oa = _context_attention(q, kvp, sk, sp)
    of, ob, st = _gla(qkg, gf, gb, vg, None, bp, sp)
    y_prompt = _merge(xp2, mod3, ctx_row, oa, za, of, ob, zg, gng, wo, fng, MERGE_TILE).reshape(bp, sp, D_MODEL)
    to_cache = lambda t: t.reshape(bp, 1, N_KV_HEADS, HEAD_DIM, sp).transpose(0, 1, 4, 2, 3)
    new_k, new_v = to_cache(kt), to_cache(vt)
    new_state = st[:, None]

    xs2 = x_sample.reshape(bs * ss, D_MODEL)
    tiles_per_seq = ss // tm
    lat_row = lambda i: 1 + i // tiles_per_seq
    q, kvp, za, vg, zg, qkg, gf, gb = _project(xs2, mod3, lat_row, ng, w1, wgk, bgk, _rope_tables(ss), tm, ss)
    past = cache_k.shape[2]
    from_cache = lambda t: t[:, l].transpose(0, 2, 3, 1).reshape(bs, KV_WIDTH, past)
    ck, cv = from_cache(cache_k), from_cache(cache_v)
    oa = _latent_attention(q, kvp, ck, cv, sk, bs, ss)
    of, ob = _gla(qkg, gf, gb, vg, state_gla[:, l], bs, ss)
    merge_row = lambda i: 1 + i // (ss // MERGE_TILE)
    y_sample = _merge(xs2, mod3, merge_row, oa, za, of, ob, zg, gng, wo, fng, MERGE_TILE).reshape(bs, ss, D_MODEL)

    return (y_prompt, y_sample, new_k, new_v, new_state)
```

```python
import functools
import math

import numpy as np
import jax
import jax.numpy as jnp
from jax import lax
from jax.experimental import pallas as pl
from jax.experimental.pallas import tpu as pltpu

D_MODEL = 1024
GRID_W = 64
HEAD_DIM = 64
N_ATTN_HEADS = 8
N_KV_HEADS = 2
ATTN_WIDTH = 512
KV_WIDTH = 128
BLOCK = 128
N_GLA_HEADS = 4
GLA_DK = 64
GLA_DV = 128
GLA_K_WIDTH = 256
GLA_V_WIDTH = 512
GATE_RANK = 16
GATE_NORMALIZER = 16.0
CHUNK = 64
ROPE_BASE = 10000.0
EPS = 1e-6
NEG_INF = -1e30

LANES = 128
TOKEN_TILE = 512
MERGE_TILE = 1024
GLA_TILE = 2048
ATTN_TILE = 2048
CTX_ATTN_BATCH = 8
PIPE_SLOTS = 4
GLA_PIPE_SLOTS = 4
KVP_WIDTH = 8 * LANES
N_PLACEMENTS = 2 * N_KV_HEADS
SOFTMAX_ROWS = 32
MOD_TILE = 512
VMEM_LIMIT = 56 * 1024 * 1024
LOG2E = math.log2(math.e)

F32 = jnp.float32
BF16 = jnp.bfloat16


def _dot(a, b):
    return jnp.dot(a, b, preferred_element_type=F32)


def _dot_nt(a, b):
    return lax.dot_general(a, b, (((1,), (1,)), ((), ())), preferred_element_type=F32)


def _dot_tn(a, b):
    return lax.dot_general(a, b, (((0,), (0,)), ((), ())), preferred_element_type=F32)


def _silu(x):
    h = 0.5 * x
    return h + h * jnp.tanh(h)


def _log_sigmoid(x):
    return jnp.minimum(x, 0.0) - jnp.log1p(jnp.exp(-jnp.abs(x)))


def _params(*sem):
    return pltpu.CompilerParams(dimension_semantics=sem, vmem_limit_bytes=VMEM_LIMIT)


def _split_bf16(x):
    hi = x.astype(BF16)
    return hi, (x - hi.astype(F32)).astype(BF16)


def _mod_kernel(c_ref, w_ref, b_ref, o_ref):
    rows = c_ref.shape[0]
    s_hi, s_lo = _split_bf16(_silu(c_ref[...]))
    w_hi, w_lo = _split_bf16(w_ref[...])
    by_hi = _dot(jnp.concatenate([s_hi, s_lo], axis=0), w_hi)
    o_ref[...] = by_hi[0:rows] + by_hi[rows:2 * rows] + _dot(s_hi, w_lo) + b_ref[...]


def _modulation(cvecs, w_mod, b_mod):
    rows = cvecs.shape[0]
    n_out = w_mod.shape[1]
    return pl.pallas_call(
        _mod_kernel,
        grid=(n_out // MOD_TILE,),
        in_specs=[
            pl.BlockSpec((rows, D_MODEL), lambda j: (0, 0)),
            pl.BlockSpec((D_MODEL, MOD_TILE), lambda j: (0, j)),
            pl.BlockSpec((1, MOD_TILE), lambda j: (0, j)),
        ],
        out_specs=pl.BlockSpec((rows, MOD_TILE), lambda j: (0, j)),
        out_shape=jax.ShapeDtypeStruct((rows, n_out), F32),
        compiler_params=_params("arbitrary"),
        name="adaln_mod",
    )(cvecs, w_mod, b_mod)


_Q0, _K0, _V0, _ZA0, _QG0, _KG0, _VG0, _ZG0, _W1_COLS = 0, 512, 640, 768, 1280, 1536, 1792, 2304, 2816
_LR0 = _W1_COLS
_W_COLS = _W1_COLS + LANES
_Q_PRESCALE = (HEAD_DIM ** -0.5) * LOG2E


def _rope(x, cos, sin_signed):
    n = x.shape[1] // LANES
    lane = lax.broadcasted_iota(jnp.int32, (x.shape[0], LANES), 1)
    first = (lane % 32) < 16
    outs = []
    for c in range(n):
        xc = x[:, c * LANES:(c + 1) * LANES]
        partner = jnp.where(first, pltpu.roll(xc, LANES - 16, 1), pltpu.roll(xc, 16, 1))
        outs.append(xc * cos + partner * sin_signed)
    return outs[0] if n == 1 else jnp.concatenate(outs, axis=-1)


def _kv_placements(k, v):
    lane = lax.broadcasted_iota(jnp.int32, k.shape, 1)
    lo = lane < HEAD_DIM
    kr = pltpu.roll(k, HEAD_DIM, 1)
    vr = pltpu.roll(v, HEAD_DIM, 1)
    pieces = [jnp.where(lo, k, 0.0), jnp.where(lo, 0.0, kr), jnp.where(lo, kr, 0.0), jnp.where(lo, 0.0, k),
              jnp.where(lo, v, 1.0), jnp.where(lo, 1.0, vr), jnp.where(lo, vr, 1.0), jnp.where(lo, 1.0, v)]
    return jnp.concatenate(pieces, axis=-1).astype(BF16)


def _proj_kernel(*refs, rope):
    if rope:
        (x_ref, mod_ref, ng_ref, w1_ref, wgk_ref, bgk_ref, cos_ref, sin_ref,
         q_ref, kvp_ref, za_ref, vg_ref, zg_ref, qkg_ref, gf_ref, gb_ref) = refs
    else:
        (x_ref, mod_ref, ng_ref, w1_ref, wgk_ref, bgk_ref,
         q_ref, kvp_ref, za_ref, vg_ref, zg_ref, qkg_ref, gf_ref, gb_ref, k_ref, v_ref) = refs
    x = x_ref[...]
    xn = x * lax.rsqrt(jnp.mean(x * x, axis=-1, keepdims=True) + EPS)
    mod = mod_ref[0]
    shift = mod[:, 0:D_MODEL]
    scale = mod[:, D_MODEL:2 * D_MODEL]
    h = ((xn * ng_ref[...]) * (1.0 + scale) + shift).astype(BF16)

    def piece(c0, c1):
        return _dot(h, w1_ref[:, c0:c1])

    q = piece(_Q0, _K0)
    kv = piece(_K0, _ZA0)
    k = kv[:, 0:KV_WIDTH]
    v = kv[:, KV_WIDTH:2 * KV_WIDTH]
    if rope:
        cos = cos_ref[...]
        sin = sin_ref[...]
        q = _rope(q, cos, sin)
        k = _rope(k, cos, sin)
    else:
        seq = k_ref.shape[2]
        for b in range(k_ref.shape[0]):
            k_ref[b] = k[b * seq:(b + 1) * seq, :].T
            v_ref[b] = v[b * seq:(b + 1) * seq, :].T
    q_ref[...] = (q * _Q_PRESCALE).astype(BF16)
    kvp_ref[...] = _kv_placements(k, v)
    za_ref[...] = piece(_ZA0, _QG0).astype(BF16)
    vg_ref[...] = piece(_VG0, _ZG0).astype(BF16)
    zg_ref[...] = piece(_ZG0, _W1_COLS).astype(BF16)
    qkg_ref[...] = piece(_QG0, _VG0)
    lr = piece(_LR0, _W_COLS).astype(BF16)
    g = _log_sigmoid(_dot(lr, wgk_ref[...]) + bgk_ref[...]) * (1.0 / GATE_NORMALIZER)
    gf_ref[...] = g[:, 0:GLA_K_WIDTH]
    gb_ref[...] = g[:, GLA_K_WIDTH:2 * GLA_K_WIDTH]


def _project(x2d, mod3, mod_row_of_tile, norm_g, w1, wgk, bgk, rope_tabs, tm, seq):
    n_tok = x2d.shape[0]
    rope = rope_tabs is not None
    row = lambda i: (i, 0)
    const = lambda i: (0, 0)
    in_specs = [
        pl.BlockSpec((tm, D_MODEL), row),
        pl.BlockSpec((1, 1, 3 * D_MODEL), lambda i: (mod_row_of_tile(i), 0, 0)),
        pl.BlockSpec((1, D_MODEL), const),
        pl.BlockSpec((D_MODEL, _W_COLS), const),
        pl.BlockSpec((LANES, 2 * GLA_K_WIDTH), const),
        pl.BlockSpec((1, 2 * GLA_K_WIDTH), const),
    ]
    args = [x2d, mod3, norm_g, w1, wgk, bgk]
    if rope:
        tiles_per_seq = rope_tabs[0].shape[0] // tm
        pos = lambda i: (i % tiles_per_seq, 0)
        in_specs += [pl.BlockSpec((tm, LANES), pos), pl.BlockSpec((tm, LANES), pos)]
        args += list(rope_tabs)
    widths = [(ATTN_WIDTH, BF16), (KVP_WIDTH, BF16), (ATTN_WIDTH, BF16), (GLA_V_WIDTH, BF16), (GLA_V_WIDTH, BF16),
              (2 * GLA_K_WIDTH, F32), (GLA_K_WIDTH, F32), (GLA_K_WIDTH, F32)]
    out_specs = [pl.BlockSpec((tm, w), row) for w, _ in widths]
    out_shape = [jax.ShapeDtypeStruct((n_tok, w), dt) for w, dt in widths]
    if not rope:
        for _ in range(2):
            out_specs.append(pl.BlockSpec((tm // seq, KV_WIDTH, seq), lambda i: (i, 0, 0)))
            out_shape.append(jax.ShapeDtypeStruct((n_tok // seq, KV_WIDTH, seq), F32))
    return pl.pallas_call(
        functools.partial(_proj_kernel, rope=rope),
        grid=(n_tok // tm,),
        in_specs=in_specs,
        out_specs=out_specs,
        out_shape=out_shape,
        compiler_params=_params("arbitrary"),
        name="project_rope" if rope else "project",
    )(*args)


def _software_pipeline(n_groups, n_slots, stages):
    last_g = n_groups - 1

    def body(k, carry):
        for age in (2, 1, 0):
            for u in range(n_slots):
                g = n_slots * (k - age) + u
                stages[age](jnp.clip(g, 0, last_g), u, jnp.logical_and(g >= 0, g <= last_g))
        return carry

    lax.fori_loop(0, n_groups // n_slots + 2, body, 0)


def _init_pipeline_scratch(first, zero_refs, one_refs=()):
    @pl.when(first)
    def _():
        for ref in zero_refs:
            ref[...] = jnp.zeros_like(ref)
        for ref in one_refs:
            ref[...] = jnp.ones_like(ref)


def _key_rows(start, size):
    return pl.ds(pl.multiple_of(start, BLOCK), size)


def _softmax_rows(s_buf, p_buf, t_buf, slot, n_rows, n_keys, masks_of, sink_of):
    for r0 in range(0, n_rows, SOFTMAX_ROWS):
        rs = slice(r0, r0 + SOFTMAX_ROWS)
        masks = masks_of(r0)
        cols = []
        for c in range(n_keys // LANES):
            sc = s_buf[slot, rs, c * LANES:(c + 1) * LANES]
            cols.append(jnp.where(masks[c], sc, NEG_INF) if c in masks else sc)
        mx = cols[0]
        for sc in cols[1:]:
            mx = jnp.maximum(mx, sc)
        sink = sink_of(r0)
        m = jnp.maximum(jnp.max(mx, axis=-1, keepdims=True), sink)
        for c, sc in enumerate(cols):
            p_buf[slot, rs, c * LANES:(c + 1) * LANES] = jnp.exp2(sc - m).astype(BF16)
        t_buf[slot, rs, :] = jnp.broadcast_to(jnp.exp2(sink - m), (SOFTMAX_ROWS, LANES))


def _normalize(acc, t_buf, slot):
    return acc / (pltpu.roll(acc, HEAD_DIM, 1) + t_buf[slot])


def _attn_scratch(n_groups, n_keys):
    rows = 2 * BLOCK
    return [pltpu.VMEM((n_groups // 2, rows, LANES), BF16),
            pltpu.VMEM((PIPE_SLOTS, rows, n_keys), F32), pltpu.VMEM((PIPE_SLOTS, rows, n_keys), BF16),
            pltpu.VMEM((PIPE_SLOTS, rows, LANES), F32), pltpu.VMEM((n_groups, rows, LANES), F32)]


def _stack_queries(q_ref, q2s, n_blocks):
    for blk in range(n_blocks):
        rows = slice(blk * BLOCK, (blk + 1) * BLOCK)
        for j in range(N_KV_HEADS):
            idx = blk * N_KV_HEADS + j
            q2s[idx, 0:BLOCK, :] = q_ref[rows, (2 * j) * LANES:(2 * j + 1) * LANES]
            q2s[idx, BLOCK:2 * BLOCK, :] = q_ref[rows, (2 * j + 1) * LANES:(2 * j + 2) * LANES]


def _write_heads(o_buf, o_ref, n_blocks):
    lo = lax.broadcasted_iota(jnp.int32, (2 * BLOCK, LANES), 1) < HEAD_DIM
    for blk in range(n_blocks):
        rows = slice(blk * BLOCK, (blk + 1) * BLOCK)
        for j in range(N_KV_HEADS):
            g0 = blk * N_PLACEMENTS + 2 * j
            o = jnp.where(lo, o_buf[g0], o_buf[g0 + 1]).astype(o_ref.dtype)
            o_ref[rows, (2 * j) * LANES:(2 * j + 1) * LANES] = o[0:BLOCK]
            o_ref[rows, (2 * j + 1) * LANES:(2 * j + 2) * LANES] = o[BLOCK:2 * BLOCK]


def _group_sinks(sink_ref, g):
    j, e = (g // 2) % N_KV_HEADS, g % 2
    sink_a = sink_ref[4 * j + e] * LOG2E
    sink_b = sink_ref[4 * j + 2 + e] * LOG2E
    return lambda r0: sink_a if r0 < BLOCK else sink_b


def _ctx_attn_kernel(sink_ref, q_ref, kvp_ref, o_ref, kown, vown, q2s, s_buf, p_buf, t_buf, o_buf, *, seq):
    n_blocks = q_ref.shape[0] // BLOCK
    _init_pipeline_scratch(pl.program_id(0) == 0, (s_buf, p_buf), (t_buf,))
    for idx in range(N_PLACEMENTS):
        kown[idx] = kvp_ref[:, idx * LANES:(idx + 1) * LANES]
        vown[idx] = kvp_ref[:, (N_PLACEMENTS + idx) * LANES:(N_PLACEMENTS + idx + 1) * LANES]
    _stack_queries(q_ref, q2s, n_blocks)
    blocks_per_seq = seq // BLOCK

    def keys_of(g):
        return _key_rows(((g // N_PLACEMENTS) // blocks_per_seq) * seq, seq)

    def qk(g, u, valid):
        s_buf[u] = _dot_nt(q2s[g // 2], kown[g % N_PLACEMENTS, keys_of(g), :])

    def softmax(g, u, valid):
        _softmax_rows(s_buf, p_buf, t_buf, u, 2 * BLOCK, seq, lambda r0: {}, _group_sinks(sink_ref, g))

    def pv(g, u, valid):
        o_buf[g] = _normalize(_dot(p_buf[u], vown[g % N_PLACEMENTS, keys_of(g), :]), t_buf, u)

    _software_pipeline(n_blocks * N_PLACEMENTS, PIPE_SLOTS, (qk, softmax, pv))
    _write_heads(o_buf, o_ref, n_blocks)


def _context_attention(q, kvp, sink, seq):
    n_tok = q.shape[0]
    rows = CTX_ATTN_BATCH * seq
    row = lambda b: (b, 0)
    n_groups = (rows // BLOCK) * N_PLACEMENTS
    return pl.pallas_call(
        functools.partial(_ctx_attn_kernel, seq=seq),
        grid=(n_tok // rows,),
        in_specs=[
            pl.BlockSpec(memory_space=pltpu.SMEM),
            pl.BlockSpec((rows, ATTN_WIDTH), row),
            pl.BlockSpec((rows, KVP_WIDTH), row),
        ],
        out_specs=pl.BlockSpec((rows, ATTN_WIDTH), row),
        out_shape=jax.ShapeDtypeStruct((n_tok, ATTN_WIDTH), BF16),
        scratch_shapes=[pltpu.VMEM((N_PLACEMENTS, rows, LANES), BF16)] * 2 + _attn_scratch(n_groups, seq),
        compiler_params=_params("arbitrary"),
        name="context_attention",
    )(sink, q, kvp)


def _lat_attn_kernel(sink_ref, q_ref, kvp_prev, kvp_own, kvp_next, ck_ref, cv_ref, o_ref,
                     kwin, vwin, kctx, vctx, q2s, s_buf, p_buf, t_buf, o_buf, *, n_steps, past):
    i = pl.program_id(1)
    n_blocks = q_ref.shape[0] // BLOCK
    own_end = BLOCK + q_ref.shape[0]
    win = 3 * BLOCK
    _init_pipeline_scratch(jnp.logical_and(pl.program_id(0) == 0, i == 0), (s_buf, p_buf), (t_buf,))

    @pl.when(i == 0)
    def _():
        cpl = _kv_placements(ck_ref[0].T, cv_ref[0].T)
        for idx in range(N_PLACEMENTS):
            kctx[idx] = cpl[:, idx * LANES:(idx + 1) * LANES]
            vctx[idx] = cpl[:, (N_PLACEMENTS + idx) * LANES:(N_PLACEMENTS + idx + 1) * LANES]

    for src, r0, r1 in ((kvp_prev, 0, BLOCK), (kvp_own, BLOCK, own_end), (kvp_next, own_end, own_end + BLOCK)):
        for idx in range(N_PLACEMENTS):
            kwin[idx, r0:r1, :] = src[:, idx * LANES:(idx + 1) * LANES]
            vwin[idx, r0:r1, :] = src[:, (N_PLACEMENTS + idx) * LANES:(N_PLACEMENTS + idx + 1) * LANES]
    _stack_queries(q_ref, q2s, n_blocks)

    row = lax.broadcasted_iota(jnp.int32, (SOFTMAX_ROWS, LANES), 0)
    c = lax.broadcasted_iota(jnp.int32, (SOFTMAX_ROWS, LANES), 1)
    off_prev = jnp.where(i > 0, 0, BLOCK)
    off_next = jnp.where(i < n_steps - 1, 0, BLOCK)

    def window_of(g):
        return _key_rows((g // N_PLACEMENTS) * BLOCK, win)

    def qk(g, u, valid):
        q2 = q2s[g // 2]
        s_buf[u, :, 0:win] = _dot_nt(q2, kwin[g % N_PLACEMENTS, window_of(g), :])
        s_buf[u, :, win:win + past] = _dot_nt(q2, kctx[g % N_PLACEMENTS])

    def softmax(g, u, valid):
        blk = g // N_PLACEMENTS
        off0 = jnp.where(blk == 0, off_prev, 0)
        off2 = jnp.where(blk == n_blocks - 1, off_next, 0)

        def masks_of(r0):
            r = row + r0 % BLOCK
            return {0: c >= r + off0, 2: c <= r - off2}

        _softmax_rows(s_buf, p_buf, t_buf, u, 2 * BLOCK, win + past, masks_of, _group_sinks(sink_ref, g))

    def pv(g, u, valid):
        acc = _dot(p_buf[u, :, 0:win], vwin[g % N_PLACEMENTS, window_of(g), :])
        acc = acc + _dot(p_buf[u, :, win:win + past], vctx[g % N_PLACEMENTS])
        o_buf[g] = _normalize(acc, t_buf, u)

    _software_pipeline(n_blocks * N_PLACEMENTS, PIPE_SLOTS, (qk, softmax, pv))
    _write_heads(o_buf, o_ref, n_blocks)


def _latent_attention(q, kvp, ck, cv, sink, n_batch, seq):
    ns = seq // ATTN_TILE
    nb = seq // BLOCK
    per = ATTN_TILE // BLOCK
    own = lambda b, i: (b * ns + i, 0)
    prev = lambda b, i: (b * nb + jnp.maximum(per * i - 1, 0), 0)
    nxt = lambda b, i: (b * nb + jnp.minimum(per * i + per, nb - 1), 0)
    ctx = lambda b, i: (b, 0, 0)
    past = ck.shape[2]
    win_rows = ATTN_TILE + 2 * BLOCK
    bf = lambda *shape: pltpu.VMEM(shape, BF16)
    return pl.pallas_call(
        functools.partial(_lat_attn_kernel, n_steps=ns, past=past),
        grid=(n_batch, ns),
        in_specs=[
            pl.BlockSpec(memory_space=pltpu.SMEM),
            pl.BlockSpec((ATTN_TILE, ATTN_WIDTH), own),
            pl.BlockSpec((BLOCK, KVP_WIDTH), prev),
            pl.BlockSpec((ATTN_TILE, KVP_WIDTH), own),
            pl.BlockSpec((BLOCK, KVP_WIDTH), nxt),
            pl.BlockSpec((1, KV_WIDTH, past), ctx),
            pl.BlockSpec((1, KV_WIDTH, past), ctx),
        ],
        out_specs=pl.BlockSpec((ATTN_TILE, ATTN_WIDTH), own),
        out_shape=jax.ShapeDtypeStruct((n_batch * seq, ATTN_WIDTH), BF16),
        scratch_shapes=[bf(N_PLACEMENTS, win_rows, LANES), bf(N_PLACEMENTS, win_rows, LANES),
                        bf(N_PLACEMENTS, past, LANES), bf(N_PLACEMENTS, past, LANES)]
        + _attn_scratch(per * N_PLACEMENTS, 3 * BLOCK + past),
        compiler_params=_params("arbitrary", "arbitrary"),
        name="latent_attention",
    )(sink, q, kvp, kvp, kvp, ck, cv)


_PK = 2 * GLA_DK
_PV = 2 * GLA_DV


def _cumsum_rows(x, reverse):
    n = x.shape[0]
    row = lax.broadcasted_iota(jnp.int32, x.shape, 0)
    s = 1
    while s < n:
        if reverse:
            x = x + jnp.where(row < n - s, pltpu.roll(x, n - s, 0), 0.0)
        else:
            x = x + jnp.where(row >= s, pltpu.roll(x, s, 0), 0.0)
        s *= 2
    return x


def _gla_kernel(*refs, seq_chunks, has_init, emit_state):
    it = iter(refs)
    views = [tuple(next(it) for _ in range(3)) for _ in range(2)]
    s0_ref = next(it) if has_init else None
    out_refs = (next(it), next(it))
    sout_ref = next(it) if emit_state else None
    st_ref, a_buf, a2_buf, qe_buf, qd_buf, ds_buf, sb_buf, dc_buf = (next(it) for _ in range(8))
    n_chunks = views[0][0].shape[0] // CHUNK
    n_groups = 2 * n_chunks
    n_pairs = N_GLA_HEADS // 2
    tile = pl.program_id(1)
    q_scale = GLA_DK ** -0.5
    _init_pipeline_scratch(jnp.logical_and(pl.program_id(0) == 0, tile == 0),
                           (st_ref, a_buf, a2_buf, qe_buf, qd_buf, ds_buf, sb_buf), (dc_buf,))

    if has_init:
        @pl.when(tile == 0)
        def _():
            st_ref[...] = jnp.zeros_like(st_ref)
            for d in range(2):
                for h in range(N_GLA_HEADS):
                    p, e = divmod(h, 2)
                    st_ref[d, p, e * GLA_DK:(e + 1) * GLA_DK, e * GLA_DV:(e + 1) * GLA_DV] = s0_ref[0, d, h]

    lane_k = lax.broadcasted_iota(jnp.int32, (CHUNK, _PK), 1)
    row_k = lax.broadcasted_iota(jnp.int32, (CHUNK, _PK), 0)
    lo_k = lane_k < GLA_DK
    tri = (lane_k % CHUNK <= row_k, lane_k % CHUNK >= row_k)
    lo_v = lax.broadcasted_iota(jnp.int32, (CHUNK, _PV), 1) < GLA_DV
    diag = ((lax.broadcasted_iota(jnp.int32, (_PK, _PV), 0) < GLA_DK)
            == (lax.broadcasted_iota(jnp.int32, (_PK, _PV), 1) < GLA_DV))

    def place(g, d):
        c = g // 2
        if d == 0:
            return c, c
        seq = c // seq_chunks
        return c, seq * seq_chunks + (seq_chunks - 1 - (c - seq * seq_chunks))

    def rows_of(pos):
        return pl.ds(pl.multiple_of(pos * CHUNK, CHUNK), CHUNK)

    def prep(g, u, valid):
        d = u % 2
        qk_ref, g_ref, v_ref = views[d]
        _, pos = place(g, d)
        rows = rows_of(pos)
        gc = _cumsum_rows(g_ref[rows, :], reverse=(d == 1))
        gtot = gc[0:1, :] if d == 1 else gc[CHUNK - 1:CHUNK, :]
        k = qk_ref[rows, GLA_K_WIDTH:2 * GLA_K_WIDTH]
        qe = qk_ref[rows, 0:GLA_K_WIDTH] * jnp.exp(gc)
        qe_buf[u] = qe
        q_dec = (qe * q_scale).astype(BF16)
        k_inv = k * jnp.exp(-gc)
        k_end = jnp.where(valid, k * jnp.exp(gtot - gc), 0.0).astype(BF16)
        decay = jnp.where(valid, jnp.exp(gtot), 1.0)
        for p in range(n_pairs):
            lanes = slice(p * _PK, (p + 1) * _PK)
            ki = k_inv[:, lanes]
            kst = jnp.concatenate([jnp.where(lo_k, ki, 0.0), jnp.where(lo_k, 0.0, ki)], axis=0).astype(BF16)
            a_buf[u, p] = _dot_nt(q_dec[:, lanes], kst)
            ds_buf[u, p] = jnp.where(diag, _dot_tn(k_end[:, lanes], v_ref[rows, p * _PV:(p + 1) * _PV]), 0.0)
            dc_buf[u, p] = jnp.broadcast_to(decay[:, lanes], (_PK, _PK)).T

    def state(g, u, valid):
        d = u % 2
        c, _ = place(g, d)
        qd_buf[u] = (qe_buf[u] * q_scale).astype(BF16)
        for p in range(n_pairs):
            fresh = (c % seq_chunks == 0) if emit_state else (c < 0)
            s_in = jnp.where(fresh, 0.0, st_ref[d, p])
            sb_buf[u, p] = s_in.astype(BF16)
            a2_buf[u, p] = jnp.where(tri[d], a_buf[u, p], 0.0).astype(BF16)
            dc = dc_buf[u, p]
            s_new = s_in * jnp.concatenate([dc, dc], axis=1) + ds_buf[u, p]
            st_ref[d, p] = s_new
            if emit_state:
                for e in range(2):
                    sout_ref[c // seq_chunks, d, 2 * p + e] = s_new[e * GLA_DK:(e + 1) * GLA_DK,
                                                                    e * GLA_DV:(e + 1) * GLA_DV]

    def out(g, u, valid):
        d = u % 2
        _, _, v_ref = views[d]
        _, pos = place(g, d)
        rows = rows_of(pos)
        for p in range(n_pairs):
            vpf = v_ref[rows, p * _PV:(p + 1) * _PV].astype(F32)
            vbd = jnp.concatenate([jnp.where(lo_v, vpf, 0.0), jnp.where(lo_v, 0.0, vpf)], axis=0).astype(BF16)
            x = jnp.concatenate([a2_buf[u, p], qd_buf[u, :, p * _PK:(p + 1) * _PK]], axis=1)
            w = jnp.concatenate([vbd, sb_buf[u, p]], axis=0)
            out_refs[d][rows, p * _PV:(p + 1) * _PV] = _dot(x, w).astype(out_refs[d].dtype)

    _software_pipeline(n_groups, GLA_PIPE_SLOTS, (prep, state, out))


def _gla(qkg, gf, gb, vg, s0, n_batch, seq):
    tm = GLA_TILE
    long_seq = s0 is not None
    if long_seq:
        nt = seq // tm
        grid = (n_batch, nt)
        fwd = lambda b, i: (b * nt + i, 0)
        bwd = lambda b, i: (b * nt + nt - 1 - i, 0)
        seq_chunks = tm // CHUNK
    else:
        per_step = tm // seq
        grid = (1, n_batch // per_step)
        fwd = bwd = lambda b, i: (i, 0)
        seq_chunks = seq // CHUNK
    in_specs = []
    args = []
    for m, gates in ((fwd, gf), (bwd, gb)):
        in_specs += [pl.BlockSpec((tm, 2 * GLA_K_WIDTH), m), pl.BlockSpec((tm, GLA_K_WIDTH), m),
                     pl.BlockSpec((tm, GLA_V_WIDTH), m)]
        args += [qkg, gates, vg]
    n_tok = n_batch * seq
    out_specs = [pl.BlockSpec((tm, GLA_V_WIDTH), fwd), pl.BlockSpec((tm, GLA_V_WIDTH), bwd)]
    out_shape = [jax.ShapeDtypeStruct((n_tok, GLA_V_WIDTH), BF16)] * 2
    st_dims = (2, N_GLA_HEADS, GLA_DK, GLA_DV)
    if long_seq:
        in_specs.append(pl.BlockSpec((1,) + st_dims, lambda b, i: (b, 0, 0, 0, 0)))
        args.append(s0)
    else:
        out_specs.append(pl.BlockSpec((per_step,) + st_dims, lambda b, i: (i, 0, 0, 0, 0)))
        out_shape.append(jax.ShapeDtypeStruct((n_batch,) + st_dims, F32))
    n = GLA_PIPE_SLOTS
    n_pairs = N_GLA_HEADS // 2
    return pl.pallas_call(
        functools.partial(_gla_kernel, seq_chunks=seq_chunks, has_init=long_seq, emit_state=not long_seq),
        grid=grid,
        in_specs=in_specs,
        out_specs=out_specs,
        out_shape=out_shape,
        scratch_shapes=[pltpu.VMEM((2, n_pairs, _PK, _PV), F32),
                        pltpu.VMEM((n, n_pairs, CHUNK, _PK), F32),
                        pltpu.VMEM((n, n_pairs, CHUNK, _PK), BF16),
                        pltpu.VMEM((n, CHUNK, GLA_K_WIDTH), F32),
                        pltpu.VMEM((n, CHUNK, GLA_K_WIDTH), BF16),
                        pltpu.VMEM((n, n_pairs, _PK, _PV), F32),
                        pltpu.VMEM((n, n_pairs, _PK, _PV), BF16),
                        pltpu.VMEM((n, n_pairs, _PK, _PK), F32)],
        compiler_params=_params("arbitrary", "arbitrary"),
        name="gla_long" if long_seq else "gla_short",
    )(*args)


def _merge_kernel(x_ref, mod_ref, oa_ref, za_ref, of_ref, ob_ref, zg_ref, gng_ref, wo_ref, fng_ref, y_ref):
    ya = oa_ref[...].astype(F32) * _silu(za_ref[...].astype(F32))
    og = of_ref[...].astype(F32) + ob_ref[...].astype(F32)
    zg = zg_ref[...].astype(F32)
    gng = gng_ref[...]
    parts = [ya]
    for h in range(N_GLA_HEADS):
        oh = og[:, h * GLA_DV:(h + 1) * GLA_DV]
        nh = oh * lax.rsqrt(jnp.mean(oh * oh, axis=-1, keepdims=True) + EPS)
        parts.append((nh * gng) * _silu(zg[:, h * GLA_DV:(h + 1) * GLA_DV]))
    yin = jnp.concatenate(parts, axis=-1).astype(BF16)
    gate = mod_ref[0][:, 2 * D_MODEL:3 * D_MODEL]
    y = x_ref[...] + gate * _dot(yin, wo_ref[...])
    y_ref[...] = (y * lax.rsqrt(jnp.mean(y * y, axis=-1, keepdims=True) + EPS)) * fng_ref[...]


def _merge(x2d, mod3, mod_row_of_tile, oa, za, of, ob, zg, gla_norm_g, w_out, final_norm_g, tm):
    n_tok = x2d.shape[0]
    row = lambda i: (i, 0)
    const = lambda i: (0, 0)
    wide = pl.BlockSpec((tm, GLA_V_WIDTH), row)
    return pl.pallas_call(
        _merge_kernel,
        grid=(n_tok // tm,),
        in_specs=[
            pl.BlockSpec((tm, D_MODEL), row),
            pl.BlockSpec((1, 1, 3 * D_MODEL), lambda i: (mod_row_of_tile(i), 0, 0)),
            wide, wide, wide, wide, wide,
            pl.BlockSpec((1, GLA_DV), const),
            pl.BlockSpec((D_MODEL, D_MODEL), const),
            pl.BlockSpec((1, D_MODEL), const),
        ],
        out_specs=pl.BlockSpec((tm, D_MODEL), row),
        out_shape=jax.ShapeDtypeStruct((n_tok, D_MODEL), F32),
        compiler_params=_params("arbitrary"),
        name="merge",
    )(x2d, mod3, oa, za, of, ob, zg, gla_norm_g, w_out, final_norm_g)


def _rope_tables(n_tok):
    t = np.arange(n_tok)
    rowp = (t // GRID_W).astype(np.float64)
    colp = (t % GRID_W).astype(np.float64)
    nf = HEAD_DIM // 4
    freqs = ROPE_BASE ** (-np.arange(nf, dtype=np.float64) / nf)
    lane = np.arange(LANES)
    pos = np.where(((lane % HEAD_DIM) < HEAD_DIM // 2)[None, :], rowp[:, None], colp[:, None])
    ang = pos * freqs[lane % nf][None, :]
    sign = np.where((lane % (2 * nf)) < nf, -1.0, 1.0)
    return jnp.asarray(np.cos(ang), F32), jnp.asarray(np.sin(ang) * sign[None, :], F32)


def kernel(x_prompt, x_sample, c, cache_k, cache_v, state_gla, c_ctx, w_mod, b_mod, norm_g, w_in,
           w_gk_f, b_gk_f, w_gk_b, b_gk_b, sink, gla_norm_g, w_out, final_norm_g):
    depth = w_in.shape[0]
    assert depth == 1, "single-layer step"
    l = 0
    bp, sp, _ = x_prompt.shape
    bs, ss, _ = x_sample.shape

    wl = w_in[l]
    w1 = jnp.concatenate([wl[:, :_ZG0], wl[:, _ZG0 + 2 * GATE_RANK:], wl[:, _ZG0:_ZG0 + 2 * GATE_RANK],
                          jnp.zeros((D_MODEL, LANES - 2 * GATE_RANK), F32)], axis=1).astype(BF16)
    wgk = jnp.zeros((LANES, 2 * GLA_K_WIDTH), F32)
    wgk = wgk.at[0:GATE_RANK, 0:GLA_K_WIDTH].set(w_gk_f[l])
    wgk = wgk.at[GATE_RANK:2 * GATE_RANK, GLA_K_WIDTH:].set(w_gk_b[l]).astype(BF16)
    bgk = jnp.concatenate([b_gk_f[l], b_gk_b[l]])[None, :]
    wo = w_out[l].astype(BF16)
    ng = norm_g[l][None, :]
    gng = gla_norm_g[l][None, :]
    fng = final_norm_g[None, :]
    sk = sink[l]

    n_rows = 8
    cvecs = jnp.concatenate([c_ctx[None, :], c, jnp.zeros((n_rows - 1 - bs, D_MODEL), F32)], axis=0)
    mod3 = _modulation(cvecs, w_mod[l], b_mod[l][None, :]).reshape(n_rows, 1, 3 * D_MODEL)
    tm = TOKEN_TILE

    xp2 = x_prompt.reshape(bp * sp, D_MODEL)
    ctx_row = lambda i: 0
    q, kvp, za, vg, zg, qkg, gf, gb, kt, vt = _project(xp2, mod3, ctx_row, ng, w1, wgk, bgk, None, tm, sp)
    oa = _context_attention(q, kvp, sk, sp)
    of, ob, st = _gla(qkg, gf, gb, vg, None, bp, sp)
    y_prompt = _merge(xp2, mod3, ctx_row, oa, za, of, ob, zg, gng, wo, fng, MERGE_TILE).reshape(bp, sp, D_MODEL)
    to_cache = lambda t: t.reshape(bp, 1, N_KV_HEADS, HEAD_DIM, sp).transpose(0, 1, 4, 2, 3)
    new_k, new_v = to_cache(kt), to_cache(vt)
    new_state = st[:, None]

    xs2 = x_sample.reshape(bs * ss, D_MODEL)
    tiles_per_seq = ss // tm
    lat_row = lambda i: 1 + i // tiles_per_seq
    q, kvp, za, vg, zg, qkg, gf, gb = _project(xs2, mod3, lat_row, ng, w1, wgk, bgk, _rope_tables(ss), tm, ss)
    past = cache_k.shape[2]
    from_cache = lambda t: t[:, l].transpose(0, 2, 3, 1).reshape(bs, KV_WIDTH, past)
    ck, cv = from_cache(cache_k), from_cache(cache_v)
    oa = _latent_attention(q, kvp, ck, cv, sk, bs, ss)
    of, ob = _gla(qkg, gf, gb, vg, state_gla[:, l], bs, ss)
    merge_row = lambda i: 1 + i // (ss // MERGE_TILE)
    y_sample = _merge(xs2, mod3, merge_row, oa, za, of, ob, zg, gng, wo, fng, MERGE_TILE).reshape(bs, ss, D_MODEL)

    return (y_prompt, y_sample, new_k, new_v, new_state)
```

```python
import functools
import math

import numpy as np
import jax
import jax.numpy as jnp
from jax import lax
from jax.experimental import pallas as pl
from jax.experimental.pallas import tpu as pltpu

D_MODEL = 1024
GRID_W = 64
HEAD_DIM = 64
N_ATTN_HEADS = 8
N_KV_HEADS = 2
ATTN_WIDTH = 512
KV_WIDTH = 128
BLOCK = 128
N_GLA_HEADS = 4
GLA_DK = 64
GLA_DV = 128
GLA_K_WIDTH = 256
GLA_V_WIDTH = 512
GATE_RANK = 16
GATE_NORMALIZER = 16.0
CHUNK = 64
ROPE_BASE = 10000.0
EPS = 1e-6
NEG_INF = -1e30

LANES = 128
TOKEN_TILE = 512
MERGE_TILE = 1024
GLA_TILE = 2048
ATTN_TILE = 2048
CTX_ATTN_BATCH = 8
PIPE_SLOTS = 4
GLA_PIPE_SLOTS = 4
KVP_WIDTH = 8 * LANES
N_PLACEMENTS = 2 * N_KV_HEADS
SOFTMAX_ROWS = 32
MOD_TILE = 512
VMEM_LIMIT = 56 * 1024 * 1024
LOG2E = math.log2(math.e)

F32 = jnp.float32
BF16 = jnp.bfloat16


def _dot(a, b):
    return jnp.dot(a, b, preferred_element_type=F32)


def _dot_nt(a, b):
    return lax.dot_general(a, b, (((1,), (1,)), ((), ())), preferred_element_type=F32)


def _dot_tn(a, b):
    return lax.dot_general(a, b, (((0,), (0,)), ((), ())), preferred_element_type=F32)


def _silu(x):
    h = 0.5 * x
    return h + h * jnp.tanh(h)


def _log_sigmoid(x):
    return jnp.minimum(x, 0.0) - jnp.log1p(jnp.exp(-jnp.abs(x)))


def _params(*sem):
    return pltpu.CompilerParams(dimension_semantics=sem, vmem_limit_bytes=VMEM_LIMIT)


def _split_bf16(x):
    hi = x.astype(BF16)
    return hi, (x - hi.astype(F32)).astype(BF16)


def _mod_kernel(c_ref, w_ref, b_ref, o_ref):
    rows = c_ref.shape[0]
    s_hi, s_lo = _split_bf16(_silu(c_ref[...]))
    w_hi, w_lo = _split_bf16(w_ref[...])
    by_hi = _dot(jnp.concatenate([s_hi, s_lo], axis=0), w_hi)
    o_ref[...] = by_hi[0:rows] + by_hi[rows:2 * rows] + _dot(s_hi, w_lo) + b_ref[...]


def _modulation(cvecs, w_mod, b_mod):
    rows = cvecs.shape[0]
    n_out = w_mod.shape[1]
    return pl.pallas_call(
        _mod_kernel,
        grid=(n_out // MOD_TILE,),
        in_specs=[
            pl.BlockSpec((rows, D_MODEL), lambda j: (0, 0)),
            pl.BlockSpec((D_MODEL, MOD_TILE), lambda j: (0, j)),
            pl.BlockSpec((1, MOD_TILE), lambda j: (0, j)),
        ],
        out_specs=pl.BlockSpec((rows, MOD_TILE), lambda j: (0, j)),
        out_shape=jax.ShapeDtypeStruct((rows, n_out), F32),
        compiler_params=_params("arbitrary"),
        name="adaln_mod",
    )(cvecs, w_mod, b_mod)


_Q0, _K0, _V0, _ZA0, _QG0, _KG0, _VG0, _TAIL0 = 0, 512, 640, 768, 1280, 1536, 1792, 2304
_TAIL_COLS = 2 * GATE_RANK + GLA_V_WIDTH
_Q_PRESCALE = (HEAD_DIM ** -0.5) * LOG2E


def _rope(x, cos, sin_signed):
    n = x.shape[1] // LANES
    lane = lax.broadcasted_iota(jnp.int32, (x.shape[0], LANES), 1)
    first = (lane % 32) < 16
    outs = []
    for c in range(n):
        xc = x[:, c * LANES:(c + 1) * LANES]
        partner = jnp.where(first, pltpu.roll(xc, LANES - 16, 1), pltpu.roll(xc, 16, 1))
        outs.append(xc * cos + partner * sin_signed)
    return outs[0] if n == 1 else jnp.concatenate(outs, axis=-1)


def _kv_placements(k, v):
    lane = lax.broadcasted_iota(jnp.int32, k.shape, 1)
    lo = lane < HEAD_DIM
    kr = pltpu.roll(k, HEAD_DIM, 1)
    vr = pltpu.roll(v, HEAD_DIM, 1)
    pieces = [jnp.where(lo, k, 0.0), jnp.where(lo, 0.0, kr), jnp.where(lo, kr, 0.0), jnp.where(lo, 0.0, k),
              jnp.where(lo, v, 1.0), jnp.where(lo, 1.0, vr), jnp.where(lo, vr, 1.0), jnp.where(lo, 1.0, v)]
    return jnp.concatenate(pieces, axis=-1).astype(BF16)


def _proj_kernel(*refs, rope):
    if rope:
        (x_ref, mod_ref, ng_ref, w1_ref, wtail_ref, wgk_ref, bgk_ref, cos_ref, sin_ref,
         q_ref, kvp_ref, za_ref, vg_ref, zg_ref, qkg_ref, gf_ref, gb_ref) = refs
    else:
        (x_ref, mod_ref, ng_ref, w1_ref, wtail_ref, wgk_ref, bgk_ref,
         q_ref, kvp_ref, za_ref, vg_ref, zg_ref, qkg_ref, gf_ref, gb_ref, k_ref, v_ref) = refs
    x = x_ref[...]
    xn = x * lax.rsqrt(jnp.mean(x * x, axis=-1, keepdims=True) + EPS)
    mod = mod_ref[0]
    shift = mod[:, 0:D_MODEL]
    scale = mod[:, D_MODEL:2 * D_MODEL]
    h = ((xn * ng_ref[...]) * (1.0 + scale) + shift).astype(BF16)

    def piece(c0, c1):
        return _dot(h, w1_ref[:, c0:c1])

    q = piece(_Q0, _K0)
    kv = piece(_K0, _ZA0)
    k = kv[:, 0:KV_WIDTH]
    v = kv[:, KV_WIDTH:2 * KV_WIDTH]
    if rope:
        cos = cos_ref[...]
        sin = sin_ref[...]
        q = _rope(q, cos, sin)
        k = _rope(k, cos, sin)
    else:
        seq = k_ref.shape[2]
        for b in range(k_ref.shape[0]):
            k_ref[b] = k[b * seq:(b + 1) * seq, :].T
            v_ref[b] = v[b * seq:(b + 1) * seq, :].T
    q_ref[...] = (q * _Q_PRESCALE).astype(BF16)
    kvp_ref[...] = _kv_placements(k, v)
    za_ref[...] = piece(_ZA0, _QG0).astype(BF16)
    vg_ref[...] = piece(_VG0, _TAIL0).astype(BF16)
    qkg_ref[...] = piece(_QG0, _VG0)
    tail = _dot(h, wtail_ref[...])
    zg_ref[...] = tail[:, 2 * GATE_RANK:_TAIL_COLS].astype(BF16)
    lane = lax.broadcasted_iota(jnp.int32, (tail.shape[0], LANES), 1)
    lr = jnp.where(lane < 2 * GATE_RANK, tail[:, 0:LANES], 0.0).astype(BF16)
    g = _log_sigmoid(_dot(lr, wgk_ref[...]) + bgk_ref[...]) * (1.0 / GATE_NORMALIZER)
    gf_ref[...] = g[:, 0:GLA_K_WIDTH]
    gb_ref[...] = g[:, GLA_K_WIDTH:2 * GLA_K_WIDTH]


def _project(x2d, mod3, mod_row_of_tile, norm_g, w1, wlr, wgk, bgk, rope_tabs, tm, seq):
    n_tok = x2d.shape[0]
    rope = rope_tabs is not None
    row = lambda i: (i, 0)
    const = lambda i: (0, 0)
    in_specs = [
        pl.BlockSpec((tm, D_MODEL), row),
        pl.BlockSpec((1, 1, 3 * D_MODEL), lambda i: (mod_row_of_tile(i), 0, 0)),
        pl.BlockSpec((1, D_MODEL), const),
        pl.BlockSpec((D_MODEL, _TAIL0), const),
        pl.BlockSpec((D_MODEL, _TAIL_COLS), const),
        pl.BlockSpec((LANES, 2 * GLA_K_WIDTH), const),
        pl.BlockSpec((1, 2 * GLA_K_WIDTH), const),
    ]
    args = [x2d, mod3, norm_g, w1, wlr, wgk, bgk]
    if rope:
        tiles_per_seq = rope_tabs[0].shape[0] // tm
        pos = lambda i: (i % tiles_per_seq, 0)
        in_specs += [pl.BlockSpec((tm, LANES), pos), pl.BlockSpec((tm, LANES), pos)]
        args += list(rope_tabs)
    widths = [(ATTN_WIDTH, BF16), (KVP_WIDTH, BF16), (ATTN_WIDTH, BF16), (GLA_V_WIDTH, BF16), (GLA_V_WIDTH, BF16),
              (2 * GLA_K_WIDTH, F32), (GLA_K_WIDTH, F32), (GLA_K_WIDTH, F32)]
    out_specs = [pl.BlockSpec((tm, w), row) for w, _ in widths]
    out_shape = [jax.ShapeDtypeStruct((n_tok, w), dt) for w, dt in widths]
    if not rope:
        for _ in range(2):
            out_specs.append(pl.BlockSpec((tm // seq, KV_WIDTH, seq), lambda i: (i, 0, 0)))
            out_shape.append(jax.ShapeDtypeStruct((n_tok // seq, KV_WIDTH, seq), F32))
    return pl.pallas_call(
        functools.partial(_proj_kernel, rope=rope),
        grid=(n_tok // tm,),
        in_specs=in_specs,
        out_specs=out_specs,
        out_shape=out_shape,
        compiler_params=_params("arbitrary"),
        name="project_rope" if rope else "project",
    )(*args)


def _software_pipeline(n_groups, n_slots, stages):
    last_g = n_groups - 1

    def body(k, carry):
        for age in (2, 1, 0):
            for u in range(n_slots):
                g = n_slots * (k - age) + u
                stages[age](jnp.clip(g, 0, last_g), u, jnp.logical_and(g >= 0, g <= last_g))
        return carry

    lax.fori_loop(0, n_groups // n_slots + 2, body, 0)


def _init_pipeline_scratch(first, zero_refs, one_refs=()):
    @pl.when(first)
    def _():
        for ref in zero_refs:
            ref[...] = jnp.zeros_like(ref)
        for ref in one_refs:
            ref[...] = jnp.ones_like(ref)


def _key_rows(start, size):
    return pl.ds(pl.multiple_of(start, BLOCK), size)


def _softmax_rows(s_buf, p_buf, t_buf, slot, n_rows, n_keys, masks_of, sink_of):
    for r0 in range(0, n_rows, SOFTMAX_ROWS):
        rs = slice(r0, r0 + SOFTMAX_ROWS)
        masks = masks_of(r0)
        cols = []
        for c in range(n_keys // LANES):
            sc = s_buf[slot, rs, c * LANES:(c + 1) * LANES]
            cols.append(jnp.where(masks[c], sc, NEG_INF) if c in masks else sc)
        mx = cols[0]
        for sc in cols[1:]:
            mx = jnp.maximum(mx, sc)
        sink = sink_of(r0)
        m = jnp.maximum(jnp.max(mx, axis=-1, keepdims=True), sink)
        for c, sc in enumerate(cols):
            p_buf[slot, rs, c * LANES:(c + 1) * LANES] = jnp.exp2(sc - m).astype(BF16)
        t_buf[slot, rs, :] = jnp.broadcast_to(jnp.exp2(sink - m), (SOFTMAX_ROWS, LANES))


def _normalize(acc, t_buf, slot):
    return acc / (pltpu.roll(acc, HEAD_DIM, 1) + t_buf[slot])


def _attn_scratch(n_groups, n_keys):
    rows = 2 * BLOCK
    return [pltpu.VMEM((n_groups // 2, rows, LANES), BF16),
            pltpu.VMEM((PIPE_SLOTS, rows, n_keys), F32), pltpu.VMEM((PIPE_SLOTS, rows, n_keys), BF16),
            pltpu.VMEM((PIPE_SLOTS, rows, LANES), F32), pltpu.VMEM((n_groups, rows, LANES), F32)]


def _stack_queries(q_ref, q2s, n_blocks):
    for blk in range(n_blocks):
        rows = slice(blk * BLOCK, (blk + 1) * BLOCK)
        for j in range(N_KV_HEADS):
            idx = blk * N_KV_HEADS + j
            q2s[idx, 0:BLOCK, :] = q_ref[rows, (2 * j) * LANES:(2 * j + 1) * LANES]
            q2s[idx, BLOCK:2 * BLOCK, :] = q_ref[rows, (2 * j + 1) * LANES:(2 * j + 2) * LANES]


def _write_heads(o_buf, o_ref, n_blocks):
    lo = lax.broadcasted_iota(jnp.int32, (2 * BLOCK, LANES), 1) < HEAD_DIM
    for blk in range(n_blocks):
        rows = slice(blk * BLOCK, (blk + 1) * BLOCK)
        for j in range(N_KV_HEADS):
            g0 = blk * N_PLACEMENTS + 2 * j
            o = jnp.where(lo, o_buf[g0], o_buf[g0 + 1]).astype(o_ref.dtype)
            o_ref[rows, (2 * j) * LANES:(2 * j + 1) * LANES] = o[0:BLOCK]
            o_ref[rows, (2 * j + 1) * LANES:(2 * j + 2) * LANES] = o[BLOCK:2 * BLOCK]


def _group_sinks(sink_ref, g):
    j, e = (g // 2) % N_KV_HEADS, g % 2
    sink_a = sink_ref[4 * j + e] * LOG2E
    sink_b = sink_ref[4 * j + 2 + e] * LOG2E
    return lambda r0: sink_a if r0 < BLOCK else sink_b


def _ctx_attn_kernel(sink_ref, q_ref, kvp_ref, o_ref, kown, vown, q2s, s_buf, p_buf, t_buf, o_buf, *, seq):
    n_blocks = q_ref.shape[0] // BLOCK
    _init_pipeline_scratch(pl.program_id(0) == 0, (s_buf, p_buf), (t_buf,))
    for idx in range(N_PLACEMENTS):
        kown[idx] = kvp_ref[:, idx * LANES:(idx + 1) * LANES]
        vown[idx] = kvp_ref[:, (N_PLACEMENTS + idx) * LANES:(N_PLACEMENTS + idx + 1) * LANES]
    _stack_queries(q_ref, q2s, n_blocks)
    blocks_per_seq = seq // BLOCK

    def keys_of(g):
        return _key_rows(((g // N_PLACEMENTS) // blocks_per_seq) * seq, seq)

    def qk(g, u, valid):
        s_buf[u] = _dot_nt(q2s[g // 2], kown[g % N_PLACEMENTS, keys_of(g), :])

    def softmax(g, u, valid):
        _softmax_rows(s_buf, p_buf, t_buf, u, 2 * BLOCK, seq, lambda r0: {}, _group_sinks(sink_ref, g))

    def pv(g, u, valid):
        o_buf[g] = _normalize(_dot(p_buf[u], vown[g % N_PLACEMENTS, keys_of(g), :]), t_buf, u)

    _software_pipeline(n_blocks * N_PLACEMENTS, PIPE_SLOTS, (qk, softmax, pv))
    _write_heads(o_buf, o_ref, n_blocks)


def _context_attention(q, kvp, sink, seq):
    n_tok = q.shape[0]
    rows = CTX_ATTN_BATCH * seq
    row = lambda b: (b, 0)
    n_groups = (rows // BLOCK) * N_PLACEMENTS
    return pl.pallas_call(
        functools.partial(_ctx_attn_kernel, seq=seq),
        grid=(n_tok // rows,),
        in_specs=[
            pl.BlockSpec(memory_space=pltpu.SMEM),
            pl.BlockSpec((rows, ATTN_WIDTH), row),
            pl.BlockSpec((rows, KVP_WIDTH), row),
        ],
        out_specs=pl.BlockSpec((rows, ATTN_WIDTH), row),
        out_shape=jax.ShapeDtypeStruct((n_tok, ATTN_WIDTH), BF16),
        scratch_shapes=[pltpu.VMEM((N_PLACEMENTS, rows, LANES), BF16)] * 2 + _attn_scratch(n_groups, seq),
        compiler_params=_params("arbitrary"),
        name="context_attention",
    )(sink, q, kvp)


def _lat_attn_kernel(sink_ref, q_ref, kvp_prev, kvp_own, kvp_next, ck_ref, cv_ref, o_ref,
                     kwin, vwin, kctx, vctx, q2s, s_buf, p_buf, t_buf, o_buf, *, n_steps, past):
    i = pl.program_id(1)
    n_blocks = q_ref.shape[0] // BLOCK
    own_end = BLOCK + q_ref.shape[0]
    win = 3 * BLOCK
    _init_pipeline_scratch(jnp.logical_and(pl.program_id(0) == 0, i == 0), (s_buf, p_buf), (t_buf,))

    @pl.when(i == 0)
    def _():
        cpl = _kv_placements(ck_ref[0].T, cv_ref[0].T)
        for idx in range(N_PLACEMENTS):
            kctx[idx] = cpl[:, idx * LANES:(idx + 1) * LANES]
            vctx[idx] = cpl[:, (N_PLACEMENTS + idx) * LANES:(N_PLACEMENTS + idx + 1) * LANES]

    for src, r0, r1 in ((kvp_prev, 0, BLOCK), (kvp_own, BLOCK, own_end), (kvp_next, own_end, own_end + BLOCK)):
        for idx in range(N_PLACEMENTS):
            kwin[idx, r0:r1, :] = src[:, idx * LANES:(idx + 1) * LANES]
            vwin[idx, r0:r1, :] = src[:, (N_PLACEMENTS + idx) * LANES:(N_PLACEMENTS + idx + 1) * LANES]
    _stack_queries(q_ref, q2s, n_blocks)

    row = lax.broadcasted_iota(jnp.int32, (SOFTMAX_ROWS, LANES), 0)
    c = lax.broadcasted_iota(jnp.int32, (SOFTMAX_ROWS, LANES), 1)
    off_prev = jnp.where(i > 0, 0, BLOCK)
    off_next = jnp.where(i < n_steps - 1, 0, BLOCK)

    def window_of(g):
        return _key_rows((g // N_PLACEMENTS) * BLOCK, win)

    def qk(g, u, valid):
        q2 = q2s[g // 2]
        s_buf[u, :, 0:win] = _dot_nt(q2, kwin[g % N_PLACEMENTS, window_of(g), :])
        s_buf[u, :, win:win + past] = _dot_nt(q2, kctx[g % N_PLACEMENTS])

    def softmax(g, u, valid):
        blk = g // N_PLACEMENTS
        off0 = jnp.where(blk == 0, off_prev, 0)
        off2 = jnp.where(blk == n_blocks - 1, off_next, 0)

        def masks_of(r0):
            r = row + r0 % BLOCK
            return {0: c >= r + off0, 2: c <= r - off2}

        _softmax_rows(s_buf, p_buf, t_buf, u, 2 * BLOCK, win + past, masks_of, _group_sinks(sink_ref, g))

    def pv(g, u, valid):
        acc = _dot(p_buf[u, :, 0:win], vwin[g % N_PLACEMENTS, window_of(g), :])
        acc = acc + _dot(p_buf[u, :, win:win + past], vctx[g % N_PLACEMENTS])
        o_buf[g] = _normalize(acc, t_buf, u)

    _software_pipeline(n_blocks * N_PLACEMENTS, PIPE_SLOTS, (qk, softmax, pv))
    _write_heads(o_buf, o_ref, n_blocks)


def _latent_attention(q, kvp, ck, cv, sink, n_batch, seq):
    ns = seq // ATTN_TILE
    nb = seq // BLOCK
    per = ATTN_TILE // BLOCK
    own = lambda b, i: (b * ns + i, 0)
    prev = lambda b, i: (b * nb + jnp.maximum(per * i - 1, 0), 0)
    nxt = lambda b, i: (b * nb + jnp.minimum(per * i + per, nb - 1), 0)
    ctx = lambda b, i: (b, 0, 0)
    past = ck.shape[2]
    win_rows = ATTN_TILE + 2 * BLOCK
    bf = lambda *shape: pltpu.VMEM(shape, BF16)
    return pl.pallas_call(
        functools.partial(_lat_attn_kernel, n_steps=ns, past=past),
        grid=(n_batch, ns),
        in_specs=[
            pl.BlockSpec(memory_space=pltpu.SMEM),
            pl.BlockSpec((ATTN_TILE, ATTN_WIDTH), own),
            pl.BlockSpec((BLOCK, KVP_WIDTH), prev),
            pl.BlockSpec((ATTN_TILE, KVP_WIDTH), own),
            pl.BlockSpec((BLOCK, KVP_WIDTH), nxt),
            pl.BlockSpec((1, KV_WIDTH, past), ctx),
            pl.BlockSpec((1, KV_WIDTH, past), ctx),
        ],
        out_specs=pl.BlockSpec((ATTN_TILE, ATTN_WIDTH), own),
        out_shape=jax.ShapeDtypeStruct((n_batch * seq, ATTN_WIDTH), BF16),
        scratch_shapes=[bf(N_PLACEMENTS, win_rows, LANES), bf(N_PLACEMENTS, win_rows, LANES),
                        bf(N_PLACEMENTS, past, LANES), bf(N_PLACEMENTS, past, LANES)]
        + _attn_scratch(per * N_PLACEMENTS, 3 * BLOCK + past),
        compiler_params=_params("arbitrary", "arbitrary"),
        name="latent_attention",
    )(sink, q, kvp, kvp, kvp, ck, cv)


_PK = 2 * GLA_DK
_PV = 2 * GLA_DV


def _cumsum_rows(x, reverse):
    n = x.shape[0]
    row = lax.broadcasted_iota(jnp.int32, x.shape, 0)
    s = 1
    while s < n:
        if reverse:
            x = x + jnp.where(row < n - s, pltpu.roll(x, n - s, 0), 0.0)
        else:
            x = x + jnp.where(row >= s, pltpu.roll(x, s, 0), 0.0)
        s *= 2
    return x


def _gla_kernel(*refs, seq_chunks, has_init, emit_state):
    it = iter(refs)
    views = [tuple(next(it) for _ in range(3)) for _ in range(2)]
    s0_ref = next(it) if has_init else None
    out_refs = (next(it), next(it))
    sout_ref = next(it) if emit_state else None
    st_ref, a_buf, a2_buf, qe_buf, qd_buf, ds_buf, sb_buf, dc_buf = (next(it) for _ in range(8))
    n_chunks = views[0][0].shape[0] // CHUNK
    n_groups = 2 * n_chunks
    n_pairs = N_GLA_HEADS // 2
    tile = pl.program_id(1)
    q_scale = GLA_DK ** -0.5
    _init_pipeline_scratch(jnp.logical_and(pl.program_id(0) == 0, tile == 0),
                           (st_ref, a_buf, a2_buf, qe_buf, qd_buf, ds_buf, sb_buf), (dc_buf,))

    if has_init:
        @pl.when(tile == 0)
        def _():
            st_ref[...] = jnp.zeros_like(st_ref)
            for d in range(2):
                for h in range(N_GLA_HEADS):
                    p, e = divmod(h, 2)
                    st_ref[d, p, e * GLA_DK:(e + 1) * GLA_DK, e * GLA_DV:(e + 1) * GLA_DV] = s0_ref[0, d, h]

    lane_k = lax.broadcasted_iota(jnp.int32, (CHUNK, _PK), 1)
    row_k = lax.broadcasted_iota(jnp.int32, (CHUNK, _PK), 0)
    lo_k = lane_k < GLA_DK
    tri = (lane_k % CHUNK <= row_k, lane_k % CHUNK >= row_k)
    lo_v = lax.broadcasted_iota(jnp.int32, (CHUNK, _PV), 1) < GLA_DV
    diag = ((lax.broadcasted_iota(jnp.int32, (_PK, _PV), 0) < GLA_DK)
            == (lax.broadcasted_iota(jnp.int32, (_PK, _PV), 1) < GLA_DV))

    def place(g, d):
        c = g // 2
        if d == 0:
            return c, c
        seq = c // seq_chunks
        return c, seq * seq_chunks + (seq_chunks - 1 - (c - seq * seq_chunks))

    def rows_of(pos):
        return pl.ds(pl.multiple_of(pos * CHUNK, CHUNK), CHUNK)

    def prep(g, u, valid):
        d = u % 2
        qk_ref, g_ref, v_ref = views[d]
        _, pos = place(g, d)
        rows = rows_of(pos)
        gc = _cumsum_rows(g_ref[rows, :], reverse=(d == 1))
        gtot = gc[0:1, :] if d == 1 else gc[CHUNK - 1:CHUNK, :]
        k = qk_ref[rows, GLA_K_WIDTH:2 * GLA_K_WIDTH]
        qe = qk_ref[rows, 0:GLA_K_WIDTH] * jnp.exp(gc)
        qe_buf[u] = qe
        q_dec = (qe * q_scale).astype(BF16)
        k_inv = k * jnp.exp(-gc)
        k_end = jnp.where(valid, k * jnp.exp(gtot - gc), 0.0).astype(BF16)
        decay = jnp.where(valid, jnp.exp(gtot), 1.0)
        for p in range(n_pairs):
            lanes = slice(p * _PK, (p + 1) * _PK)
            ki = k_inv[:, lanes]
            kst = jnp.concatenate([jnp.where(lo_k, ki, 0.0), jnp.where(lo_k, 0.0, ki)], axis=0).astype(BF16)
            a_buf[u, p] = _dot_nt(q_dec[:, lanes], kst)
            ds_buf[u, p] = jnp.where(diag, _dot_tn(k_end[:, lanes], v_ref[rows, p * _PV:(p + 1) * _PV]), 0.0)
            dc_buf[u, p] = jnp.broadcast_to(decay[:, lanes], (_PK, _PK)).T

    def state(g, u, valid):
        d = u % 2
        c, _ = place(g, d)
        qd_buf[u] = (qe_buf[u] * q_scale).astype(BF16)
        for p in range(n_pairs):
            fresh = (c % seq_chunks == 0) if emit_state else (c < 0)
            s_in = jnp.where(fresh, 0.0, st_ref[d, p])
            sb_buf[u, p] = s_in.astype(BF16)
            a2_buf[u, p] = jnp.where(tri[d], a_buf[u, p], 0.0).astype(BF16)
            dc = dc_buf[u, p]
            s_new = s_in * jnp.concatenate([dc, dc], axis=1) + ds_buf[u, p]
            st_ref[d, p] = s_new
            if emit_state:
                for e in range(2):
                    sout_ref[c // seq_chunks, d, 2 * p + e] = s_new[e * GLA_DK:(e + 1) * GLA_DK,
                                                                    e * GLA_DV:(e + 1) * GLA_DV]

    def out(g, u, valid):
        d = u % 2
        _, _, v_ref = views[d]
        _, pos = place(g, d)
        rows = rows_of(pos)
        for p in range(n_pairs):
            vpf = v_ref[rows, p * _PV:(p + 1) * _PV].astype(F32)
            vbd = jnp.concatenate([jnp.where(lo_v, vpf, 0.0), jnp.where(lo_v, 0.0, vpf)], axis=0).astype(BF16)
            x = jnp.concatenate([a2_buf[u, p], qd_buf[u, :, p * _PK:(p + 1) * _PK]], axis=1)
            w = jnp.concatenate([vbd, sb_buf[u, p]], axis=0)
            out_refs[d][rows, p * _PV:(p + 1) * _PV] = _dot(x, w).astype(out_refs[d].dtype)

    _software_pipeline(n_groups, GLA_PIPE_SLOTS, (prep, state, out))


def _gla(qkg, gf, gb, vg, s0, n_batch, seq):
    tm = GLA_TILE
    long_seq = s0 is not None
    if long_seq:
        nt = seq // tm
        grid = (n_batch, nt)
        fwd = lambda b, i: (b * nt + i, 0)
        bwd = lambda b, i: (b * nt + nt - 1 - i, 0)
        seq_chunks = tm // CHUNK
    else:
        per_step = tm // seq
        grid = (1, n_batch // per_step)
        fwd = bwd = lambda b, i: (i, 0)
        seq_chunks = seq // CHUNK
    in_specs = []
    args = []
    for m, gates in ((fwd, gf), (bwd, gb)):
        in_specs += [pl.BlockSpec((tm, 2 * GLA_K_WIDTH), m), pl.BlockSpec((tm, GLA_K_WIDTH), m),
                     pl.BlockSpec((tm, GLA_V_WIDTH), m)]
        args += [qkg, gates, vg]
    n_tok = n_batch * seq
    out_specs = [pl.BlockSpec((tm, GLA_V_WIDTH), fwd), pl.BlockSpec((tm, GLA_V_WIDTH), bwd)]
    out_shape = [jax.ShapeDtypeStruct((n_tok, GLA_V_WIDTH), BF16)] * 2
    st_dims = (2, N_GLA_HEADS, GLA_DK, GLA_DV)
    if long_seq:
        in_specs.append(pl.BlockSpec((1,) + st_dims, lambda b, i: (b, 0, 0, 0, 0)))
        args.append(s0)
    else:
        out_specs.append(pl.BlockSpec((per_step,) + st_dims, lambda b, i: (i, 0, 0, 0, 0)))
        out_shape.append(jax.ShapeDtypeStruct((n_batch,) + st_dims, F32))
    n = GLA_PIPE_SLOTS
    n_pairs = N_GLA_HEADS // 2
    return pl.pallas_call(
        functools.partial(_gla_kernel, seq_chunks=seq_chunks, has_init=long_seq, emit_state=not long_seq),
        grid=grid,
        in_specs=in_specs,
        out_specs=out_specs,
        out_shape=out_shape,
        scratch_shapes=[pltpu.VMEM((2, n_pairs, _PK, _PV), F32),
                        pltpu.VMEM((n, n_pairs, CHUNK, _PK), F32),
                        pltpu.VMEM((n, n_pairs, CHUNK, _PK), BF16),
                        pltpu.VMEM((n, CHUNK, GLA_K_WIDTH), F32),
                        pltpu.VMEM((n, CHUNK, GLA_K_WIDTH), BF16),
                        pltpu.VMEM((n, n_pairs, _PK, _PV), F32),
                        pltpu.VMEM((n, n_pairs, _PK, _PV), BF16),
                        pltpu.VMEM((n, n_pairs, _PK, _PK), F32)],
        compiler_params=_params("arbitrary", "arbitrary"),
        name="gla_long" if long_seq else "gla_short",
    )(*args)


def _merge_kernel(x_ref, mod_ref, oa_ref, za_ref, of_ref, ob_ref, zg_ref, gng_ref, wo_ref, fng_ref, y_ref):
    ya = oa_ref[...].astype(F32) * _silu(za_ref[...].astype(F32))
    og = of_ref[...].astype(F32) + ob_ref[...].astype(F32)
    zg = zg_ref[...].astype(F32)
    gng = gng_ref[...]
    parts = [ya]
    for h in range(N_GLA_HEADS):
        oh = og[:, h * GLA_DV:(h + 1) * GLA_DV]
        nh = oh * lax.rsqrt(jnp.mean(oh * oh, axis=-1, keepdims=True) + EPS)
        parts.append((nh * gng) * _silu(zg[:, h * GLA_DV:(h + 1) * GLA_DV]))
    yin = jnp.concatenate(parts, axis=-1).astype(BF16)
    gate = mod_ref[0][:, 2 * D_MODEL:3 * D_MODEL]
    y = x_ref[...] + gate * _dot(yin, wo_ref[...])
    y_ref[...] = (y * lax.rsqrt(jnp.mean(y * y, axis=-1, keepdims=True) + EPS)) * fng_ref[...]


def _merge(x2d, mod3, mod_row_of_tile, oa, za, of, ob, zg, gla_norm_g, w_out, final_norm_g, tm):
    n_tok = x2d.shape[0]
    row = lambda i: (i, 0)
    const = lambda i: (0, 0)
    wide = pl.BlockSpec((tm, GLA_V_WIDTH), row)
    return pl.pallas_call(
        _merge_kernel,
        grid=(n_tok // tm,),
        in_specs=[
            pl.BlockSpec((tm, D_MODEL), row),
            pl.BlockSpec((1, 1, 3 * D_MODEL), lambda i: (mod_row_of_tile(i), 0, 0)),
            wide, wide, wide, wide, wide,
            pl.BlockSpec((1, GLA_DV), const),
            pl.BlockSpec((D_MODEL, D_MODEL), const),
            pl.BlockSpec((1, D_MODEL), const),
        ],
        out_specs=pl.BlockSpec((tm, D_MODEL), row),
        out_shape=jax.ShapeDtypeStruct((n_tok, D_MODEL), F32),
        compiler_params=_params("arbitrary"),
        name="merge",
    )(x2d, mod3, oa, za, of, ob, zg, gla_norm_g, w_out, final_norm_g)


def _rope_tables(n_tok):
    t = np.arange(n_tok)
    rowp = (t // GRID_W).astype(np.float64)
    colp = (t % GRID_W).astype(np.float64)
    nf = HEAD_DIM // 4
    freqs = ROPE_BASE ** (-np.arange(nf, dtype=np.float64) / nf)
    lane = np.arange(LANES)
    pos = np.where(((lane % HEAD_DIM) < HEAD_DIM // 2)[None, :], rowp[:, None], colp[:, None])
    ang = pos * freqs[lane % nf][None, :]
    sign = np.where((lane % (2 * nf)) < nf, -1.0, 1.0)
    return jnp.asarray(np.cos(ang), F32), jnp.asarray(np.sin(ang) * sign[None, :], F32)


def kernel(x_prompt, x_sample, c, cache_k, cache_v, state_gla, c_ctx, w_mod, b_mod, norm_g, w_in,
           w_gk_f, b_gk_f, w_gk_b, b_gk_b, sink, gla_norm_g, w_out, final_norm_g):
    depth = w_in.shape[0]
    assert depth == 1, "single-layer step"
    l = 0
    bp, sp, _ = x_prompt.shape
    bs, ss, _ = x_sample.shape

    wl = w_in[l]
    w1 = wl[:, :_TAIL0].astype(BF16)
    wlr = wl[:, _TAIL0:].astype(BF16)
    wgk = jnp.zeros((LANES, 2 * GLA_K_WIDTH), F32)
    wgk = wgk.at[0:GATE_RANK, 0:GLA_K_WIDTH].set(w_gk_f[l])
    wgk = wgk.at[GATE_RANK:2 * GATE_RANK, GLA_K_WIDTH:].set(w_gk_b[l]).astype(BF16)
    bgk = jnp.concatenate([b_gk_f[l], b_gk_b[l]])[None, :]
    wo = w_out[l].astype(BF16)
    ng = norm_g[l][None, :]
    gng = gla_norm_g[l][None, :]
    fng = final_norm_g[None, :]
    sk = sink[l]

    n_rows = 8
    cvecs = jnp.concatenate([c_ctx[None, :], c, jnp.zeros((n_rows - 1 - bs, D_MODEL), F32)], axis=0)
    mod3 = _modulation(cvecs, w_mod[l], b_mod[l][None, :]).reshape(n_rows, 1, 3 * D_MODEL)
    tm = TOKEN_TILE

    xp2 = x_prompt.reshape(bp * sp, D_MODEL)
    ctx_row = lambda i: 0
    q, kvp, za, vg, zg, qkg, gf, gb, kt, vt = _project(xp2, mod3, ctx_row, ng, w1, wlr, wgk, bgk, None, tm, sp)
    oa = _context_attention(q, kvp, sk, sp)
    of, ob, st = _gla(qkg, gf, gb, vg, None, bp, sp)
    y_prompt = _merge(xp2, mod3, ctx_row, oa, za, of, ob, zg, gng, wo, fng, MERGE_TILE).reshape(bp, sp, D_MODEL)
    to_cache = lambda t: t.reshape(bp, 1, N_KV_HEADS, HEAD_DIM, sp).transpose(0, 1, 4, 2, 3)
    new_k, new_v = to_cache(kt), to_cache(vt)
    new_state = st[:, None]

    xs2 = x_sample.reshape(bs * ss, D_MODEL)
    tiles_per_seq = ss // tm
    lat_row = lambda i: 1 + i // tiles_per_seq
    q, kvp, za, vg, zg, qkg, gf, gb = _project(xs2, mod3, lat_row, ng, w1, wlr, wgk, bgk, _rope_tables(ss), tm, ss)
    past = cache_k.shape[2]
    from_cache = lambda t: t[:, l].transpose(0, 2, 3, 1).reshape(bs, KV_WIDTH, past)
    ck, cv = from_cache(cache_k), from_cache(cache_v)
    oa = _latent_attention(q, kvp, ck, cv, sk, bs, ss)
    of, ob = _gla(qkg, gf, gb, vg, state_gla[:, l], bs, ss)
    merge_row = lambda i: 1 + i // (ss // MERGE_TILE)
    y_sample = _merge(xs2, mod3, merge_row, oa, za, of, ob, zg, gng, wo, fng, MERGE_TILE).reshape(bs, ss, D_MODEL)

    return (y_prompt, y_sample, new_k, new_v, new_state)
```

```python
import functools
import math

import numpy as np
import jax
import jax.numpy as jnp
from jax import lax
from jax.experimental import pallas as pl
from jax.experimental.pallas import tpu as pltpu

D_MODEL = 1024
GRID_W = 64
HEAD_DIM = 64
N_ATTN_HEADS = 8
N_KV_HEADS = 2
ATTN_WIDTH = 512
KV_WIDTH = 128
BLOCK = 128
N_GLA_HEADS = 4
GLA_DK = 64
GLA_DV = 128
GLA_K_WIDTH = 256
GLA_V_WIDTH = 512
GATE_RANK = 16
GATE_NORMALIZER = 16.0
CHUNK = 64
ROPE_BASE = 10000.0
EPS = 1e-6
NEG_INF = -1e30

LANES = 128
TOKEN_TILE = 1024
MERGE_TILE = 1024
GLA_TILE = 2048
ATTN_TILE = 2048
CTX_ATTN_BATCH = 8
PIPE_SLOTS = 4
GLA_PIPE_SLOTS = 4
KVP_WIDTH = 8 * LANES
N_PLACEMENTS = 2 * N_KV_HEADS
SOFTMAX_ROWS = 32
MOD_TILE = 512
VMEM_LIMIT = 56 * 1024 * 1024
LOG2E = math.log2(math.e)

F32 = jnp.float32
BF16 = jnp.bfloat16


def _dot(a, b):
    return jnp.dot(a, b, preferred_element_type=F32)


def _dot_nt(a, b):
    return lax.dot_general(a, b, (((1,), (1,)), ((), ())), preferred_element_type=F32)


def _dot_tn(a, b):
    return lax.dot_general(a, b, (((0,), (0,)), ((), ())), preferred_element_type=F32)


def _silu(x):
    h = 0.5 * x
    return h + h * jnp.tanh(h)


def _log_sigmoid(x):
    return jnp.minimum(x, 0.0) - jnp.log1p(jnp.exp(-jnp.abs(x)))


def _params(*sem):
    return pltpu.CompilerParams(dimension_semantics=sem, vmem_limit_bytes=VMEM_LIMIT)


def _split_bf16(x):
    hi = x.astype(BF16)
    return hi, (x - hi.astype(F32)).astype(BF16)


def _mod_kernel(c_ref, w_ref, b_ref, o_ref):
    rows = c_ref.shape[0]
    s_hi, s_lo = _split_bf16(_silu(c_ref[...]))
    w_hi, w_lo = _split_bf16(w_ref[...])
    by_hi = _dot(jnp.concatenate([s_hi, s_lo], axis=0), w_hi)
    o_ref[...] = by_hi[0:rows] + by_hi[rows:2 * rows] + _dot(s_hi, w_lo) + b_ref[...]


def _modulation(cvecs, w_mod, b_mod):
    rows = cvecs.shape[0]
    n_out = w_mod.shape[1]
    return pl.pallas_call(
        _mod_kernel,
        grid=(n_out // MOD_TILE,),
        in_specs=[
            pl.BlockSpec((rows, D_MODEL), lambda j: (0, 0)),
            pl.BlockSpec((D_MODEL, MOD_TILE), lambda j: (0, j)),
            pl.BlockSpec((1, MOD_TILE), lambda j: (0, j)),
        ],
        out_specs=pl.BlockSpec((rows, MOD_TILE), lambda j: (0, j)),
        out_shape=jax.ShapeDtypeStruct((rows, n_out), F32),
        compiler_params=_params("arbitrary"),
        name="adaln_mod",
    )(cvecs, w_mod, b_mod)


_Q0, _K0, _V0, _ZA0, _QG0, _KG0, _VG0, _TAIL0 = 0, 512, 640, 768, 1280, 1536, 1792, 2304
_TAIL_COLS = 2 * GATE_RANK + GLA_V_WIDTH
_Q_PRESCALE = (HEAD_DIM ** -0.5) * LOG2E


def _rope(x, cos, sin_signed):
    n = x.shape[1] // LANES
    lane = lax.broadcasted_iota(jnp.int32, (x.shape[0], LANES), 1)
    first = (lane % 32) < 16
    outs = []
    for c in range(n):
        xc = x[:, c * LANES:(c + 1) * LANES]
        partner = jnp.where(first, pltpu.roll(xc, LANES - 16, 1), pltpu.roll(xc, 16, 1))
        outs.append(xc * cos + partner * sin_signed)
    return outs[0] if n == 1 else jnp.concatenate(outs, axis=-1)


def _kv_placements(k, v):
    lane = lax.broadcasted_iota(jnp.int32, k.shape, 1)
    lo = lane < HEAD_DIM
    kr = pltpu.roll(k, HEAD_DIM, 1)
    vr = pltpu.roll(v, HEAD_DIM, 1)
    pieces = [jnp.where(lo, k, 0.0), jnp.where(lo, 0.0, kr), jnp.where(lo, kr, 0.0), jnp.where(lo, 0.0, k),
              jnp.where(lo, v, 1.0), jnp.where(lo, 1.0, vr), jnp.where(lo, vr, 1.0), jnp.where(lo, 1.0, v)]
    return jnp.concatenate(pieces, axis=-1).astype(BF16)


def _proj_kernel(*refs, rope):
    if rope:
        (x_ref, mod_ref, ng_ref, w1_ref, wtail_ref, wgk_ref, bgk_ref, cos_ref, sin_ref,
         q_ref, kvp_ref, za_ref, vg_ref, zg_ref, qkg_ref, gf_ref, gb_ref) = refs
    else:
        (x_ref, mod_ref, ng_ref, w1_ref, wtail_ref, wgk_ref, bgk_ref,
         q_ref, kvp_ref, za_ref, vg_ref, zg_ref, qkg_ref, gf_ref, gb_ref, k_ref, v_ref) = refs
    x = x_ref[...]
    xn = x * lax.rsqrt(jnp.mean(x * x, axis=-1, keepdims=True) + EPS)
    mod = mod_ref[0]
    shift = mod[:, 0:D_MODEL]
    scale = mod[:, D_MODEL:2 * D_MODEL]
    h = ((xn * ng_ref[...]) * (1.0 + scale) + shift).astype(BF16)

    def piece(c0, c1):
        return _dot(h, w1_ref[:, c0:c1])

    q = piece(_Q0, _K0)
    kv = piece(_K0, _ZA0)
    k = kv[:, 0:KV_WIDTH]
    v = kv[:, KV_WIDTH:2 * KV_WIDTH]
    if rope:
        cos = cos_ref[...]
        sin = sin_ref[...]
        q = _rope(q, cos, sin)
        k = _rope(k, cos, sin)
    else:
        seq = k_ref.shape[2]
        for b in range(k_ref.shape[0]):
            k_ref[b] = k[b * seq:(b + 1) * seq, :].T
            v_ref[b] = v[b * seq:(b + 1) * seq, :].T
    q_ref[...] = (q * _Q_PRESCALE).astype(BF16)
    kvp_ref[...] = _kv_placements(k, v)
    za_ref[...] = piece(_ZA0, _QG0).astype(BF16)
    vg_ref[...] = piece(_VG0, _TAIL0).astype(BF16)
    qkg_ref[...] = piece(_QG0, _VG0)
    tail = _dot(h, wtail_ref[...])
    zg_ref[...] = tail[:, 2 * GATE_RANK:_TAIL_COLS].astype(BF16)
    lane = lax.broadcasted_iota(jnp.int32, (tail.shape[0], LANES), 1)
    lr = jnp.where(lane < 2 * GATE_RANK, tail[:, 0:LANES], 0.0).astype(BF16)
    g = _log_sigmoid(_dot(lr, wgk_ref[...]) + bgk_ref[...]) * (1.0 / GATE_NORMALIZER)
    gf_ref[...] = g[:, 0:GLA_K_WIDTH]
    gb_ref[...] = g[:, GLA_K_WIDTH:2 * GLA_K_WIDTH]


def _project(x2d, mod3, mod_row_of_tile, norm_g, w1, wtail, wgk, bgk, rope_tabs, tm, seq):
    n_tok = x2d.shape[0]
    rope = rope_tabs is not None
    row = lambda i: (i, 0)
    const = lambda i: (0, 0)
    in_specs = [
        pl.BlockSpec((tm, D_MODEL), row),
        pl.BlockSpec((1, 1, 3 * D_MODEL), lambda i: (mod_row_of_tile(i), 0, 0)),
        pl.BlockSpec((1, D_MODEL), const),
        pl.BlockSpec((D_MODEL, _TAIL0), const),
        pl.BlockSpec((D_MODEL, _TAIL_COLS), const),
        pl.BlockSpec((LANES, 2 * GLA_K_WIDTH), const),
        pl.BlockSpec((1, 2 * GLA_K_WIDTH), const),
    ]
    args = [x2d, mod3, norm_g, w1, wtail, wgk, bgk]
    if rope:
        tiles_per_seq = rope_tabs[0].shape[0] // tm
        pos = lambda i: (i % tiles_per_seq, 0)
        in_specs += [pl.BlockSpec((tm, LANES), pos), pl.BlockSpec((tm, LANES), pos)]
        args += list(rope_tabs)
    widths = [(ATTN_WIDTH, BF16), (KVP_WIDTH, BF16), (ATTN_WIDTH, BF16), (GLA_V_WIDTH, BF16), (GLA_V_WIDTH, BF16),
              (2 * GLA_K_WIDTH, F32), (GLA_K_WIDTH, F32), (GLA_K_WIDTH, F32)]
    out_specs = [pl.BlockSpec((tm, w), row) for w, _ in widths]
    out_shape = [jax.ShapeDtypeStruct((n_tok, w), dt) for w, dt in widths]
    if not rope:
        for _ in range(2):
            out_specs.append(pl.BlockSpec((tm // seq, KV_WIDTH, seq), lambda i: (i, 0, 0)))
            out_shape.append(jax.ShapeDtypeStruct((n_tok // seq, KV_WIDTH, seq), F32))
    return pl.pallas_call(
        functools.partial(_proj_kernel, rope=rope),
        grid=(n_tok // tm,),
        in_specs=in_specs,
        out_specs=out_specs,
        out_shape=out_shape,
        compiler_params=_params("arbitrary"),
        name="project_rope" if rope else "project",
    )(*args)


def _software_pipeline(n_groups, n_slots, stages):
    last_g = n_groups - 1

    def body(k, carry):
        for age in (2, 1, 0):
            for u in range(n_slots):
                g = n_slots * (k - age) + u
                stages[age](jnp.clip(g, 0, last_g), u, jnp.logical_and(g >= 0, g <= last_g))
        return carry

    lax.fori_loop(0, n_groups // n_slots + 2, body, 0)


def _init_pipeline_scratch(first, zero_refs, one_refs=()):
    @pl.when(first)
    def _():
        for ref in zero_refs:
            ref[...] = jnp.zeros_like(ref)
        for ref in one_refs:
            ref[...] = jnp.ones_like(ref)


def _key_rows(start, size):
    return pl.ds(pl.multiple_of(start, BLOCK), size)


def _softmax_rows(s_buf, p_buf, t_buf, slot, n_rows, n_keys, masks_of, sink_of):
    for r0 in range(0, n_rows, SOFTMAX_ROWS):
        rs = slice(r0, r0 + SOFTMAX_ROWS)
        masks = masks_of(r0)
        cols = []
        for c in range(n_keys // LANES):
            sc = s_buf[slot, rs, c * LANES:(c + 1) * LANES]
            cols.append(jnp.where(masks[c], sc, NEG_INF) if c in masks else sc)
        mx = cols[0]
        for sc in cols[1:]:
            mx = jnp.maximum(mx, sc)
        sink = sink_of(r0)
        m = jnp.maximum(jnp.max(mx, axis=-1, keepdims=True), sink)
        for c, sc in enumerate(cols):
            p_buf[slot, rs, c * LANES:(c + 1) * LANES] = jnp.exp2(sc - m).astype(BF16)
        t_buf[slot, rs, :] = jnp.broadcast_to(jnp.exp2(sink - m), (SOFTMAX_ROWS, LANES))


def _normalize(acc, t_buf, slot):
    return acc / (pltpu.roll(acc, HEAD_DIM, 1) + t_buf[slot])


def _attn_scratch(n_groups, n_keys):
    rows = 2 * BLOCK
    return [pltpu.VMEM((n_groups // 2, rows, LANES), BF16),
            pltpu.VMEM((PIPE_SLOTS, rows, n_keys), F32), pltpu.VMEM((PIPE_SLOTS, rows, n_keys), BF16),
            pltpu.VMEM((PIPE_SLOTS, rows, LANES), F32), pltpu.VMEM((n_groups, rows, LANES), F32)]


def _stack_queries(q_ref, q2s, n_blocks):
    for blk in range(n_blocks):
        rows = slice(blk * BLOCK, (blk + 1) * BLOCK)
        for j in range(N_KV_HEADS):
            idx = blk * N_KV_HEADS + j
            q2s[idx, 0:BLOCK, :] = q_ref[rows, (2 * j) * LANES:(2 * j + 1) * LANES]
            q2s[idx, BLOCK:2 * BLOCK, :] = q_ref[rows, (2 * j + 1) * LANES:(2 * j + 2) * LANES]


def _write_heads(o_buf, o_ref, n_blocks):
    lo = lax.broadcasted_iota(jnp.int32, (2 * BLOCK, LANES), 1) < HEAD_DIM
    for blk in range(n_blocks):
        rows = slice(blk * BLOCK, (blk + 1) * BLOCK)
        for j in range(N_KV_HEADS):
            g0 = blk * N_PLACEMENTS + 2 * j
            o = jnp.where(lo, o_buf[g0], o_buf[g0 + 1]).astype(o_ref.dtype)
            o_ref[rows, (2 * j) * LANES:(2 * j + 1) * LANES] = o[0:BLOCK]
            o_ref[rows, (2 * j + 1) * LANES:(2 * j + 2) * LANES] = o[BLOCK:2 * BLOCK]


def _group_sinks(sink_ref, g):
    j, e = (g // 2) % N_KV_HEADS, g % 2
    sink_a = sink_ref[4 * j + e] * LOG2E
    sink_b = sink_ref[4 * j + 2 + e] * LOG2E
    return lambda r0: sink_a if r0 < BLOCK else sink_b


def _ctx_attn_kernel(sink_ref, q_ref, kvp_ref, o_ref, kown, vown, q2s, s_buf, p_buf, t_buf, o_buf, *, seq):
    n_blocks = q_ref.shape[0] // BLOCK
    _init_pipeline_scratch(pl.program_id(0) == 0, (s_buf, p_buf), (t_buf,))
    for idx in range(N_PLACEMENTS):
        kown[idx] = kvp_ref[:, idx * LANES:(idx + 1) * LANES]
        vown[idx] = kvp_ref[:, (N_PLACEMENTS + idx) * LANES:(N_PLACEMENTS + idx + 1) * LANES]
    _stack_queries(q_ref, q2s, n_blocks)
    blocks_per_seq = seq // BLOCK

    def keys_of(g):
        return _key_rows(((g // N_PLACEMENTS) // blocks_per_seq) * seq, seq)

    def qk(g, u, valid):
        s_buf[u] = _dot_nt(q2s[g // 2], kown[g % N_PLACEMENTS, keys_of(g), :])

    def softmax(g, u, valid):
        _softmax_rows(s_buf, p_buf, t_buf, u, 2 * BLOCK, seq, lambda r0: {}, _group_sinks(sink_ref, g))

    def pv(g, u, valid):
        o_buf[g] = _normalize(_dot(p_buf[u], vown[g % N_PLACEMENTS, keys_of(g), :]), t_buf, u)

    _software_pipeline(n_blocks * N_PLACEMENTS, PIPE_SLOTS, (qk, softmax, pv))
    _write_heads(o_buf, o_ref, n_blocks)


def _context_attention(q, kvp, sink, seq):
    n_tok = q.shape[0]
    rows = CTX_ATTN_BATCH * seq
    row = lambda b: (b, 0)
    n_groups = (rows // BLOCK) * N_PLACEMENTS
    return pl.pallas_call(
        functools.partial(_ctx_attn_kernel, seq=seq),
        grid=(n_tok // rows,),
        in_specs=[
            pl.BlockSpec(memory_space=pltpu.SMEM),
            pl.BlockSpec((rows, ATTN_WIDTH), row),
            pl.BlockSpec((rows, KVP_WIDTH), row),
        ],
        out_specs=pl.BlockSpec((rows, ATTN_WIDTH), row),
        out_shape=jax.ShapeDtypeStruct((n_tok, ATTN_WIDTH), BF16),
        scratch_shapes=[pltpu.VMEM((N_PLACEMENTS, rows, LANES), BF16)] * 2 + _attn_scratch(n_groups, seq),
        compiler_params=_params("arbitrary"),
        name="context_attention",
    )(sink, q, kvp)


def _lat_attn_kernel(sink_ref, q_ref, kvp_prev, kvp_own, kvp_next, ck_ref, cv_ref, o_ref,
                     kwin, vwin, kctx, vctx, q2s, s_buf, p_buf, t_buf, o_buf, *, n_steps, past):
    i = pl.program_id(1)
    n_blocks = q_ref.shape[0] // BLOCK
    own_end = BLOCK + q_ref.shape[0]
    win = 3 * BLOCK
    _init_pipeline_scratch(jnp.logical_and(pl.program_id(0) == 0, i == 0), (s_buf, p_buf), (t_buf,))

    @pl.when(i == 0)
    def _():
        cpl = _kv_placements(ck_ref[0].T, cv_ref[0].T)
        for idx in range(N_PLACEMENTS):
            kctx[idx] = cpl[:, idx * LANES:(idx + 1) * LANES]
            vctx[idx] = cpl[:, (N_PLACEMENTS + idx) * LANES:(N_PLACEMENTS + idx + 1) * LANES]

    for src, r0, r1 in ((kvp_prev, 0, BLOCK), (kvp_own, BLOCK, own_end), (kvp_next, own_end, own_end + BLOCK)):
        for idx in range(N_PLACEMENTS):
            kwin[idx, r0:r1, :] = src[:, idx * LANES:(idx + 1) * LANES]
            vwin[idx, r0:r1, :] = src[:, (N_PLACEMENTS + idx) * LANES:(N_PLACEMENTS + idx + 1) * LANES]
    _stack_queries(q_ref, q2s, n_blocks)

    row = lax.broadcasted_iota(jnp.int32, (SOFTMAX_ROWS, LANES), 0)
    c = lax.broadcasted_iota(jnp.int32, (SOFTMAX_ROWS, LANES), 1)
    off_prev = jnp.where(i > 0, 0, BLOCK)
    off_next = jnp.where(i < n_steps - 1, 0, BLOCK)

    def window_of(g):
        return _key_rows((g // N_PLACEMENTS) * BLOCK, win)

    def qk(g, u, valid):
        q2 = q2s[g // 2]
        s_buf[u, :, 0:win] = _dot_nt(q2, kwin[g % N_PLACEMENTS, window_of(g), :])
        s_buf[u, :, win:win + past] = _dot_nt(q2, kctx[g % N_PLACEMENTS])

    def softmax(g, u, valid):
        blk = g // N_PLACEMENTS
        off0 = jnp.where(blk == 0, off_prev, 0)
        off2 = jnp.where(blk == n_blocks - 1, off_next, 0)

        def masks_of(r0):
            r = row + r0 % BLOCK
            return {0: c >= r + off0, 2: c <= r - off2}

        _softmax_rows(s_buf, p_buf, t_buf, u, 2 * BLOCK, win + past, masks_of, _group_sinks(sink_ref, g))

    def pv(g, u, valid):
        acc = _dot(p_buf[u, :, 0:win], vwin[g % N_PLACEMENTS, window_of(g), :])
        acc = acc + _dot(p_buf[u, :, win:win + past], vctx[g % N_PLACEMENTS])
        o_buf[g] = _normalize(acc, t_buf, u)

    _software_pipeline(n_blocks * N_PLACEMENTS, PIPE_SLOTS, (qk, softmax, pv))
    _write_heads(o_buf, o_ref, n_blocks)


def _latent_attention(q, kvp, ck, cv, sink, n_batch, seq):
    ns = seq // ATTN_TILE
    nb = seq // BLOCK
    per = ATTN_TILE // BLOCK
    own = lambda b, i: (b * ns + i, 0)
    prev = lambda b, i: (b * nb + jnp.maximum(per * i - 1, 0), 0)
    nxt = lambda b, i: (b * nb + jnp.minimum(per * i + per, nb - 1), 0)
    ctx = lambda b, i: (b, 0, 0)
    past = ck.shape[2]
    win_rows = ATTN_TILE + 2 * BLOCK
    bf = lambda *shape: pltpu.VMEM(shape, BF16)
    return pl.pallas_call(
        functools.partial(_lat_attn_kernel, n_steps=ns, past=past),
        grid=(n_batch, ns),
        in_specs=[
            pl.BlockSpec(memory_space=pltpu.SMEM),
            pl.BlockSpec((ATTN_TILE, ATTN_WIDTH), own),
            pl.BlockSpec((BLOCK, KVP_WIDTH), prev),
            pl.BlockSpec((ATTN_TILE, KVP_WIDTH), own),
            pl.BlockSpec((BLOCK, KVP_WIDTH), nxt),
            pl.BlockSpec((1, KV_WIDTH, past), ctx),
            pl.BlockSpec((1, KV_WIDTH, past), ctx),
        ],
        out_specs=pl.BlockSpec((ATTN_TILE, ATTN_WIDTH), own),
        out_shape=jax.ShapeDtypeStruct((n_batch * seq, ATTN_WIDTH), BF16),
        scratch_shapes=[bf(N_PLACEMENTS, win_rows, LANES), bf(N_PLACEMENTS, win_rows, LANES),
                        bf(N_PLACEMENTS, past, LANES), bf(N_PLACEMENTS, past, LANES)]
        + _attn_scratch(per * N_PLACEMENTS, 3 * BLOCK + past),
        compiler_params=_params("arbitrary", "arbitrary"),
        name="latent_attention",
    )(sink, q, kvp, kvp, kvp, ck, cv)


_PK = 2 * GLA_DK
_PV = 2 * GLA_DV


def _cumsum_rows(x, reverse):
    n = x.shape[0]
    row = lax.broadcasted_iota(jnp.int32, x.shape, 0)
    s = 1
    while s < n:
        if reverse:
            x = x + jnp.where(row < n - s, pltpu.roll(x, n - s, 0), 0.0)
        else:
            x = x + jnp.where(row >= s, pltpu.roll(x, s, 0), 0.0)
        s *= 2
    return x


def _gla_kernel(*refs, seq_chunks, has_init, emit_state):
    it = iter(refs)
    views = [tuple(next(it) for _ in range(3)) for _ in range(2)]
    s0_ref = next(it) if has_init else None
    out_refs = (next(it), next(it))
    sout_ref = next(it) if emit_state else None
    st_ref, a_buf, a2_buf, qe_buf, qd_buf, ds_buf, sb_buf, dc_buf = (next(it) for _ in range(8))
    n_chunks = views[0][0].shape[0] // CHUNK
    n_groups = 2 * n_chunks
    n_pairs = N_GLA_HEADS // 2
    tile = pl.program_id(1)
    q_scale = GLA_DK ** -0.5
    _init_pipeline_scratch(jnp.logical_and(pl.program_id(0) == 0, tile == 0),
                           (st_ref, a_buf, a2_buf, qe_buf, qd_buf, ds_buf, sb_buf), (dc_buf,))

    if has_init:
        @pl.when(tile == 0)
        def _():
            st_ref[...] = jnp.zeros_like(st_ref)
            for d in range(2):
                for h in range(N_GLA_HEADS):
                    p, e = divmod(h, 2)
                    st_ref[d, p, e * GLA_DK:(e + 1) * GLA_DK, e * GLA_DV:(e + 1) * GLA_DV] = s0_ref[0, d, h]

    lane_k = lax.broadcasted_iota(jnp.int32, (CHUNK, _PK), 1)
    row_k = lax.broadcasted_iota(jnp.int32, (CHUNK, _PK), 0)
    lo_k = lane_k < GLA_DK
    tri = (lane_k % CHUNK <= row_k, lane_k % CHUNK >= row_k)
    lo_v = lax.broadcasted_iota(jnp.int32, (CHUNK, _PV), 1) < GLA_DV
    diag = ((lax.broadcasted_iota(jnp.int32, (_PK, _PV), 0) < GLA_DK)
            == (lax.broadcasted_iota(jnp.int32, (_PK, _PV), 1) < GLA_DV))

    def place(g, d):
        c = g // 2
        if d == 0:
            return c, c
        seq = c // seq_chunks
        return c, seq * seq_chunks + (seq_chunks - 1 - (c - seq * seq_chunks))

    def rows_of(pos):
        return pl.ds(pl.multiple_of(pos * CHUNK, CHUNK), CHUNK)

    def prep(g, u, valid):
        d = u % 2
        qk_ref, g_ref, v_ref = views[d]
        _, pos = place(g, d)
        rows = rows_of(pos)
        gc = _cumsum_rows(g_ref[rows, :], reverse=(d == 1))
        gtot = gc[0:1, :] if d == 1 else gc[CHUNK - 1:CHUNK, :]
        k = qk_ref[rows, GLA_K_WIDTH:2 * GLA_K_WIDTH]
        qe = qk_ref[rows, 0:GLA_K_WIDTH] * jnp.exp(gc)
        qe_buf[u] = qe
        q_dec = (qe * q_scale).astype(BF16)
        k_inv = k * jnp.exp(-gc)
        k_end = jnp.where(valid, k * jnp.exp(gtot - gc), 0.0).astype(BF16)
        decay = jnp.where(valid, jnp.exp(gtot), 1.0)
        for p in range(n_pairs):
            lanes = slice(p * _PK, (p + 1) * _PK)
            ki = k_inv[:, lanes]
            kst = jnp.concatenate([jnp.where(lo_k, ki, 0.0), jnp.where(lo_k, 0.0, ki)], axis=0).astype(BF16)
            a_buf[u, p] = _dot_nt(q_dec[:, lanes], kst)
            ds_buf[u, p] = jnp.where(diag, _dot_tn(k_end[:, lanes], v_ref[rows, p * _PV:(p + 1) * _PV]), 0.0)
            dc_buf[u, p] = jnp.broadcast_to(decay[:, lanes], (_PK, _PK)).T

    def state(g, u, valid):
        d = u % 2
        c, _ = place(g, d)
        qd_buf[u] = (qe_buf[u] * q_scale).astype(BF16)
        for p in range(n_pairs):
            fresh = (c % seq_chunks == 0) if emit_state else (c < 0)
            s_in = jnp.where(fresh, 0.0, st_ref[d, p])
            sb_buf[u, p] = s_in.astype(BF16)
            a2_buf[u, p] = jnp.where(tri[d], a_buf[u, p], 0.0).astype(BF16)
            dc = dc_buf[u, p]
            s_new = s_in * jnp.concatenate([dc, dc], axis=1) + ds_buf[u, p]
            st_ref[d, p] = s_new
            if emit_state:
                for e in range(2):
                    sout_ref[c // seq_chunks, d, 2 * p + e] = s_new[e * GLA_DK:(e + 1) * GLA_DK,
                                                                    e * GLA_DV:(e + 1) * GLA_DV]

    def out(g, u, valid):
        d = u % 2
        _, _, v_ref = views[d]
        _, pos = place(g, d)
        rows = rows_of(pos)
        for p in range(n_pairs):
            vpf = v_ref[rows, p * _PV:(p + 1) * _PV].astype(F32)
            vbd = jnp.concatenate([jnp.where(lo_v, vpf, 0.0), jnp.where(lo_v, 0.0, vpf)], axis=0).astype(BF16)
            x = jnp.concatenate([a2_buf[u, p], qd_buf[u, :, p * _PK:(p + 1) * _PK]], axis=1)
            w = jnp.concatenate([vbd, sb_buf[u, p]], axis=0)
            out_refs[d][rows, p * _PV:(p + 1) * _PV] = _dot(x, w).astype(out_refs[d].dtype)

    _software_pipeline(n_groups, GLA_PIPE_SLOTS, (prep, state, out))


def _gla(qkg, gf, gb, vg, s0, n_batch, seq):
    tm = GLA_TILE
    long_seq = s0 is not None
    if long_seq:
        nt = seq // tm
        grid = (n_batch, nt)
        fwd = lambda b, i: (b * nt + i, 0)
        bwd = lambda b, i: (b * nt + nt - 1 - i, 0)
        seq_chunks = tm // CHUNK
    else:
        per_step = tm // seq
        grid = (1, n_batch // per_step)
        fwd = bwd = lambda b, i: (i, 0)
        seq_chunks = seq // CHUNK
    in_specs = []
    args = []
    for m, gates in ((fwd, gf), (bwd, gb)):
        in_specs += [pl.BlockSpec((tm, 2 * GLA_K_WIDTH), m), pl.BlockSpec((tm, GLA_K_WIDTH), m),
                     pl.BlockSpec((tm, GLA_V_WIDTH), m)]
        args += [qkg, gates, vg]
    n_tok = n_batch * seq
    out_specs = [pl.BlockSpec((tm, GLA_V_WIDTH), fwd), pl.BlockSpec((tm, GLA_V_WIDTH), bwd)]
    out_shape = [jax.ShapeDtypeStruct((n_tok, GLA_V_WIDTH), BF16)] * 2
    st_dims = (2, N_GLA_HEADS, GLA_DK, GLA_DV)
    if long_seq:
        in_specs.append(pl.BlockSpec((1,) + st_dims, lambda b, i: (b, 0, 0, 0, 0)))
        args.append(s0)
    else:
        out_specs.append(pl.BlockSpec((per_step,) + st_dims, lambda b, i: (i, 0, 0, 0, 0)))
        out_shape.append(jax.ShapeDtypeStruct((n_batch,) + st_dims, F32))
    n = GLA_PIPE_SLOTS
    n_pairs = N_GLA_HEADS // 2
    return pl.pallas_call(
        functools.partial(_gla_kernel, seq_chunks=seq_chunks, has_init=long_seq, emit_state=not long_seq),
        grid=grid,
        in_specs=in_specs,
        out_specs=out_specs,
        out_shape=out_shape,
        scratch_shapes=[pltpu.VMEM((2, n_pairs, _PK, _PV), F32),
                        pltpu.VMEM((n, n_pairs, CHUNK, _PK), F32),
                        pltpu.VMEM((n, n_pairs, CHUNK, _PK), BF16),
                        pltpu.VMEM((n, CHUNK, GLA_K_WIDTH), F32),
                        pltpu.VMEM((n, CHUNK, GLA_K_WIDTH), BF16),
                        pltpu.VMEM((n, n_pairs, _PK, _PV), F32),
                        pltpu.VMEM((n, n_pairs, _PK, _PV), BF16),
                        pltpu.VMEM((n, n_pairs, _PK, _PK), F32)],
        compiler_params=_params("arbitrary", "arbitrary"),
        name="gla_long" if long_seq else "gla_short",
    )(*args)


def _merge_kernel(x_ref, mod_ref, oa_ref, za_ref, of_ref, ob_ref, zg_ref, gng_ref, wo_ref, fng_ref, y_ref):
    ya = oa_ref[...].astype(F32) * _silu(za_ref[...].astype(F32))
    og = of_ref[...].astype(F32) + ob_ref[...].astype(F32)
    zg = zg_ref[...].astype(F32)
    gng = gng_ref[...]
    parts = [ya]
    for h in range(N_GLA_HEADS):
        oh = og[:, h * GLA_DV:(h + 1) * GLA_DV]
        nh = oh * lax.rsqrt(jnp.mean(oh * oh, axis=-1, keepdims=True) + EPS)
        parts.append((nh * gng) * _silu(zg[:, h * GLA_DV:(h + 1) * GLA_DV]))
    yin = jnp.concatenate(parts, axis=-1).astype(BF16)
    gate = mod_ref[0][:, 2 * D_MODEL:3 * D_MODEL]
    y = x_ref[...] + gate * _dot(yin, wo_ref[...])
    y_ref[...] = (y * lax.rsqrt(jnp.mean(y * y, axis=-1, keepdims=True) + EPS)) * fng_ref[...]


def _merge(x2d, mod3, mod_row_of_tile, oa, za, of, ob, zg, gla_norm_g, w_out, final_norm_g, tm):
    n_tok = x2d.shape[0]
    row = lambda i: (i, 0)
    const = lambda i: (0, 0)
    wide = pl.BlockSpec((tm, GLA_V_WIDTH), row)
    return pl.pallas_call(
        _merge_kernel,
        grid=(n_tok // tm,),
        in_specs=[
            pl.BlockSpec((tm, D_MODEL), row),
            pl.BlockSpec((1, 1, 3 * D_MODEL), lambda i: (mod_row_of_tile(i), 0, 0)),
            wide, wide, wide, wide, wide,
            pl.BlockSpec((1, GLA_DV), const),
            pl.BlockSpec((D_MODEL, D_MODEL), const),
            pl.BlockSpec((1, D_MODEL), const),
        ],
        out_specs=pl.BlockSpec((tm, D_MODEL), row),
        out_shape=jax.ShapeDtypeStruct((n_tok, D_MODEL), F32),
        compiler_params=_params("arbitrary"),
        name="merge",
    )(x2d, mod3, oa, za, of, ob, zg, gla_norm_g, w_out, final_norm_g)


def _rope_tables(n_tok):
    t = np.arange(n_tok)
    rowp = (t // GRID_W).astype(np.float64)
    colp = (t % GRID_W).astype(np.float64)
    nf = HEAD_DIM // 4
    freqs = ROPE_BASE ** (-np.arange(nf, dtype=np.float64) / nf)
    lane = np.arange(LANES)
    pos = np.where(((lane % HEAD_DIM) < HEAD_DIM // 2)[None, :], rowp[:, None], colp[:, None])
    ang = pos * freqs[lane % nf][None, :]
    sign = np.where((lane % (2 * nf)) < nf, -1.0, 1.0)
    return jnp.asarray(np.cos(ang), F32), jnp.asarray(np.sin(ang) * sign[None, :], F32)


def kernel(x_prompt, x_sample, c, cache_k, cache_v, state_gla, c_ctx, w_mod, b_mod, norm_g, w_in,
           w_gk_f, b_gk_f, w_gk_b, b_gk_b, sink, gla_norm_g, w_out, final_norm_g):
    depth = w_in.shape[0]
    assert depth == 1, "single-layer step"
    l = 0
    bp, sp, _ = x_prompt.shape
    bs, ss, _ = x_sample.shape

    wl = w_in[l]
    w1 = wl[:, :_TAIL0].astype(BF16)
    wtail = wl[:, _TAIL0:].astype(BF16)
    wgk = jnp.zeros((LANES, 2 * GLA_K_WIDTH), F32)
    wgk = wgk.at[0:GATE_RANK, 0:GLA_K_WIDTH].set(w_gk_f[l])
    wgk = wgk.at[GATE_RANK:2 * GATE_RANK, GLA_K_WIDTH:].set(w_gk_b[l]).astype(BF16)
    bgk = jnp.concatenate([b_gk_f[l], b_gk_b[l]])[None, :]
    wo = w_out[l].astype(BF16)
    ng = norm_g[l][None, :]
    gng = gla_norm_g[l][None, :]
    fng = final_norm_g[None, :]
    sk = sink[l]

    n_rows = 8
    cvecs = jnp.concatenate([c_ctx[None, :], c, jnp.zeros((n_rows - 1 - bs, D_MODEL), F32)], axis=0)
    mod3 = _modulation(cvecs, w_mod[l], b_mod[l][None, :]).reshape(n_rows, 1, 3 * D_MODEL)
    tm = TOKEN_TILE

    xp2 = x_prompt.reshape(bp * sp, D_MODEL)
    ctx_row = lambda i: 0
    q, kvp, za, vg, zg, qkg, gf, gb, kt, vt = _project(xp2, mod3, ctx_row, ng, w1, wtail, wgk, bgk, None, tm, sp)
    oa = _context_attention(q, kvp, sk, sp)
    of, ob, st = _gla(qkg, gf, gb, vg, None, bp, sp)
    y_prompt = _merge(xp2, mod3, ctx_row, oa, za, of, ob, zg, gng, wo, fng, MERGE_TILE).reshape(bp, sp, D_MODEL)
    to_cache = lambda t: t.reshape(bp, 1, N_KV_HEADS, HEAD_DIM, sp).transpose(0, 1, 4, 2, 3)
    new_k, new_v = to_cache(kt), to_cache(vt)
    new_state = st[:, None]

    xs2 = x_sample.reshape(bs * ss, D_MODEL)
    tiles_per_seq = ss // tm
    lat_row = lambda i: 1 + i // tiles_per_seq
    q, kvp, za, vg, zg, qkg, gf, gb = _project(xs2, mod3, lat_row, ng, w1, wtail, wgk, bgk, _rope_tables(ss), tm, ss)
    past = cache_k.shape[2]
    from_cache = lambda t: t[:, l].transpose(0, 2, 3, 1).reshape(bs, KV_WIDTH, past)
    ck, cv = from_cache(cache_k), from_cache(cache_v)
    oa = _latent_attention(q, kvp, ck, cv, sk, bs, ss)
    of, ob = _gla(qkg, gf, gb, vg, state_gla[:, l], bs, ss)
    merge_row = lambda i: 1 + i // (ss // MERGE_TILE)
    y_sample = _merge(xs2, mod3, merge_row, oa, za, of, ob, zg, gng, wo, fng, MERGE_TILE).reshape(bs, ss, D_MODEL)

    return (y_prompt, y_sample, new_k, new_v, new_state)
```

```python
import functools
import math

import numpy as np
import jax
import jax.numpy as jnp
from jax import lax
from jax.experimental import pallas as pl
from jax.experimental.pallas import tpu as pltpu

D_MODEL = 1024
GRID_W = 64
HEAD_DIM = 64
N_ATTN_HEADS = 8
N_KV_HEADS = 2
ATTN_WIDTH = 512
KV_WIDTH = 128
BLOCK = 128
N_GLA_HEADS = 4
GLA_DK = 64
GLA_DV = 128
GLA_K_WIDTH = 256
GLA_V_WIDTH = 512
GATE_RANK = 16
GATE_NORMALIZER = 16.0
CHUNK = 64
ROPE_BASE = 10000.0
EPS = 1e-6
NEG_INF = -1e30

LANES = 128
TOKEN_TILE = 1024
MERGE_TILE = 1024
GLA_TILE = 2048
ATTN_TILE = 2048
CTX_ATTN_BATCH = 8
PIPE_SLOTS = 4
GLA_PIPE_SLOTS = 4
KVP_WIDTH = 8 * LANES
N_PLACEMENTS = 2 * N_KV_HEADS
SOFTMAX_ROWS = 32
MOD_TILE = 512
WEIGHT_CAST_COLS = 256
VMEM_LIMIT = 56 * 1024 * 1024
LOG2E = math.log2(math.e)

F32 = jnp.float32
BF16 = jnp.bfloat16


def _dot(a, b):
    return jnp.dot(a, b, preferred_element_type=F32)


def _dot_nt(a, b):
    return lax.dot_general(a, b, (((1,), (1,)), ((), ())), preferred_element_type=F32)


def _dot_tn(a, b):
    return lax.dot_general(a, b, (((0,), (0,)), ((), ())), preferred_element_type=F32)


def _silu(x):
    h = 0.5 * x
    return h + h * jnp.tanh(h)


def _log_sigmoid(x):
    return jnp.minimum(x, 0.0) - jnp.log1p(jnp.exp(-jnp.abs(x)))


def _params(*sem):
    return pltpu.CompilerParams(dimension_semantics=sem, vmem_limit_bytes=VMEM_LIMIT)


def _split_bf16(x):
    hi = x.astype(BF16)
    return hi, (x - hi.astype(F32)).astype(BF16)


def _mod_kernel(c_ref, w_ref, b_ref, o_ref):
    rows = c_ref.shape[0]
    s_hi, s_lo = _split_bf16(_silu(c_ref[...]))
    w_hi, w_lo = _split_bf16(w_ref[...])
    by_hi = _dot(jnp.concatenate([s_hi, s_lo], axis=0), w_hi)
    o_ref[...] = by_hi[0:rows] + by_hi[rows:2 * rows] + _dot(s_hi, w_lo) + b_ref[...]


def _modulation(cvecs, w_mod, b_mod):
    rows = cvecs.shape[0]
    n_out = w_mod.shape[1]
    return pl.pallas_call(
        _mod_kernel,
        grid=(n_out // MOD_TILE,),
        in_specs=[
            pl.BlockSpec((rows, D_MODEL), lambda j: (0, 0)),
            pl.BlockSpec((D_MODEL, MOD_TILE), lambda j: (0, j)),
            pl.BlockSpec((1, MOD_TILE), lambda j: (0, j)),
        ],
        out_specs=pl.BlockSpec((rows, MOD_TILE), lambda j: (0, j)),
        out_shape=jax.ShapeDtypeStruct((rows, n_out), F32),
        compiler_params=_params("arbitrary"),
        name="adaln_mod",
    )(cvecs, w_mod, b_mod)


_Q0, _K0, _V0, _ZA0, _QG0, _KG0, _VG0, _TAIL0 = 0, 512, 640, 768, 1280, 1536, 1792, 2304
_TAIL_COLS = 2 * GATE_RANK + GLA_V_WIDTH
_Q_PRESCALE = (HEAD_DIM ** -0.5) * LOG2E


def _rope(x, cos, sin_signed):
    n = x.shape[1] // LANES
    lane = lax.broadcasted_iota(jnp.int32, (x.shape[0], LANES), 1)
    first = (lane % 32) < 16
    outs = []
    for c in range(n):
        xc = x[:, c * LANES:(c + 1) * LANES]
        partner = jnp.where(first, pltpu.roll(xc, LANES - 16, 1), pltpu.roll(xc, 16, 1))
        outs.append(xc * cos + partner * sin_signed)
    return outs[0] if n == 1 else jnp.concatenate(outs, axis=-1)


def _kv_placements(k, v):
    lane = lax.broadcasted_iota(jnp.int32, k.shape, 1)
    lo = lane < HEAD_DIM
    kr = pltpu.roll(k, HEAD_DIM, 1)
    vr = pltpu.roll(v, HEAD_DIM, 1)
    pieces = [jnp.where(lo, k, 0.0), jnp.where(lo, 0.0, kr), jnp.where(lo, kr, 0.0), jnp.where(lo, 0.0, k),
              jnp.where(lo, v, 1.0), jnp.where(lo, 1.0, vr), jnp.where(lo, vr, 1.0), jnp.where(lo, 1.0, v)]
    return jnp.concatenate(pieces, axis=-1).astype(BF16)


def _proj_kernel(*refs, rope):
    w1_ref = refs[-1]
    if rope:
        (x_ref, mod_ref, ng_ref, w_ref, wgk_ref, bgk_ref, cos_ref, sin_ref,
         q_ref, kvp_ref, za_ref, vg_ref, zg_ref, qkg_ref, gf_ref, gb_ref) = refs[:-1]
    else:
        (x_ref, mod_ref, ng_ref, w_ref, wgk_ref, bgk_ref,
         q_ref, kvp_ref, za_ref, vg_ref, zg_ref, qkg_ref, gf_ref, gb_ref, k_ref, v_ref) = refs[:-1]

    @pl.when(pl.program_id(0) == 0)
    def _():
        n_cols = w_ref.shape[1]
        for c0 in range(0, n_cols, WEIGHT_CAST_COLS):
            c1 = min(c0 + WEIGHT_CAST_COLS, n_cols)
            w1_ref[:, c0:c1] = w_ref[:, c0:c1].astype(BF16)

    x = x_ref[...]
    xn = x * lax.rsqrt(jnp.mean(x * x, axis=-1, keepdims=True) + EPS)
    mod = mod_ref[0]
    shift = mod[:, 0:D_MODEL]
    scale = mod[:, D_MODEL:2 * D_MODEL]
    h = ((xn * ng_ref[...]) * (1.0 + scale) + shift).astype(BF16)

    def piece(c0, c1):
        return _dot(h, w1_ref[:, c0:c1])

    q = piece(_Q0, _K0)
    kv = piece(_K0, _ZA0)
    k = kv[:, 0:KV_WIDTH]
    v = kv[:, KV_WIDTH:2 * KV_WIDTH]
    if rope:
        cos = cos_ref[...]
        sin = sin_ref[...]
        q = _rope(q, cos, sin)
        k = _rope(k, cos, sin)
    else:
        seq = k_ref.shape[2]
        for b in range(k_ref.shape[0]):
            k_ref[b] = k[b * seq:(b + 1) * seq, :].T
            v_ref[b] = v[b * seq:(b + 1) * seq, :].T
    q_ref[...] = (q * _Q_PRESCALE).astype(BF16)
    kvp_ref[...] = _kv_placements(k, v)
    za_ref[...] = piece(_ZA0, _QG0).astype(BF16)
    vg_ref[...] = piece(_VG0, _TAIL0).astype(BF16)
    qkg_ref[...] = piece(_QG0, _VG0)
    tail = _dot(h, w1_ref[:, _TAIL0:_TAIL0 + _TAIL_COLS])
    zg_ref[...] = tail[:, 2 * GATE_RANK:_TAIL_COLS].astype(BF16)
    lane = lax.broadcasted_iota(jnp.int32, (tail.shape[0], LANES), 1)
    lr = jnp.where(lane < 2 * GATE_RANK, tail[:, 0:LANES], 0.0).astype(BF16)
    g = _log_sigmoid(_dot(lr, wgk_ref[...]) + bgk_ref[...]) * (1.0 / GATE_NORMALIZER)
    gf_ref[...] = g[:, 0:GLA_K_WIDTH]
    gb_ref[...] = g[:, GLA_K_WIDTH:2 * GLA_K_WIDTH]


def _project(x2d, mod3, mod_row_of_tile, norm_g, w, wgk, bgk, rope_tabs, tm, seq):
    n_tok = x2d.shape[0]
    rope = rope_tabs is not None
    row = lambda i: (i, 0)
    const = lambda i: (0, 0)
    in_specs = [
        pl.BlockSpec((tm, D_MODEL), row),
        pl.BlockSpec((1, 1, 3 * D_MODEL), lambda i: (mod_row_of_tile(i), 0, 0)),
        pl.BlockSpec((1, D_MODEL), const),
        pl.BlockSpec(w.shape, const, pipeline_mode=pl.Buffered(1)),
        pl.BlockSpec((LANES, 2 * GLA_K_WIDTH), const),
        pl.BlockSpec((1, 2 * GLA_K_WIDTH), const),
    ]
    args = [x2d, mod3, norm_g, w, wgk, bgk]
    if rope:
        tiles_per_seq = rope_tabs[0].shape[0] // tm
        pos = lambda i: (i % tiles_per_seq, 0)
        in_specs += [pl.BlockSpec((tm, LANES), pos), pl.BlockSpec((tm, LANES), pos)]
        args += list(rope_tabs)
    widths = [(ATTN_WIDTH, BF16), (KVP_WIDTH, BF16), (ATTN_WIDTH, BF16), (GLA_V_WIDTH, BF16), (GLA_V_WIDTH, BF16),
              (2 * GLA_K_WIDTH, F32), (GLA_K_WIDTH, F32), (GLA_K_WIDTH, F32)]
    out_specs = [pl.BlockSpec((tm, w), row) for w, _ in widths]
    out_shape = [jax.ShapeDtypeStruct((n_tok, w), dt) for w, dt in widths]
    if not rope:
        for _ in range(2):
            out_specs.append(pl.BlockSpec((tm // seq, KV_WIDTH, seq), lambda i: (i, 0, 0)))
            out_shape.append(jax.ShapeDtypeStruct((n_tok // seq, KV_WIDTH, seq), F32))
    return pl.pallas_call(
        functools.partial(_proj_kernel, rope=rope),
        grid=(n_tok // tm,),
        in_specs=in_specs,
        out_specs=out_specs,
        out_shape=out_shape,
        scratch_shapes=[pltpu.VMEM(w.shape, BF16)],
        compiler_params=_params("arbitrary"),
        name="project_rope" if rope else "project",
    )(*args)


def _software_pipeline(n_groups, n_slots, stages):
    last_g = n_groups - 1

    def body(k, carry):
        for age in (2, 1, 0):
            for u in range(n_slots):
                g = n_slots * (k - age) + u
                stages[age](jnp.clip(g, 0, last_g), u, jnp.logical_and(g >= 0, g <= last_g))
        return carry

    lax.fori_loop(0, n_groups // n_slots + 2, body, 0)


def _init_pipeline_scratch(first, zero_refs, one_refs=()):
    @pl.when(first)
    def _():
        for ref in zero_refs:
            ref[...] = jnp.zeros_like(ref)
        for ref in one_refs:
            ref[...] = jnp.ones_like(ref)


def _key_rows(start, size):
    return pl.ds(pl.multiple_of(start, BLOCK), size)


def _softmax_rows(s_buf, p_buf, t_buf, slot, n_rows, n_keys, masks_of, sink_of):
    for r0 in range(0, n_rows, SOFTMAX_ROWS):
        rs = slice(r0, r0 + SOFTMAX_ROWS)
        masks = masks_of(r0)
        cols = []
        for c in range(n_keys // LANES):
            sc = s_buf[slot, rs, c * LANES:(c + 1) * LANES]
            cols.append(jnp.where(masks[c], sc, NEG_INF) if c in masks else sc)
        mx = cols[0]
        for sc in cols[1:]:
            mx = jnp.maximum(mx, sc)
        sink = sink_of(r0)
        m = jnp.maximum(jnp.max(mx, axis=-1, keepdims=True), sink)
        for c, sc in enumerate(cols):
            p_buf[slot, rs, c * LANES:(c + 1) * LANES] = jnp.exp2(sc - m).astype(BF16)
        t_buf[slot, rs, :] = jnp.broadcast_to(jnp.exp2(sink - m), (SOFTMAX_ROWS, LANES))


def _normalize(acc, t_buf, slot):
    return acc / (pltpu.roll(acc, HEAD_DIM, 1) + t_buf[slot])


def _attn_scratch(n_groups, n_keys):
    rows = 2 * BLOCK
    return [pltpu.VMEM((n_groups // 2, rows, LANES), BF16),
            pltpu.VMEM((PIPE_SLOTS, rows, n_keys), F32), pltpu.VMEM((PIPE_SLOTS, rows, n_keys), BF16),
            pltpu.VMEM((PIPE_SLOTS, rows, LANES), F32), pltpu.VMEM((n_groups, rows, LANES), F32)]


def _stack_queries(q_ref, q2s, n_blocks):
    for blk in range(n_blocks):
        rows = slice(blk * BLOCK, (blk + 1) * BLOCK)
        for j in range(N_KV_HEADS):
            idx = blk * N_KV_HEADS + j
            q2s[idx, 0:BLOCK, :] = q_ref[rows, (2 * j) * LANES:(2 * j + 1) * LANES]
            q2s[idx, BLOCK:2 * BLOCK, :] = q_ref[rows, (2 * j + 1) * LANES:(2 * j + 2) * LANES]


def _write_heads(o_buf, o_ref, n_blocks):
    lo = lax.broadcasted_iota(jnp.int32, (2 * BLOCK, LANES), 1) < HEAD_DIM
    for blk in range(n_blocks):
        rows = slice(blk * BLOCK, (blk + 1) * BLOCK)
        for j in range(N_KV_HEADS):
            g0 = blk * N_PLACEMENTS + 2 * j
            o = jnp.where(lo, o_buf[g0], o_buf[g0 + 1]).astype(o_ref.dtype)
            o_ref[rows, (2 * j) * LANES:(2 * j + 1) * LANES] = o[0:BLOCK]
            o_ref[rows, (2 * j + 1) * LANES:(2 * j + 2) * LANES] = o[BLOCK:2 * BLOCK]


def _group_sinks(sink_ref, g):
    j, e = (g // 2) % N_KV_HEADS, g % 2
    sink_a = sink_ref[4 * j + e] * LOG2E
    sink_b = sink_ref[4 * j + 2 + e] * LOG2E
    return lambda r0: sink_a if r0 < BLOCK else sink_b


def _ctx_attn_kernel(sink_ref, q_ref, kvp_ref, o_ref, kown, vown, q2s, s_buf, p_buf, t_buf, o_buf, *, seq):
    n_blocks = q_ref.shape[0] // BLOCK
    _init_pipeline_scratch(pl.program_id(0) == 0, (s_buf, p_buf), (t_buf,))
    for idx in range(N_PLACEMENTS):
        kown[idx] = kvp_ref[:, idx * LANES:(idx + 1) * LANES]
        vown[idx] = kvp_ref[:, (N_PLACEMENTS + idx) * LANES:(N_PLACEMENTS + idx + 1) * LANES]
    _stack_queries(q_ref, q2s, n_blocks)
    blocks_per_seq = seq // BLOCK

    def keys_of(g):
        return _key_rows(((g // N_PLACEMENTS) // blocks_per_seq) * seq, seq)

    def qk(g, u, valid):
        s_buf[u] = _dot_nt(q2s[g // 2], kown[g % N_PLACEMENTS, keys_of(g), :])

    def softmax(g, u, valid):
        _softmax_rows(s_buf, p_buf, t_buf, u, 2 * BLOCK, seq, lambda r0: {}, _group_sinks(sink_ref, g))

    def pv(g, u, valid):
        o_buf[g] = _normalize(_dot(p_buf[u], vown[g % N_PLACEMENTS, keys_of(g), :]), t_buf, u)

    _software_pipeline(n_blocks * N_PLACEMENTS, PIPE_SLOTS, (qk, softmax, pv))
    _write_heads(o_buf, o_ref, n_blocks)


def _context_attention(q, kvp, sink, seq):
    n_tok = q.shape[0]
    rows = CTX_ATTN_BATCH * seq
    row = lambda b: (b, 0)
    n_groups = (rows // BLOCK) * N_PLACEMENTS
    return pl.pallas_call(
        functools.partial(_ctx_attn_kernel, seq=seq),
        grid=(n_tok // rows,),
        in_specs=[
            pl.BlockSpec(memory_space=pltpu.SMEM),
            pl.BlockSpec((rows, ATTN_WIDTH), row),
            pl.BlockSpec((rows, KVP_WIDTH), row),
        ],
        out_specs=pl.BlockSpec((rows, ATTN_WIDTH), row),
        out_shape=jax.ShapeDtypeStruct((n_tok, ATTN_WIDTH), BF16),
        scratch_shapes=[pltpu.VMEM((N_PLACEMENTS, rows, LANES), BF16)] * 2 + _attn_scratch(n_groups, seq),
        compiler_params=_params("arbitrary"),
        name="context_attention",
    )(sink, q, kvp)


def _lat_attn_kernel(sink_ref, q_ref, kvp_prev, kvp_own, kvp_next, ck_ref, cv_ref, o_ref,
                     kwin, vwin, kctx, vctx, q2s, s_buf, p_buf, t_buf, o_buf, *, n_steps, past):
    i = pl.program_id(1)
    n_blocks = q_ref.shape[0] // BLOCK
    own_end = BLOCK + q_ref.shape[0]
    win = 3 * BLOCK
    _init_pipeline_scratch(jnp.logical_and(pl.program_id(0) == 0, i == 0), (s_buf, p_buf), (t_buf,))

    @pl.when(i == 0)
    def _():
        cpl = _kv_placements(ck_ref[0].T, cv_ref[0].T)
        for idx in range(N_PLACEMENTS):
            kctx[idx] = cpl[:, idx * LANES:(idx + 1) * LANES]
            vctx[idx] = cpl[:, (N_PLACEMENTS + idx) * LANES:(N_PLACEMENTS + idx + 1) * LANES]

    for src, r0, r1 in ((kvp_prev, 0, BLOCK), (kvp_own, BLOCK, own_end), (kvp_next, own_end, own_end + BLOCK)):
        for idx in range(N_PLACEMENTS):
            kwin[idx, r0:r1, :] = src[:, idx * LANES:(idx + 1) * LANES]
            vwin[idx, r0:r1, :] = src[:, (N_PLACEMENTS + idx) * LANES:(N_PLACEMENTS + idx + 1) * LANES]
    _stack_queries(q_ref, q2s, n_blocks)

    row = lax.broadcasted_iota(jnp.int32, (SOFTMAX_ROWS, LANES), 0)
    c = lax.broadcasted_iota(jnp.int32, (SOFTMAX_ROWS, LANES), 1)
    off_prev = jnp.where(i > 0, 0, BLOCK)
    off_next = jnp.where(i < n_steps - 1, 0, BLOCK)

    def window_of(g):
        return _key_rows((g // N_PLACEMENTS) * BLOCK, win)

    def qk(g, u, valid):
        q2 = q2s[g // 2]
        s_buf[u, :, 0:win] = _dot_nt(q2, kwin[g % N_PLACEMENTS, window_of(g), :])
        s_buf[u, :, win:win + past] = _dot_nt(q2, kctx[g % N_PLACEMENTS])

    def softmax(g, u, valid):
        blk = g // N_PLACEMENTS
        off0 = jnp.where(blk == 0, off_prev, 0)
        off2 = jnp.where(blk == n_blocks - 1, off_next, 0)

        def masks_of(r0):
            r = row + r0 % BLOCK
            return {0: c >= r + off0, 2: c <= r - off2}

        _softmax_rows(s_buf, p_buf, t_buf, u, 2 * BLOCK, win + past, masks_of, _group_sinks(sink_ref, g))

    def pv(g, u, valid):
        acc = _dot(p_buf[u, :, 0:win], vwin[g % N_PLACEMENTS, window_of(g), :])
        acc = acc + _dot(p_buf[u, :, win:win + past], vctx[g % N_PLACEMENTS])
        o_buf[g] = _normalize(acc, t_buf, u)

    _software_pipeline(n_blocks * N_PLACEMENTS, PIPE_SLOTS, (qk, softmax, pv))
    _write_heads(o_buf, o_ref, n_blocks)


def _latent_attention(q, kvp, ck, cv, sink, n_batch, seq):
    ns = seq // ATTN_TILE
    nb = seq // BLOCK
    per = ATTN_TILE // BLOCK
    own = lambda b, i: (b * ns + i, 0)
    prev = lambda b, i: (b * nb + jnp.maximum(per * i - 1, 0), 0)
    nxt = lambda b, i: (b * nb + jnp.minimum(per * i + per, nb - 1), 0)
    ctx = lambda b, i: (b, 0, 0)
    past = ck.shape[2]
    win_rows = ATTN_TILE + 2 * BLOCK
    bf = lambda *shape: pltpu.VMEM(shape, BF16)
    return pl.pallas_call(
        functools.partial(_lat_attn_kernel, n_steps=ns, past=past),
        grid=(n_batch, ns),
        in_specs=[
            pl.BlockSpec(memory_space=pltpu.SMEM),
            pl.BlockSpec((ATTN_TILE, ATTN_WIDTH), own),
            pl.BlockSpec((BLOCK, KVP_WIDTH), prev),
            pl.BlockSpec((ATTN_TILE, KVP_WIDTH), own),
            pl.BlockSpec((BLOCK, KVP_WIDTH), nxt),
            pl.BlockSpec((1, KV_WIDTH, past), ctx),
            pl.BlockSpec((1, KV_WIDTH, past), ctx),
        ],
        out_specs=pl.BlockSpec((ATTN_TILE, ATTN_WIDTH), own),
        out_shape=jax.ShapeDtypeStruct((n_batch * seq, ATTN_WIDTH), BF16),
        scratch_shapes=[bf(N_PLACEMENTS, win_rows, LANES), bf(N_PLACEMENTS, win_rows, LANES),
                        bf(N_PLACEMENTS, past, LANES), bf(N_PLACEMENTS, past, LANES)]
        + _attn_scratch(per * N_PLACEMENTS, 3 * BLOCK + past),
        compiler_params=_params("arbitrary", "arbitrary"),
        name="latent_attention",
    )(sink, q, kvp, kvp, kvp, ck, cv)


_PK = 2 * GLA_DK
_PV = 2 * GLA_DV


def _cumsum_rows(x, reverse):
    n = x.shape[0]
    row = lax.broadcasted_iota(jnp.int32, x.shape, 0)
    s = 1
    while s < n:
        if reverse:
            x = x + jnp.where(row < n - s, pltpu.roll(x, n - s, 0), 0.0)
        else:
            x = x + jnp.where(row >= s, pltpu.roll(x, s, 0), 0.0)
        s *= 2
    return x


def _gla_kernel(*refs, seq_chunks, has_init, emit_state):
    it = iter(refs)
    views = [tuple(next(it) for _ in range(3)) for _ in range(2)]
    s0_ref = next(it) if has_init else None
    out_refs = (next(it), next(it))
    sout_ref = next(it) if emit_state else None
    st_ref, a_buf, a2_buf, qe_buf, qd_buf, ds_buf, sb_buf, dc_buf = (next(it) for _ in range(8))
    n_chunks = views[0][0].shape[0] // CHUNK
    n_groups = 2 * n_chunks
    n_pairs = N_GLA_HEADS // 2
    tile = pl.program_id(1)
    q_scale = GLA_DK ** -0.5
    _init_pipeline_scratch(jnp.logical_and(pl.program_id(0) == 0, tile == 0),
                           (st_ref, a_buf, a2_buf, qe_buf, qd_buf, ds_buf, sb_buf), (dc_buf,))

    if has_init:
        @pl.when(tile == 0)
        def _():
            st_ref[...] = jnp.zeros_like(st_ref)
            for d in range(2):
                for h in range(N_GLA_HEADS):
                    p, e = divmod(h, 2)
                    st_ref[d, p, e * GLA_DK:(e + 1) * GLA_DK, e * GLA_DV:(e + 1) * GLA_DV] = s0_ref[0, d, h]

    lane_k = lax.broadcasted_iota(jnp.int32, (CHUNK, _PK), 1)
    row_k = lax.broadcasted_iota(jnp.int32, (CHUNK, _PK), 0)
    lo_k = lane_k < GLA_DK
    tri = (lane_k % CHUNK <= row_k, lane_k % CHUNK >= row_k)
    lo_v = lax.broadcasted_iota(jnp.int32, (CHUNK, _PV), 1) < GLA_DV
    diag = ((lax.broadcasted_iota(jnp.int32, (_PK, _PV), 0) < GLA_DK)
            == (lax.broadcasted_iota(jnp.int32, (_PK, _PV), 1) < GLA_DV))

    def place(g, d):
        c = g // 2
        if d == 0:
            return c, c
        seq = c // seq_chunks
        return c, seq * seq_chunks + (seq_chunks - 1 - (c - seq * seq_chunks))

    def rows_of(pos):
        return pl.ds(pl.multiple_of(pos * CHUNK, CHUNK), CHUNK)

    def prep(g, u, valid):
        d = u % 2
        qk_ref, g_ref, v_ref = views[d]
        _, pos = place(g, d)
        rows = rows_of(pos)
        gc = _cumsum_rows(g_ref[rows, :], reverse=(d == 1))
        gtot = gc[0:1, :] if d == 1 else gc[CHUNK - 1:CHUNK, :]
        k = qk_ref[rows, GLA_K_WIDTH:2 * GLA_K_WIDTH]
        qe = qk_ref[rows, 0:GLA_K_WIDTH] * jnp.exp(gc)
        qe_buf[u] = qe
        q_dec = (qe * q_scale).astype(BF16)
        k_inv = k * jnp.exp(-gc)
        k_end = jnp.where(valid, k * jnp.exp(gtot - gc), 0.0).astype(BF16)
        decay = jnp.where(valid, jnp.exp(gtot), 1.0)
        for p in range(n_pairs):
            lanes = slice(p * _PK, (p + 1) * _PK)
            ki = k_inv[:, lanes]
            kst = jnp.concatenate([jnp.where(lo_k, ki, 0.0), jnp.where(lo_k, 0.0, ki)], axis=0).astype(BF16)
            a_buf[u, p] = _dot_nt(q_dec[:, lanes], kst)
            ds_buf[u, p] = jnp.where(diag, _dot_tn(k_end[:, lanes], v_ref[rows, p * _PV:(p + 1) * _PV]), 0.0)
            dc_buf[u, p] = jnp.broadcast_to(decay[:, lanes], (_PK, _PK)).T

    def state(g, u, valid):
        d = u % 2
        c, _ = place(g, d)
        qd_buf[u] = (qe_buf[u] * q_scale).astype(BF16)
        for p in range(n_pairs):
            fresh = (c % seq_chunks == 0) if emit_state else (c < 0)
            s_in = jnp.where(fresh, 0.0, st_ref[d, p])
            sb_buf[u, p] = s_in.astype(BF16)
            a2_buf[u, p] = jnp.where(tri[d], a_buf[u, p], 0.0).astype(BF16)
            dc = dc_buf[u, p]
            s_new = s_in * jnp.concatenate([dc, dc], axis=1) + ds_buf[u, p]
            st_ref[d, p] = s_new
            if emit_state:
                for e in range(2):
                    sout_ref[c // seq_chunks, d, 2 * p + e] = s_new[e * GLA_DK:(e + 1) * GLA_DK,
                                                                    e * GLA_DV:(e + 1) * GLA_DV]

    def out(g, u, valid):
        d = u % 2
        _, _, v_ref = views[d]
        _, pos = place(g, d)
        rows = rows_of(pos)
        for p in range(n_pairs):
            vpf = v_ref[rows, p * _PV:(p + 1) * _PV].astype(F32)
            vbd = jnp.concatenate([jnp.where(lo_v, vpf, 0.0), jnp.where(lo_v, 0.0, vpf)], axis=0).astype(BF16)
            x = jnp.concatenate([a2_buf[u, p], qd_buf[u, :, p * _PK:(p + 1) * _PK]], axis=1)
            w = jnp.concatenate([vbd, sb_buf[u, p]], axis=0)
            out_refs[d][rows, p * _PV:(p + 1) * _PV] = _dot(x, w).astype(out_refs[d].dtype)

    _software_pipeline(n_groups, GLA_PIPE_SLOTS, (prep, state, out))


def _gla(qkg, gf, gb, vg, s0, n_batch, seq):
    tm = GLA_TILE
    long_seq = s0 is not None
    if long_seq:
        nt = seq // tm
        grid = (n_batch, nt)
        fwd = lambda b, i: (b * nt + i, 0)
        bwd = lambda b, i: (b * nt + nt - 1 - i, 0)
        seq_chunks = tm // CHUNK
    else:
        per_step = tm // seq
        grid = (1, n_batch // per_step)
        fwd = bwd = lambda b, i: (i, 0)
        seq_chunks = seq // CHUNK
    in_specs = []
    args = []
    for m, gates in ((fwd, gf), (bwd, gb)):
        in_specs += [pl.BlockSpec((tm, 2 * GLA_K_WIDTH), m), pl.BlockSpec((tm, GLA_K_WIDTH), m),
                     pl.BlockSpec((tm, GLA_V_WIDTH), m)]
        args += [qkg, gates, vg]
    n_tok = n_batch * seq
    out_specs = [pl.BlockSpec((tm, GLA_V_WIDTH), fwd), pl.BlockSpec((tm, GLA_V_WIDTH), bwd)]
    out_shape = [jax.ShapeDtypeStruct((n_tok, GLA_V_WIDTH), BF16)] * 2
    st_dims = (2, N_GLA_HEADS, GLA_DK, GLA_DV)
    if long_seq:
        in_specs.append(pl.BlockSpec((1,) + st_dims, lambda b, i: (b, 0, 0, 0, 0)))
        args.append(s0)
    else:
        out_specs.append(pl.BlockSpec((per_step,) + st_dims, lambda b, i: (i, 0, 0, 0, 0)))
        out_shape.append(jax.ShapeDtypeStruct((n_batch,) + st_dims, F32))
    n = GLA_PIPE_SLOTS
    n_pairs = N_GLA_HEADS // 2
    return pl.pallas_call(
        functools.partial(_gla_kernel, seq_chunks=seq_chunks, has_init=long_seq, emit_state=not long_seq),
        grid=grid,
        in_specs=in_specs,
        out_specs=out_specs,
        out_shape=out_shape,
        scratch_shapes=[pltpu.VMEM((2, n_pairs, _PK, _PV), F32),
                        pltpu.VMEM((n, n_pairs, CHUNK, _PK), F32),
                        pltpu.VMEM((n, n_pairs, CHUNK, _PK), BF16),
                        pltpu.VMEM((n, CHUNK, GLA_K_WIDTH), F32),
                        pltpu.VMEM((n, CHUNK, GLA_K_WIDTH), BF16),
                        pltpu.VMEM((n, n_pairs, _PK, _PV), F32),
                        pltpu.VMEM((n, n_pairs, _PK, _PV), BF16),
                        pltpu.VMEM((n, n_pairs, _PK, _PK), F32)],
        compiler_params=_params("arbitrary", "arbitrary"),
        name="gla_long" if long_seq else "gla_short",
    )(*args)


def _merge_kernel(x_ref, mod_ref, oa_ref, za_ref, of_ref, ob_ref, zg_ref, gng_ref, wo_ref, fng_ref, y_ref):
    ya = oa_ref[...].astype(F32) * _silu(za_ref[...].astype(F32))
    og = of_ref[...].astype(F32) + ob_ref[...].astype(F32)
    zg = zg_ref[...].astype(F32)
    gng = gng_ref[...]
    parts = [ya]
    for h in range(N_GLA_HEADS):
        oh = og[:, h * GLA_DV:(h + 1) * GLA_DV]
        nh = oh * lax.rsqrt(jnp.mean(oh * oh, axis=-1, keepdims=True) + EPS)
        parts.append((nh * gng) * _silu(zg[:, h * GLA_DV:(h + 1) * GLA_DV]))
    yin = jnp.concatenate(parts, axis=-1).astype(BF16)
    gate = mod_ref[0][:, 2 * D_MODEL:3 * D_MODEL]
    y = x_ref[...] + gate * _dot(yin, wo_ref[...])
    y_ref[...] = (y * lax.rsqrt(jnp.mean(y * y, axis=-1, keepdims=True) + EPS)) * fng_ref[...]


def _merge(x2d, mod3, mod_row_of_tile, oa, za, of, ob, zg, gla_norm_g, w_out, final_norm_g, tm):
    n_tok = x2d.shape[0]
    row = lambda i: (i, 0)
    const = lambda i: (0, 0)
    wide = pl.BlockSpec((tm, GLA_V_WIDTH), row)
    return pl.pallas_call(
        _merge_kernel,
        grid=(n_tok // tm,),
        in_specs=[
            pl.BlockSpec((tm, D_MODEL), row),
            pl.BlockSpec((1, 1, 3 * D_MODEL), lambda i: (mod_row_of_tile(i), 0, 0)),
            wide, wide, wide, wide, wide,
            pl.BlockSpec((1, GLA_DV), const),
            pl.BlockSpec((D_MODEL, D_MODEL), const),
            pl.BlockSpec((1, D_MODEL), const),
        ],
        out_specs=pl.BlockSpec((tm, D_MODEL), row),
        out_shape=jax.ShapeDtypeStruct((n_tok, D_MODEL), F32),
        compiler_params=_params("arbitrary"),
        name="merge",
    )(x2d, mod3, oa, za, of, ob, zg, gla_norm_g, w_out, final_norm_g)


def _rope_tables(n_tok):
    t = np.arange(n_tok)
    rowp = (t // GRID_W).astype(np.float64)
    colp = (t % GRID_W).astype(np.float64)
    nf = HEAD_DIM // 4
    freqs = ROPE_BASE ** (-np.arange(nf, dtype=np.float64) / nf)
    lane = np.arange(LANES)
    pos = np.where(((lane % HEAD_DIM) < HEAD_DIM // 2)[None, :], rowp[:, None], colp[:, None])
    ang = pos * freqs[lane % nf][None, :]
    sign = np.where((lane % (2 * nf)) < nf, -1.0, 1.0)
    return jnp.asarray(np.cos(ang), F32), jnp.asarray(np.sin(ang) * sign[None, :], F32)


def kernel(x_prompt, x_sample, c, cache_k, cache_v, state_gla, c_ctx, w_mod, b_mod, norm_g, w_in,
           w_gk_f, b_gk_f, w_gk_b, b_gk_b, sink, gla_norm_g, w_out, final_norm_g):
    depth = w_in.shape[0]
    assert depth == 1, "single-layer step"
    l = 0
    bp, sp, _ = x_prompt.shape
    bs, ss, _ = x_sample.shape

    wl = w_in[l]
    wgk = jnp.zeros((LANES, 2 * GLA_K_WIDTH), F32)
    wgk = wgk.at[0:GATE_RANK, 0:GLA_K_WIDTH].set(w_gk_f[l])
    wgk = wgk.at[GATE_RANK:2 * GATE_RANK, GLA_K_WIDTH:].set(w_gk_b[l]).astype(BF16)
    bgk = jnp.concatenate([b_gk_f[l], b_gk_b[l]])[None, :]
    wo = w_out[l].astype(BF16)
    ng = norm_g[l][None, :]
    gng = gla_norm_g[l][None, :]
    fng = final_norm_g[None, :]
    sk = sink[l]

    n_rows = 8
    cvecs = jnp.concatenate([c_ctx[None, :], c, jnp.zeros((n_rows - 1 - bs, D_MODEL), F32)], axis=0)
    mod3 = _modulation(cvecs, w_mod[l], b_mod[l][None, :]).reshape(n_rows, 1, 3 * D_MODEL)
    tm = TOKEN_TILE

    xp2 = x_prompt.reshape(bp * sp, D_MODEL)
    ctx_row = lambda i: 0
    q, kvp, za, vg, zg, qkg, gf, gb, kt, vt = _project(xp2, mod3, ctx_row, ng, wl, wgk, bgk, None, tm, sp)
    oa = _context_attention(q, kvp, sk, sp)
    of, ob, st = _gla(qkg, gf, gb, vg, None, bp, sp)
    y_prompt = _merge(xp2, mod3, ctx_row, oa, za, of, ob, zg, gng, wo, fng, MERGE_TILE).reshape(bp, sp, D_MODEL)
    to_cache = lambda t: t.reshape(bp, 1, N_KV_HEADS, HEAD_DIM, sp).transpose(0, 1, 4, 2, 3)
    new_k, new_v = to_cache(kt), to_cache(vt)
    new_state = st[:, None]

    xs2 = x_sample.reshape(bs * ss, D_MODEL)
    tiles_per_seq = ss // tm
    lat_row = lambda i: 1 + i // tiles_per_seq
    q, kvp, za, vg, zg, qkg, gf, gb = _project(xs2, mod3, lat_row, ng, wl, wgk, bgk, _rope_tables(ss), tm, ss)
    past = cache_k.shape[2]
    from_cache = lambda t: t[:, l].transpose(0, 2, 3, 1).reshape(bs, KV_WIDTH, past)
    ck, cv = from_cache(cache_k), from_cache(cache_v)
    oa = _latent_attention(q, kvp, ck, cv, sk, bs, ss)
    of, ob = _gla(qkg, gf, gb, vg, state_gla[:, l], bs, ss)
    merge_row = lambda i: 1 + i // (ss // MERGE_TILE)
    y_sample = _merge(xs2, mod3, merge_row, oa, za, of, ob, zg, gng, wo, fng, MERGE_TILE).reshape(bs, ss, D_MODEL)

    return (y_prompt, y_sample, new_k, new_v, new_state)
```

```python
import functools
import math

import numpy as np
import jax
import jax.numpy as jnp
from jax import lax
from jax.experimental import pallas as pl
from jax.experimental.pallas import tpu as pltpu

D_MODEL = 1024
GRID_W = 64
HEAD_DIM = 64
N_ATTN_HEADS = 8
N_KV_HEADS = 2
ATTN_WIDTH = 512
KV_WIDTH = 128
BLOCK = 128
N_GLA_HEADS = 4
GLA_DK = 64
GLA_DV = 128
GLA_K_WIDTH = 256
GLA_V_WIDTH = 512
GATE_RANK = 16
GATE_NORMALIZER = 16.0
CHUNK = 64
ROPE_BASE = 10000.0
EPS = 1e-6
NEG_INF = -1e30

KV_GROUP = N_ATTN_HEADS // N_KV_HEADS
LANES = 128
SUBLANES = 8
TOKEN_TILE = 1024
MERGE_TILE = 1024
GLA_TILE = 2048
ATTN_TILE = 2048
CTX_ATTN_BATCH = 8
PIPE_SLOTS = 4
GLA_PIPE_SLOTS = 4
KVP_WIDTH = 8 * LANES
N_PLACEMENTS = 2 * N_KV_HEADS
SOFTMAX_ROWS = 32
MOD_TILE = 1024
VMEM_LIMIT = 56 * 1024 * 1024
LOG2E = math.log2(math.e)

F32 = jnp.float32
BF16 = jnp.bfloat16


def _dot(a, b):
    return jnp.dot(a, b, preferred_element_type=F32)


def _dot_nt(a, b):
    return lax.dot_general(a, b, (((1,), (1,)), ((), ())), preferred_element_type=F32)


def _dot_tn(a, b):
    return lax.dot_general(a, b, (((0,), (0,)), ((), ())), preferred_element_type=F32)


def _silu(x):
    h = 0.5 * x
    return h + h * jnp.tanh(h)


def _log_sigmoid(x):
    return jnp.minimum(x, 0.0) - jnp.log1p(jnp.exp(-jnp.abs(x)))


def _params(*sem):
    return pltpu.CompilerParams(dimension_semantics=sem, vmem_limit_bytes=VMEM_LIMIT)


def _split_bf16(x):
    hi = x.astype(BF16)
    return hi, (x - hi.astype(F32)).astype(BF16)


def _mod_kernel(c_ref, w_ref, b_ref, o_ref):
    rows = c_ref.shape[0]
    s_hi, s_lo = _split_bf16(_silu(c_ref[...]))
    w_hi, w_lo = _split_bf16(w_ref[...])
    by_hi = _dot(jnp.concatenate([s_hi, s_lo], axis=0), w_hi)
    o_ref[...] = by_hi[0:rows] + by_hi[rows:2 * rows] + _dot(s_hi, w_lo) + b_ref[...]


def _modulation(cvecs, w_mod, b_mod):
    rows = cvecs.shape[0]
    n_out = w_mod.shape[1]
    return pl.pallas_call(
        _mod_kernel,
        grid=(n_out // MOD_TILE,),
        in_specs=[
            pl.BlockSpec((rows, D_MODEL), lambda j: (0, 0)),
            pl.BlockSpec((D_MODEL, MOD_TILE), lambda j: (0, j)),
            pl.BlockSpec((1, MOD_TILE), lambda j: (0, j)),
        ],
        out_specs=pl.BlockSpec((rows, MOD_TILE), lambda j: (0, j)),
        out_shape=jax.ShapeDtypeStruct((rows, n_out), F32),
        compiler_params=_params("arbitrary"),
        name="adaln_mod",
    )(cvecs, w_mod, b_mod)


_Q0, _K0, _V0, _ZA0, _QG0, _KG0, _VG0, _TAIL0 = 0, 512, 640, 768, 1280, 1536, 1792, 2304
_TAIL_COLS = 2 * GATE_RANK + GLA_V_WIDTH
_Q_PRESCALE = (HEAD_DIM ** -0.5) * LOG2E


def _rope(x, cos, sin_signed):
    n = x.shape[1] // LANES
    lane = lax.broadcasted_iota(jnp.int32, (x.shape[0], LANES), 1)
    first = (lane % 32) < 16
    outs = []
    for c in range(n):
        xc = x[:, c * LANES:(c + 1) * LANES]
        partner = jnp.where(first, pltpu.roll(xc, LANES - 16, 1), pltpu.roll(xc, 16, 1))
        outs.append(xc * cos + partner * sin_signed)
    return outs[0] if n == 1 else jnp.concatenate(outs, axis=-1)


def _kv_placements(k, v):
    lane = lax.broadcasted_iota(jnp.int32, k.shape, 1)
    lo = lane < HEAD_DIM
    kr = pltpu.roll(k, HEAD_DIM, 1)
    vr = pltpu.roll(v, HEAD_DIM, 1)
    pieces = [jnp.where(lo, k, 0.0), jnp.where(lo, 0.0, kr), jnp.where(lo, kr, 0.0), jnp.where(lo, 0.0, k),
              jnp.where(lo, v, 1.0), jnp.where(lo, 1.0, vr), jnp.where(lo, vr, 1.0), jnp.where(lo, 1.0, v)]
    return jnp.concatenate(pieces, axis=-1).astype(BF16)


def _proj_kernel(*refs, rope):
    if rope:
        (x_ref, mod_ref, ng_ref, w1_ref, wtail_ref, wgk_ref, bgk_ref, cos_ref, sin_ref,
         q_ref, kvp_ref, za_ref, vg_ref, zg_ref, qkg_ref, gf_ref, gb_ref) = refs
    else:
        (x_ref, mod_ref, ng_ref, w1_ref, wtail_ref, wgk_ref, bgk_ref,
         q_ref, kvp_ref, za_ref, vg_ref, zg_ref, qkg_ref, gf_ref, gb_ref, k_ref, v_ref) = refs
    x = x_ref[...]
    xn = x * lax.rsqrt(jnp.mean(x * x, axis=-1, keepdims=True) + EPS)
    mod = mod_ref[0]
    shift = mod[:, 0:D_MODEL]
    scale = mod[:, D_MODEL:2 * D_MODEL]
    h = ((xn * ng_ref[...]) * (1.0 + scale) + shift).astype(BF16)

    def piece(c0, c1):
        return _dot(h, w1_ref[:, c0:c1])

    q = piece(_Q0, _K0)
    kv = piece(_K0, _ZA0)
    k = kv[:, 0:KV_WIDTH]
    v = kv[:, KV_WIDTH:2 * KV_WIDTH]
    if rope:
        cos = cos_ref[...]
        sin = sin_ref[...]
        q = _rope(q, cos, sin)
        k = _rope(k, cos, sin)
    else:
        seq = k_ref.shape[2]
        for b in range(k_ref.shape[0]):
            k_ref[b] = k[b * seq:(b + 1) * seq, :].T
            v_ref[b] = v[b * seq:(b + 1) * seq, :].T
    q_ref[...] = (q * _Q_PRESCALE).astype(BF16)
    kvp_ref[...] = _kv_placements(k, v)
    za_ref[...] = piece(_ZA0, _QG0).astype(BF16)
    vg_ref[...] = piece(_VG0, _TAIL0).astype(BF16)
    qkg_ref[...] = piece(_QG0, _VG0)
    tail = _dot(h, wtail_ref[...])
    zg_ref[...] = tail[:, 2 * GATE_RANK:_TAIL_COLS].astype(BF16)
    lane = lax.broadcasted_iota(jnp.int32, (tail.shape[0], LANES), 1)
    lr = jnp.where(lane < 2 * GATE_RANK, tail[:, 0:LANES], 0.0).astype(BF16)
    g = _log_sigmoid(_dot(lr, wgk_ref[...]) + bgk_ref[...]) * (1.0 / GATE_NORMALIZER)
    gf_ref[...] = g[:, 0:GLA_K_WIDTH]
    gb_ref[...] = g[:, GLA_K_WIDTH:2 * GLA_K_WIDTH]


def _project(x2d, mod3, mod_row_of_tile, norm_g, w1, wtail, wgk, bgk, rope_tabs, tm, seq):
    n_tok = x2d.shape[0]
    rope = rope_tabs is not None
    row = lambda i: (i, 0)
    const = lambda i: (0, 0)
    in_specs = [
        pl.BlockSpec((tm, D_MODEL), row),
        pl.BlockSpec((1, 1, 3 * D_MODEL), lambda i: (mod_row_of_tile(i), 0, 0)),
        pl.BlockSpec((1, D_MODEL), const),
        pl.BlockSpec((D_MODEL, _TAIL0), const),
        pl.BlockSpec((D_MODEL, _TAIL_COLS), const),
        pl.BlockSpec((LANES, 2 * GLA_K_WIDTH), const),
        pl.BlockSpec((1, 2 * GLA_K_WIDTH), const),
    ]
    args = [x2d, mod3, norm_g, w1, wtail, wgk, bgk]
    if rope:
        tiles_per_seq = rope_tabs[0].shape[0] // tm
        pos = lambda i: (i % tiles_per_seq, 0)
        in_specs += [pl.BlockSpec((tm, LANES), pos), pl.BlockSpec((tm, LANES), pos)]
        args += list(rope_tabs)
    widths = [(ATTN_WIDTH, BF16), (KVP_WIDTH, BF16), (ATTN_WIDTH, BF16), (GLA_V_WIDTH, BF16), (GLA_V_WIDTH, BF16),
              (2 * GLA_K_WIDTH, F32), (GLA_K_WIDTH, F32), (GLA_K_WIDTH, F32)]
    out_specs = [pl.BlockSpec((tm, w), row) for w, _ in widths]
    out_shape = [jax.ShapeDtypeStruct((n_tok, w), dt) for w, dt in widths]
    if not rope:
        for _ in range(2):
            out_specs.append(pl.BlockSpec((tm // seq, KV_WIDTH, seq), lambda i: (i, 0, 0)))
            out_shape.append(jax.ShapeDtypeStruct((n_tok // seq, KV_WIDTH, seq), F32))
    return pl.pallas_call(
        functools.partial(_proj_kernel, rope=rope),
        grid=(n_tok // tm,),
        in_specs=in_specs,
        out_specs=out_specs,
        out_shape=out_shape,
        compiler_params=_params("arbitrary"),
        name="project_rope" if rope else "project",
    )(*args)


def _software_pipeline(n_groups, n_slots, stages):
    last_g = n_groups - 1

    def body(k, carry):
        for age in (2, 1, 0):
            for u in range(n_slots):
                g = n_slots * (k - age) + u
                stages[age](jnp.clip(g, 0, last_g), u, jnp.logical_and(g >= 0, g <= last_g))
        return carry

    lax.fori_loop(0, n_groups // n_slots + 2, body, 0)


def _init_pipeline_scratch(first, zero_refs, one_refs=()):
    @pl.when(first)
    def _():
        for ref in zero_refs:
            ref[...] = jnp.zeros_like(ref)
        for ref in one_refs:
            ref[...] = jnp.ones_like(ref)


def _key_rows(start, size):
    return pl.ds(pl.multiple_of(start, BLOCK), size)


def _softmax_rows(s_buf, p_buf, t_buf, slot, n_rows, n_keys, masks_of, sink_of):
    for r0 in range(0, n_rows, SOFTMAX_ROWS):
        rs = slice(r0, r0 + SOFTMAX_ROWS)
        masks = masks_of(r0)
        cols = []
        for c in range(n_keys // LANES):
            sc = s_buf[slot, rs, c * LANES:(c + 1) * LANES]
            cols.append(jnp.where(masks[c], sc, NEG_INF) if c in masks else sc)
        mx = cols[0]
        for sc in cols[1:]:
            mx = jnp.maximum(mx, sc)
        sink = sink_of(r0)
        m = jnp.maximum(jnp.max(mx, axis=-1, keepdims=True), sink)
        for c, sc in enumerate(cols):
            p_buf[slot, rs, c * LANES:(c + 1) * LANES] = jnp.exp2(sc - m).astype(BF16)
        t_buf[slot, rs, :] = jnp.broadcast_to(jnp.exp2(sink - m), (SOFTMAX_ROWS, LANES))


def _normalize(acc, t_buf, slot):
    return acc / (pltpu.roll(acc, HEAD_DIM, 1) + t_buf[slot])


def _attn_scratch(n_groups, n_keys):
    rows = 2 * BLOCK
    return [pltpu.VMEM((n_groups // 2, rows, LANES), BF16),
            pltpu.VMEM((PIPE_SLOTS, rows, n_keys), F32), pltpu.VMEM((PIPE_SLOTS, rows, n_keys), BF16),
            pltpu.VMEM((PIPE_SLOTS, rows, LANES), F32), pltpu.VMEM((n_groups, rows, LANES), F32)]


def _stack_queries(q_ref, q2s, n_blocks):
    for blk in range(n_blocks):
        rows = slice(blk * BLOCK, (blk + 1) * BLOCK)
        for j in range(N_KV_HEADS):
            idx = blk * N_KV_HEADS + j
            q2s[idx, 0:BLOCK, :] = q_ref[rows, (2 * j) * LANES:(2 * j + 1) * LANES]
            q2s[idx, BLOCK:2 * BLOCK, :] = q_ref[rows, (2 * j + 1) * LANES:(2 * j + 2) * LANES]


def _write_heads(o_buf, o_ref, n_blocks):
    lo = lax.broadcasted_iota(jnp.int32, (2 * BLOCK, LANES), 1) < HEAD_DIM
    for blk in range(n_blocks):
        rows = slice(blk * BLOCK, (blk + 1) * BLOCK)
        for j in range(N_KV_HEADS):
            g0 = blk * N_PLACEMENTS + 2 * j
            o = jnp.where(lo, o_buf[g0], o_buf[g0 + 1]).astype(o_ref.dtype)
            o_ref[rows, (2 * j) * LANES:(2 * j + 1) * LANES] = o[0:BLOCK]
            o_ref[rows, (2 * j + 1) * LANES:(2 * j + 2) * LANES] = o[BLOCK:2 * BLOCK]


def _group_sinks(sink_ref, g):
    j, e = (g // 2) % N_KV_HEADS, g % 2
    sink_a = sink_ref[KV_GROUP * j + e] * LOG2E
    sink_b = sink_ref[KV_GROUP * j + 2 + e] * LOG2E
    return lambda r0: sink_a if r0 < BLOCK else sink_b


def _ctx_attn_kernel(sink_ref, q_ref, kvp_ref, o_ref, kown, vown, q2s, s_buf, p_buf, t_buf, o_buf, *, seq):
    n_blocks = q_ref.shape[0] // BLOCK
    _init_pipeline_scratch(pl.program_id(0) == 0, (s_buf, p_buf), (t_buf,))
    for idx in range(N_PLACEMENTS):
        kown[idx] = kvp_ref[:, idx * LANES:(idx + 1) * LANES]
        vown[idx] = kvp_ref[:, (N_PLACEMENTS + idx) * LANES:(N_PLACEMENTS + idx + 1) * LANES]
    _stack_queries(q_ref, q2s, n_blocks)
    blocks_per_seq = seq // BLOCK

    def keys_of(g):
        return _key_rows(((g // N_PLACEMENTS) // blocks_per_seq) * seq, seq)

    def qk(g, u, valid):
        s_buf[u] = _dot_nt(q2s[g // 2], kown[g % N_PLACEMENTS, keys_of(g), :])

    def softmax(g, u, valid):
        _softmax_rows(s_buf, p_buf, t_buf, u, 2 * BLOCK, seq, lambda r0: {}, _group_sinks(sink_ref, g))

    def pv(g, u, valid):
        o_buf[g] = _normalize(_dot(p_buf[u], vown[g % N_PLACEMENTS, keys_of(g), :]), t_buf, u)

    _software_pipeline(n_blocks * N_PLACEMENTS, PIPE_SLOTS, (qk, softmax, pv))
    _write_heads(o_buf, o_ref, n_blocks)


def _context_attention(q, kvp, sink, seq):
    n_tok = q.shape[0]
    rows = CTX_ATTN_BATCH * seq
    row = lambda b: (b, 0)
    n_groups = (rows // BLOCK) * N_PLACEMENTS
    return pl.pallas_call(
        functools.partial(_ctx_attn_kernel, seq=seq),
        grid=(n_tok // rows,),
        in_specs=[
            pl.BlockSpec(memory_space=pltpu.SMEM),
            pl.BlockSpec((rows, ATTN_WIDTH), row),
            pl.BlockSpec((rows, KVP_WIDTH), row),
        ],
        out_specs=pl.BlockSpec((rows, ATTN_WIDTH), row),
        out_shape=jax.ShapeDtypeStruct((n_tok, ATTN_WIDTH), BF16),
        scratch_shapes=[pltpu.VMEM((N_PLACEMENTS, rows, LANES), BF16)] * 2 + _attn_scratch(n_groups, seq),
        compiler_params=_params("arbitrary"),
        name="context_attention",
    )(sink, q, kvp)


def _lat_attn_kernel(sink_ref, q_ref, kvp_prev, kvp_own, kvp_next, ck_ref, cv_ref, o_ref,
                     kwin, vwin, kctx, vctx, q2s, s_buf, p_buf, t_buf, o_buf, *, n_steps, past):
    i = pl.program_id(1)
    n_blocks = q_ref.shape[0] // BLOCK
    own_end = BLOCK + q_ref.shape[0]
    win = 3 * BLOCK
    _init_pipeline_scratch(jnp.logical_and(pl.program_id(0) == 0, i == 0), (s_buf, p_buf), (t_buf,))

    @pl.when(i == 0)
    def _():
        cpl = _kv_placements(ck_ref[0].T, cv_ref[0].T)
        for idx in range(N_PLACEMENTS):
            kctx[idx] = cpl[:, idx * LANES:(idx + 1) * LANES]
            vctx[idx] = cpl[:, (N_PLACEMENTS + idx) * LANES:(N_PLACEMENTS + idx + 1) * LANES]

    for src, r0, r1 in ((kvp_prev, 0, BLOCK), (kvp_own, BLOCK, own_end), (kvp_next, own_end, own_end + BLOCK)):
        for idx in range(N_PLACEMENTS):
            kwin[idx, r0:r1, :] = src[:, idx * LANES:(idx + 1) * LANES]
            vwin[idx, r0:r1, :] = src[:, (N_PLACEMENTS + idx) * LANES:(N_PLACEMENTS + idx + 1) * LANES]
    _stack_queries(q_ref, q2s, n_blocks)

    row = lax.broadcasted_iota(jnp.int32, (SOFTMAX_ROWS, LANES), 0)
    c = lax.broadcasted_iota(jnp.int32, (SOFTMAX_ROWS, LANES), 1)
    off_prev = jnp.where(i > 0, 0, BLOCK)
    off_next = jnp.where(i < n_steps - 1, 0, BLOCK)

    def window_of(g):
        return _key_rows((g // N_PLACEMENTS) * BLOCK, win)

    def qk(g, u, valid):
        q2 = q2s[g // 2]
        s_buf[u, :, 0:win] = _dot_nt(q2, kwin[g % N_PLACEMENTS, window_of(g), :])
        s_buf[u, :, win:win + past] = _dot_nt(q2, kctx[g % N_PLACEMENTS])

    def softmax(g, u, valid):
        blk = g // N_PLACEMENTS
        off0 = jnp.where(blk == 0, off_prev, 0)
        off2 = jnp.where(blk == n_blocks - 1, off_next, 0)

        def masks_of(r0):
            r = row + r0 % BLOCK
            return {0: c >= r + off0, 2: c <= r - off2}

        _softmax_rows(s_buf, p_buf, t_buf, u, 2 * BLOCK, win + past, masks_of, _group_sinks(sink_ref, g))

    def pv(g, u, valid):
        acc = _dot(p_buf[u, :, 0:win], vwin[g % N_PLACEMENTS, window_of(g), :])
        acc = acc + _dot(p_buf[u, :, win:win + past], vctx[g % N_PLACEMENTS])
        o_buf[g] = _normalize(acc, t_buf, u)

    _software_pipeline(n_blocks * N_PLACEMENTS, PIPE_SLOTS, (qk, softmax, pv))
    _write_heads(o_buf, o_ref, n_blocks)


def _latent_attention(q, kvp, ck, cv, sink, n_batch, seq):
    ns = seq // ATTN_TILE
    nb = seq // BLOCK
    per = ATTN_TILE // BLOCK
    own = lambda b, i: (b * ns + i, 0)
    prev = lambda b, i: (b * nb + jnp.maximum(per * i - 1, 0), 0)
    nxt = lambda b, i: (b * nb + jnp.minimum(per * i + per, nb - 1), 0)
    ctx = lambda b, i: (b, 0, 0)
    past = ck.shape[2]
    win_rows = ATTN_TILE + 2 * BLOCK
    bf = lambda *shape: pltpu.VMEM(shape, BF16)
    return pl.pallas_call(
        functools.partial(_lat_attn_kernel, n_steps=ns, past=past),
        grid=(n_batch, ns),
        in_specs=[
            pl.BlockSpec(memory_space=pltpu.SMEM),
            pl.BlockSpec((ATTN_TILE, ATTN_WIDTH), own),
            pl.BlockSpec((BLOCK, KVP_WIDTH), prev),
            pl.BlockSpec((ATTN_TILE, KVP_WIDTH), own),
            pl.BlockSpec((BLOCK, KVP_WIDTH), nxt),
            pl.BlockSpec((1, KV_WIDTH, past), ctx),
            pl.BlockSpec((1, KV_WIDTH, past), ctx),
        ],
        out_specs=pl.BlockSpec((ATTN_TILE, ATTN_WIDTH), own),
        out_shape=jax.ShapeDtypeStruct((n_batch * seq, ATTN_WIDTH), BF16),
        scratch_shapes=[bf(N_PLACEMENTS, win_rows, LANES), bf(N_PLACEMENTS, win_rows, LANES),
                        bf(N_PLACEMENTS, past, LANES), bf(N_PLACEMENTS, past, LANES)]
        + _attn_scratch(per * N_PLACEMENTS, 3 * BLOCK + past),
        compiler_params=_params("arbitrary", "arbitrary"),
        name="latent_attention",
    )(sink, q, kvp, kvp, kvp, ck, cv)


_PK = 2 * GLA_DK
_PV = 2 * GLA_DV


def _cumsum_rows(x, reverse):
    n = x.shape[0]
    row = lax.broadcasted_iota(jnp.int32, x.shape, 0)
    s = 1
    while s < n:
        if reverse:
            x = x + jnp.where(row < n - s, pltpu.roll(x, n - s, 0), 0.0)
        else:
            x = x + jnp.where(row >= s, pltpu.roll(x, s, 0), 0.0)
        s *= 2
    return x


def _gla_kernel(*refs, seq_chunks, has_init, emit_state):
    it = iter(refs)
    views = [tuple(next(it) for _ in range(3)) for _ in range(2)]
    s0_ref = next(it) if has_init else None
    out_refs = (next(it), next(it))
    sout_ref = next(it) if emit_state else None
    st_ref, a_buf, a2_buf, qe_buf, qd_buf, ds_buf, sb_buf, dc_buf = (next(it) for _ in range(8))
    n_chunks = views[0][0].shape[0] // CHUNK
    n_groups = 2 * n_chunks
    n_pairs = N_GLA_HEADS // 2
    tile = pl.program_id(1)
    q_scale = GLA_DK ** -0.5
    _init_pipeline_scratch(jnp.logical_and(pl.program_id(0) == 0, tile == 0),
                           (st_ref, a_buf, a2_buf, qe_buf, qd_buf, ds_buf, sb_buf), (dc_buf,))

    if has_init:
        @pl.when(tile == 0)
        def _():
            st_ref[...] = jnp.zeros_like(st_ref)
            for d in range(2):
                for h in range(N_GLA_HEADS):
                    p, e = divmod(h, 2)
                    st_ref[d, p, e * GLA_DK:(e + 1) * GLA_DK, e * GLA_DV:(e + 1) * GLA_DV] = s0_ref[0, d, h]

    lane_k = lax.broadcasted_iota(jnp.int32, (CHUNK, _PK), 1)
    row_k = lax.broadcasted_iota(jnp.int32, (CHUNK, _PK), 0)
    lo_k = lane_k < GLA_DK
    tri = (lane_k % CHUNK <= row_k, lane_k % CHUNK >= row_k)
    lo_v = lax.broadcasted_iota(jnp.int32, (CHUNK, _PV), 1) < GLA_DV
    diag = ((lax.broadcasted_iota(jnp.int32, (_PK, _PV), 0) < GLA_DK)
            == (lax.broadcasted_iota(jnp.int32, (_PK, _PV), 1) < GLA_DV))

    def place(g, d):
        c = g // 2
        if d == 0:
            return c, c
        seq = c // seq_chunks
        return c, seq * seq_chunks + (seq_chunks - 1 - (c - seq * seq_chunks))

    def rows_of(pos):
        return pl.ds(pl.multiple_of(pos * CHUNK, CHUNK), CHUNK)

    def prep(g, u, valid):
        d = u % 2
        qk_ref, g_ref, v_ref = views[d]
        _, pos = place(g, d)
        rows = rows_of(pos)
        gc = _cumsum_rows(g_ref[rows, :], reverse=(d == 1))
        gtot = gc[0:1, :] if d == 1 else gc[CHUNK - 1:CHUNK, :]
        k = qk_ref[rows, GLA_K_WIDTH:2 * GLA_K_WIDTH]
        qe = qk_ref[rows, 0:GLA_K_WIDTH] * jnp.exp(gc)
        qe_buf[u] = qe
        q_dec = (qe * q_scale).astype(BF16)
        k_inv = k * jnp.exp(-gc)
        k_end = jnp.where(valid, k * jnp.exp(gtot - gc), 0.0).astype(BF16)
        decay = jnp.where(valid, jnp.exp(gtot), 1.0)
        for p in range(n_pairs):
            lanes = slice(p * _PK, (p + 1) * _PK)
            ki = k_inv[:, lanes]
            kst = jnp.concatenate([jnp.where(lo_k, ki, 0.0), jnp.where(lo_k, 0.0, ki)], axis=0).astype(BF16)
            a_buf[u, p] = _dot_nt(q_dec[:, lanes], kst)
            ds_buf[u, p] = jnp.where(diag, _dot_tn(k_end[:, lanes], v_ref[rows, p * _PV:(p + 1) * _PV]), 0.0)
            dc_buf[u, p] = jnp.broadcast_to(decay[:, lanes], (_PK, _PK)).T

    def state(g, u, valid):
        d = u % 2
        c, _ = place(g, d)
        qd_buf[u] = (qe_buf[u] * q_scale).astype(BF16)
        for p in range(n_pairs):
            fresh = (c % seq_chunks == 0) if emit_state else (c < 0)
            s_in = jnp.where(fresh, 0.0, st_ref[d, p])
            sb_buf[u, p] = s_in.astype(BF16)
            a2_buf[u, p] = jnp.where(tri[d], a_buf[u, p], 0.0).astype(BF16)
            dc = dc_buf[u, p]
            s_new = s_in * jnp.concatenate([dc, dc], axis=1) + ds_buf[u, p]
            st_ref[d, p] = s_new
            if emit_state:
                for e in range(2):
                    sout_ref[c // seq_chunks, d, 2 * p + e] = s_new[e * GLA_DK:(e + 1) * GLA_DK,
                                                                    e * GLA_DV:(e + 1) * GLA_DV]

    def out(g, u, valid):
        d = u % 2
        _, _, v_ref = views[d]
        _, pos = place(g, d)
        rows = rows_of(pos)
        for p in range(n_pairs):
            vpf = v_ref[rows, p * _PV:(p + 1) * _PV].astype(F32)
            vbd = jnp.concatenate([jnp.where(lo_v, vpf, 0.0), jnp.where(lo_v, 0.0, vpf)], axis=0).astype(BF16)
            x = jnp.concatenate([a2_buf[u, p], qd_buf[u, :, p * _PK:(p + 1) * _PK]], axis=1)
            w = jnp.concatenate([vbd, sb_buf[u, p]], axis=0)
            out_refs[d][rows, p * _PV:(p + 1) * _PV] = _dot(x, w).astype(out_refs[d].dtype)

    _software_pipeline(n_groups, GLA_PIPE_SLOTS, (prep, state, out))


def _gla(qkg, gf, gb, vg, s0, n_batch, seq):
    tm = GLA_TILE
    long_seq = s0 is not None
    if long_seq:
        nt = seq // tm
        grid = (n_batch, nt)
        fwd = lambda b, i: (b * nt + i, 0)
        bwd = lambda b, i: (b * nt + nt - 1 - i, 0)
        seq_chunks = tm // CHUNK
    else:
        per_step = tm // seq
        grid = (1, n_batch // per_step)
        fwd = bwd = lambda b, i: (i, 0)
        seq_chunks = seq // CHUNK
    in_specs = []
    args = []
    for m, gates in ((fwd, gf), (bwd, gb)):
        in_specs += [pl.BlockSpec((tm, 2 * GLA_K_WIDTH), m), pl.BlockSpec((tm, GLA_K_WIDTH), m),
                     pl.BlockSpec((tm, GLA_V_WIDTH), m)]
        args += [qkg, gates, vg]
    n_tok = n_batch * seq
    out_specs = [pl.BlockSpec((tm, GLA_V_WIDTH), fwd), pl.BlockSpec((tm, GLA_V_WIDTH), bwd)]
    out_shape = [jax.ShapeDtypeStruct((n_tok, GLA_V_WIDTH), BF16)] * 2
    st_dims = (2, N_GLA_HEADS, GLA_DK, GLA_DV)
    if long_seq:
        in_specs.append(pl.BlockSpec((1,) + st_dims, lambda b, i: (b, 0, 0, 0, 0)))
        args.append(s0)
    else:
        out_specs.append(pl.BlockSpec((per_step,) + st_dims, lambda b, i: (i, 0, 0, 0, 0)))
        out_shape.append(jax.ShapeDtypeStruct((n_batch,) + st_dims, F32))
    n = GLA_PIPE_SLOTS
    n_pairs = N_GLA_HEADS // 2
    return pl.pallas_call(
        functools.partial(_gla_kernel, seq_chunks=seq_chunks, has_init=long_seq, emit_state=not long_seq),
        grid=grid,
        in_specs=in_specs,
        out_specs=out_specs,
        out_shape=out_shape,
        scratch_shapes=[pltpu.VMEM((2, n_pairs, _PK, _PV), F32),
                        pltpu.VMEM((n, n_pairs, CHUNK, _PK), F32),
                        pltpu.VMEM((n, n_pairs, CHUNK, _PK), BF16),
                        pltpu.VMEM((n, CHUNK, GLA_K_WIDTH), F32),
                        pltpu.VMEM((n, CHUNK, GLA_K_WIDTH), BF16),
                        pltpu.VMEM((n, n_pairs, _PK, _PV), F32),
                        pltpu.VMEM((n, n_pairs, _PK, _PV), BF16),
                        pltpu.VMEM((n, n_pairs, _PK, _PK), F32)],
        compiler_params=_params("arbitrary", "arbitrary"),
        name="gla_long" if long_seq else "gla_short",
    )(*args)


def _merge_kernel(x_ref, mod_ref, oa_ref, za_ref, of_ref, ob_ref, zg_ref, gng_ref, wo_ref, fng_ref, y_ref):
    ya = oa_ref[...].astype(F32) * _silu(za_ref[...].astype(F32))
    og = of_ref[...].astype(F32) + ob_ref[...].astype(F32)
    zg = zg_ref[...].astype(F32)
    gng = gng_ref[...]
    parts = [ya]
    for h in range(N_GLA_HEADS):
        oh = og[:, h * GLA_DV:(h + 1) * GLA_DV]
        nh = oh * lax.rsqrt(jnp.mean(oh * oh, axis=-1, keepdims=True) + EPS)
        parts.append((nh * gng) * _silu(zg[:, h * GLA_DV:(h + 1) * GLA_DV]))
    yin = jnp.concatenate(parts, axis=-1).astype(BF16)
    gate = mod_ref[0][:, 2 * D_MODEL:3 * D_MODEL]
    y = x_ref[...] + gate * _dot(yin, wo_ref[...])
    y_ref[...] = (y * lax.rsqrt(jnp.mean(y * y, axis=-1, keepdims=True) + EPS)) * fng_ref[...]


def _merge(x2d, mod3, mod_row_of_tile, oa, za, of, ob, zg, gla_norm_g, w_out, final_norm_g, tm):
    n_tok = x2d.shape[0]
    row = lambda i: (i, 0)
    const = lambda i: (0, 0)
    wide = pl.BlockSpec((tm, GLA_V_WIDTH), row)
    return pl.pallas_call(
        _merge_kernel,
        grid=(n_tok // tm,),
        in_specs=[
            pl.BlockSpec((tm, D_MODEL), row),
            pl.BlockSpec((1, 1, 3 * D_MODEL), lambda i: (mod_row_of_tile(i), 0, 0)),
            wide, wide, wide, wide, wide,
            pl.BlockSpec((1, GLA_DV), const),
            pl.BlockSpec((D_MODEL, D_MODEL), const),
            pl.BlockSpec((1, D_MODEL), const),
        ],
        out_specs=pl.BlockSpec((tm, D_MODEL), row),
        out_shape=jax.ShapeDtypeStruct((n_tok, D_MODEL), F32),
        compiler_params=_params("arbitrary"),
        name="merge",
    )(x2d, mod3, oa, za, of, ob, zg, gla_norm_g, w_out, final_norm_g)


def _rope_tables(n_tok):
    t = np.arange(n_tok)
    rowp = (t // GRID_W).astype(np.float64)
    colp = (t % GRID_W).astype(np.float64)
    nf = HEAD_DIM // 4
    freqs = ROPE_BASE ** (-np.arange(nf, dtype=np.float64) / nf)
    lane = np.arange(LANES)
    pos = np.where(((lane % HEAD_DIM) < HEAD_DIM // 2)[None, :], rowp[:, None], colp[:, None])
    ang = pos * freqs[lane % nf][None, :]
    sign = np.where((lane % (2 * nf)) < nf, -1.0, 1.0)
    return jnp.asarray(np.cos(ang), F32), jnp.asarray(np.sin(ang) * sign[None, :], F32)


def kernel(x_prompt, x_sample, c, cache_k, cache_v, state_gla, c_ctx, w_mod, b_mod, norm_g, w_in,
           w_gk_f, b_gk_f, w_gk_b, b_gk_b, sink, gla_norm_g, w_out, final_norm_g):
    depth = w_in.shape[0]
    assert depth == 1, "single-layer step"
    l = 0
    bp, sp, _ = x_prompt.shape
    bs, ss, _ = x_sample.shape
    assert (bp * sp) % TOKEN_TILE == 0 and TOKEN_TILE % sp == 0 and ss % TOKEN_TILE == 0
    assert (bp * sp) % MERGE_TILE == 0 and ss % MERGE_TILE == 0
    assert bp % CTX_ATTN_BATCH == 0 and sp % BLOCK == 0 and ss % ATTN_TILE == 0 and ss // BLOCK >= 3
    assert GLA_TILE % sp == 0 and (bp * sp) % GLA_TILE == 0 and ss % GLA_TILE == 0 and sp % CHUNK == 0
    assert ss % GRID_W == 0 and cache_k.shape[2] % LANES == 0

    wl = w_in[l]
    w1 = wl[:, :_TAIL0].astype(BF16)
    wtail = wl[:, _TAIL0:].astype(BF16)
    wgk = jnp.zeros((LANES, 2 * GLA_K_WIDTH), F32)
    wgk = wgk.at[0:GATE_RANK, 0:GLA_K_WIDTH].set(w_gk_f[l])
    wgk = wgk.at[GATE_RANK:2 * GATE_RANK, GLA_K_WIDTH:].set(w_gk_b[l]).astype(BF16)
    bgk = jnp.concatenate([b_gk_f[l], b_gk_b[l]])[None, :]
    wo = w_out[l].astype(BF16)
    ng = norm_g[l][None, :]
    gng = gla_norm_g[l][None, :]
    fng = final_norm_g[None, :]
    sk = sink[l]

    n_rows = SUBLANES
    assert 1 + bs <= n_rows
    cvecs = jnp.concatenate([c_ctx[None, :], c, jnp.zeros((n_rows - 1 - bs, D_MODEL), F32)], axis=0)
    mod3 = _modulation(cvecs, w_mod[l], b_mod[l][None, :]).reshape(n_rows, 1, 3 * D_MODEL)
    tm = TOKEN_TILE

    xp2 = x_prompt.reshape(bp * sp, D_MODEL)
    ctx_row = lambda i: 0
    q, kvp, za, vg, zg, qkg, gf, gb, kt, vt = _project(xp2, mod3, ctx_row, ng, w1, wtail, wgk, bgk, None, tm, sp)
    oa = _context_attention(q, kvp, sk, sp)
    of, ob, st = _gla(qkg, gf, gb, vg, None, bp, sp)
    y_prompt = _merge(xp2, mod3, ctx_row, oa, za, of, ob, zg, gng, wo, fng, MERGE_TILE).reshape(bp, sp, D_MODEL)
    to_cache = lambda t: t.reshape(bp, 1, N_KV_HEADS, HEAD_DIM, sp).transpose(0, 1, 4, 2, 3)
    new_k, new_v = to_cache(kt), to_cache(vt)
    new_state = st[:, None]

    xs2 = x_sample.reshape(bs * ss, D_MODEL)
    tiles_per_seq = ss // tm
    lat_row = lambda i: 1 + i // tiles_per_seq
    q, kvp, za, vg, zg, qkg, gf, gb = _project(xs2, mod3, lat_row, ng, w1, wtail, wgk, bgk, _rope_tables(ss), tm, ss)
    past = cache_k.shape[2]
    from_cache = lambda t: t[:, l].transpose(0, 2, 3, 1).reshape(bs, KV_WIDTH, past)
    ck, cv = from_cache(cache_k), from_cache(cache_v)
    oa = _latent_attention(q, kvp, ck, cv, sk, bs, ss)
    of, ob = _gla(qkg, gf, gb, vg, state_gla[:, l], bs, ss)
    merge_row = lambda i: 1 + i // (ss // MERGE_TILE)
    y_sample = _merge(xs2, mod3, merge_row, oa, za, of, ob, zg, gng, wo, fng, MERGE_TILE).reshape(bs, ss, D_MODEL)

    return (y_prompt, y_sample, new_k, new_v, new_state)
```

```python
import functools
import math

import numpy as np
import jax
import jax.numpy as jnp
from jax import lax
from jax.experimental import pallas as pl
from jax.experimental.pallas import tpu as pltpu

D_MODEL = 1024
GRID_W = 64
HEAD_DIM = 64
N_ATTN_HEADS = 8
N_KV_HEADS = 2
ATTN_WIDTH = 512
KV_WIDTH = 128
BLOCK = 128
N_GLA_HEADS = 4
GLA_DK = 64
GLA_DV = 128
GLA_K_WIDTH = 256
GLA_V_WIDTH = 512
GATE_RANK = 16
GATE_NORMALIZER = 16.0
CHUNK = 64
ROPE_BASE = 10000.0
EPS = 1e-6
NEG_INF = -1e30

KV_GROUP = N_ATTN_HEADS // N_KV_HEADS
LANES = 128
SUBLANES = 8
TOKEN_TILE = 1024
MERGE_TILE = 1024
GLA_TILE = 2048
ATTN_TILE = 2048
CTX_ATTN_BATCH = 8
PIPE_SLOTS = 4
GLA_PIPE_SLOTS = 4
KVP_WIDTH = 8 * LANES
N_PLACEMENTS = 2 * N_KV_HEADS
SOFTMAX_ROWS = 32
MOD_TILE = 1024
VMEM_LIMIT = 56 * 1024 * 1024
LOG2E = math.log2(math.e)

F32 = jnp.float32
BF16 = jnp.bfloat16


def _dot(a, b):
    return jnp.dot(a, b, preferred_element_type=F32)


def _dot_nt(a, b):
    return lax.dot_general(a, b, (((1,), (1,)), ((), ())), preferred_element_type=F32)


def _dot_tn(a, b):
    return lax.dot_general(a, b, (((0,), (0,)), ((), ())), preferred_element_type=F32)


def _silu(x):
    h = 0.5 * x
    return h + h * jnp.tanh(h)


def _log_sigmoid(x):
    return jnp.minimum(x, 0.0) - jnp.log1p(jnp.exp(-jnp.abs(x)))


def _params(*sem):
    return pltpu.CompilerParams(dimension_semantics=sem, vmem_limit_bytes=VMEM_LIMIT)


def _split_bf16(x):
    hi = x.astype(BF16)
    return hi, (x - hi.astype(F32)).astype(BF16)


def _mod_kernel(c_ref, w_ref, b_ref, o_ref):
    rows = c_ref.shape[0]
    s_hi, s_lo = _split_bf16(_silu(c_ref[...]))
    w_hi, w_lo = _split_bf16(w_ref[...])
    by_hi = _dot(jnp.concatenate([s_hi, s_lo], axis=0), w_hi)
    o_ref[...] = by_hi[0:rows] + by_hi[rows:2 * rows] + _dot(s_hi, w_lo) + b_ref[...]


def _modulation(cvecs, w_mod, b_mod):
    rows = cvecs.shape[0]
    n_out = w_mod.shape[1]
    return pl.pallas_call(
        _mod_kernel,
        grid=(n_out // MOD_TILE,),
        in_specs=[
            pl.BlockSpec((rows, D_MODEL), lambda j: (0, 0)),
            pl.BlockSpec((D_MODEL, MOD_TILE), lambda j: (0, j)),
            pl.BlockSpec((1, MOD_TILE), lambda j: (0, j)),
        ],
        out_specs=pl.BlockSpec((rows, MOD_TILE), lambda j: (0, j)),
        out_shape=jax.ShapeDtypeStruct((rows, n_out), F32),
        compiler_params=_params("arbitrary"),
        name="adaln_mod",
    )(cvecs, w_mod, b_mod)


_Q0, _K0, _V0, _ZA0, _QG0, _KG0, _VG0, _TAIL0 = 0, 512, 640, 768, 1280, 1536, 1792, 2304
_TAIL_COLS = 2 * GATE_RANK + GLA_V_WIDTH
_Q_PRESCALE = (HEAD_DIM ** -0.5) * LOG2E


def _rope(x, cos, sin_signed):
    n = x.shape[1] // LANES
    lane = lax.broadcasted_iota(jnp.int32, (x.shape[0], LANES), 1)
    first = (lane % 32) < 16
    outs = []
    for c in range(n):
        xc = x[:, c * LANES:(c + 1) * LANES]
        partner = jnp.where(first, pltpu.roll(xc, LANES - 16, 1), pltpu.roll(xc, 16, 1))
        outs.append(xc * cos + partner * sin_signed)
    return outs[0] if n == 1 else jnp.concatenate(outs, axis=-1)


def _kv_placements(k, v):
    lane = lax.broadcasted_iota(jnp.int32, k.shape, 1)
    lo = lane < HEAD_DIM
    kr = pltpu.roll(k, HEAD_DIM, 1)
    vr = pltpu.roll(v, HEAD_DIM, 1)
    pieces = [jnp.where(lo, k, 0.0), jnp.where(lo, 0.0, kr), jnp.where(lo, kr, 0.0), jnp.where(lo, 0.0, k),
              jnp.where(lo, v, 1.0), jnp.where(lo, 1.0, vr), jnp.where(lo, vr, 1.0), jnp.where(lo, 1.0, v)]
    return jnp.concatenate(pieces, axis=-1).astype(BF16)


def _proj_kernel(*refs, rope):
    if rope:
        (x_ref, mod_ref, ng_ref, w1_ref, wtail_ref, wgk_ref, bgk_ref, cos_ref, sin_ref,
         q_ref, kvp_ref, za_ref, vg_ref, zg_ref, qkg_ref, gf_ref, gb_ref) = refs
    else:
        (x_ref, mod_ref, ng_ref, w1_ref, wtail_ref, wgk_ref, bgk_ref,
         q_ref, kvp_ref, za_ref, vg_ref, zg_ref, qkg_ref, gf_ref, gb_ref, k_ref, v_ref) = refs
    x = x_ref[...]
    xn = x * lax.rsqrt(jnp.mean(x * x, axis=-1, keepdims=True) + EPS)
    mod = mod_ref[0]
    shift = mod[:, 0:D_MODEL]
    scale = mod[:, D_MODEL:2 * D_MODEL]
    h = ((xn * ng_ref[...]) * (1.0 + scale) + shift).astype(BF16)

    def piece(c0, c1):
        return _dot(h, w1_ref[:, c0:c1])

    q = piece(_Q0, _K0)
    kv = piece(_K0, _ZA0)
    k = kv[:, 0:KV_WIDTH]
    v = kv[:, KV_WIDTH:2 * KV_WIDTH]
    if rope:
        cos = cos_ref[...]
        sin = sin_ref[...]
        q = _rope(q, cos, sin)
        k = _rope(k, cos, sin)
    else:
        seq = k_ref.shape[2]
        for b in range(k_ref.shape[0]):
            k_ref[b] = k[b * seq:(b + 1) * seq, :].T
            v_ref[b] = v[b * seq:(b + 1) * seq, :].T
    q_ref[...] = (q * _Q_PRESCALE).astype(BF16)
    kvp_ref[...] = _kv_placements(k, v)
    za_ref[...] = piece(_ZA0, _QG0).astype(BF16)
    vg_ref[...] = piece(_VG0, _TAIL0).astype(BF16)
    qkg_ref[...] = piece(_QG0, _VG0)
    tail = _dot(h, wtail_ref[...])
    zg_ref[...] = tail[:, 2 * GATE_RANK:_TAIL_COLS].astype(BF16)
    lane = lax.broadcasted_iota(jnp.int32, (tail.shape[0], LANES), 1)
    lr = jnp.where(lane < 2 * GATE_RANK, tail[:, 0:LANES], 0.0).astype(BF16)
    g = _log_sigmoid(_dot(lr, wgk_ref[...]) + bgk_ref[...]) * (1.0 / GATE_NORMALIZER)
    gf_ref[...] = g[:, 0:GLA_K_WIDTH]
    gb_ref[...] = g[:, GLA_K_WIDTH:2 * GLA_K_WIDTH]


def _project(x2d, mod3, mod_row_of_tile, norm_g, w1, wtail, wgk, bgk, rope_tabs, tm, seq):
    n_tok = x2d.shape[0]
    rope = rope_tabs is not None
    row = lambda i: (i, 0)
    const = lambda i: (0, 0)
    in_specs = [
        pl.BlockSpec((tm, D_MODEL), row),
        pl.BlockSpec((1, 1, 3 * D_MODEL), lambda i: (mod_row_of_tile(i), 0, 0)),
        pl.BlockSpec((1, D_MODEL), const),
        pl.BlockSpec((D_MODEL, _TAIL0), const),
        pl.BlockSpec((D_MODEL, _TAIL_COLS), const),
        pl.BlockSpec((LANES, 2 * GLA_K_WIDTH), const),
        pl.BlockSpec((1, 2 * GLA_K_WIDTH), const),
    ]
    args = [x2d, mod3, norm_g, w1, wtail, wgk, bgk]
    if rope:
        tiles_per_seq = rope_tabs[0].shape[0] // tm
        pos = lambda i: (i % tiles_per_seq, 0)
        in_specs += [pl.BlockSpec((tm, LANES), pos), pl.BlockSpec((tm, LANES), pos)]
        args += list(rope_tabs)
    widths = [(ATTN_WIDTH, BF16), (KVP_WIDTH, BF16), (ATTN_WIDTH, BF16), (GLA_V_WIDTH, BF16), (GLA_V_WIDTH, BF16),
              (2 * GLA_K_WIDTH, F32), (GLA_K_WIDTH, F32), (GLA_K_WIDTH, F32)]
    out_specs = [pl.BlockSpec((tm, w), row) for w, _ in widths]
    out_shape = [jax.ShapeDtypeStruct((n_tok, w), dt) for w, dt in widths]
    if not rope:
        for _ in range(2):
            out_specs.append(pl.BlockSpec((tm // seq, KV_WIDTH, seq), lambda i: (i, 0, 0)))
            out_shape.append(jax.ShapeDtypeStruct((n_tok // seq, KV_WIDTH, seq), F32))
    return pl.pallas_call(
        functools.partial(_proj_kernel, rope=rope),
        grid=(n_tok // tm,),
        in_specs=in_specs,
        out_specs=out_specs,
        out_shape=out_shape,
        compiler_params=_params("arbitrary"),
        name="project_rope" if rope else "project",
    )(*args)


def _software_pipeline(n_groups, n_slots, stages):
    last_g = n_groups - 1

    def body(k, carry):
        for age in (2, 1, 0):
            for u in range(n_slots):
                g = n_slots * (k - age) + u
                stages[age](jnp.clip(g, 0, last_g), u, jnp.logical_and(g >= 0, g <= last_g))
        return carry

    lax.fori_loop(0, n_groups // n_slots + 2, body, 0)


def _init_pipeline_scratch(first, zero_refs, one_refs=()):
    @pl.when(first)
    def _():
        for ref in zero_refs:
            ref[...] = jnp.zeros_like(ref)
        for ref in one_refs:
            ref[...] = jnp.ones_like(ref)


def _key_rows(start, size):
    return pl.ds(pl.multiple_of(start, BLOCK), size)


def _softmax_rows(s_buf, p_buf, t_buf, slot, n_rows, n_keys, masks_of, sink_of):
    for r0 in range(0, n_rows, SOFTMAX_ROWS):
        rs = slice(r0, r0 + SOFTMAX_ROWS)
        masks = masks_of(r0)
        cols = []
        for c in range(n_keys // LANES):
            sc = s_buf[slot, rs, c * LANES:(c + 1) * LANES]
            cols.append(jnp.where(masks[c], sc, NEG_INF) if c in masks else sc)
        mx = cols[0]
        for sc in cols[1:]:
            mx = jnp.maximum(mx, sc)
        sink = sink_of(r0)
        m = jnp.maximum(jnp.max(mx, axis=-1, keepdims=True), sink)
        for c, sc in enumerate(cols):
            p_buf[slot, rs, c * LANES:(c + 1) * LANES] = jnp.exp2(sc - m).astype(BF16)
        t_buf[slot, rs, :] = jnp.broadcast_to(jnp.exp2(sink - m), (SOFTMAX_ROWS, LANES))


def _normalize(acc, t_buf, slot):
    return acc / (pltpu.roll(acc, HEAD_DIM, 1) + t_buf[slot])


def _attn_scratch(n_groups, n_keys):
    rows = 2 * BLOCK
    return [pltpu.VMEM((n_groups // 2, rows, LANES), BF16),
            pltpu.VMEM((PIPE_SLOTS, rows, n_keys), F32), pltpu.VMEM((PIPE_SLOTS, rows, n_keys), BF16),
            pltpu.VMEM((PIPE_SLOTS, rows, LANES), F32), pltpu.VMEM((n_groups, rows, LANES), F32)]


def _stack_queries(q_ref, q2s, n_blocks):
    for blk in range(n_blocks):
        rows = slice(blk * BLOCK, (blk + 1) * BLOCK)
        for j in range(N_KV_HEADS):
            idx = blk * N_KV_HEADS + j
            q2s[idx, 0:BLOCK, :] = q_ref[rows, (2 * j) * LANES:(2 * j + 1) * LANES]
            q2s[idx, BLOCK:2 * BLOCK, :] = q_ref[rows, (2 * j + 1) * LANES:(2 * j + 2) * LANES]


def _write_heads(o_buf, o_ref, n_blocks):
    lo = lax.broadcasted_iota(jnp.int32, (2 * BLOCK, LANES), 1) < HEAD_DIM
    for blk in range(n_blocks):
        rows = slice(blk * BLOCK, (blk + 1) * BLOCK)
        for j in range(N_KV_HEADS):
            g0 = blk * N_PLACEMENTS + 2 * j
            o = jnp.where(lo, o_buf[g0], o_buf[g0 + 1]).astype(o_ref.dtype)
            o_ref[rows, (2 * j) * LANES:(2 * j + 1) * LANES] = o[0:BLOCK]
            o_ref[rows, (2 * j + 1) * LANES:(2 * j + 2) * LANES] = o[BLOCK:2 * BLOCK]


def _group_sinks(sink_ref, g):
    j, e = (g // 2) % N_KV_HEADS, g % 2
    sink_a = sink_ref[KV_GROUP * j + e] * LOG2E
    sink_b = sink_ref[KV_GROUP * j + 2 + e] * LOG2E
    return lambda r0: sink_a if r0 < BLOCK else sink_b


def _ctx_attn_kernel(sink_ref, q_ref, kvp_ref, o_ref, kown, vown, q2s, s_buf, p_buf, t_buf, o_buf, *, seq):
    n_blocks = q_ref.shape[0] // BLOCK
    _init_pipeline_scratch(pl.program_id(0) == 0, (s_buf, p_buf), (t_buf,))
    for idx in range(N_PLACEMENTS):
        kown[idx] = kvp_ref[:, idx * LANES:(idx + 1) * LANES]
        vown[idx] = kvp_ref[:, (N_PLACEMENTS + idx) * LANES:(N_PLACEMENTS + idx + 1) * LANES]
    _stack_queries(q_ref, q2s, n_blocks)
    blocks_per_seq = seq // BLOCK

    def keys_of(g):
        return _key_rows(((g // N_PLACEMENTS) // blocks_per_seq) * seq, seq)

    def qk(g, u, valid):
        s_buf[u] = _dot_nt(q2s[g // 2], kown[g % N_PLACEMENTS, keys_of(g), :])

    def softmax(g, u, valid):
        _softmax_rows(s_buf, p_buf, t_buf, u, 2 * BLOCK, seq, lambda r0: {}, _group_sinks(sink_ref, g))

    def pv(g, u, valid):
        o_buf[g] = _normalize(_dot(p_buf[u], vown[g % N_PLACEMENTS, keys_of(g), :]), t_buf, u)

    _software_pipeline(n_blocks * N_PLACEMENTS, PIPE_SLOTS, (qk, softmax, pv))
    _write_heads(o_buf, o_ref, n_blocks)


def _context_attention(q, kvp, sink, seq):
    n_tok = q.shape[0]
    rows = CTX_ATTN_BATCH * seq
    row = lambda b: (b, 0)
    n_groups = (rows // BLOCK) * N_PLACEMENTS
    return pl.pallas_call(
        functools.partial(_ctx_attn_kernel, seq=seq),
        grid=(n_tok // rows,),
        in_specs=[
            pl.BlockSpec(memory_space=pltpu.SMEM),
            pl.BlockSpec((rows, ATTN_WIDTH), row),
            pl.BlockSpec((rows, KVP_WIDTH), row),
        ],
        out_specs=pl.BlockSpec((rows, ATTN_WIDTH), row),
        out_shape=jax.ShapeDtypeStruct((n_tok, ATTN_WIDTH), BF16),
        scratch_shapes=[pltpu.VMEM((N_PLACEMENTS, rows, LANES), BF16)] * 2 + _attn_scratch(n_groups, seq),
        compiler_params=_params("arbitrary"),
        name="context_attention",
    )(sink, q, kvp)


def _lat_attn_kernel(sink_ref, q_ref, kvp_prev, kvp_own, kvp_next, ck_ref, cv_ref, o_ref,
                     kwin, vwin, kctx, vctx, q2s, s_buf, p_buf, t_buf, o_buf, *, n_steps, past):
    i = pl.program_id(1)
    n_blocks = q_ref.shape[0] // BLOCK
    own_end = BLOCK + q_ref.shape[0]
    win = 3 * BLOCK
    _init_pipeline_scratch(jnp.logical_and(pl.program_id(0) == 0, i == 0), (s_buf, p_buf), (t_buf,))

    @pl.when(i == 0)
    def _():
        cpl = _kv_placements(ck_ref[0].T, cv_ref[0].T)
        for idx in range(N_PLACEMENTS):
            kctx[idx] = cpl[:, idx * LANES:(idx + 1) * LANES]
            vctx[idx] = cpl[:, (N_PLACEMENTS + idx) * LANES:(N_PLACEMENTS + idx + 1) * LANES]

    for src, r0, r1 in ((kvp_prev, 0, BLOCK), (kvp_own, BLOCK, own_end), (kvp_next, own_end, own_end + BLOCK)):
        for idx in range(N_PLACEMENTS):
            kwin[idx, r0:r1, :] = src[:, idx * LANES:(idx + 1) * LANES]
            vwin[idx, r0:r1, :] = src[:, (N_PLACEMENTS + idx) * LANES:(N_PLACEMENTS + idx + 1) * LANES]
    _stack_queries(q_ref, q2s, n_blocks)

    row = lax.broadcasted_iota(jnp.int32, (SOFTMAX_ROWS, LANES), 0)
    c = lax.broadcasted_iota(jnp.int32, (SOFTMAX_ROWS, LANES), 1)
    off_prev = jnp.where(i > 0, 0, BLOCK)
    off_next = jnp.where(i < n_steps - 1, 0, BLOCK)

    def window_of(g):
        return _key_rows((g // N_PLACEMENTS) * BLOCK, win)

    def qk(g, u, valid):
        q2 = q2s[g // 2]
        s_buf[u, :, 0:win] = _dot_nt(q2, kwin[g % N_PLACEMENTS, window_of(g), :])
        s_buf[u, :, win:win + past] = _dot_nt(q2, kctx[g % N_PLACEMENTS])

    def softmax(g, u, valid):
        blk = g // N_PLACEMENTS
        off0 = jnp.where(blk == 0, off_prev, 0)
        off2 = jnp.where(blk == n_blocks - 1, off_next, 0)

        def masks_of(r0):
            r = row + r0 % BLOCK
            return {0: c >= r + off0, 2: c <= r - off2}

        _softmax_rows(s_buf, p_buf, t_buf, u, 2 * BLOCK, win + past, masks_of, _group_sinks(sink_ref, g))

    def pv(g, u, valid):
        acc = _dot(p_buf[u, :, 0:win], vwin[g % N_PLACEMENTS, window_of(g), :])
        acc = acc + _dot(p_buf[u, :, win:win + past], vctx[g % N_PLACEMENTS])
        o_buf[g] = _normalize(acc, t_buf, u)

    _software_pipeline(n_blocks * N_PLACEMENTS, PIPE_SLOTS, (qk, softmax, pv))
    _write_heads(o_buf, o_ref, n_blocks)


def _latent_attention(q, kvp, ck, cv, sink, n_batch, seq):
    ns = seq // ATTN_TILE
    nb = seq // BLOCK
    per = ATTN_TILE // BLOCK
    own = lambda b, i: (b * ns + i, 0)
    prev = lambda b, i: (b * nb + jnp.maximum(per * i - 1, 0), 0)
    nxt = lambda b, i: (b * nb + jnp.minimum(per * i + per, nb - 1), 0)
    ctx = lambda b, i: (b, 0, 0)
    past = ck.shape[2]
    win_rows = ATTN_TILE + 2 * BLOCK
    bf = lambda *shape: pltpu.VMEM(shape, BF16)
    return pl.pallas_call(
        functools.partial(_lat_attn_kernel, n_steps=ns, past=past),
        grid=(n_batch, ns),
        in_specs=[
            pl.BlockSpec(memory_space=pltpu.SMEM),
            pl.BlockSpec((ATTN_TILE, ATTN_WIDTH), own),
            pl.BlockSpec((BLOCK, KVP_WIDTH), prev),
            pl.BlockSpec((ATTN_TILE, KVP_WIDTH), own),
            pl.BlockSpec((BLOCK, KVP_WIDTH), nxt),
            pl.BlockSpec((1, KV_WIDTH, past), ctx),
            pl.BlockSpec((1, KV_WIDTH, past), ctx),
        ],
        out_specs=pl.BlockSpec((ATTN_TILE, ATTN_WIDTH), own),
        out_shape=jax.ShapeDtypeStruct((n_batch * seq, ATTN_WIDTH), BF16),
        scratch_shapes=[bf(N_PLACEMENTS, win_rows, LANES), bf(N_PLACEMENTS, win_rows, LANES),
                        bf(N_PLACEMENTS, past, LANES), bf(N_PLACEMENTS, past, LANES)]
        + _attn_scratch(per * N_PLACEMENTS, 3 * BLOCK + past),
        compiler_params=_params("arbitrary", "arbitrary"),
        name="latent_attention",
    )(sink, q, kvp, kvp, kvp, ck, cv)


_PK = 2 * GLA_DK
_PV = 2 * GLA_DV


def _cumsum_rows(x, reverse):
    n, width = x.shape
    row = lax.broadcasted_iota(jnp.int32, (SUBLANES, width), 0)
    blocks = [x[j:j + SUBLANES, :] for j in range(0, n, SUBLANES)]
    s = 1
    while s < SUBLANES:
        if reverse:
            blocks = [b + jnp.where(row < SUBLANES - s, pltpu.roll(b, SUBLANES - s, 0), 0.0) for b in blocks]
        else:
            blocks = [b + jnp.where(row >= s, pltpu.roll(b, s, 0), 0.0) for b in blocks]
        s *= 2
    edge = 0 if reverse else SUBLANES - 1
    order = range(len(blocks) - 1, -1, -1) if reverse else range(len(blocks))
    carry = None
    for j in order:
        if carry is not None:
            blocks[j] = blocks[j] + carry
        carry = jnp.broadcast_to(blocks[j][edge:edge + 1, :], (SUBLANES, width))
    return jnp.concatenate(blocks, axis=0)


def _gla_kernel(*refs, seq_chunks, has_init, emit_state):
    it = iter(refs)
    views = [tuple(next(it) for _ in range(3)) for _ in range(2)]
    s0_ref = next(it) if has_init else None
    out_refs = (next(it), next(it))
    sout_ref = next(it) if emit_state else None
    st_ref, a_buf, a2_buf, qe_buf, qd_buf, ds_buf, sb_buf, dc_buf = (next(it) for _ in range(8))
    n_chunks = views[0][0].shape[0] // CHUNK
    n_groups = 2 * n_chunks
    n_pairs = N_GLA_HEADS // 2
    tile = pl.program_id(1)
    q_scale = GLA_DK ** -0.5
    _init_pipeline_scratch(jnp.logical_and(pl.program_id(0) == 0, tile == 0),
                           (st_ref, a_buf, a2_buf, qe_buf, qd_buf, ds_buf, sb_buf), (dc_buf,))

    if has_init:
        @pl.when(tile == 0)
        def _():
            st_ref[...] = jnp.zeros_like(st_ref)
            for d in range(2):
                for h in range(N_GLA_HEADS):
                    p, e = divmod(h, 2)
                    st_ref[d, p, e * GLA_DK:(e + 1) * GLA_DK, e * GLA_DV:(e + 1) * GLA_DV] = s0_ref[0, d, h]

    lane_k = lax.broadcasted_iota(jnp.int32, (CHUNK, _PK), 1)
    row_k = lax.broadcasted_iota(jnp.int32, (CHUNK, _PK), 0)
    lo_k = lane_k < GLA_DK
    tri = (lane_k % CHUNK <= row_k, lane_k % CHUNK >= row_k)
    lo_v = lax.broadcasted_iota(jnp.int32, (CHUNK, _PV), 1) < GLA_DV
    diag = ((lax.broadcasted_iota(jnp.int32, (_PK, _PV), 0) < GLA_DK)
            == (lax.broadcasted_iota(jnp.int32, (_PK, _PV), 1) < GLA_DV))

    def place(g, d):
        c = g // 2
        if d == 0:
            return c, c
        seq = c // seq_chunks
        return c, seq * seq_chunks + (seq_chunks - 1 - (c - seq * seq_chunks))

    def rows_of(pos):
        return pl.ds(pl.multiple_of(pos * CHUNK, CHUNK), CHUNK)

    def prep(g, u, valid):
        d = u % 2
        qk_ref, g_ref, v_ref = views[d]
        _, pos = place(g, d)
        rows = rows_of(pos)
        gc = _cumsum_rows(g_ref[rows, :], reverse=(d == 1))
        gtot = gc[0:1, :] if d == 1 else gc[CHUNK - 1:CHUNK, :]
        k = qk_ref[rows, GLA_K_WIDTH:2 * GLA_K_WIDTH]
        qe = qk_ref[rows, 0:GLA_K_WIDTH] * jnp.exp(gc)
        qe_buf[u] = qe
        q_dec = (qe * q_scale).astype(BF16)
        k_inv = k * jnp.exp(-gc)
        e_tot = jnp.exp(gtot)
        k_end = (k_inv * jnp.where(valid, e_tot, 0.0)).astype(BF16)
        decay = jnp.where(valid, e_tot, 1.0)
        for p in range(n_pairs):
            lanes = slice(p * _PK, (p + 1) * _PK)
            ki = k_inv[:, lanes]
            kst = jnp.concatenate([jnp.where(lo_k, ki, 0.0), jnp.where(lo_k, 0.0, ki)], axis=0).astype(BF16)
            a_buf[u, p] = _dot_nt(q_dec[:, lanes], kst)
            ds_buf[u, p] = jnp.where(diag, _dot_tn(k_end[:, lanes], v_ref[rows, p * _PV:(p + 1) * _PV]), 0.0)
            dc_buf[u, p] = jnp.broadcast_to(decay[:, lanes], (_PK, _PK)).T

    def state(g, u, valid):
        d = u % 2
        c, _ = place(g, d)
        qd_buf[u] = (qe_buf[u] * q_scale).astype(BF16)
        for p in range(n_pairs):
            fresh = (c % seq_chunks == 0) if emit_state else (c < 0)
            s_in = jnp.where(fresh, 0.0, st_ref[d, p])
            sb_buf[u, p] = s_in.astype(BF16)
            a2_buf[u, p] = jnp.where(tri[d], a_buf[u, p], 0.0).astype(BF16)
            dc = dc_buf[u, p]
            s_new = s_in * jnp.concatenate([dc, dc], axis=1) + ds_buf[u, p]
            st_ref[d, p] = s_new
            if emit_state:
                for e in range(2):
                    sout_ref[c // seq_chunks, d, 2 * p + e] = s_new[e * GLA_DK:(e + 1) * GLA_DK,
                                                                    e * GLA_DV:(e + 1) * GLA_DV]

    def out(g, u, valid):
        d = u % 2
        _, _, v_ref = views[d]
        _, pos = place(g, d)
        rows = rows_of(pos)
        for p in range(n_pairs):
            vpf = v_ref[rows, p * _PV:(p + 1) * _PV].astype(F32)
            vbd = jnp.concatenate([jnp.where(lo_v, vpf, 0.0), jnp.where(lo_v, 0.0, vpf)], axis=0).astype(BF16)
            x = jnp.concatenate([a2_buf[u, p], qd_buf[u, :, p * _PK:(p + 1) * _PK]], axis=1)
            w = jnp.concatenate([vbd, sb_buf[u, p]], axis=0)
            out_refs[d][rows, p * _PV:(p + 1) * _PV] = _dot(x, w).astype(out_refs[d].dtype)

    _software_pipeline(n_groups, GLA_PIPE_SLOTS, (prep, state, out))


def _gla(qkg, gf, gb, vg, s0, n_batch, seq):
    tm = GLA_TILE
    long_seq = s0 is not None
    if long_seq:
        nt = seq // tm
        grid = (n_batch, nt)
        fwd = lambda b, i: (b * nt + i, 0)
        bwd = lambda b, i: (b * nt + nt - 1 - i, 0)
        seq_chunks = tm // CHUNK
    else:
        per_step = tm // seq
        grid = (1, n_batch // per_step)
        fwd = bwd = lambda b, i: (i, 0)
        seq_chunks = seq // CHUNK
    in_specs = []
    args = []
    for m, gates in ((fwd, gf), (bwd, gb)):
        in_specs += [pl.BlockSpec((tm, 2 * GLA_K_WIDTH), m), pl.BlockSpec((tm, GLA_K_WIDTH), m),
                     pl.BlockSpec((tm, GLA_V_WIDTH), m)]
        args += [qkg, gates, vg]
    n_tok = n_batch * seq
    out_specs = [pl.BlockSpec((tm, GLA_V_WIDTH), fwd), pl.BlockSpec((tm, GLA_V_WIDTH), bwd)]
    out_shape = [jax.ShapeDtypeStruct((n_tok, GLA_V_WIDTH), BF16)] * 2
    st_dims = (2, N_GLA_HEADS, GLA_DK, GLA_DV)
    if long_seq:
        in_specs.append(pl.BlockSpec((1,) + st_dims, lambda b, i: (b, 0, 0, 0, 0)))
        args.append(s0)
    else:
        out_specs.append(pl.BlockSpec((per_step,) + st_dims, lambda b, i: (i, 0, 0, 0, 0)))
        out_shape.append(jax.ShapeDtypeStruct((n_batch,) + st_dims, F32))
    n = GLA_PIPE_SLOTS
    n_pairs = N_GLA_HEADS // 2
    return pl.pallas_call(
        functools.partial(_gla_kernel, seq_chunks=seq_chunks, has_init=long_seq, emit_state=not long_seq),
        grid=grid,
        in_specs=in_specs,
        out_specs=out_specs,
        out_shape=out_shape,
        scratch_shapes=[pltpu.VMEM((2, n_pairs, _PK, _PV), F32),
                        pltpu.VMEM((n, n_pairs, CHUNK, _PK), F32),
                        pltpu.VMEM((n, n_pairs, CHUNK, _PK), BF16),
                        pltpu.VMEM((n, CHUNK, GLA_K_WIDTH), F32),
                        pltpu.VMEM((n, CHUNK, GLA_K_WIDTH), BF16),
                        pltpu.VMEM((n, n_pairs, _PK, _PV), F32),
                        pltpu.VMEM((n, n_pairs, _PK, _PV), BF16),
                        pltpu.VMEM((n, n_pairs, _PK, _PK), F32)],
        compiler_params=_params("arbitrary", "arbitrary"),
        name="gla_long" if long_seq else "gla_short",
    )(*args)


def _merge_kernel(x_ref, mod_ref, oa_ref, za_ref, of_ref, ob_ref, zg_ref, gng_ref, wo_ref, fng_ref, y_ref):
    ya = oa_ref[...].astype(F32) * _silu(za_ref[...].astype(F32))
    og = of_ref[...].astype(F32) + ob_ref[...].astype(F32)
    zg = zg_ref[...].astype(F32)
    gng = gng_ref[...]
    parts = [ya]
    for h in range(N_GLA_HEADS):
        oh = og[:, h * GLA_DV:(h + 1) * GLA_DV]
        nh = oh * lax.rsqrt(jnp.mean(oh * oh, axis=-1, keepdims=True) + EPS)
        parts.append((nh * gng) * _silu(zg[:, h * GLA_DV:(h + 1) * GLA_DV]))
    yin = jnp.concatenate(parts, axis=-1).astype(BF16)
    gate = mod_ref[0][:, 2 * D_MODEL:3 * D_MODEL]
    y = x_ref[...] + gate * _dot(yin, wo_ref[...])
    y_ref[...] = (y * lax.rsqrt(jnp.mean(y * y, axis=-1, keepdims=True) + EPS)) * fng_ref[...]


def _merge(x2d, mod3, mod_row_of_tile, oa, za, of, ob, zg, gla_norm_g, w_out, final_norm_g, tm):
    n_tok = x2d.shape[0]
    row = lambda i: (i, 0)
    const = lambda i: (0, 0)
    wide = pl.BlockSpec((tm, GLA_V_WIDTH), row)
    return pl.pallas_call(
        _merge_kernel,
        grid=(n_tok // tm,),
        in_specs=[
            pl.BlockSpec((tm, D_MODEL), row),
            pl.BlockSpec((1, 1, 3 * D_MODEL), lambda i: (mod_row_of_tile(i), 0, 0)),
            wide, wide, wide, wide, wide,
            pl.BlockSpec((1, GLA_DV), const),
            pl.BlockSpec((D_MODEL, D_MODEL), const),
            pl.BlockSpec((1, D_MODEL), const),
        ],
        out_specs=pl.BlockSpec((tm, D_MODEL), row),
        out_shape=jax.ShapeDtypeStruct((n_tok, D_MODEL), F32),
        compiler_params=_params("arbitrary"),
        name="merge",
    )(x2d, mod3, oa, za, of, ob, zg, gla_norm_g, w_out, final_norm_g)


def _rope_tables(n_tok):
    t = np.arange(n_tok)
    rowp = (t // GRID_W).astype(np.float64)
    colp = (t % GRID_W).astype(np.float64)
    nf = HEAD_DIM // 4
    freqs = ROPE_BASE ** (-np.arange(nf, dtype=np.float64) / nf)
    lane = np.arange(LANES)
    pos = np.where(((lane % HEAD_DIM) < HEAD_DIM // 2)[None, :], rowp[:, None], colp[:, None])
    ang = pos * freqs[lane % nf][None, :]
    sign = np.where((lane % (2 * nf)) < nf, -1.0, 1.0)
    return jnp.asarray(np.cos(ang), F32), jnp.asarray(np.sin(ang) * sign[None, :], F32)


def kernel(x_prompt, x_sample, c, cache_k, cache_v, state_gla, c_ctx, w_mod, b_mod, norm_g, w_in,
           w_gk_f, b_gk_f, w_gk_b, b_gk_b, sink, gla_norm_g, w_out, final_norm_g):
    depth = w_in.shape[0]
    assert depth == 1, "single-layer step"
    l = 0
    bp, sp, _ = x_prompt.shape
    bs, ss, _ = x_sample.shape
    assert (bp * sp) % TOKEN_TILE == 0 and TOKEN_TILE % sp == 0 and ss % TOKEN_TILE == 0
    assert (bp * sp) % MERGE_TILE == 0 and ss % MERGE_TILE == 0
    assert bp % CTX_ATTN_BATCH == 0 and sp % BLOCK == 0 and ss % ATTN_TILE == 0 and ss // BLOCK >= 3
    assert GLA_TILE % sp == 0 and (bp * sp) % GLA_TILE == 0 and ss % GLA_TILE == 0 and sp % CHUNK == 0
    assert ss % GRID_W == 0 and cache_k.shape[2] % LANES == 0

    wl = w_in[l]
    w1 = wl[:, :_TAIL0].astype(BF16)
    wtail = wl[:, _TAIL0:].astype(BF16)
    wgk = jnp.zeros((LANES, 2 * GLA_K_WIDTH), F32)
    wgk = wgk.at[0:GATE_RANK, 0:GLA_K_WIDTH].set(w_gk_f[l])
    wgk = wgk.at[GATE_RANK:2 * GATE_RANK, GLA_K_WIDTH:].set(w_gk_b[l]).astype(BF16)
    bgk = jnp.concatenate([b_gk_f[l], b_gk_b[l]])[None, :]
    wo = w_out[l].astype(BF16)
    ng = norm_g[l][None, :]
    gng = gla_norm_g[l][None, :]
    fng = final_norm_g[None, :]
    sk = sink[l]

    n_rows = SUBLANES
    assert 1 + bs <= n_rows
    cvecs = jnp.concatenate([c_ctx[None, :], c, jnp.zeros((n_rows - 1 - bs, D_MODEL), F32)], axis=0)
    mod3 = _modulation(cvecs, w_mod[l], b_mod[l][None, :]).reshape(n_rows, 1, 3 * D_MODEL)
    tm = TOKEN_TILE

    xp2 = x_prompt.reshape(bp * sp, D_MODEL)
    ctx_row = lambda i: 0
    q, kvp, za, vg, zg, qkg, gf, gb, kt, vt = _project(xp2, mod3, ctx_row, ng, w1, wtail, wgk, bgk, None, tm, sp)
    oa = _context_attention(q, kvp, sk, sp)
    of, ob, st = _gla(qkg, gf, gb, vg, None, bp, sp)
    y_prompt = _merge(xp2, mod3, ctx_row, oa, za, of, ob, zg, gng, wo, fng, MERGE_TILE).reshape(bp, sp, D_MODEL)
    to_cache = lambda t: t.reshape(bp, 1, N_KV_HEADS, HEAD_DIM, sp).transpose(0, 1, 4, 2, 3)
    new_k, new_v = to_cache(kt), to_cache(vt)
    new_state = st[:, None]

    xs2 = x_sample.reshape(bs * ss, D_MODEL)
    tiles_per_seq = ss // tm
    lat_row = lambda i: 1 + i // tiles_per_seq
    q, kvp, za, vg, zg, qkg, gf, gb = _project(xs2, mod3, lat_row, ng, w1, wtail, wgk, bgk, _rope_tables(ss), tm, ss)
    past = cache_k.shape[2]
    from_cache = lambda t: t[:, l].transpose(0, 2, 3, 1).reshape(bs, KV_WIDTH, past)
    ck, cv = from_cache(cache_k), from_cache(cache_v)
    oa = _latent_attention(q, kvp, ck, cv, sk, bs, ss)
    of, ob = _gla(qkg, gf, gb, vg, state_gla[:, l], bs, ss)
    merge_row = lambda i: 1 + i // (ss // MERGE_TILE)
    y_sample = _merge(xs2, mod3, merge_row, oa, za, of, ob, zg, gng, wo, fng, MERGE_TILE).reshape(bs, ss, D_MODEL)

    return (y_prompt, y_sample, new_k, new_v, new_state)
```

```python
import functools
import math

import numpy as np
import jax
import jax.numpy as jnp
from jax import lax
from jax.experimental import pallas as pl
from jax.experimental.pallas import tpu as pltpu

D_MODEL = 1024
GRID_W = 64
HEAD_DIM = 64
N_ATTN_HEADS = 8
N_KV_HEADS = 2
ATTN_WIDTH = 512
KV_WIDTH = 128
BLOCK = 128
N_GLA_HEADS = 4
GLA_DK = 64
GLA_DV = 128
GLA_K_WIDTH = 256
GLA_V_WIDTH = 512
GATE_RANK = 16
GATE_NORMALIZER = 16.0
CHUNK = 64
ROPE_BASE = 10000.0
EPS = 1e-6
NEG_INF = -1e30

KV_GROUP = N_ATTN_HEADS // N_KV_HEADS
LANES = 128
SUBLANES = 8
TOKEN_TILE = 1024
MERGE_TILE = 1024
GLA_TILE = 2048
ATTN_TILE = 2048
CTX_ATTN_BATCH = 8
PIPE_SLOTS = 4
GLA_PIPE_SLOTS = 4
KVP_WIDTH = 8 * LANES
N_PLACEMENTS = 2 * N_KV_HEADS
SOFTMAX_ROWS = 32
MOD_TILE = 1024
VMEM_LIMIT = 56 * 1024 * 1024
LOG2E = math.log2(math.e)

F32 = jnp.float32
BF16 = jnp.bfloat16


def _dot(a, b):
    return jnp.dot(a, b, preferred_element_type=F32)


def _dot_nt(a, b):
    return lax.dot_general(a, b, (((1,), (1,)), ((), ())), preferred_element_type=F32)


def _dot_tn(a, b):
    return lax.dot_general(a, b, (((0,), (0,)), ((), ())), preferred_element_type=F32)


def _silu(x):
    h = 0.5 * x
    return h + h * jnp.tanh(h)


def _log_sigmoid(x):
    return jnp.minimum(x, 0.0) - jnp.log1p(jnp.exp(-jnp.abs(x)))


def _params(*sem):
    return pltpu.CompilerParams(dimension_semantics=sem, vmem_limit_bytes=VMEM_LIMIT)


def _split_bf16(x):
    hi = x.astype(BF16)
    return hi, (x - hi.astype(F32)).astype(BF16)


def _mod_kernel(c_ref, w_ref, b_ref, o_ref):
    rows = c_ref.shape[0]
    s_hi, s_lo = _split_bf16(_silu(c_ref[...]))
    w_hi, w_lo = _split_bf16(w_ref[...])
    by_hi = _dot(jnp.concatenate([s_hi, s_lo], axis=0), w_hi)
    o_ref[...] = by_hi[0:rows] + by_hi[rows:2 * rows] + _dot(s_hi, w_lo) + b_ref[...]


def _modulation(cvecs, w_mod, b_mod):
    rows = cvecs.shape[0]
    n_out = w_mod.shape[1]
    return pl.pallas_call(
        _mod_kernel,
        grid=(n_out // MOD_TILE,),
        in_specs=[
            pl.BlockSpec((rows, D_MODEL), lambda j: (0, 0)),
            pl.BlockSpec((D_MODEL, MOD_TILE), lambda j: (0, j)),
            pl.BlockSpec((1, MOD_TILE), lambda j: (0, j)),
        ],
        out_specs=pl.BlockSpec((rows, MOD_TILE), lambda j: (0, j)),
        out_shape=jax.ShapeDtypeStruct((rows, n_out), F32),
        compiler_params=_params("arbitrary"),
        name="adaln_mod",
    )(cvecs, w_mod, b_mod)


_Q0, _K0, _V0, _ZA0, _QG0, _KG0, _VG0, _TAIL0 = 0, 512, 640, 768, 1280, 1536, 1792, 2304
_TAIL_COLS = 2 * GATE_RANK + GLA_V_WIDTH
_Q_PRESCALE = (HEAD_DIM ** -0.5) * LOG2E


def _rope(x, cos, sin_signed):
    n = x.shape[1] // LANES
    lane = lax.broadcasted_iota(jnp.int32, (x.shape[0], LANES), 1)
    first = (lane % 32) < 16
    outs = []
    for c in range(n):
        xc = x[:, c * LANES:(c + 1) * LANES]
        partner = jnp.where(first, pltpu.roll(xc, LANES - 16, 1), pltpu.roll(xc, 16, 1))
        outs.append(xc * cos + partner * sin_signed)
    return outs[0] if n == 1 else jnp.concatenate(outs, axis=-1)


def _kv_placements(k, v):
    lane = lax.broadcasted_iota(jnp.int32, k.shape, 1)
    lo = lane < HEAD_DIM
    kr = pltpu.roll(k, HEAD_DIM, 1)
    vr = pltpu.roll(v, HEAD_DIM, 1)
    pieces = [jnp.where(lo, k, 0.0), jnp.where(lo, 0.0, kr), jnp.where(lo, kr, 0.0), jnp.where(lo, 0.0, k),
              jnp.where(lo, v, 1.0), jnp.where(lo, 1.0, vr), jnp.where(lo, vr, 1.0), jnp.where(lo, 1.0, v)]
    return jnp.concatenate(pieces, axis=-1).astype(BF16)


def _proj_kernel(*refs, rope):
    if rope:
        (x_ref, mod_ref, ng_ref, w1_ref, wtail_ref, wgk_ref, bgk_ref, cos_ref, sin_ref,
         q_ref, kvp_ref, za_ref, vg_ref, zg_ref, qkg_ref, gf_ref, gb_ref) = refs
    else:
        (x_ref, mod_ref, ng_ref, w1_ref, wtail_ref, wgk_ref, bgk_ref,
         q_ref, kvp_ref, za_ref, vg_ref, zg_ref, qkg_ref, gf_ref, gb_ref, k_ref, v_ref) = refs
    x = x_ref[...]
    xn = x * lax.rsqrt(jnp.mean(x * x, axis=-1, keepdims=True) + EPS)
    mod = mod_ref[0]
    shift = mod[:, 0:D_MODEL]
    scale = mod[:, D_MODEL:2 * D_MODEL]
    h = ((xn * ng_ref[...]) * (1.0 + scale) + shift).astype(BF16)

    def piece(c0, c1):
        return _dot(h, w1_ref[:, c0:c1])

    q = piece(_Q0, _K0)
    kv = piece(_K0, _ZA0)
    k = kv[:, 0:KV_WIDTH]
    v = kv[:, KV_WIDTH:2 * KV_WIDTH]
    if rope:
        cos = cos_ref[...]
        sin = sin_ref[...]
        q = _rope(q, cos, sin)
        k = _rope(k, cos, sin)
    else:
        seq = k_ref.shape[2]
        for b in range(k_ref.shape[0]):
            k_ref[b] = k[b * seq:(b + 1) * seq, :].T
            v_ref[b] = v[b * seq:(b + 1) * seq, :].T
    _store_stacked((q * _Q_PRESCALE).astype(BF16), q_ref)
    kvp_ref[...] = _kv_placements(k, v)
    za_ref[...] = piece(_ZA0, _QG0).astype(BF16)
    vg_ref[...] = piece(_VG0, _TAIL0).astype(BF16)
    qkg_ref[...] = piece(_QG0, _VG0)
    tail = _dot(h, wtail_ref[...])
    zg_ref[...] = tail[:, 2 * GATE_RANK:_TAIL_COLS].astype(BF16)
    lane = lax.broadcasted_iota(jnp.int32, (tail.shape[0], LANES), 1)
    lr = jnp.where(lane < 2 * GATE_RANK, tail[:, 0:LANES], 0.0).astype(BF16)
    g = _log_sigmoid(_dot(lr, wgk_ref[...]) + bgk_ref[...]) * (1.0 / GATE_NORMALIZER)
    gf_ref[...] = g[:, 0:GLA_K_WIDTH]
    gb_ref[...] = g[:, GLA_K_WIDTH:2 * GLA_K_WIDTH]


def _project(x2d, mod3, mod_row_of_tile, norm_g, w1, wtail, wgk, bgk, rope_tabs, tm, seq):
    n_tok = x2d.shape[0]
    rope = rope_tabs is not None
    row = lambda i: (i, 0)
    const = lambda i: (0, 0)
    in_specs = [
        pl.BlockSpec((tm, D_MODEL), row),
        pl.BlockSpec((1, 1, 3 * D_MODEL), lambda i: (mod_row_of_tile(i), 0, 0)),
        pl.BlockSpec((1, D_MODEL), const),
        pl.BlockSpec((D_MODEL, _TAIL0), const),
        pl.BlockSpec((D_MODEL, _TAIL_COLS), const),
        pl.BlockSpec((LANES, 2 * GLA_K_WIDTH), const),
        pl.BlockSpec((1, 2 * GLA_K_WIDTH), const),
    ]
    args = [x2d, mod3, norm_g, w1, wtail, wgk, bgk]
    if rope:
        tiles_per_seq = rope_tabs[0].shape[0] // tm
        pos = lambda i: (i % tiles_per_seq, 0)
        in_specs += [pl.BlockSpec((tm, LANES), pos), pl.BlockSpec((tm, LANES), pos)]
        args += list(rope_tabs)
    widths = [(KVP_WIDTH, BF16), (ATTN_WIDTH, BF16), (GLA_V_WIDTH, BF16), (GLA_V_WIDTH, BF16),
              (2 * GLA_K_WIDTH, F32), (GLA_K_WIDTH, F32), (GLA_K_WIDTH, F32)]
    out_specs = [pl.BlockSpec(_stacked_shape(tm), lambda i: (i, 0, 0))] + [pl.BlockSpec((tm, w), row) for w, _ in widths]
    out_shape = ([jax.ShapeDtypeStruct(_stacked_shape(n_tok), BF16)]
                 + [jax.ShapeDtypeStruct((n_tok, w), dt) for w, dt in widths])
    if not rope:
        for _ in range(2):
            out_specs.append(pl.BlockSpec((tm // seq, KV_WIDTH, seq), lambda i: (i, 0, 0)))
            out_shape.append(jax.ShapeDtypeStruct((n_tok // seq, KV_WIDTH, seq), F32))
    return pl.pallas_call(
        functools.partial(_proj_kernel, rope=rope),
        grid=(n_tok // tm,),
        in_specs=in_specs,
        out_specs=out_specs,
        out_shape=out_shape,
        compiler_params=_params("arbitrary"),
        name="project_rope" if rope else "project",
    )(*args)


def _software_pipeline(n_groups, n_slots, stages):
    last_g = n_groups - 1

    def body(k, carry):
        for age in (2, 1, 0):
            for u in range(n_slots):
                g = n_slots * (k - age) + u
                stages[age](jnp.clip(g, 0, last_g), u, jnp.logical_and(g >= 0, g <= last_g))
        return carry

    lax.fori_loop(0, n_groups // n_slots + 2, body, 0)


def _init_pipeline_scratch(first, zero_refs, one_refs=()):
    @pl.when(first)
    def _():
        for ref in zero_refs:
            ref[...] = jnp.zeros_like(ref)
        for ref in one_refs:
            ref[...] = jnp.ones_like(ref)


def _key_rows(start, size):
    return pl.ds(pl.multiple_of(start, BLOCK), size)


def _softmax_rows(s_buf, p_buf, t_buf, slot, n_rows, n_keys, masks_of, sink_of):
    for r0 in range(0, n_rows, SOFTMAX_ROWS):
        rs = slice(r0, r0 + SOFTMAX_ROWS)
        masks = masks_of(r0)
        cols = []
        for c in range(n_keys // LANES):
            sc = s_buf[slot, rs, c * LANES:(c + 1) * LANES]
            cols.append(jnp.where(masks[c], sc, NEG_INF) if c in masks else sc)
        mx = cols[0]
        for sc in cols[1:]:
            mx = jnp.maximum(mx, sc)
        sink = sink_of(r0)
        m = jnp.maximum(jnp.max(mx, axis=-1, keepdims=True), sink)
        for c, sc in enumerate(cols):
            p_buf[slot, rs, c * LANES:(c + 1) * LANES] = jnp.exp2(sc - m).astype(BF16)
        t_buf[slot, rs, :] = jnp.broadcast_to(jnp.exp2(sink - m), (SOFTMAX_ROWS, LANES))


def _normalize(acc, t_buf, slot):
    return acc / (pltpu.roll(acc, HEAD_DIM, 1) + t_buf[slot])


def _attn_scratch(n_keys):
    rows = 2 * BLOCK
    return [pltpu.VMEM((PIPE_SLOTS, rows, n_keys), F32), pltpu.VMEM((PIPE_SLOTS, rows, n_keys), BF16),
            pltpu.VMEM((PIPE_SLOTS, rows, LANES), F32)]


def _stacked_shape(n_tok):
    return (n_tok // BLOCK * N_KV_HEADS, 2 * BLOCK, LANES)


def _store_stacked(x, ref):
    for blk in range(x.shape[0] // BLOCK):
        for j in range(N_KV_HEADS):
            for e in range(2):
                ref[blk * N_KV_HEADS + j, e * BLOCK:(e + 1) * BLOCK, :] = (
                    x[blk * BLOCK:(blk + 1) * BLOCK, (2 * j + e) * LANES:(2 * j + e + 1) * LANES])


def _load_stacked(ref):
    blocks = []
    for blk in range(ref.shape[0] // N_KV_HEADS):
        blocks.append(jnp.concatenate([ref[blk * N_KV_HEADS + j, e * BLOCK:(e + 1) * BLOCK, :]
                                       for j in range(N_KV_HEADS) for e in range(2)], axis=-1))
    return jnp.concatenate(blocks, axis=0)


assert PIPE_SLOTS % 2 == 0


def _store_group(o_ref, g, u, o):
    half = slice((u % 2) * HEAD_DIM, (u % 2 + 1) * HEAD_DIM)
    o_ref[g // 2, :, half] = o[:, half].astype(o_ref.dtype)


def _group_sinks(sink_ref, g):
    j, e = (g // 2) % N_KV_HEADS, g % 2
    sink_a = sink_ref[KV_GROUP * j + e] * LOG2E
    sink_b = sink_ref[KV_GROUP * j + 2 + e] * LOG2E
    return lambda r0: sink_a if r0 < BLOCK else sink_b


def _ctx_attn_kernel(sink_ref, q_ref, kvp_ref, o_ref, kown, vown, s_buf, p_buf, t_buf, *, seq):
    n_blocks = q_ref.shape[0] // N_KV_HEADS
    _init_pipeline_scratch(pl.program_id(0) == 0, (s_buf, p_buf), (t_buf,))
    for idx in range(N_PLACEMENTS):
        kown[idx] = kvp_ref[:, idx * LANES:(idx + 1) * LANES]
        vown[idx] = kvp_ref[:, (N_PLACEMENTS + idx) * LANES:(N_PLACEMENTS + idx + 1) * LANES]
    blocks_per_seq = seq // BLOCK

    def keys_of(g):
        return _key_rows(((g // N_PLACEMENTS) // blocks_per_seq) * seq, seq)

    def qk(g, u, valid):
        s_buf[u] = _dot_nt(q_ref[g // 2], kown[g % N_PLACEMENTS, keys_of(g), :])

    def softmax(g, u, valid):
        _softmax_rows(s_buf, p_buf, t_buf, u, 2 * BLOCK, seq, lambda r0: {}, _group_sinks(sink_ref, g))

    def pv(g, u, valid):
        _store_group(o_ref, g, u, _normalize(_dot(p_buf[u], vown[g % N_PLACEMENTS, keys_of(g), :]), t_buf, u))

    _software_pipeline(n_blocks * N_PLACEMENTS, PIPE_SLOTS, (qk, softmax, pv))


def _context_attention(q, kvp, sink, seq):
    n_tok = kvp.shape[0]
    rows = CTX_ATTN_BATCH * seq
    row = lambda b: (b, 0)
    stacked = pl.BlockSpec(_stacked_shape(rows), lambda b: (b, 0, 0))
    return pl.pallas_call(
        functools.partial(_ctx_attn_kernel, seq=seq),
        grid=(n_tok // rows,),
        in_specs=[
            pl.BlockSpec(memory_space=pltpu.SMEM),
            stacked,
            pl.BlockSpec((rows, KVP_WIDTH), row),
        ],
        out_specs=stacked,
        out_shape=jax.ShapeDtypeStruct(_stacked_shape(n_tok), BF16),
        scratch_shapes=[pltpu.VMEM((N_PLACEMENTS, rows, LANES), BF16)] * 2 + _attn_scratch(seq),
        compiler_params=_params("arbitrary"),
        name="context_attention",
    )(sink, q, kvp)


def _lat_attn_kernel(sink_ref, q_ref, kvp_prev, kvp_own, kvp_next, ck_ref, cv_ref, o_ref,
                     kwin, vwin, kctx, vctx, s_buf, p_buf, t_buf, *, n_steps, past):
    i = pl.program_id(1)
    n_blocks = q_ref.shape[0] // N_KV_HEADS
    own_end = BLOCK + kvp_own.shape[0]
    win = 3 * BLOCK
    _init_pipeline_scratch(jnp.logical_and(pl.program_id(0) == 0, i == 0), (s_buf, p_buf), (t_buf,))

    @pl.when(i == 0)
    def _():
        cpl = _kv_placements(ck_ref[0].T, cv_ref[0].T)
        for idx in range(N_PLACEMENTS):
            kctx[idx] = cpl[:, idx * LANES:(idx + 1) * LANES]
            vctx[idx] = cpl[:, (N_PLACEMENTS + idx) * LANES:(N_PLACEMENTS + idx + 1) * LANES]

    for src, r0, r1 in ((kvp_prev, 0, BLOCK), (kvp_own, BLOCK, own_end), (kvp_next, own_end, own_end + BLOCK)):
        for idx in range(N_PLACEMENTS):
            kwin[idx, r0:r1, :] = src[:, idx * LANES:(idx + 1) * LANES]
            vwin[idx, r0:r1, :] = src[:, (N_PLACEMENTS + idx) * LANES:(N_PLACEMENTS + idx + 1) * LANES]

    row = lax.broadcasted_iota(jnp.int32, (SOFTMAX_ROWS, LANES), 0)
    c = lax.broadcasted_iota(jnp.int32, (SOFTMAX_ROWS, LANES), 1)
    off_prev = jnp.where(i > 0, 0, BLOCK)
    off_next = jnp.where(i < n_steps - 1, 0, BLOCK)

    def window_of(g):
        return _key_rows((g // N_PLACEMENTS) * BLOCK, win)

    def qk(g, u, valid):
        q2 = q_ref[g // 2]
        s_buf[u, :, 0:win] = _dot_nt(q2, kwin[g % N_PLACEMENTS, window_of(g), :])
        s_buf[u, :, win:win + past] = _dot_nt(q2, kctx[g % N_PLACEMENTS])

    def softmax(g, u, valid):
        blk = g // N_PLACEMENTS
        off0 = jnp.where(blk == 0, off_prev, 0)
        off2 = jnp.where(blk == n_blocks - 1, off_next, 0)

        def masks_of(r0):
            r = row + r0 % BLOCK
            return {0: c >= r + off0, 2: c <= r - off2}

        _softmax_rows(s_buf, p_buf, t_buf, u, 2 * BLOCK, win + past, masks_of, _group_sinks(sink_ref, g))

    def pv(g, u, valid):
        acc = _dot(p_buf[u, :, 0:win], vwin[g % N_PLACEMENTS, window_of(g), :])
        acc = acc + _dot(p_buf[u, :, win:win + past], vctx[g % N_PLACEMENTS])
        _store_group(o_ref, g, u, _normalize(acc, t_buf, u))

    _software_pipeline(n_blocks * N_PLACEMENTS, PIPE_SLOTS, (qk, softmax, pv))


def _latent_attention(q, kvp, ck, cv, sink, n_batch, seq):
    ns = seq // ATTN_TILE
    nb = seq // BLOCK
    per = ATTN_TILE // BLOCK
    own = lambda b, i: (b * ns + i, 0)
    prev = lambda b, i: (b * nb + jnp.maximum(per * i - 1, 0), 0)
    nxt = lambda b, i: (b * nb + jnp.minimum(per * i + per, nb - 1), 0)
    ctx = lambda b, i: (b, 0, 0)
    past = ck.shape[2]
    win_rows = ATTN_TILE + 2 * BLOCK
    bf = lambda *shape: pltpu.VMEM(shape, BF16)
    stacked = pl.BlockSpec(_stacked_shape(ATTN_TILE), lambda b, i: (b * ns + i, 0, 0))
    return pl.pallas_call(
        functools.partial(_lat_attn_kernel, n_steps=ns, past=past),
        grid=(n_batch, ns),
        in_specs=[
            pl.BlockSpec(memory_space=pltpu.SMEM),
            stacked,
            pl.BlockSpec((BLOCK, KVP_WIDTH), prev),
            pl.BlockSpec((ATTN_TILE, KVP_WIDTH), own),
            pl.BlockSpec((BLOCK, KVP_WIDTH), nxt),
            pl.BlockSpec((1, KV_WIDTH, past), ctx),
            pl.BlockSpec((1, KV_WIDTH, past), ctx),
        ],
        out_specs=stacked,
        out_shape=jax.ShapeDtypeStruct(_stacked_shape(n_batch * seq), BF16),
        scratch_shapes=[bf(N_PLACEMENTS, win_rows, LANES), bf(N_PLACEMENTS, win_rows, LANES),
                        bf(N_PLACEMENTS, past, LANES), bf(N_PLACEMENTS, past, LANES)]
        + _attn_scratch(3 * BLOCK + past),
        compiler_params=_params("arbitrary", "arbitrary"),
        name="latent_attention",
    )(sink, q, kvp, kvp, kvp, ck, cv)


_PK = 2 * GLA_DK
_PV = 2 * GLA_DV


def _cumsum_rows(x, reverse):
    n, width = x.shape
    row = lax.broadcasted_iota(jnp.int32, (SUBLANES, width), 0)
    blocks = [x[j:j + SUBLANES, :] for j in range(0, n, SUBLANES)]
    s = 1
    while s < SUBLANES:
        if reverse:
            blocks = [b + jnp.where(row < SUBLANES - s, pltpu.roll(b, SUBLANES - s, 0), 0.0) for b in blocks]
        else:
            blocks = [b + jnp.where(row >= s, pltpu.roll(b, s, 0), 0.0) for b in blocks]
        s *= 2
    edge = 0 if reverse else SUBLANES - 1
    order = range(len(blocks) - 1, -1, -1) if reverse else range(len(blocks))
    carry = None
    for j in order:
        if carry is not None:
            blocks[j] = blocks[j] + carry
        carry = jnp.broadcast_to(blocks[j][edge:edge + 1, :], (SUBLANES, width))
    return jnp.concatenate(blocks, axis=0)


def _gla_kernel(*refs, seq_chunks, has_init, emit_state):
    it = iter(refs)
    views = [tuple(next(it) for _ in range(3)) for _ in range(2)]
    s0_ref = next(it) if has_init else None
    out_refs = (next(it), next(it))
    sout_ref = next(it) if emit_state else None
    st_ref, a_buf, a2_buf, qe_buf, qd_buf, ds_buf, sb_buf, dc_buf = (next(it) for _ in range(8))
    n_chunks = views[0][0].shape[0] // CHUNK
    n_groups = 2 * n_chunks
    n_pairs = N_GLA_HEADS // 2
    tile = pl.program_id(1)
    q_scale = GLA_DK ** -0.5
    _init_pipeline_scratch(jnp.logical_and(pl.program_id(0) == 0, tile == 0),
                           (st_ref, a_buf, a2_buf, qe_buf, qd_buf, ds_buf, sb_buf), (dc_buf,))

    if has_init:
        @pl.when(tile == 0)
        def _():
            st_ref[...] = jnp.zeros_like(st_ref)
            for d in range(2):
                for h in range(N_GLA_HEADS):
                    p, e = divmod(h, 2)
                    st_ref[d, p, e * GLA_DK:(e + 1) * GLA_DK, e * GLA_DV:(e + 1) * GLA_DV] = s0_ref[0, d, h]

    lane_k = lax.broadcasted_iota(jnp.int32, (CHUNK, _PK), 1)
    row_k = lax.broadcasted_iota(jnp.int32, (CHUNK, _PK), 0)
    lo_k = lane_k < GLA_DK
    tri = (lane_k % CHUNK <= row_k, lane_k % CHUNK >= row_k)
    lo_v = lax.broadcasted_iota(jnp.int32, (CHUNK, _PV), 1) < GLA_DV
    diag = ((lax.broadcasted_iota(jnp.int32, (_PK, _PV), 0) < GLA_DK)
            == (lax.broadcasted_iota(jnp.int32, (_PK, _PV), 1) < GLA_DV))

    def place(g, d):
        c = g // 2
        if d == 0:
            return c, c
        seq = c // seq_chunks
        return c, seq * seq_chunks + (seq_chunks - 1 - (c - seq * seq_chunks))

    def rows_of(pos):
        return pl.ds(pl.multiple_of(pos * CHUNK, CHUNK), CHUNK)

    def prep(g, u, valid):
        d = u % 2
        qk_ref, g_ref, v_ref = views[d]
        _, pos = place(g, d)
        rows = rows_of(pos)
        gc = _cumsum_rows(g_ref[rows, :], reverse=(d == 1))
        gtot = gc[0:1, :] if d == 1 else gc[CHUNK - 1:CHUNK, :]
        k = qk_ref[rows, GLA_K_WIDTH:2 * GLA_K_WIDTH]
        qe = qk_ref[rows, 0:GLA_K_WIDTH] * jnp.exp(gc)
        qe_buf[u] = qe
        q_dec = (qe * q_scale).astype(BF16)
        k_inv = k * jnp.exp(-gc)
        e_tot = jnp.exp(gtot)
        k_end = (k_inv * jnp.where(valid, e_tot, 0.0)).astype(BF16)
        decay = jnp.where(valid, e_tot, 1.0)
        for p in range(n_pairs):
            lanes = slice(p * _PK, (p + 1) * _PK)
            ki = k_inv[:, lanes]
            kst = jnp.concatenate([jnp.where(lo_k, ki, 0.0), jnp.where(lo_k, 0.0, ki)], axis=0).astype(BF16)
            a_buf[u, p] = _dot_nt(q_dec[:, lanes], kst)
            ds_buf[u, p] = jnp.where(diag, _dot_tn(k_end[:, lanes], v_ref[rows, p * _PV:(p + 1) * _PV]), 0.0)
            dc_buf[u, p] = jnp.broadcast_to(decay[:, lanes], (_PK, _PK)).T

    def state(g, u, valid):
        d = u % 2
        c, _ = place(g, d)
        qd_buf[u] = (qe_buf[u] * q_scale).astype(BF16)
        for p in range(n_pairs):
            fresh = (c % seq_chunks == 0) if emit_state else (c < 0)
            s_in = jnp.where(fresh, 0.0, st_ref[d, p])
            sb_buf[u, p] = s_in.astype(BF16)
            a2_buf[u, p] = jnp.where(tri[d], a_buf[u, p], 0.0).astype(BF16)
            dc = dc_buf[u, p]
            s_new = s_in * jnp.concatenate([dc, dc], axis=1) + ds_buf[u, p]
            st_ref[d, p] = s_new
            if emit_state:
                for e in range(2):
                    sout_ref[c // seq_chunks, d, 2 * p + e] = s_new[e * GLA_DK:(e + 1) * GLA_DK,
                                                                    e * GLA_DV:(e + 1) * GLA_DV]

    def out(g, u, valid):
        d = u % 2
        _, _, v_ref = views[d]
        _, pos = place(g, d)
        rows = rows_of(pos)
        for p in range(n_pairs):
            vpf = v_ref[rows, p * _PV:(p + 1) * _PV].astype(F32)
            vbd = jnp.concatenate([jnp.where(lo_v, vpf, 0.0), jnp.where(lo_v, 0.0, vpf)], axis=0).astype(BF16)
            x = jnp.concatenate([a2_buf[u, p], qd_buf[u, :, p * _PK:(p + 1) * _PK]], axis=1)
            w = jnp.concatenate([vbd, sb_buf[u, p]], axis=0)
            out_refs[d][rows, p * _PV:(p + 1) * _PV] = _dot(x, w).astype(out_refs[d].dtype)

    _software_pipeline(n_groups, GLA_PIPE_SLOTS, (prep, state, out))


def _gla(qkg, gf, gb, vg, s0, n_batch, seq):
    tm = GLA_TILE
    long_seq = s0 is not None
    if long_seq:
        nt = seq // tm
        grid = (n_batch, nt)
        fwd = lambda b, i: (b * nt + i, 0)
        bwd = lambda b, i: (b * nt + nt - 1 - i, 0)
        seq_chunks = tm // CHUNK
    else:
        per_step = tm // seq
        grid = (1, n_batch // per_step)
        fwd = bwd = lambda b, i: (i, 0)
        seq_chunks = seq // CHUNK
    in_specs = []
    args = []
    for m, gates in ((fwd, gf), (bwd, gb)):
        in_specs += [pl.BlockSpec((tm, 2 * GLA_K_WIDTH), m), pl.BlockSpec((tm, GLA_K_WIDTH), m),
                     pl.BlockSpec((tm, GLA_V_WIDTH), m)]
        args += [qkg, gates, vg]
    n_tok = n_batch * seq
    out_specs = [pl.BlockSpec((tm, GLA_V_WIDTH), fwd), pl.BlockSpec((tm, GLA_V_WIDTH), bwd)]
    out_shape = [jax.ShapeDtypeStruct((n_tok, GLA_V_WIDTH), BF16)] * 2
    st_dims = (2, N_GLA_HEADS, GLA_DK, GLA_DV)
    if long_seq:
        in_specs.append(pl.BlockSpec((1,) + st_dims, lambda b, i: (b, 0, 0, 0, 0)))
        args.append(s0)
    else:
        out_specs.append(pl.BlockSpec((per_step,) + st_dims, lambda b, i: (i, 0, 0, 0, 0)))
        out_shape.append(jax.ShapeDtypeStruct((n_batch,) + st_dims, F32))
    n = GLA_PIPE_SLOTS
    n_pairs = N_GLA_HEADS // 2
    return pl.pallas_call(
        functools.partial(_gla_kernel, seq_chunks=seq_chunks, has_init=long_seq, emit_state=not long_seq),
        grid=grid,
        in_specs=in_specs,
        out_specs=out_specs,
        out_shape=out_shape,
        scratch_shapes=[pltpu.VMEM((2, n_pairs, _PK, _PV), F32),
                        pltpu.VMEM((n, n_pairs, CHUNK, _PK), F32),
                        pltpu.VMEM((n, n_pairs, CHUNK, _PK), BF16),
                        pltpu.VMEM((n, CHUNK, GLA_K_WIDTH), F32),
                        pltpu.VMEM((n, CHUNK, GLA_K_WIDTH), BF16),
                        pltpu.VMEM((n, n_pairs, _PK, _PV), F32),
                        pltpu.VMEM((n, n_pairs, _PK, _PV), BF16),
                        pltpu.VMEM((n, n_pairs, _PK, _PK), F32)],
        compiler_params=_params("arbitrary", "arbitrary"),
        name="gla_long" if long_seq else "gla_short",
    )(*args)


def _merge_kernel(x_ref, mod_ref, oa_ref, za_ref, of_ref, ob_ref, zg_ref, gng_ref, wo_ref, fng_ref, y_ref):
    ya = _load_stacked(oa_ref).astype(F32) * _silu(za_ref[...].astype(F32))
    og = of_ref[...].astype(F32) + ob_ref[...].astype(F32)
    zg = zg_ref[...].astype(F32)
    gng = gng_ref[...]
    parts = [ya]
    for h in range(N_GLA_HEADS):
        oh = og[:, h * GLA_DV:(h + 1) * GLA_DV]
        nh = oh * lax.rsqrt(jnp.mean(oh * oh, axis=-1, keepdims=True) + EPS)
        parts.append((nh * gng) * _silu(zg[:, h * GLA_DV:(h + 1) * GLA_DV]))
    yin = jnp.concatenate(parts, axis=-1).astype(BF16)
    gate = mod_ref[0][:, 2 * D_MODEL:3 * D_MODEL]
    y = x_ref[...] + gate * _dot(yin, wo_ref[...])
    y_ref[...] = (y * lax.rsqrt(jnp.mean(y * y, axis=-1, keepdims=True) + EPS)) * fng_ref[...]


def _merge(x2d, mod3, mod_row_of_tile, oa, za, of, ob, zg, gla_norm_g, w_out, final_norm_g, tm):
    n_tok = x2d.shape[0]
    row = lambda i: (i, 0)
    const = lambda i: (0, 0)
    wide = pl.BlockSpec((tm, GLA_V_WIDTH), row)
    return pl.pallas_call(
        _merge_kernel,
        grid=(n_tok // tm,),
        in_specs=[
            pl.BlockSpec((tm, D_MODEL), row),
            pl.BlockSpec((1, 1, 3 * D_MODEL), lambda i: (mod_row_of_tile(i), 0, 0)),
            pl.BlockSpec(_stacked_shape(tm), lambda i: (i, 0, 0)), wide, wide, wide, wide,
            pl.BlockSpec((1, GLA_DV), const),
            pl.BlockSpec((D_MODEL, D_MODEL), const),
            pl.BlockSpec((1, D_MODEL), const),
        ],
        out_specs=pl.BlockSpec((tm, D_MODEL), row),
        out_shape=jax.ShapeDtypeStruct((n_tok, D_MODEL), F32),
        compiler_params=_params("arbitrary"),
        name="merge",
    )(x2d, mod3, oa, za, of, ob, zg, gla_norm_g, w_out, final_norm_g)


def _rope_tables(n_tok):
    t = np.arange(n_tok)
    rowp = (t // GRID_W).astype(np.float64)
    colp = (t % GRID_W).astype(np.float64)
    nf = HEAD_DIM // 4
    freqs = ROPE_BASE ** (-np.arange(nf, dtype=np.float64) / nf)
    lane = np.arange(LANES)
    pos = np.where(((lane % HEAD_DIM) < HEAD_DIM // 2)[None, :], rowp[:, None], colp[:, None])
    ang = pos * freqs[lane % nf][None, :]
    sign = np.where((lane % (2 * nf)) < nf, -1.0, 1.0)
    return jnp.asarray(np.cos(ang), F32), jnp.asarray(np.sin(ang) * sign[None, :], F32)


def kernel(x_prompt, x_sample, c, cache_k, cache_v, state_gla, c_ctx, w_mod, b_mod, norm_g, w_in,
           w_gk_f, b_gk_f, w_gk_b, b_gk_b, sink, gla_norm_g, w_out, final_norm_g):
    depth = w_in.shape[0]
    assert depth == 1, "single-layer step"
    l = 0
    bp, sp, _ = x_prompt.shape
    bs, ss, _ = x_sample.shape
    assert (bp * sp) % TOKEN_TILE == 0 and TOKEN_TILE % sp == 0 and ss % TOKEN_TILE == 0
    assert (bp * sp) % MERGE_TILE == 0 and ss % MERGE_TILE == 0
    assert bp % CTX_ATTN_BATCH == 0 and sp % BLOCK == 0 and ss % ATTN_TILE == 0 and ss // BLOCK >= 3
    assert GLA_TILE % sp == 0 and (bp * sp) % GLA_TILE == 0 and ss % GLA_TILE == 0 and sp % CHUNK == 0
    assert ss % GRID_W == 0 and cache_k.shape[2] % LANES == 0

    wl = w_in[l]
    w1 = wl[:, :_TAIL0].astype(BF16)
    wtail = wl[:, _TAIL0:].astype(BF16)
    wgk = jnp.zeros((LANES, 2 * GLA_K_WIDTH), F32)
    wgk = wgk.at[0:GATE_RANK, 0:GLA_K_WIDTH].set(w_gk_f[l])
    wgk = wgk.at[GATE_RANK:2 * GATE_RANK, GLA_K_WIDTH:].set(w_gk_b[l]).astype(BF16)
    bgk = jnp.concatenate([b_gk_f[l], b_gk_b[l]])[None, :]
    wo = w_out[l].astype(BF16)
    ng = norm_g[l][None, :]
    gng = gla_norm_g[l][None, :]
    fng = final_norm_g[None, :]
    sk = sink[l]

    n_rows = SUBLANES
    assert 1 + bs <= n_rows
    cvecs = jnp.concatenate([c_ctx[None, :], c, jnp.zeros((n_rows - 1 - bs, D_MODEL), F32)], axis=0)
    mod3 = _modulation(cvecs, w_mod[l], b_mod[l][None, :]).reshape(n_rows, 1, 3 * D_MODEL)
    tm = TOKEN_TILE

    xp2 = x_prompt.reshape(bp * sp, D_MODEL)
    ctx_row = lambda i: 0
    q, kvp, za, vg, zg, qkg, gf, gb, kt, vt = _project(xp2, mod3, ctx_row, ng, w1, wtail, wgk, bgk, None, tm, sp)
    oa = _context_attention(q, kvp, sk, sp)
    of, ob, st = _gla(qkg, gf, gb, vg, None, bp, sp)
    y_prompt = _merge(xp2, mod3, ctx_row, oa, za, of, ob, zg, gng, wo, fng, MERGE_TILE).reshape(bp, sp, D_MODEL)
    to_cache = lambda t: t.reshape(bp, 1, N_KV_HEADS, HEAD_DIM, sp).transpose(0, 1, 4, 2, 3)
    new_k, new_v = to_cache(kt), to_cache(vt)
    new_state = st[:, None]

    xs2 = x_sample.reshape(bs * ss, D_MODEL)
    tiles_per_seq = ss // tm
    lat_row = lambda i: 1 + i // tiles_per_seq
    q, kvp, za, vg, zg, qkg, gf, gb = _project(xs2, mod3, lat_row, ng, w1, wtail, wgk, bgk, _rope_tables(ss), tm, ss)
    past = cache_k.shape[2]
    from_cache = lambda t: t[:, l].transpose(0, 2, 3, 1).reshape(bs, KV_WIDTH, past)
    ck, cv = from_cache(cache_k), from_cache(cache_v)
    oa = _latent_attention(q, kvp, ck, cv, sk, bs, ss)
    of, ob = _gla(qkg, gf, gb, vg, state_gla[:, l], bs, ss)
    merge_row = lambda i: 1 + i // (ss // MERGE_TILE)
    y_sample = _merge(xs2, mod3, merge_row, oa, za, of, ob, zg, gng, wo, fng, MERGE_TILE).reshape(bs, ss, D_MODEL)

    return (y_prompt, y_sample, new_k, new_v, new_state)
```

```python
import functools
import math

import numpy as np
import jax
import jax.numpy as jnp
from jax import lax
from jax.experimental import pallas as pl
from jax.experimental.pallas import tpu as pltpu

D_MODEL = 1024
GRID_W = 64
HEAD_DIM = 64
N_ATTN_HEADS = 8
N_KV_HEADS = 2
ATTN_WIDTH = 512
KV_WIDTH = 128
BLOCK = 128
N_GLA_HEADS = 4
GLA_DK = 64
GLA_DV = 128
GLA_K_WIDTH = 256
GLA_V_WIDTH = 512
GATE_RANK = 16
GATE_NORMALIZER = 16.0
CHUNK = 64
ROPE_BASE = 10000.0
EPS = 1e-6
NEG_INF = -1e30

KV_GROUP = N_ATTN_HEADS // N_KV_HEADS
LANES = 128
SUBLANES = 8
TOKEN_TILE = 1024
MERGE_TILE = 1024
GLA_TILE = 2048
ATTN_TILE = 2048
CTX_ATTN_BATCH = 8
PIPE_SLOTS = 4
GLA_PIPE_SLOTS = 4
KVP_WIDTH = 8 * LANES
N_PLACEMENTS = 2 * N_KV_HEADS
SOFTMAX_ROWS = 32
MOD_TILE = 1024
VMEM_LIMIT = 56 * 1024 * 1024
LOG2E = math.log2(math.e)

F32 = jnp.float32
BF16 = jnp.bfloat16


def _dot(a, b):
    return jnp.dot(a, b, preferred_element_type=F32)


def _dot_nt(a, b):
    return lax.dot_general(a, b, (((1,), (1,)), ((), ())), preferred_element_type=F32)


def _dot_tn(a, b):
    return lax.dot_general(a, b, (((0,), (0,)), ((), ())), preferred_element_type=F32)


def _silu(x):
    h = 0.5 * x
    return h + h * jnp.tanh(h)


def _log_sigmoid(x):
    return jnp.minimum(x, 0.0) - jnp.log1p(jnp.exp(-jnp.abs(x)))


def _params(*sem):
    return pltpu.CompilerParams(dimension_semantics=sem, vmem_limit_bytes=VMEM_LIMIT)


def _split_bf16(x):
    hi = x.astype(BF16)
    return hi, (x - hi.astype(F32)).astype(BF16)


def _mod_kernel(c_ref, w_ref, b_ref, o_ref):
    rows = c_ref.shape[0]
    s_hi, s_lo = _split_bf16(_silu(c_ref[...]))
    w_hi, w_lo = _split_bf16(w_ref[...])
    by_hi = _dot(jnp.concatenate([s_hi, s_lo], axis=0), w_hi)
    o_ref[...] = by_hi[0:rows] + by_hi[rows:2 * rows] + _dot(s_hi, w_lo) + b_ref[...]


def _modulation(cvecs, w_mod, b_mod):
    rows = cvecs.shape[0]
    n_out = w_mod.shape[1]
    return pl.pallas_call(
        _mod_kernel,
        grid=(n_out // MOD_TILE,),
        in_specs=[
            pl.BlockSpec((rows, D_MODEL), lambda j: (0, 0)),
            pl.BlockSpec((D_MODEL, MOD_TILE), lambda j: (0, j)),
            pl.BlockSpec((1, MOD_TILE), lambda j: (0, j)),
        ],
        out_specs=pl.BlockSpec((rows, MOD_TILE), lambda j: (0, j)),
        out_shape=jax.ShapeDtypeStruct((rows, n_out), F32),
        compiler_params=_params("arbitrary"),
        name="adaln_mod",
    )(cvecs, w_mod, b_mod)


_Q0, _K0, _V0, _ZA0, _QG0, _KG0, _VG0, _TAIL0 = 0, 512, 640, 768, 1280, 1536, 1792, 2304
_TAIL_COLS = 2 * GATE_RANK + GLA_V_WIDTH
_Q_PRESCALE = (HEAD_DIM ** -0.5) * LOG2E


def _rope(x, cos, sin_signed):
    n = x.shape[1] // LANES
    lane = lax.broadcasted_iota(jnp.int32, (x.shape[0], LANES), 1)
    first = (lane % 32) < 16
    outs = []
    for c in range(n):
        xc = x[:, c * LANES:(c + 1) * LANES]
        partner = jnp.where(first, pltpu.roll(xc, LANES - 16, 1), pltpu.roll(xc, 16, 1))
        outs.append(xc * cos + partner * sin_signed)
    return outs[0] if n == 1 else jnp.concatenate(outs, axis=-1)


def _kv_placements(k, v):
    lane = lax.broadcasted_iota(jnp.int32, k.shape, 1)
    lo = lane < HEAD_DIM
    kr = pltpu.roll(k, HEAD_DIM, 1)
    vr = pltpu.roll(v, HEAD_DIM, 1)
    pieces = [jnp.where(lo, k, 0.0), jnp.where(lo, 0.0, kr), jnp.where(lo, kr, 0.0), jnp.where(lo, 0.0, k),
              jnp.where(lo, v, 1.0), jnp.where(lo, 1.0, vr), jnp.where(lo, vr, 1.0), jnp.where(lo, 1.0, v)]
    return jnp.concatenate(pieces, axis=-1).astype(BF16)


def _proj_kernel(*refs, rope):
    if rope:
        (x_ref, mod_ref, ng_ref, wt_ref, wgk_ref, bgk_ref, cos_ref, sin_ref,
         q_ref, kvp_ref, za_ref, vg_ref, zg_ref, qkg_ref, gf_ref, gb_ref) = refs
    else:
        (x_ref, mod_ref, ng_ref, wt_ref, wgk_ref, bgk_ref,
         q_ref, kvp_ref, za_ref, vg_ref, zg_ref, qkg_ref, gf_ref, gb_ref, k_ref, v_ref) = refs
    x = x_ref[...]
    xn = x * lax.rsqrt(jnp.mean(x * x, axis=-1, keepdims=True) + EPS)
    mod = mod_ref[0]
    shift = mod[:, 0:D_MODEL]
    scale = mod[:, D_MODEL:2 * D_MODEL]
    h = ((xn * ng_ref[...]) * (1.0 + scale) + shift).astype(BF16)

    def piece(c0, c1):
        return _dot_nt(h, wt_ref[c0:c1, :])

    q = piece(_Q0, _K0)
    kv = piece(_K0, _ZA0)
    k = kv[:, 0:KV_WIDTH]
    v = kv[:, KV_WIDTH:2 * KV_WIDTH]
    if rope:
        cos = cos_ref[...]
        sin = sin_ref[...]
        q = _rope(q, cos, sin)
        k = _rope(k, cos, sin)
    else:
        seq = k_ref.shape[2]
        for b in range(k_ref.shape[0]):
            k_ref[b] = k[b * seq:(b + 1) * seq, :].T
            v_ref[b] = v[b * seq:(b + 1) * seq, :].T
    _store_stacked((q * _Q_PRESCALE).astype(BF16), q_ref)
    kvp_ref[...] = _kv_placements(k, v)
    za_ref[...] = piece(_ZA0, _QG0).astype(BF16)
    vg_ref[...] = piece(_VG0, _TAIL0).astype(BF16)
    qkg_ref[...] = piece(_QG0, _VG0)
    tail = piece(_TAIL0, _TAIL0 + _TAIL_COLS)
    zg_ref[...] = tail[:, 2 * GATE_RANK:_TAIL_COLS].astype(BF16)
    lane = lax.broadcasted_iota(jnp.int32, (tail.shape[0], LANES), 1)
    lr = jnp.where(lane < 2 * GATE_RANK, tail[:, 0:LANES], 0.0).astype(BF16)
    g = _log_sigmoid(_dot(lr, wgk_ref[...]) + bgk_ref[...]) * (1.0 / GATE_NORMALIZER)
    gf_ref[...] = g[:, 0:GLA_K_WIDTH]
    gb_ref[...] = g[:, GLA_K_WIDTH:2 * GLA_K_WIDTH]


def _project(x2d, mod3, mod_row_of_tile, norm_g, wt, wgk, bgk, rope_tabs, tm, seq):
    n_tok = x2d.shape[0]
    rope = rope_tabs is not None
    row = lambda i: (i, 0)
    const = lambda i: (0, 0)
    in_specs = [
        pl.BlockSpec((tm, D_MODEL), row),
        pl.BlockSpec((1, 1, 3 * D_MODEL), lambda i: (mod_row_of_tile(i), 0, 0)),
        pl.BlockSpec((1, D_MODEL), const),
        pl.BlockSpec((_TAIL0 + _TAIL_COLS, D_MODEL), const),
        pl.BlockSpec((LANES, 2 * GLA_K_WIDTH), const),
        pl.BlockSpec((1, 2 * GLA_K_WIDTH), const),
    ]
    args = [x2d, mod3, norm_g, wt, wgk, bgk]
    if rope:
        tiles_per_seq = rope_tabs[0].shape[0] // tm
        pos = lambda i: (i % tiles_per_seq, 0)
        in_specs += [pl.BlockSpec((tm, LANES), pos), pl.BlockSpec((tm, LANES), pos)]
        args += list(rope_tabs)
    widths = [(KVP_WIDTH, BF16), (ATTN_WIDTH, BF16), (GLA_V_WIDTH, BF16), (GLA_V_WIDTH, BF16),
              (2 * GLA_K_WIDTH, F32), (GLA_K_WIDTH, F32), (GLA_K_WIDTH, F32)]
    out_specs = [pl.BlockSpec(_stacked_shape(tm), lambda i: (i, 0, 0))] + [pl.BlockSpec((tm, w), row) for w, _ in widths]
    out_shape = ([jax.ShapeDtypeStruct(_stacked_shape(n_tok), BF16)]
                 + [jax.ShapeDtypeStruct((n_tok, w), dt) for w, dt in widths])
    if not rope:
        for _ in range(2):
            out_specs.append(pl.BlockSpec((tm // seq, KV_WIDTH, seq), lambda i: (i, 0, 0)))
            out_shape.append(jax.ShapeDtypeStruct((n_tok // seq, KV_WIDTH, seq), F32))
    return pl.pallas_call(
        functools.partial(_proj_kernel, rope=rope),
        grid=(n_tok // tm,),
        in_specs=in_specs,
        out_specs=out_specs,
        out_shape=out_shape,
        compiler_params=_params("arbitrary"),
        name="project_rope" if rope else "project",
    )(*args)


def _software_pipeline(n_groups, n_slots, stages):
    last_g = n_groups - 1

    def body(k, carry):
        for age in (2, 1, 0):
            for u in range(n_slots):
                g = n_slots * (k - age) + u
                stages[age](jnp.clip(g, 0, last_g), u, jnp.logical_and(g >= 0, g <= last_g))
        return carry

    lax.fori_loop(0, n_groups // n_slots + 2, body, 0)


def _init_pipeline_scratch(first, zero_refs, one_refs=()):
    @pl.when(first)
    def _():
        for ref in zero_refs:
            ref[...] = jnp.zeros_like(ref)
        for ref in one_refs:
            ref[...] = jnp.ones_like(ref)


def _key_rows(start, size):
    return pl.ds(pl.multiple_of(start, BLOCK), size)


def _softmax_rows(s_buf, p_buf, t_buf, slot, n_rows, n_keys, masks_of, sink_of):
    for r0 in range(0, n_rows, SOFTMAX_ROWS):
        rs = slice(r0, r0 + SOFTMAX_ROWS)
        masks = masks_of(r0)
        cols = []
        for c in range(n_keys // LANES):
            sc = s_buf[slot, rs, c * LANES:(c + 1) * LANES]
            cols.append(jnp.where(masks[c], sc, NEG_INF) if c in masks else sc)
        mx = cols[0]
        for sc in cols[1:]:
            mx = jnp.maximum(mx, sc)
        sink = sink_of(r0)
        m = jnp.maximum(jnp.max(mx, axis=-1, keepdims=True), sink)
        for c, sc in enumerate(cols):
            p_buf[slot, rs, c * LANES:(c + 1) * LANES] = jnp.exp2(sc - m).astype(BF16)
        t_buf[slot, rs, :] = jnp.broadcast_to(jnp.exp2(sink - m), (SOFTMAX_ROWS, LANES))


def _normalize(acc, t_buf, slot):
    return acc / (pltpu.roll(acc, HEAD_DIM, 1) + t_buf[slot])


def _attn_scratch(n_keys):
    rows = 2 * BLOCK
    return [pltpu.VMEM((PIPE_SLOTS, rows, n_keys), F32), pltpu.VMEM((PIPE_SLOTS, rows, n_keys), BF16),
            pltpu.VMEM((PIPE_SLOTS, rows, LANES), F32)]


def _stacked_shape(n_tok):
    return (n_tok // BLOCK * N_KV_HEADS, 2 * BLOCK, LANES)


def _store_stacked(x, ref):
    for blk in range(x.shape[0] // BLOCK):
        for j in range(N_KV_HEADS):
            for e in range(2):
                ref[blk * N_KV_HEADS + j, e * BLOCK:(e + 1) * BLOCK, :] = (
                    x[blk * BLOCK:(blk + 1) * BLOCK, (2 * j + e) * LANES:(2 * j + e + 1) * LANES])


def _load_stacked(ref):
    blocks = []
    for blk in range(ref.shape[0] // N_KV_HEADS):
        blocks.append(jnp.concatenate([ref[blk * N_KV_HEADS + j, e * BLOCK:(e + 1) * BLOCK, :]
                                       for j in range(N_KV_HEADS) for e in range(2)], axis=-1))
    return jnp.concatenate(blocks, axis=0)


assert PIPE_SLOTS % 2 == 0


def _store_group(o_ref, g, u, o):
    half = slice((u % 2) * HEAD_DIM, (u % 2 + 1) * HEAD_DIM)
    o_ref[g // 2, :, half] = o[:, half].astype(o_ref.dtype)


def _group_sinks(sink_ref, g):
    j, e = (g // 2) % N_KV_HEADS, g % 2
    sink_a = sink_ref[KV_GROUP * j + e] * LOG2E
    sink_b = sink_ref[KV_GROUP * j + 2 + e] * LOG2E
    return lambda r0: sink_a if r0 < BLOCK else sink_b


def _ctx_attn_kernel(sink_ref, q_ref, kvp_ref, o_ref, kown, vown, s_buf, p_buf, t_buf, *, seq):
    n_blocks = q_ref.shape[0] // N_KV_HEADS
    _init_pipeline_scratch(pl.program_id(0) == 0, (s_buf, p_buf), (t_buf,))
    for idx in range(N_PLACEMENTS):
        kown[idx] = kvp_ref[:, idx * LANES:(idx + 1) * LANES]
        vown[idx] = kvp_ref[:, (N_PLACEMENTS + idx) * LANES:(N_PLACEMENTS + idx + 1) * LANES]
    blocks_per_seq = seq // BLOCK

    def keys_of(g):
        return _key_rows(((g // N_PLACEMENTS) // blocks_per_seq) * seq, seq)

    def qk(g, u, valid):
        s_buf[u] = _dot_nt(q_ref[g // 2], kown[g % N_PLACEMENTS, keys_of(g), :])

    def softmax(g, u, valid):
        _softmax_rows(s_buf, p_buf, t_buf, u, 2 * BLOCK, seq, lambda r0: {}, _group_sinks(sink_ref, g))

    def pv(g, u, valid):
        _store_group(o_ref, g, u, _normalize(_dot(p_buf[u], vown[g % N_PLACEMENTS, keys_of(g), :]), t_buf, u))

    _software_pipeline(n_blocks * N_PLACEMENTS, PIPE_SLOTS, (qk, softmax, pv))


def _context_attention(q, kvp, sink, seq):
    n_tok = kvp.shape[0]
    rows = CTX_ATTN_BATCH * seq
    row = lambda b: (b, 0)
    stacked = pl.BlockSpec(_stacked_shape(rows), lambda b: (b, 0, 0))
    return pl.pallas_call(
        functools.partial(_ctx_attn_kernel, seq=seq),
        grid=(n_tok // rows,),
        in_specs=[
            pl.BlockSpec(memory_space=pltpu.SMEM),
            stacked,
            pl.BlockSpec((rows, KVP_WIDTH), row),
        ],
        out_specs=stacked,
        out_shape=jax.ShapeDtypeStruct(_stacked_shape(n_tok), BF16),
        scratch_shapes=[pltpu.VMEM((N_PLACEMENTS, rows, LANES), BF16)] * 2 + _attn_scratch(seq),
        compiler_params=_params("arbitrary"),
        name="context_attention",
    )(sink, q, kvp)


def _lat_attn_kernel(sink_ref, q_ref, kvp_prev, kvp_own, kvp_next, ck_ref, cv_ref, o_ref,
                     kwin, vwin, kctx, vctx, s_buf, p_buf, t_buf, *, n_steps, past):
    i = pl.program_id(1)
    n_blocks = q_ref.shape[0] // N_KV_HEADS
    own_end = BLOCK + kvp_own.shape[0]
    win = 3 * BLOCK
    _init_pipeline_scratch(jnp.logical_and(pl.program_id(0) == 0, i == 0), (s_buf, p_buf), (t_buf,))

    @pl.when(i == 0)
    def _():
        cpl = _kv_placements(ck_ref[0].T, cv_ref[0].T)
        for idx in range(N_PLACEMENTS):
            kctx[idx] = cpl[:, idx * LANES:(idx + 1) * LANES]
            vctx[idx] = cpl[:, (N_PLACEMENTS + idx) * LANES:(N_PLACEMENTS + idx + 1) * LANES]

    for src, r0, r1 in ((kvp_prev, 0, BLOCK), (kvp_own, BLOCK, own_end), (kvp_next, own_end, own_end + BLOCK)):
        for idx in range(N_PLACEMENTS):
            kwin[idx, r0:r1, :] = src[:, idx * LANES:(idx + 1) * LANES]
            vwin[idx, r0:r1, :] = src[:, (N_PLACEMENTS + idx) * LANES:(N_PLACEMENTS + idx + 1) * LANES]

    row = lax.broadcasted_iota(jnp.int32, (SOFTMAX_ROWS, LANES), 0)
    c = lax.broadcasted_iota(jnp.int32, (SOFTMAX_ROWS, LANES), 1)
    off_prev = jnp.where(i > 0, 0, BLOCK)
    off_next = jnp.where(i < n_steps - 1, 0, BLOCK)

    def window_of(g):
        return _key_rows((g // N_PLACEMENTS) * BLOCK, win)

    def qk(g, u, valid):
        q2 = q_ref[g // 2]
        s_buf[u, :, 0:win] = _dot_nt(q2, kwin[g % N_PLACEMENTS, window_of(g), :])
        s_buf[u, :, win:win + past] = _dot_nt(q2, kctx[g % N_PLACEMENTS])

    def softmax(g, u, valid):
        blk = g // N_PLACEMENTS
        off0 = jnp.where(blk == 0, off_prev, 0)
        off2 = jnp.where(blk == n_blocks - 1, off_next, 0)

        def masks_of(r0):
            r = row + r0 % BLOCK
            return {0: c >= r + off0, 2: c <= r - off2}

        _softmax_rows(s_buf, p_buf, t_buf, u, 2 * BLOCK, win + past, masks_of, _group_sinks(sink_ref, g))

    def pv(g, u, valid):
        acc = _dot(p_buf[u, :, 0:win], vwin[g % N_PLACEMENTS, window_of(g), :])
        acc = acc + _dot(p_buf[u, :, win:win + past], vctx[g % N_PLACEMENTS])
        _store_group(o_ref, g, u, _normalize(acc, t_buf, u))

    _software_pipeline(n_blocks * N_PLACEMENTS, PIPE_SLOTS, (qk, softmax, pv))


def _latent_attention(q, kvp, ck, cv, sink, n_batch, seq):
    ns = seq // ATTN_TILE
    nb = seq // BLOCK
    per = ATTN_TILE // BLOCK
    own = lambda b, i: (b * ns + i, 0)
    prev = lambda b, i: (b * nb + jnp.maximum(per * i - 1, 0), 0)
    nxt = lambda b, i: (b * nb + jnp.minimum(per * i + per, nb - 1), 0)
    ctx = lambda b, i: (b, 0, 0)
    past = ck.shape[2]
    win_rows = ATTN_TILE + 2 * BLOCK
    bf = lambda *shape: pltpu.VMEM(shape, BF16)
    stacked = pl.BlockSpec(_stacked_shape(ATTN_TILE), lambda b, i: (b * ns + i, 0, 0))
    return pl.pallas_call(
        functools.partial(_lat_attn_kernel, n_steps=ns, past=past),
        grid=(n_batch, ns),
        in_specs=[
            pl.BlockSpec(memory_space=pltpu.SMEM),
            stacked,
            pl.BlockSpec((BLOCK, KVP_WIDTH), prev),
            pl.BlockSpec((ATTN_TILE, KVP_WIDTH), own),
            pl.BlockSpec((BLOCK, KVP_WIDTH), nxt),
            pl.BlockSpec((1, KV_WIDTH, past), ctx),
            pl.BlockSpec((1, KV_WIDTH, past), ctx),
        ],
        out_specs=stacked,
        out_shape=jax.ShapeDtypeStruct(_stacked_shape(n_batch * seq), BF16),
        scratch_shapes=[bf(N_PLACEMENTS, win_rows, LANES), bf(N_PLACEMENTS, win_rows, LANES),
                        bf(N_PLACEMENTS, past, LANES), bf(N_PLACEMENTS, past, LANES)]
        + _attn_scratch(3 * BLOCK + past),
        compiler_params=_params("arbitrary", "arbitrary"),
        name="latent_attention",
    )(sink, q, kvp, kvp, kvp, ck, cv)


_PK = 2 * GLA_DK
_PV = 2 * GLA_DV


def _cumsum_rows(x, reverse):
    n, width = x.shape
    row = lax.broadcasted_iota(jnp.int32, (SUBLANES, width), 0)
    blocks = [x[j:j + SUBLANES, :] for j in range(0, n, SUBLANES)]
    s = 1
    while s < SUBLANES:
        if reverse:
            blocks = [b + jnp.where(row < SUBLANES - s, pltpu.roll(b, SUBLANES - s, 0), 0.0) for b in blocks]
        else:
            blocks = [b + jnp.where(row >= s, pltpu.roll(b, s, 0), 0.0) for b in blocks]
        s *= 2
    edge = 0 if reverse else SUBLANES - 1
    order = range(len(blocks) - 1, -1, -1) if reverse else range(len(blocks))
    carry = None
    for j in order:
        if carry is not None:
            blocks[j] = blocks[j] + carry
        carry = jnp.broadcast_to(blocks[j][edge:edge + 1, :], (SUBLANES, width))
    return jnp.concatenate(blocks, axis=0)


def _gla_kernel(*refs, seq_chunks, has_init, emit_state):
    it = iter(refs)
    views = [tuple(next(it) for _ in range(3)) for _ in range(2)]
    s0_ref = next(it) if has_init else None
    out_refs = (next(it), next(it))
    sout_ref = next(it) if emit_state else None
    st_ref, a_buf, a2_buf, qe_buf, qd_buf, ds_buf, sb_buf, dc_buf = (next(it) for _ in range(8))
    n_chunks = views[0][0].shape[0] // CHUNK
    n_groups = 2 * n_chunks
    n_pairs = N_GLA_HEADS // 2
    tile = pl.program_id(1)
    q_scale = GLA_DK ** -0.5
    _init_pipeline_scratch(jnp.logical_and(pl.program_id(0) == 0, tile == 0),
                           (st_ref, a_buf, a2_buf, qe_buf, qd_buf, ds_buf, sb_buf), (dc_buf,))

    if has_init:
        @pl.when(tile == 0)
        def _():
            st_ref[...] = jnp.zeros_like(st_ref)
            for d in range(2):
                for h in range(N_GLA_HEADS):
                    p, e = divmod(h, 2)
                    st_ref[d, p, e * GLA_DK:(e + 1) * GLA_DK, e * GLA_DV:(e + 1) * GLA_DV] = s0_ref[0, d, h]

    lane_k = lax.broadcasted_iota(jnp.int32, (CHUNK, _PK), 1)
    row_k = lax.broadcasted_iota(jnp.int32, (CHUNK, _PK), 0)
    lo_k = lane_k < GLA_DK
    tri = (lane_k % CHUNK <= row_k, lane_k % CHUNK >= row_k)
    lo_v = lax.broadcasted_iota(jnp.int32, (CHUNK, _PV), 1) < GLA_DV
    diag = ((lax.broadcasted_iota(jnp.int32, (_PK, _PV), 0) < GLA_DK)
            == (lax.broadcasted_iota(jnp.int32, (_PK, _PV), 1) < GLA_DV))

    def place(g, d):
        c = g // 2
        if d == 0:
            return c, c
        seq = c // seq_chunks
        return c, seq * seq_chunks + (seq_chunks - 1 - (c - seq * seq_chunks))

    def rows_of(pos):
        return pl.ds(pl.multiple_of(pos * CHUNK, CHUNK), CHUNK)

    def prep(g, u, valid):
        d = u % 2
        qk_ref, g_ref, v_ref = views[d]
        _, pos = place(g, d)
        rows = rows_of(pos)
        gc = _cumsum_rows(g_ref[rows, :], reverse=(d == 1))
        gtot = gc[0:1, :] if d == 1 else gc[CHUNK - 1:CHUNK, :]
        k = qk_ref[rows, GLA_K_WIDTH:2 * GLA_K_WIDTH]
        qe = qk_ref[rows, 0:GLA_K_WIDTH] * jnp.exp(gc)
        qe_buf[u] = qe
        q_dec = (qe * q_scale).astype(BF16)
        k_inv = k * jnp.exp(-gc)
        e_tot = jnp.exp(gtot)
        k_end = (k_inv * jnp.where(valid, e_tot, 0.0)).astype(BF16)
        decay = jnp.where(valid, e_tot, 1.0)
        for p in range(n_pairs):
            lanes = slice(p * _PK, (p + 1) * _PK)
            ki = k_inv[:, lanes]
            kst = jnp.concatenate([jnp.where(lo_k, ki, 0.0), jnp.where(lo_k, 0.0, ki)], axis=0).astype(BF16)
            a_buf[u, p] = _dot_nt(q_dec[:, lanes], kst)
            ds_buf[u, p] = jnp.where(diag, _dot_tn(k_end[:, lanes], v_ref[rows, p * _PV:(p + 1) * _PV]), 0.0)
            dc_buf[u, p] = jnp.broadcast_to(decay[:, lanes], (_PK, _PK)).T

    def state(g, u, valid):
        d = u % 2
        c, _ = place(g, d)
        qd_buf[u] = (qe_buf[u] * q_scale).astype(BF16)
        for p in range(n_pairs):
            fresh = (c % seq_chunks == 0) if emit_state else (c < 0)
            s_in = jnp.where(fresh, 0.0, st_ref[d, p])
            sb_buf[u, p] = s_in.astype(BF16)
            a2_buf[u, p] = jnp.where(tri[d], a_buf[u, p], 0.0).astype(BF16)
            dc = dc_buf[u, p]
            s_new = s_in * jnp.concatenate([dc, dc], axis=1) + ds_buf[u, p]
            st_ref[d, p] = s_new
            if emit_state:
                for e in range(2):
                    sout_ref[c // seq_chunks, d, 2 * p + e] = s_new[e * GLA_DK:(e + 1) * GLA_DK,
                                                                    e * GLA_DV:(e + 1) * GLA_DV]

    def out(g, u, valid):
        d = u % 2
        _, _, v_ref = views[d]
        _, pos = place(g, d)
        rows = rows_of(pos)
        for p in range(n_pairs):
            vpf = v_ref[rows, p * _PV:(p + 1) * _PV].astype(F32)
            vbd = jnp.concatenate([jnp.where(lo_v, vpf, 0.0), jnp.where(lo_v, 0.0, vpf)], axis=0).astype(BF16)
            x = jnp.concatenate([a2_buf[u, p], qd_buf[u, :, p * _PK:(p + 1) * _PK]], axis=1)
            w = jnp.concatenate([vbd, sb_buf[u, p]], axis=0)
            out_refs[d][rows, p * _PV:(p + 1) * _PV] = _dot(x, w).astype(out_refs[d].dtype)

    _software_pipeline(n_groups, GLA_PIPE_SLOTS, (prep, state, out))


def _gla(qkg, gf, gb, vg, s0, n_batch, seq):
    tm = GLA_TILE
    long_seq = s0 is not None
    if long_seq:
        nt = seq // tm
        grid = (n_batch, nt)
        fwd = lambda b, i: (b * nt + i, 0)
        bwd = lambda b, i: (b * nt + nt - 1 - i, 0)
        seq_chunks = tm // CHUNK
    else:
        per_step = tm // seq
        grid = (1, n_batch // per_step)
        fwd = bwd = lambda b, i: (i, 0)
        seq_chunks = seq // CHUNK
    in_specs = []
    args = []
    for m, gates in ((fwd, gf), (bwd, gb)):
        in_specs += [pl.BlockSpec((tm, 2 * GLA_K_WIDTH), m), pl.BlockSpec((tm, GLA_K_WIDTH), m),
                     pl.BlockSpec((tm, GLA_V_WIDTH), m)]
        args += [qkg, gates, vg]
    n_tok = n_batch * seq
    out_specs = [pl.BlockSpec((tm, GLA_V_WIDTH), fwd), pl.BlockSpec((tm, GLA_V_WIDTH), bwd)]
    out_shape = [jax.ShapeDtypeStruct((n_tok, GLA_V_WIDTH), BF16)] * 2
    st_dims = (2, N_GLA_HEADS, GLA_DK, GLA_DV)
    if long_seq:
        in_specs.append(pl.BlockSpec((1,) + st_dims, lambda b, i: (b, 0, 0, 0, 0)))
        args.append(s0)
    else:
        out_specs.append(pl.BlockSpec((per_step,) + st_dims, lambda b, i: (i, 0, 0, 0, 0)))
        out_shape.append(jax.ShapeDtypeStruct((n_batch,) + st_dims, F32))
    n = GLA_PIPE_SLOTS
    n_pairs = N_GLA_HEADS // 2
    return pl.pallas_call(
        functools.partial(_gla_kernel, seq_chunks=seq_chunks, has_init=long_seq, emit_state=not long_seq),
        grid=grid,
        in_specs=in_specs,
        out_specs=out_specs,
        out_shape=out_shape,
        scratch_shapes=[pltpu.VMEM((2, n_pairs, _PK, _PV), F32),
                        pltpu.VMEM((n, n_pairs, CHUNK, _PK), F32),
                        pltpu.VMEM((n, n_pairs, CHUNK, _PK), BF16),
                        pltpu.VMEM((n, CHUNK, GLA_K_WIDTH), F32),
                        pltpu.VMEM((n, CHUNK, GLA_K_WIDTH), BF16),
                        pltpu.VMEM((n, n_pairs, _PK, _PV), F32),
                        pltpu.VMEM((n, n_pairs, _PK, _PV), BF16),
                        pltpu.VMEM((n, n_pairs, _PK, _PK), F32)],
        compiler_params=_params("arbitrary", "arbitrary"),
        name="gla_long" if long_seq else "gla_short",
    )(*args)


def _merge_kernel(x_ref, mod_ref, oa_ref, za_ref, of_ref, ob_ref, zg_ref, gng_ref, wo_ref, fng_ref, y_ref):
    ya = _load_stacked(oa_ref).astype(F32) * _silu(za_ref[...].astype(F32))
    og = of_ref[...].astype(F32) + ob_ref[...].astype(F32)
    zg = zg_ref[...].astype(F32)
    gng = gng_ref[...]
    parts = [ya]
    for h in range(N_GLA_HEADS):
        oh = og[:, h * GLA_DV:(h + 1) * GLA_DV]
        nh = oh * lax.rsqrt(jnp.mean(oh * oh, axis=-1, keepdims=True) + EPS)
        parts.append((nh * gng) * _silu(zg[:, h * GLA_DV:(h + 1) * GLA_DV]))
    yin = jnp.concatenate(parts, axis=-1).astype(BF16)
    gate = mod_ref[0][:, 2 * D_MODEL:3 * D_MODEL]
    y = x_ref[...] + gate * _dot(yin, wo_ref[...])
    y_ref[...] = (y * lax.rsqrt(jnp.mean(y * y, axis=-1, keepdims=True) + EPS)) * fng_ref[...]


def _merge(x2d, mod3, mod_row_of_tile, oa, za, of, ob, zg, gla_norm_g, w_out, final_norm_g, tm):
    n_tok = x2d.shape[0]
    row = lambda i: (i, 0)
    const = lambda i: (0, 0)
    wide = pl.BlockSpec((tm, GLA_V_WIDTH), row)
    return pl.pallas_call(
        _merge_kernel,
        grid=(n_tok // tm,),
        in_specs=[
            pl.BlockSpec((tm, D_MODEL), row),
            pl.BlockSpec((1, 1, 3 * D_MODEL), lambda i: (mod_row_of_tile(i), 0, 0)),
            pl.BlockSpec(_stacked_shape(tm), lambda i: (i, 0, 0)), wide, wide, wide, wide,
            pl.BlockSpec((1, GLA_DV), const),
            pl.BlockSpec((D_MODEL, D_MODEL), const),
            pl.BlockSpec((1, D_MODEL), const),
        ],
        out_specs=pl.BlockSpec((tm, D_MODEL), row),
        out_shape=jax.ShapeDtypeStruct((n_tok, D_MODEL), F32),
        compiler_params=_params("arbitrary"),
        name="merge",
    )(x2d, mod3, oa, za, of, ob, zg, gla_norm_g, w_out, final_norm_g)


def _rope_tables(n_tok):
    t = np.arange(n_tok)
    rowp = (t // GRID_W).astype(np.float64)
    colp = (t % GRID_W).astype(np.float64)
    nf = HEAD_DIM // 4
    freqs = ROPE_BASE ** (-np.arange(nf, dtype=np.float64) / nf)
    lane = np.arange(LANES)
    pos = np.where(((lane % HEAD_DIM) < HEAD_DIM // 2)[None, :], rowp[:, None], colp[:, None])
    ang = pos * freqs[lane % nf][None, :]
    sign = np.where((lane % (2 * nf)) < nf, -1.0, 1.0)
    return jnp.asarray(np.cos(ang), F32), jnp.asarray(np.sin(ang) * sign[None, :], F32)


def kernel(x_prompt, x_sample, c, cache_k, cache_v, state_gla, c_ctx, w_mod, b_mod, norm_g, w_in,
           w_gk_f, b_gk_f, w_gk_b, b_gk_b, sink, gla_norm_g, w_out, final_norm_g):
    depth = w_in.shape[0]
    assert depth == 1, "single-layer step"
    l = 0
    bp, sp, _ = x_prompt.shape
    bs, ss, _ = x_sample.shape
    assert (bp * sp) % TOKEN_TILE == 0 and TOKEN_TILE % sp == 0 and ss % TOKEN_TILE == 0
    assert (bp * sp) % MERGE_TILE == 0 and ss % MERGE_TILE == 0
    assert bp % CTX_ATTN_BATCH == 0 and sp % BLOCK == 0 and ss % ATTN_TILE == 0 and ss // BLOCK >= 3
    assert GLA_TILE % sp == 0 and (bp * sp) % GLA_TILE == 0 and ss % GLA_TILE == 0 and sp % CHUNK == 0
    assert ss % GRID_W == 0 and cache_k.shape[2] % LANES == 0

    wt = w_in[l].T.astype(BF16)
    wgk = jnp.zeros((LANES, 2 * GLA_K_WIDTH), F32)
    wgk = wgk.at[0:GATE_RANK, 0:GLA_K_WIDTH].set(w_gk_f[l])
    wgk = wgk.at[GATE_RANK:2 * GATE_RANK, GLA_K_WIDTH:].set(w_gk_b[l]).astype(BF16)
    bgk = jnp.concatenate([b_gk_f[l], b_gk_b[l]])[None, :]
    wo = w_out[l].astype(BF16)
    ng = norm_g[l][None, :]
    gng = gla_norm_g[l][None, :]
    fng = final_norm_g[None, :]
    sk = sink[l]

    n_rows = SUBLANES
    assert 1 + bs <= n_rows
    cvecs = jnp.concatenate([c_ctx[None, :], c, jnp.zeros((n_rows - 1 - bs, D_MODEL), F32)], axis=0)
    mod3 = _modulation(cvecs, w_mod[l], b_mod[l][None, :]).reshape(n_rows, 1, 3 * D_MODEL)
    tm = TOKEN_TILE

    xp2 = x_prompt.reshape(bp * sp, D_MODEL)
    ctx_row = lambda i: 0
    q, kvp, za, vg, zg, qkg, gf, gb, kt, vt = _project(xp2, mod3, ctx_row, ng, wt, wgk, bgk, None, tm, sp)
    oa = _context_attention(q, kvp, sk, sp)
    of, ob, st = _gla(qkg, gf, gb, vg, None, bp, sp)
    y_prompt = _merge(xp2, mod3, ctx_row, oa, za, of, ob, zg, gng, wo, fng, MERGE_TILE).reshape(bp, sp, D_MODEL)
    to_cache = lambda t: t.reshape(bp, 1, N_KV_HEADS, HEAD_DIM, sp).transpose(0, 1, 4, 2, 3)
    new_k, new_v = to_cache(kt), to_cache(vt)
    new_state = st[:, None]

    xs2 = x_sample.reshape(bs * ss, D_MODEL)
    tiles_per_seq = ss // tm
    lat_row = lambda i: 1 + i // tiles_per_seq
    q, kvp, za, vg, zg, qkg, gf, gb = _project(xs2, mod3, lat_row, ng, wt, wgk, bgk, _rope_tables(ss), tm, ss)
    past = cache_k.shape[2]
    from_cache = lambda t: t[:, l].transpose(0, 2, 3, 1).reshape(bs, KV_WIDTH, past)
    ck, cv = from_cache(cache_k), from_cache(cache_v)
    oa = _latent_attention(q, kvp, ck, cv, sk, bs, ss)
    of, ob = _gla(qkg, gf, gb, vg, state_gla[:, l], bs, ss)
    merge_row = lambda i: 1 + i // (ss // MERGE_TILE)
    y_sample = _merge(xs2, mod3, merge_row, oa, za, of, ob, zg, gng, wo, fng, MERGE_TILE).reshape(bs, ss, D_MODEL)

    return (y_prompt, y_sample, new_k, new_v, new_state)
```

```python
import functools
import math

import numpy as np
import jax
import jax.numpy as jnp
from jax import lax
from jax.experimental import pallas as pl
from jax.experimental.pallas import tpu as pltpu

D_MODEL = 1024
GRID_W = 64
HEAD_DIM = 64
N_ATTN_HEADS = 8
N_KV_HEADS = 2
ATTN_WIDTH = 512
KV_WIDTH = 128
BLOCK = 128
N_GLA_HEADS = 4
GLA_DK = 64
GLA_DV = 128
GLA_K_WIDTH = 256
GLA_V_WIDTH = 512
GATE_RANK = 16
GATE_NORMALIZER = 16.0
CHUNK = 64
ROPE_BASE = 10000.0
EPS = 1e-6
NEG_INF = -1e30

KV_GROUP = N_ATTN_HEADS // N_KV_HEADS
LANES = 128
SUBLANES = 8
TOKEN_TILE = 1024
MERGE_TILE = 1024
GLA_TILE = 2048
ATTN_TILE = 2048
CTX_ATTN_BATCH = 8
PIPE_SLOTS = 4
GLA_PIPE_SLOTS = 4
KVP_WIDTH = 8 * LANES
N_PLACEMENTS = 2 * N_KV_HEADS
SOFTMAX_ROWS = 32
MOD_TILE = 1024
VMEM_LIMIT = 56 * 1024 * 1024
LOG2E = math.log2(math.e)

F32 = jnp.float32
BF16 = jnp.bfloat16


def _dot(a, b):
    return jnp.dot(a, b, preferred_element_type=F32)


def _dot_nt(a, b):
    return lax.dot_general(a, b, (((1,), (1,)), ((), ())), preferred_element_type=F32)


def _dot_tn(a, b):
    return lax.dot_general(a, b, (((0,), (0,)), ((), ())), preferred_element_type=F32)


def _silu(x):
    h = 0.5 * x
    return h + h * jnp.tanh(h)


def _log_sigmoid(x):
    return jnp.minimum(x, 0.0) - jnp.log1p(jnp.exp(-jnp.abs(x)))


def _params(*sem):
    return pltpu.CompilerParams(dimension_semantics=sem, vmem_limit_bytes=VMEM_LIMIT)


def _split_bf16(x):
    hi = x.astype(BF16)
    return hi, (x - hi.astype(F32)).astype(BF16)


def _mod_kernel(c_ref, w_ref, b_ref, o_ref):
    rows = c_ref.shape[0]
    s_hi, s_lo = _split_bf16(_silu(c_ref[...]))
    w_hi, w_lo = _split_bf16(w_ref[...])
    by_hi = _dot(jnp.concatenate([s_hi, s_lo], axis=0), w_hi)
    o_ref[...] = by_hi[0:rows] + by_hi[rows:2 * rows] + _dot(s_hi, w_lo) + b_ref[...]


def _modulation(cvecs, w_mod, b_mod):
    rows = cvecs.shape[0]
    n_out = w_mod.shape[1]
    return pl.pallas_call(
        _mod_kernel,
        grid=(n_out // MOD_TILE,),
        in_specs=[
            pl.BlockSpec((rows, D_MODEL), lambda j: (0, 0)),
            pl.BlockSpec((D_MODEL, MOD_TILE), lambda j: (0, j)),
            pl.BlockSpec((1, MOD_TILE), lambda j: (0, j)),
        ],
        out_specs=pl.BlockSpec((rows, MOD_TILE), lambda j: (0, j)),
        out_shape=jax.ShapeDtypeStruct((rows, n_out), F32),
        compiler_params=_params("arbitrary"),
        name="adaln_mod",
    )(cvecs, w_mod, b_mod)


_Q0, _K0, _V0, _ZA0, _QG0, _KG0, _VG0, _TAIL0 = 0, 512, 640, 768, 1280, 1536, 1792, 2304
_TAIL_COLS = 2 * GATE_RANK + GLA_V_WIDTH
_Q_PRESCALE = (HEAD_DIM ** -0.5) * LOG2E


def _rope(x, cos, sin_signed):
    n = x.shape[1] // LANES
    lane = lax.broadcasted_iota(jnp.int32, (x.shape[0], LANES), 1)
    first = (lane % 32) < 16
    outs = []
    for c in range(n):
        xc = x[:, c * LANES:(c + 1) * LANES]
        partner = jnp.where(first, pltpu.roll(xc, LANES - 16, 1), pltpu.roll(xc, 16, 1))
        outs.append(xc * cos + partner * sin_signed)
    return outs[0] if n == 1 else jnp.concatenate(outs, axis=-1)


def _kv_placements(k, v):
    lane = lax.broadcasted_iota(jnp.int32, k.shape, 1)
    lo = lane < HEAD_DIM
    kr = pltpu.roll(k, HEAD_DIM, 1)
    vr = pltpu.roll(v, HEAD_DIM, 1)
    pieces = [jnp.where(lo, k, 0.0), jnp.where(lo, 0.0, kr), jnp.where(lo, kr, 0.0), jnp.where(lo, 0.0, k),
              jnp.where(lo, v, 1.0), jnp.where(lo, 1.0, vr), jnp.where(lo, vr, 1.0), jnp.where(lo, 1.0, v)]
    return jnp.concatenate(pieces, axis=-1).astype(BF16)


def _proj_kernel(*refs, rope):
    if rope:
        (x_ref, mod_ref, ng_ref, wt_ref, wgk_ref, bgk_ref, cos_ref, sin_ref,
         q_ref, kvp_ref, za_ref, vg_ref, zg_ref, qkg_ref, gf_ref, gb_ref) = refs
    else:
        (x_ref, mod_ref, ng_ref, wt_ref, wgk_ref, bgk_ref,
         q_ref, kvp_ref, za_ref, vg_ref, zg_ref, qkg_ref, gf_ref, gb_ref, k_ref, v_ref) = refs
    x = x_ref[...]
    xn = x * lax.rsqrt(jnp.mean(x * x, axis=-1, keepdims=True) + EPS)
    mod = mod_ref[0]
    shift = mod[:, 0:D_MODEL]
    scale = mod[:, D_MODEL:2 * D_MODEL]
    h = ((xn * ng_ref[...]) * (1.0 + scale) + shift).astype(BF16)

    def piece(c0, c1):
        return _dot_nt(h, wt_ref[c0:c1, :])

    q = piece(_Q0, _K0)
    kv = piece(_K0, _ZA0)
    k = kv[:, 0:KV_WIDTH]
    v = kv[:, KV_WIDTH:2 * KV_WIDTH]
    if rope:
        cos = cos_ref[...]
        sin = sin_ref[...]
        q = _rope(q, cos, sin)
        k = _rope(k, cos, sin)
    else:
        seq = k_ref.shape[2]
        for b in range(k_ref.shape[0]):
            k_ref[b] = k[b * seq:(b + 1) * seq, :].T
            v_ref[b] = v[b * seq:(b + 1) * seq, :].T
    _store_stacked((q * _Q_PRESCALE).astype(BF16), q_ref)
    kvp_ref[...] = _kv_placements(k, v)
    za_ref[...] = piece(_ZA0, _QG0).astype(BF16)
    vg_ref[...] = piece(_VG0, _TAIL0).astype(BF16)
    qkg_ref[...] = piece(_QG0, _VG0)
    tail = piece(_TAIL0, _TAIL0 + _TAIL_COLS)
    zg_ref[...] = tail[:, 2 * GATE_RANK:_TAIL_COLS].astype(BF16)
    lane = lax.broadcasted_iota(jnp.int32, (tail.shape[0], LANES), 1)
    lr = jnp.where(lane < 2 * GATE_RANK, tail[:, 0:LANES], 0.0).astype(BF16)
    g = _log_sigmoid(_dot(lr, wgk_ref[...]) + bgk_ref[...]) * (1.0 / GATE_NORMALIZER)
    gf_ref[...] = g[:, 0:GLA_K_WIDTH]
    gb_ref[...] = g[:, GLA_K_WIDTH:2 * GLA_K_WIDTH]


def _project(x2d, mod3, mod_row_of_tile, norm_g, wt, wgk, bgk, rope_tabs, tm, seq):
    n_tok = x2d.shape[0]
    rope = rope_tabs is not None
    row = lambda i: (i, 0)
    const = lambda i: (0, 0)
    in_specs = [
        pl.BlockSpec((tm, D_MODEL), row),
        pl.BlockSpec((1, 1, 3 * D_MODEL), lambda i: (mod_row_of_tile(i), 0, 0)),
        pl.BlockSpec((1, D_MODEL), const),
        pl.BlockSpec((_TAIL0 + _TAIL_COLS, D_MODEL), const),
        pl.BlockSpec((LANES, 2 * GLA_K_WIDTH), const),
        pl.BlockSpec((1, 2 * GLA_K_WIDTH), const),
    ]
    args = [x2d, mod3, norm_g, wt, wgk, bgk]
    if rope:
        tiles_per_seq = rope_tabs[0].shape[0] // tm
        pos = lambda i: (i % tiles_per_seq, 0)
        in_specs += [pl.BlockSpec((tm, LANES), pos), pl.BlockSpec((tm, LANES), pos)]
        args += list(rope_tabs)
    widths = [(KVP_WIDTH, BF16), (ATTN_WIDTH, BF16), (GLA_V_WIDTH, BF16), (GLA_V_WIDTH, BF16),
              (2 * GLA_K_WIDTH, F32), (GLA_K_WIDTH, F32), (GLA_K_WIDTH, F32)]
    out_specs = [pl.BlockSpec(_stacked_shape(tm), lambda i: (i, 0, 0))] + [pl.BlockSpec((tm, w), row) for w, _ in widths]
    out_shape = ([jax.ShapeDtypeStruct(_stacked_shape(n_tok), BF16)]
                 + [jax.ShapeDtypeStruct((n_tok, w), dt) for w, dt in widths])
    if not rope:
        for _ in range(2):
            out_specs.append(pl.BlockSpec((tm // seq, KV_WIDTH, seq), lambda i: (i, 0, 0)))
            out_shape.append(jax.ShapeDtypeStruct((n_tok // seq, KV_WIDTH, seq), F32))
    return pl.pallas_call(
        functools.partial(_proj_kernel, rope=rope),
        grid=(n_tok // tm,),
        in_specs=in_specs,
        out_specs=out_specs,
        out_shape=out_shape,
        compiler_params=_params("arbitrary"),
        name="project_rope" if rope else "project",
    )(*args)


def _software_pipeline(n_groups, n_slots, stages):
    last_g = n_groups - 1

    def body(k, carry):
        for age in (2, 1, 0):
            for u in range(n_slots):
                g = n_slots * (k - age) + u
                stages[age](jnp.clip(g, 0, last_g), u, jnp.logical_and(g >= 0, g <= last_g))
        return carry

    lax.fori_loop(0, n_groups // n_slots + 2, body, 0)


def _init_pipeline_scratch(first, zero_refs, one_refs=()):
    @pl.when(first)
    def _():
        for ref in zero_refs:
            ref[...] = jnp.zeros_like(ref)
        for ref in one_refs:
            ref[...] = jnp.ones_like(ref)


def _key_rows(start, size):
    return pl.ds(pl.multiple_of(start, BLOCK), size)


def _softmax_rows(s_buf, p_buf, t_buf, slot, n_rows, n_keys, masks_of, sink_of):
    for r0 in range(0, n_rows, SOFTMAX_ROWS):
        rs = slice(r0, r0 + SOFTMAX_ROWS)
        masks = masks_of(r0)
        cols = []
        for c in range(n_keys // LANES):
            sc = s_buf[slot, rs, c * LANES:(c + 1) * LANES]
            cols.append(jnp.where(masks[c], sc, NEG_INF) if c in masks else sc)
        mx = cols[0]
        for sc in cols[1:]:
            mx = jnp.maximum(mx, sc)
        sink = sink_of(r0)
        m = jnp.maximum(jnp.max(mx, axis=-1, keepdims=True), sink)
        for c, sc in enumerate(cols):
            p_buf[slot, rs, c * LANES:(c + 1) * LANES] = jnp.exp2(sc - m).astype(BF16)
        t_buf[slot, rs, :] = jnp.broadcast_to(jnp.exp2(sink - m), (SOFTMAX_ROWS, LANES))


def _normalize(acc, t_buf, slot):
    return acc / (pltpu.roll(acc, HEAD_DIM, 1) + t_buf[slot])


def _attn_scratch(n_keys):
    rows = 2 * BLOCK
    return [pltpu.VMEM((PIPE_SLOTS, rows, n_keys), F32), pltpu.VMEM((PIPE_SLOTS, rows, n_keys), BF16),
            pltpu.VMEM((PIPE_SLOTS, rows, LANES), F32)]


def _stacked_shape(n_tok):
    return (n_tok // BLOCK * N_KV_HEADS, 2 * BLOCK, LANES)


def _store_stacked(x, ref):
    for blk in range(x.shape[0] // BLOCK):
        for j in range(N_KV_HEADS):
            for e in range(2):
                ref[blk * N_KV_HEADS + j, e * BLOCK:(e + 1) * BLOCK, :] = (
                    x[blk * BLOCK:(blk + 1) * BLOCK, (2 * j + e) * LANES:(2 * j + e + 1) * LANES])


def _load_stacked(ref):
    blocks = []
    for blk in range(ref.shape[0] // N_KV_HEADS):
        blocks.append(jnp.concatenate([ref[blk * N_KV_HEADS + j, e * BLOCK:(e + 1) * BLOCK, :]
                                       for j in range(N_KV_HEADS) for e in range(2)], axis=-1))
    return jnp.concatenate(blocks, axis=0)


assert PIPE_SLOTS % N_PLACEMENTS == 0


def _store_group(o_ref, g, u, o):
    half = slice((u % 2) * HEAD_DIM, (u % 2 + 1) * HEAD_DIM)
    o_ref[g // 2, :, half] = o[:, half].astype(o_ref.dtype)


def _group_sinks(sink_ref, g):
    j, e = (g // 2) % N_KV_HEADS, g % 2
    sink_a = sink_ref[KV_GROUP * j + e] * LOG2E
    sink_b = sink_ref[KV_GROUP * j + 2 + e] * LOG2E
    return lambda r0: sink_a if r0 < BLOCK else sink_b


def _ctx_attn_kernel(sink_ref, q_ref, kvp_ref, o_ref, s_buf, p_buf, t_buf, *, seq):
    n_blocks = q_ref.shape[0] // N_KV_HEADS
    _init_pipeline_scratch(pl.program_id(0) == 0, (s_buf, p_buf), (t_buf,))
    blocks_per_seq = seq // BLOCK

    def keys_of(g):
        return _key_rows(((g // N_PLACEMENTS) // blocks_per_seq) * seq, seq)

    def placement(u, is_v):
        idx = is_v * N_PLACEMENTS + u % N_PLACEMENTS
        return slice(idx * LANES, (idx + 1) * LANES)

    def qk(g, u, valid):
        s_buf[u] = _dot_nt(q_ref[g // 2], kvp_ref[keys_of(g), placement(u, 0)])

    def softmax(g, u, valid):
        _softmax_rows(s_buf, p_buf, t_buf, u, 2 * BLOCK, seq, lambda r0: {}, _group_sinks(sink_ref, g))

    def pv(g, u, valid):
        _store_group(o_ref, g, u, _normalize(_dot(p_buf[u], kvp_ref[keys_of(g), placement(u, 1)]), t_buf, u))

    _software_pipeline(n_blocks * N_PLACEMENTS, PIPE_SLOTS, (qk, softmax, pv))


def _context_attention(q, kvp, sink, seq):
    n_tok = kvp.shape[0]
    rows = CTX_ATTN_BATCH * seq
    row = lambda b: (b, 0)
    stacked = pl.BlockSpec(_stacked_shape(rows), lambda b: (b, 0, 0))
    return pl.pallas_call(
        functools.partial(_ctx_attn_kernel, seq=seq),
        grid=(n_tok // rows,),
        in_specs=[
            pl.BlockSpec(memory_space=pltpu.SMEM),
            stacked,
            pl.BlockSpec((rows, KVP_WIDTH), row),
        ],
        out_specs=stacked,
        out_shape=jax.ShapeDtypeStruct(_stacked_shape(n_tok), BF16),
        scratch_shapes=_attn_scratch(seq),
        compiler_params=_params("arbitrary"),
        name="context_attention",
    )(sink, q, kvp)


def _lat_attn_kernel(sink_ref, q_ref, kvp_prev, kvp_own, kvp_next, ck_ref, cv_ref, o_ref,
                     kwin, vwin, kctx, vctx, s_buf, p_buf, t_buf, *, n_steps, past):
    i = pl.program_id(1)
    n_blocks = q_ref.shape[0] // N_KV_HEADS
    own_end = BLOCK + kvp_own.shape[0]
    win = 3 * BLOCK
    _init_pipeline_scratch(jnp.logical_and(pl.program_id(0) == 0, i == 0), (s_buf, p_buf), (t_buf,))

    @pl.when(i == 0)
    def _():
        cpl = _kv_placements(ck_ref[0].T, cv_ref[0].T)
        for idx in range(N_PLACEMENTS):
            kctx[idx] = cpl[:, idx * LANES:(idx + 1) * LANES]
            vctx[idx] = cpl[:, (N_PLACEMENTS + idx) * LANES:(N_PLACEMENTS + idx + 1) * LANES]

    for src, r0, r1 in ((kvp_prev, 0, BLOCK), (kvp_own, BLOCK, own_end), (kvp_next, own_end, own_end + BLOCK)):
        for idx in range(N_PLACEMENTS):
            kwin[idx, r0:r1, :] = src[:, idx * LANES:(idx + 1) * LANES]
            vwin[idx, r0:r1, :] = src[:, (N_PLACEMENTS + idx) * LANES:(N_PLACEMENTS + idx + 1) * LANES]

    row = lax.broadcasted_iota(jnp.int32, (SOFTMAX_ROWS, LANES), 0)
    c = lax.broadcasted_iota(jnp.int32, (SOFTMAX_ROWS, LANES), 1)
    off_prev = jnp.where(i > 0, 0, BLOCK)
    off_next = jnp.where(i < n_steps - 1, 0, BLOCK)

    def window_of(g):
        return _key_rows((g // N_PLACEMENTS) * BLOCK, win)

    def qk(g, u, valid):
        q2 = q_ref[g // 2]
        s_buf[u, :, 0:win] = _dot_nt(q2, kwin[g % N_PLACEMENTS, window_of(g), :])
        s_buf[u, :, win:win + past] = _dot_nt(q2, kctx[g % N_PLACEMENTS])

    def softmax(g, u, valid):
        blk = g // N_PLACEMENTS
        off0 = jnp.where(blk == 0, off_prev, 0)
        off2 = jnp.where(blk == n_blocks - 1, off_next, 0)

        def masks_of(r0):
            r = row + r0 % BLOCK
            return {0: c >= r + off0, 2: c <= r - off2}

        _softmax_rows(s_buf, p_buf, t_buf, u, 2 * BLOCK, win + past, masks_of, _group_sinks(sink_ref, g))

    def pv(g, u, valid):
        acc = _dot(p_buf[u, :, 0:win], vwin[g % N_PLACEMENTS, window_of(g), :])
        acc = acc + _dot(p_buf[u, :, win:win + past], vctx[g % N_PLACEMENTS])
        _store_group(o_ref, g, u, _normalize(acc, t_buf, u))

    _software_pipeline(n_blocks * N_PLACEMENTS, PIPE_SLOTS, (qk, softmax, pv))


def _latent_attention(q, kvp, ck, cv, sink, n_batch, seq):
    ns = seq // ATTN_TILE
    nb = seq // BLOCK
    per = ATTN_TILE // BLOCK
    own = lambda b, i: (b * ns + i, 0)
    prev = lambda b, i: (b * nb + jnp.maximum(per * i - 1, 0), 0)
    nxt = lambda b, i: (b * nb + jnp.minimum(per * i + per, nb - 1), 0)
    ctx = lambda b, i: (b, 0, 0)
    past = ck.shape[2]
    win_rows = ATTN_TILE + 2 * BLOCK
    bf = lambda *shape: pltpu.VMEM(shape, BF16)
    stacked = pl.BlockSpec(_stacked_shape(ATTN_TILE), lambda b, i: (b * ns + i, 0, 0))
    return pl.pallas_call(
        functools.partial(_lat_attn_kernel, n_steps=ns, past=past),
        grid=(n_batch, ns),
        in_specs=[
            pl.BlockSpec(memory_space=pltpu.SMEM),
            stacked,
            pl.BlockSpec((BLOCK, KVP_WIDTH), prev),
            pl.BlockSpec((ATTN_TILE, KVP_WIDTH), own),
            pl.BlockSpec((BLOCK, KVP_WIDTH), nxt),
            pl.BlockSpec((1, KV_WIDTH, past), ctx),
            pl.BlockSpec((1, KV_WIDTH, past), ctx),
        ],
        out_specs=stacked,
        out_shape=jax.ShapeDtypeStruct(_stacked_shape(n_batch * seq), BF16),
        scratch_shapes=[bf(N_PLACEMENTS, win_rows, LANES), bf(N_PLACEMENTS, win_rows, LANES),
                        bf(N_PLACEMENTS, past, LANES), bf(N_PLACEMENTS, past, LANES)]
        + _attn_scratch(3 * BLOCK + past),
        compiler_params=_params("arbitrary", "arbitrary"),
        name="latent_attention",
    )(sink, q, kvp, kvp, kvp, ck, cv)


_PK = 2 * GLA_DK
_PV = 2 * GLA_DV


def _cumsum_rows(x, reverse):
    n, width = x.shape
    row = lax.broadcasted_iota(jnp.int32, (SUBLANES, width), 0)
    blocks = [x[j:j + SUBLANES, :] for j in range(0, n, SUBLANES)]
    s = 1
    while s < SUBLANES:
        if reverse:
            blocks = [b + jnp.where(row < SUBLANES - s, pltpu.roll(b, SUBLANES - s, 0), 0.0) for b in blocks]
        else:
            blocks = [b + jnp.where(row >= s, pltpu.roll(b, s, 0), 0.0) for b in blocks]
        s *= 2
    edge = 0 if reverse else SUBLANES - 1
    order = range(len(blocks) - 1, -1, -1) if reverse else range(len(blocks))
    carry = None
    for j in order:
        if carry is not None:
            blocks[j] = blocks[j] + carry
        carry = jnp.broadcast_to(blocks[j][edge:edge + 1, :], (SUBLANES, width))
    return jnp.concatenate(blocks, axis=0)


def _gla_kernel(*refs, seq_chunks, has_init, emit_state):
    it = iter(refs)
    views = [tuple(next(it) for _ in range(3)) for _ in range(2)]
    s0_ref = next(it) if has_init else None
    out_refs = (next(it), next(it))
    sout_ref = next(it) if emit_state else None
    st_ref, a_buf, a2_buf, qe_buf, qd_buf, ds_buf, sb_buf, dc_buf = (next(it) for _ in range(8))
    n_chunks = views[0][0].shape[0] // CHUNK
    n_groups = 2 * n_chunks
    n_pairs = N_GLA_HEADS // 2
    tile = pl.program_id(1)
    q_scale = GLA_DK ** -0.5
    _init_pipeline_scratch(jnp.logical_and(pl.program_id(0) == 0, tile == 0),
                           (st_ref, a_buf, a2_buf, qe_buf, qd_buf, ds_buf, sb_buf), (dc_buf,))

    if has_init:
        @pl.when(tile == 0)
        def _():
            st_ref[...] = jnp.zeros_like(st_ref)
            for d in range(2):
                for h in range(N_GLA_HEADS):
                    p, e = divmod(h, 2)
                    st_ref[d, p, e * GLA_DK:(e + 1) * GLA_DK, e * GLA_DV:(e + 1) * GLA_DV] = s0_ref[0, d, h]

    lane_k = lax.broadcasted_iota(jnp.int32, (CHUNK, _PK), 1)
    row_k = lax.broadcasted_iota(jnp.int32, (CHUNK, _PK), 0)
    lo_k = lane_k < GLA_DK
    tri = (lane_k % CHUNK <= row_k, lane_k % CHUNK >= row_k)
    lo_v = lax.broadcasted_iota(jnp.int32, (CHUNK, _PV), 1) < GLA_DV
    diag = ((lax.broadcasted_iota(jnp.int32, (_PK, _PV), 0) < GLA_DK)
            == (lax.broadcasted_iota(jnp.int32, (_PK, _PV), 1) < GLA_DV))

    def place(g, d):
        c = g // 2
        if d == 0:
            return c, c
        seq = c // seq_chunks
        return c, seq * seq_chunks + (seq_chunks - 1 - (c - seq * seq_chunks))

    def rows_of(pos):
        return pl.ds(pl.multiple_of(pos * CHUNK, CHUNK), CHUNK)

    def prep(g, u, valid):
        d = u % 2
        qk_ref, g_ref, v_ref = views[d]
        _, pos = place(g, d)
        rows = rows_of(pos)
        gc = _cumsum_rows(g_ref[rows, :], reverse=(d == 1))
        gtot = gc[0:1, :] if d == 1 else gc[CHUNK - 1:CHUNK, :]
        k = qk_ref[rows, GLA_K_WIDTH:2 * GLA_K_WIDTH]
        qe = qk_ref[rows, 0:GLA_K_WIDTH] * jnp.exp(gc)
        qe_buf[u] = qe
        q_dec = (qe * q_scale).astype(BF16)
        k_inv = k * jnp.exp(-gc)
        e_tot = jnp.exp(gtot)
        k_end = (k_inv * jnp.where(valid, e_tot, 0.0)).astype(BF16)
        decay = jnp.where(valid, e_tot, 1.0)
        for p in range(n_pairs):
            lanes = slice(p * _PK, (p + 1) * _PK)
            ki = k_inv[:, lanes]
            kst = jnp.concatenate([jnp.where(lo_k, ki, 0.0), jnp.where(lo_k, 0.0, ki)], axis=0).astype(BF16)
            a_buf[u, p] = _dot_nt(q_dec[:, lanes], kst)
            ds_buf[u, p] = jnp.where(diag, _dot_tn(k_end[:, lanes], v_ref[rows, p * _PV:(p + 1) * _PV]), 0.0)
            dc_buf[u, p] = jnp.broadcast_to(decay[:, lanes], (_PK, _PK)).T

    def state(g, u, valid):
        d = u % 2
        c, _ = place(g, d)
        qd_buf[u] = (qe_buf[u] * q_scale).astype(BF16)
        for p in range(n_pairs):
            fresh = (c % seq_chunks == 0) if emit_state else (c < 0)
            s_in = jnp.where(fresh, 0.0, st_ref[d, p])
            sb_buf[u, p] = s_in.astype(BF16)
            a2_buf[u, p] = jnp.where(tri[d], a_buf[u, p], 0.0).astype(BF16)
            dc = dc_buf[u, p]
            s_new = s_in * jnp.concatenate([dc, dc], axis=1) + ds_buf[u, p]
            st_ref[d, p] = s_new
            if emit_state:
                for e in range(2):
                    sout_ref[c // seq_chunks, d, 2 * p + e] = s_new[e * GLA_DK:(e + 1) * GLA_DK,
                                                                    e * GLA_DV:(e + 1) * GLA_DV]

    def out(g, u, valid):
        d = u % 2
        _, _, v_ref = views[d]
        _, pos = place(g, d)
        rows = rows_of(pos)
        for p in range(n_pairs):
            vpf = v_ref[rows, p * _PV:(p + 1) * _PV].astype(F32)
            vbd = jnp.concatenate([jnp.where(lo_v, vpf, 0.0), jnp.where(lo_v, 0.0, vpf)], axis=0).astype(BF16)
            x = jnp.concatenate([a2_buf[u, p], qd_buf[u, :, p * _PK:(p + 1) * _PK]], axis=1)
            w = jnp.concatenate([vbd, sb_buf[u, p]], axis=0)
            out_refs[d][rows, p * _PV:(p + 1) * _PV] = _dot(x, w).astype(out_refs[d].dtype)

    _software_pipeline(n_groups, GLA_PIPE_SLOTS, (prep, state, out))


def _gla(qkg, gf, gb, vg, s0, n_batch, seq):
    tm = GLA_TILE
    long_seq = s0 is not None
    if long_seq:
        nt = seq // tm
        grid = (n_batch, nt)
        fwd = lambda b, i: (b * nt + i, 0)
        bwd = lambda b, i: (b * nt + nt - 1 - i, 0)
        seq_chunks = tm // CHUNK
    else:
        per_step = tm // seq
        grid = (1, n_batch // per_step)
        fwd = bwd = lambda b, i: (i, 0)
        seq_chunks = seq // CHUNK
    in_specs = []
    args = []
    for m, gates in ((fwd, gf), (bwd, gb)):
        in_specs += [pl.BlockSpec((tm, 2 * GLA_K_WIDTH), m), pl.BlockSpec((tm, GLA_K_WIDTH), m),
                     pl.BlockSpec((tm, GLA_V_WIDTH), m)]
        args += [qkg, gates, vg]
    n_tok = n_batch * seq
    out_specs = [pl.BlockSpec((tm, GLA_V_WIDTH), fwd), pl.BlockSpec((tm, GLA_V_WIDTH), bwd)]
    out_shape = [jax.ShapeDtypeStruct((n_tok, GLA_V_WIDTH), BF16)] * 2
    st_dims = (2, N_GLA_HEADS, GLA_DK, GLA_DV)
    if long_seq:
        in_specs.append(pl.BlockSpec((1,) + st_dims, lambda b, i: (b, 0, 0, 0, 0)))
        args.append(s0)
    else:
        out_specs.append(pl.BlockSpec((per_step,) + st_dims, lambda b, i: (i, 0, 0, 0, 0)))
        out_shape.append(jax.ShapeDtypeStruct((n_batch,) + st_dims, F32))
    n = GLA_PIPE_SLOTS
    n_pairs = N_GLA_HEADS // 2
    return pl.pallas_call(
        functools.partial(_gla_kernel, seq_chunks=seq_chunks, has_init=long_seq, emit_state=not long_seq),
        grid=grid,
        in_specs=in_specs,
        out_specs=out_specs,
        out_shape=out_shape,
        scratch_shapes=[pltpu.VMEM((2, n_pairs, _PK, _PV), F32),
                        pltpu.VMEM((n, n_pairs, CHUNK, _PK), F32),
                        pltpu.VMEM((n, n_pairs, CHUNK, _PK), BF16),
                        pltpu.VMEM((n, CHUNK, GLA_K_WIDTH), F32),
                        pltpu.VMEM((n, CHUNK, GLA_K_WIDTH), BF16),
                        pltpu.VMEM((n, n_pairs, _PK, _PV), F32),
                        pltpu.VMEM((n, n_pairs, _PK, _PV), BF16),
                        pltpu.VMEM((n, n_pairs, _PK, _PK), F32)],
        compiler_params=_params("arbitrary", "arbitrary"),
        name="gla_long" if long_seq else "gla_short",
    )(*args)


def _merge_kernel(x_ref, mod_ref, oa_ref, za_ref, of_ref, ob_ref, zg_ref, gng_ref, wo_ref, fng_ref, y_ref):
    ya = _load_stacked(oa_ref).astype(F32) * _silu(za_ref[...].astype(F32))
    og = of_ref[...].astype(F32) + ob_ref[...].astype(F32)
    zg = zg_ref[...].astype(F32)
    gng = gng_ref[...]
    parts = [ya]
    for h in range(N_GLA_HEADS):
        oh = og[:, h * GLA_DV:(h + 1) * GLA_DV]
        nh = oh * lax.rsqrt(jnp.mean(oh * oh, axis=-1, keepdims=True) + EPS)
        parts.append((nh * gng) * _silu(zg[:, h * GLA_DV:(h + 1) * GLA_DV]))
    yin = jnp.concatenate(parts, axis=-1).astype(BF16)
    gate = mod_ref[0][:, 2 * D_MODEL:3 * D_MODEL]
    y = x_ref[...] + gate * _dot(yin, wo_ref[...])
    y_ref[...] = (y * lax.rsqrt(jnp.mean(y * y, axis=-1, keepdims=True) + EPS)) * fng_ref[...]


def _merge(x2d, mod3, mod_row_of_tile, oa, za, of, ob, zg, gla_norm_g, w_out, final_norm_g, tm):
    n_tok = x2d.shape[0]
    row = lambda i: (i, 0)
    const = lambda i: (0, 0)
    wide = pl.BlockSpec((tm, GLA_V_WIDTH), row)
    return pl.pallas_call(
        _merge_kernel,
        grid=(n_tok // tm,),
        in_specs=[
            pl.BlockSpec((tm, D_MODEL), row),
            pl.BlockSpec((1, 1, 3 * D_MODEL), lambda i: (mod_row_of_tile(i), 0, 0)),
            pl.BlockSpec(_stacked_shape(tm), lambda i: (i, 0, 0)), wide, wide, wide, wide,
            pl.BlockSpec((1, GLA_DV), const),
            pl.BlockSpec((D_MODEL, D_MODEL), const),
            pl.BlockSpec((1, D_MODEL), const),
        ],
        out_specs=pl.BlockSpec((tm, D_MODEL), row),
        out_shape=jax.ShapeDtypeStruct((n_tok, D_MODEL), F32),
        compiler_params=_params("arbitrary"),
        name="merge",
    )(x2d, mod3, oa, za, of, ob, zg, gla_norm_g, w_out, final_norm_g)


def _rope_tables(n_tok):
    t = np.arange(n_tok)
    rowp = (t // GRID_W).astype(np.float64)
    colp = (t % GRID_W).astype(np.float64)
    nf = HEAD_DIM // 4
    freqs = ROPE_BASE ** (-np.arange(nf, dtype=np.float64) / nf)
    lane = np.arange(LANES)
    pos = np.where(((lane % HEAD_DIM) < HEAD_DIM // 2)[None, :], rowp[:, None], colp[:, None])
    ang = pos * freqs[lane % nf][None, :]
    sign = np.where((lane % (2 * nf)) < nf, -1.0, 1.0)
    return jnp.asarray(np.cos(ang), F32), jnp.asarray(np.sin(ang) * sign[None, :], F32)


def kernel(x_prompt, x_sample, c, cache_k, cache_v, state_gla, c_ctx, w_mod, b_mod, norm_g, w_in,
           w_gk_f, b_gk_f, w_gk_b, b_gk_b, sink, gla_norm_g, w_out, final_norm_g):
    depth = w_in.shape[0]
    assert depth == 1, "single-layer step"
    l = 0
    bp, sp, _ = x_prompt.shape
    bs, ss, _ = x_sample.shape
    assert (bp * sp) % TOKEN_TILE == 0 and TOKEN_TILE % sp == 0 and ss % TOKEN_TILE == 0
    assert (bp * sp) % MERGE_TILE == 0 and ss % MERGE_TILE == 0
    assert bp % CTX_ATTN_BATCH == 0 and sp % BLOCK == 0 and ss % ATTN_TILE == 0 and ss // BLOCK >= 3
    assert GLA_TILE % sp == 0 and (bp * sp) % GLA_TILE == 0 and ss % GLA_TILE == 0 and sp % CHUNK == 0
    assert ss % GRID_W == 0 and cache_k.shape[2] % LANES == 0

    wt = w_in[l].T.astype(BF16)
    wgk = jnp.zeros((LANES, 2 * GLA_K_WIDTH), F32)
    wgk = wgk.at[0:GATE_RANK, 0:GLA_K_WIDTH].set(w_gk_f[l])
    wgk = wgk.at[GATE_RANK:2 * GATE_RANK, GLA_K_WIDTH:].set(w_gk_b[l]).astype(BF16)
    bgk = jnp.concatenate([b_gk_f[l], b_gk_b[l]])[None, :]
    wo = w_out[l].astype(BF16)
    ng = norm_g[l][None, :]
    gng = gla_norm_g[l][None, :]
    fng = final_norm_g[None, :]
    sk = sink[l]

    n_rows = SUBLANES
    assert 1 + bs <= n_rows
    cvecs = jnp.concatenate([c_ctx[None, :], c, jnp.zeros((n_rows - 1 - bs, D_MODEL), F32)], axis=0)
    mod3 = _modulation(cvecs, w_mod[l], b_mod[l][None, :]).reshape(n_rows, 1, 3 * D_MODEL)
    tm = TOKEN_TILE

    xp2 = x_prompt.reshape(bp * sp, D_MODEL)
    ctx_row = lambda i: 0
    q, kvp, za, vg, zg, qkg, gf, gb, kt, vt = _project(xp2, mod3, ctx_row, ng, wt, wgk, bgk, None, tm, sp)
    oa = _context_attention(q, kvp, sk, sp)
    of, ob, st = _gla(qkg, gf, gb, vg, None, bp, sp)
    y_prompt = _merge(xp2, mod3, ctx_row, oa, za, of, ob, zg, gng, wo, fng, MERGE_TILE).reshape(bp, sp, D_MODEL)
    to_cache = lambda t: t.reshape(bp, 1, N_KV_HEADS, HEAD_DIM, sp).transpose(0, 1, 4, 2, 3)
    new_k, new_v = to_cache(kt), to_cache(vt)
    new_state = st[:, None]

    xs2 = x_sample.reshape(bs * ss, D_MODEL)
    tiles_per_seq = ss // tm
    lat_row = lambda i: 1 + i // tiles_per_seq
    q, kvp, za, vg, zg, qkg, gf, gb = _project(xs2, mod3, lat_row, ng, wt, wgk, bgk, _rope_tables(ss), tm, ss)
    past = cache_k.shape[2]
    from_cache = lambda t: t[:, l].transpose(0, 2, 3, 1).reshape(bs, KV_WIDTH, past)
    ck, cv = from_cache(cache_k), from_cache(cache_v)
    oa = _latent_attention(q, kvp, ck, cv, sk, bs, ss)
    of, ob = _gla(qkg, gf, gb, vg, state_gla[:, l], bs, ss)
    merge_row = lambda i: 1 + i // (ss // MERGE_TILE)
    y_sample = _merge(xs2, mod3, merge_row, oa, za, of, ob, zg, gng, wo, fng, MERGE_TILE).reshape(bs, ss, D_MODEL)

    return (y_prompt, y_sample, new_k, new_v, new_state)
```

```python
import functools
import math

import numpy as np
import jax
import jax.numpy as jnp
from jax import lax
from jax.experimental import pallas as pl
from jax.experimental.pallas import tpu as pltpu

D_MODEL = 1024
GRID_W = 64
HEAD_DIM = 64
N_ATTN_HEADS = 8
N_KV_HEADS = 2
ATTN_WIDTH = 512
KV_WIDTH = 128
BLOCK = 128
N_GLA_HEADS = 4
GLA_DK = 64
GLA_DV = 128
GLA_K_WIDTH = 256
GLA_V_WIDTH = 512
GATE_RANK = 16
GATE_NORMALIZER = 16.0
CHUNK = 64
ROPE_BASE = 10000.0
EPS = 1e-6
NEG_INF = -1e30

KV_GROUP = N_ATTN_HEADS // N_KV_HEADS
LANES = 128
SUBLANES = 8
TOKEN_TILE = 1024
MERGE_TILE = 1024
GLA_TILE = 2048
CTX_ATTN_BATCH = 8
PIPE_SLOTS = 4
GLA_PIPE_SLOTS = 4
KVP_WIDTH = 8 * LANES
N_PLACEMENTS = 2 * N_KV_HEADS
SOFTMAX_ROWS = 32
MOD_TILE = 1024
VMEM_LIMIT = 56 * 1024 * 1024
LOG2E = math.log2(math.e)

F32 = jnp.float32
BF16 = jnp.bfloat16


def _dot(a, b):
    return jnp.dot(a, b, preferred_element_type=F32)


def _dot_nt(a, b):
    return lax.dot_general(a, b, (((1,), (1,)), ((), ())), preferred_element_type=F32)


def _dot_tn(a, b):
    return lax.dot_general(a, b, (((0,), (0,)), ((), ())), preferred_element_type=F32)


def _silu(x):
    h = 0.5 * x
    return h + h * jnp.tanh(h)


def _log_sigmoid(x):
    return jnp.minimum(x, 0.0) - jnp.log1p(jnp.exp(-jnp.abs(x)))


def _params(*sem):
    return pltpu.CompilerParams(dimension_semantics=sem, vmem_limit_bytes=VMEM_LIMIT)


def _split_bf16(x):
    hi = x.astype(BF16)
    return hi, (x - hi.astype(F32)).astype(BF16)


def _mod_kernel(c_ref, w_ref, b_ref, o_ref):
    rows = c_ref.shape[0]
    s_hi, s_lo = _split_bf16(_silu(c_ref[...]))
    w_hi, w_lo = _split_bf16(w_ref[...])
    by_hi = _dot(jnp.concatenate([s_hi, s_lo], axis=0), w_hi)
    o_ref[...] = by_hi[0:rows] + by_hi[rows:2 * rows] + _dot(s_hi, w_lo) + b_ref[...]


def _modulation(cvecs, w_mod, b_mod):
    rows = cvecs.shape[0]
    n_out = w_mod.shape[1]
    return pl.pallas_call(
        _mod_kernel,
        grid=(n_out // MOD_TILE,),
        in_specs=[
            pl.BlockSpec((rows, D_MODEL), lambda j: (0, 0)),
            pl.BlockSpec((D_MODEL, MOD_TILE), lambda j: (0, j)),
            pl.BlockSpec((1, MOD_TILE), lambda j: (0, j)),
        ],
        out_specs=pl.BlockSpec((rows, MOD_TILE), lambda j: (0, j)),
        out_shape=jax.ShapeDtypeStruct((rows, n_out), F32),
        compiler_params=_params("arbitrary"),
        name="adaln_mod",
    )(cvecs, w_mod, b_mod)


_Q0, _K0, _V0, _ZA0, _QG0, _KG0, _VG0, _TAIL0 = 0, 512, 640, 768, 1280, 1536, 1792, 2304
_TAIL_COLS = 2 * GATE_RANK + GLA_V_WIDTH
_Q_PRESCALE = (HEAD_DIM ** -0.5) * LOG2E


def _rope(x, cos, sin_signed):
    n = x.shape[1] // LANES
    lane = lax.broadcasted_iota(jnp.int32, (x.shape[0], LANES), 1)
    first = (lane % 32) < 16
    outs = []
    for c in range(n):
        xc = x[:, c * LANES:(c + 1) * LANES]
        partner = jnp.where(first, pltpu.roll(xc, LANES - 16, 1), pltpu.roll(xc, 16, 1))
        outs.append(xc * cos + partner * sin_signed)
    return outs[0] if n == 1 else jnp.concatenate(outs, axis=-1)


def _kv_placements(k, v):
    lane = lax.broadcasted_iota(jnp.int32, k.shape, 1)
    lo = lane < HEAD_DIM
    kr = pltpu.roll(k, HEAD_DIM, 1)
    vr = pltpu.roll(v, HEAD_DIM, 1)
    pieces = [jnp.where(lo, k, 0.0), jnp.where(lo, 0.0, kr), jnp.where(lo, kr, 0.0), jnp.where(lo, 0.0, k),
              jnp.where(lo, v, 1.0), jnp.where(lo, 1.0, vr), jnp.where(lo, vr, 1.0), jnp.where(lo, 1.0, v)]
    return jnp.concatenate(pieces, axis=-1).astype(BF16)


def _proj_kernel(*refs, rope):
    if rope:
        (x_ref, mod_ref, ng_ref, wt_ref, wgk_ref, bgk_ref, cos_ref, sin_ref,
         q_ref, kvp_ref, za_ref, vg_ref, zg_ref, qkg_ref, gf_ref, gb_ref) = refs
    else:
        (x_ref, mod_ref, ng_ref, wt_ref, wgk_ref, bgk_ref,
         q_ref, kvp_ref, za_ref, vg_ref, zg_ref, qkg_ref, gf_ref, gb_ref, k_ref, v_ref) = refs
    x = x_ref[...]
    xn = x * lax.rsqrt(jnp.mean(x * x, axis=-1, keepdims=True) + EPS)
    mod = mod_ref[0]
    shift = mod[:, 0:D_MODEL]
    scale = mod[:, D_MODEL:2 * D_MODEL]
    h = ((xn * ng_ref[...]) * (1.0 + scale) + shift).astype(BF16)

    def piece(c0, c1):
        return _dot_nt(h, wt_ref[c0:c1, :])

    q = piece(_Q0, _K0)
    kv = piece(_K0, _ZA0)
    k = kv[:, 0:KV_WIDTH]
    v = kv[:, KV_WIDTH:2 * KV_WIDTH]
    if rope:
        cos = cos_ref[...]
        sin = sin_ref[...]
        q = _rope(q, cos, sin)
        k = _rope(k, cos, sin)
    else:
        seq = k_ref.shape[2]
        for b in range(k_ref.shape[0]):
            k_ref[b] = k[b * seq:(b + 1) * seq, :].T
            v_ref[b] = v[b * seq:(b + 1) * seq, :].T
    _store_stacked((q * _Q_PRESCALE).astype(BF16), q_ref)
    kvp_ref[...] = _kv_placements(k, v)
    za_ref[...] = piece(_ZA0, _QG0).astype(BF16)
    vg_ref[...] = piece(_VG0, _TAIL0).astype(BF16)
    qkg_ref[...] = piece(_QG0, _VG0)
    tail = piece(_TAIL0, _TAIL0 + _TAIL_COLS)
    zg_ref[...] = tail[:, 2 * GATE_RANK:_TAIL_COLS].astype(BF16)
    lane = lax.broadcasted_iota(jnp.int32, (tail.shape[0], LANES), 1)
    lr = jnp.where(lane < 2 * GATE_RANK, tail[:, 0:LANES], 0.0).astype(BF16)
    g = _log_sigmoid(_dot(lr, wgk_ref[...]) + bgk_ref[...]) * (1.0 / GATE_NORMALIZER)
    gf_ref[...] = g[:, 0:GLA_K_WIDTH]
    gb_ref[...] = g[:, GLA_K_WIDTH:2 * GLA_K_WIDTH]


def _project(x2d, mod3, mod_row_of_tile, norm_g, wt, wgk, bgk, rope_tabs, tm, seq):
    n_tok = x2d.shape[0]
    rope = rope_tabs is not None
    row = lambda i: (i, 0)
    const = lambda i: (0, 0)
    in_specs = [
        pl.BlockSpec((tm, D_MODEL), row),
        pl.BlockSpec((1, 1, 3 * D_MODEL), lambda i: (mod_row_of_tile(i), 0, 0)),
        pl.BlockSpec((1, D_MODEL), const),
        pl.BlockSpec((_TAIL0 + _TAIL_COLS, D_MODEL), const),
        pl.BlockSpec((LANES, 2 * GLA_K_WIDTH), const),
        pl.BlockSpec((1, 2 * GLA_K_WIDTH), const),
    ]
    args = [x2d, mod3, norm_g, wt, wgk, bgk]
    if rope:
        tiles_per_seq = rope_tabs[0].shape[0] // tm
        pos = lambda i: (i % tiles_per_seq, 0)
        in_specs += [pl.BlockSpec((tm, LANES), pos), pl.BlockSpec((tm, LANES), pos)]
        args += list(rope_tabs)
    widths = [(KVP_WIDTH, BF16), (ATTN_WIDTH, BF16), (GLA_V_WIDTH, BF16), (GLA_V_WIDTH, BF16),
              (2 * GLA_K_WIDTH, F32), (GLA_K_WIDTH, F32), (GLA_K_WIDTH, F32)]
    out_specs = [pl.BlockSpec(_stacked_shape(tm), lambda i: (i, 0, 0))] + [pl.BlockSpec((tm, w), row) for w, _ in widths]
    out_shape = ([jax.ShapeDtypeStruct(_stacked_shape(n_tok), BF16)]
                 + [jax.ShapeDtypeStruct((n_tok, w), dt) for w, dt in widths])
    if not rope:
        for _ in range(2):
            out_specs.append(pl.BlockSpec((tm // seq, KV_WIDTH, seq), lambda i: (i, 0, 0)))
            out_shape.append(jax.ShapeDtypeStruct((n_tok // seq, KV_WIDTH, seq), F32))
    return pl.pallas_call(
        functools.partial(_proj_kernel, rope=rope),
        grid=(n_tok // tm,),
        in_specs=in_specs,
        out_specs=out_specs,
        out_shape=out_shape,
        compiler_params=_params("arbitrary"),
        name="project_rope" if rope else "project",
    )(*args)


def _software_pipeline(n_groups, n_slots, stages):
    last_g = n_groups - 1

    def body(k, carry):
        for age in (2, 1, 0):
            for u in range(n_slots):
                g = n_slots * (k - age) + u
                stages[age](jnp.clip(g, 0, last_g), u, jnp.logical_and(g >= 0, g <= last_g))
        return carry

    lax.fori_loop(0, n_groups // n_slots + 2, body, 0)


def _init_pipeline_scratch(first, zero_refs, one_refs=()):
    @pl.when(first)
    def _():
        for ref in zero_refs:
            ref[...] = jnp.zeros_like(ref)
        for ref in one_refs:
            ref[...] = jnp.ones_like(ref)


def _key_rows(start, size):
    return pl.ds(pl.multiple_of(start, BLOCK), size)


def _softmax_rows(s_buf, p_buf, t_buf, slot, n_rows, n_keys, masks_of, sink_of):
    for r0 in range(0, n_rows, SOFTMAX_ROWS):
        rs = slice(r0, r0 + SOFTMAX_ROWS)
        masks = masks_of(r0)
        cols = []
        for c in range(n_keys // LANES):
            sc = s_buf[slot, rs, c * LANES:(c + 1) * LANES]
            cols.append(jnp.where(masks[c], sc, NEG_INF) if c in masks else sc)
        mx = cols[0]
        for sc in cols[1:]:
            mx = jnp.maximum(mx, sc)
        sink = sink_of(r0)
        m = jnp.maximum(jnp.max(mx, axis=-1, keepdims=True), sink)
        for c, sc in enumerate(cols):
            p_buf[slot, rs, c * LANES:(c + 1) * LANES] = jnp.exp2(sc - m).astype(BF16)
        t_buf[slot, rs, :] = jnp.broadcast_to(jnp.exp2(sink - m), (SOFTMAX_ROWS, LANES))


def _normalize(acc, t_buf, slot):
    return acc / (pltpu.roll(acc, HEAD_DIM, 1) + t_buf[slot])


def _attn_scratch(n_keys):
    rows = 2 * BLOCK
    return [pltpu.VMEM((PIPE_SLOTS, rows, n_keys), F32), pltpu.VMEM((PIPE_SLOTS, rows, n_keys), BF16),
            pltpu.VMEM((PIPE_SLOTS, rows, LANES), F32)]


def _stacked_shape(n_tok):
    return (n_tok // BLOCK * N_KV_HEADS, 2 * BLOCK, LANES)


def _store_stacked(x, ref):
    for blk in range(x.shape[0] // BLOCK):
        for j in range(N_KV_HEADS):
            for e in range(2):
                ref[blk * N_KV_HEADS + j, e * BLOCK:(e + 1) * BLOCK, :] = (
                    x[blk * BLOCK:(blk + 1) * BLOCK, (2 * j + e) * LANES:(2 * j + e + 1) * LANES])


def _load_stacked(ref):
    blocks = []
    for blk in range(ref.shape[0] // N_KV_HEADS):
        blocks.append(jnp.concatenate([ref[blk * N_KV_HEADS + j, e * BLOCK:(e + 1) * BLOCK, :]
                                       for j in range(N_KV_HEADS) for e in range(2)], axis=-1))
    return jnp.concatenate(blocks, axis=0)


assert PIPE_SLOTS % N_PLACEMENTS == 0


def _placement_lanes(u, is_v):
    idx = is_v * N_PLACEMENTS + u % N_PLACEMENTS
    return slice(idx * LANES, (idx + 1) * LANES)


def _store_group(o_ref, g, u, o):
    half = slice((u % 2) * HEAD_DIM, (u % 2 + 1) * HEAD_DIM)
    o_ref[g // 2, :, half] = o[:, half].astype(o_ref.dtype)


def _group_sinks(sink_ref, g):
    j, e = (g // 2) % N_KV_HEADS, g % 2
    sink_a = sink_ref[KV_GROUP * j + e] * LOG2E
    sink_b = sink_ref[KV_GROUP * j + 2 + e] * LOG2E
    return lambda r0: sink_a if r0 < BLOCK else sink_b


def _ctx_attn_kernel(sink_ref, q_ref, kvp_ref, o_ref, s_buf, p_buf, t_buf, *, seq):
    n_blocks = q_ref.shape[0] // N_KV_HEADS
    _init_pipeline_scratch(pl.program_id(0) == 0, (s_buf, p_buf), (t_buf,))
    blocks_per_seq = seq // BLOCK

    def keys_of(g):
        return _key_rows(((g // N_PLACEMENTS) // blocks_per_seq) * seq, seq)

    def qk(g, u, valid):
        s_buf[u] = _dot_nt(q_ref[g // 2], kvp_ref[keys_of(g), _placement_lanes(u, 0)])

    def softmax(g, u, valid):
        _softmax_rows(s_buf, p_buf, t_buf, u, 2 * BLOCK, seq, lambda r0: {}, _group_sinks(sink_ref, g))

    def pv(g, u, valid):
        _store_group(o_ref, g, u, _normalize(_dot(p_buf[u], kvp_ref[keys_of(g), _placement_lanes(u, 1)]), t_buf, u))

    _software_pipeline(n_blocks * N_PLACEMENTS, PIPE_SLOTS, (qk, softmax, pv))


def _context_attention(q, kvp, sink, seq):
    n_tok = kvp.shape[0]
    rows = CTX_ATTN_BATCH * seq
    row = lambda b: (b, 0)
    stacked = pl.BlockSpec(_stacked_shape(rows), lambda b: (b, 0, 0))
    return pl.pallas_call(
        functools.partial(_ctx_attn_kernel, seq=seq),
        grid=(n_tok // rows,),
        in_specs=[
            pl.BlockSpec(memory_space=pltpu.SMEM),
            stacked,
            pl.BlockSpec((rows, KVP_WIDTH), row),
        ],
        out_specs=stacked,
        out_shape=jax.ShapeDtypeStruct(_stacked_shape(n_tok), BF16),
        scratch_shapes=_attn_scratch(seq),
        compiler_params=_params("arbitrary"),
        name="context_attention",
    )(sink, q, kvp)


def _lat_attn_kernel(sink_ref, q_ref, kvp_ref, ck_ref, cv_ref, o_ref, kctx, vctx, s_buf, p_buf, t_buf, *, past):
    n_blocks = q_ref.shape[0] // N_KV_HEADS
    win = 3 * BLOCK
    _init_pipeline_scratch(pl.program_id(0) == 0, (s_buf, p_buf), (t_buf,))

    cpl = _kv_placements(ck_ref[0].T, cv_ref[0].T)
    for idx in range(N_PLACEMENTS):
        kctx[idx] = cpl[:, idx * LANES:(idx + 1) * LANES]
        vctx[idx] = cpl[:, (N_PLACEMENTS + idx) * LANES:(N_PLACEMENTS + idx + 1) * LANES]

    row = lax.broadcasted_iota(jnp.int32, (SOFTMAX_ROWS, LANES), 0)
    c = lax.broadcasted_iota(jnp.int32, (SOFTMAX_ROWS, LANES), 1)

    def first_block(g):
        return jnp.clip(g // N_PLACEMENTS - 1, 0, n_blocks - 3)

    def window_of(g):
        return _key_rows(first_block(g) * BLOCK, win)

    def qk(g, u, valid):
        q2 = q_ref[g // 2]
        s_buf[u, :, 0:win] = _dot_nt(q2, kvp_ref[window_of(g), _placement_lanes(u, 0)])
        s_buf[u, :, win:win + past] = _dot_nt(q2, kctx[g % N_PLACEMENTS])

    def softmax(g, u, valid):
        shift = first_block(g) - (g // N_PLACEMENTS - 1)
        off = BLOCK * shift

        def masks_of(r0):
            r = row + r0 % BLOCK
            return {0: c >= r - off, 1: (c - r) * shift <= 0, 2: c <= r - off}

        _softmax_rows(s_buf, p_buf, t_buf, u, 2 * BLOCK, win + past, masks_of, _group_sinks(sink_ref, g))

    def pv(g, u, valid):
        acc = _dot(p_buf[u, :, 0:win], kvp_ref[window_of(g), _placement_lanes(u, 1)])
        acc = acc + _dot(p_buf[u, :, win:win + past], vctx[g % N_PLACEMENTS])
        _store_group(o_ref, g, u, _normalize(acc, t_buf, u))

    _software_pipeline(n_blocks * N_PLACEMENTS, PIPE_SLOTS, (qk, softmax, pv))


def _latent_attention(q, kvp, ck, cv, sink, n_batch, seq):
    past = ck.shape[2]
    stacked = pl.BlockSpec(_stacked_shape(seq), lambda b: (b, 0, 0))
    cache = pl.BlockSpec((1, KV_WIDTH, past), lambda b: (b, 0, 0))
    return pl.pallas_call(
        functools.partial(_lat_attn_kernel, past=past),
        grid=(n_batch,),
        in_specs=[
            pl.BlockSpec(memory_space=pltpu.SMEM),
            stacked,
            pl.BlockSpec((seq, KVP_WIDTH), lambda b: (b, 0)),
            cache,
            cache,
        ],
        out_specs=stacked,
        out_shape=jax.ShapeDtypeStruct(_stacked_shape(n_batch * seq), BF16),
        scratch_shapes=[pltpu.VMEM((N_PLACEMENTS, past, LANES), BF16)] * 2 + _attn_scratch(3 * BLOCK + past),
        compiler_params=_params("arbitrary"),
        name="latent_attention",
    )(sink, q, kvp, ck, cv)


_PK = 2 * GLA_DK
_PV = 2 * GLA_DV


def _cumsum_rows(x, reverse):
    n, width = x.shape
    row = lax.broadcasted_iota(jnp.int32, (SUBLANES, width), 0)
    blocks = [x[j:j + SUBLANES, :] for j in range(0, n, SUBLANES)]
    s = 1
    while s < SUBLANES:
        if reverse:
            blocks = [b + jnp.where(row < SUBLANES - s, pltpu.roll(b, SUBLANES - s, 0), 0.0) for b in blocks]
        else:
            blocks = [b + jnp.where(row >= s, pltpu.roll(b, s, 0), 0.0) for b in blocks]
        s *= 2
    edge = 0 if reverse else SUBLANES - 1
    order = range(len(blocks) - 1, -1, -1) if reverse else range(len(blocks))
    carry = None
    for j in order:
        if carry is not None:
            blocks[j] = blocks[j] + carry
        carry = jnp.broadcast_to(blocks[j][edge:edge + 1, :], (SUBLANES, width))
    return jnp.concatenate(blocks, axis=0)


def _gla_kernel(*refs, seq_chunks, has_init, emit_state):
    it = iter(refs)
    views = [tuple(next(it) for _ in range(3)) for _ in range(2)]
    s0_ref = next(it) if has_init else None
    out_refs = (next(it), next(it))
    sout_ref = next(it) if emit_state else None
    st_ref, a_buf, a2_buf, qe_buf, qd_buf, ds_buf, sb_buf, dc_buf = (next(it) for _ in range(8))
    n_chunks = views[0][0].shape[0] // CHUNK
    n_groups = 2 * n_chunks
    n_pairs = N_GLA_HEADS // 2
    tile = pl.program_id(1)
    q_scale = GLA_DK ** -0.5
    _init_pipeline_scratch(jnp.logical_and(pl.program_id(0) == 0, tile == 0),
                           (st_ref, a_buf, a2_buf, qe_buf, qd_buf, ds_buf, sb_buf), (dc_buf,))

    if has_init:
        @pl.when(tile == 0)
        def _():
            st_ref[...] = jnp.zeros_like(st_ref)
            for d in range(2):
                for h in range(N_GLA_HEADS):
                    p, e = divmod(h, 2)
                    st_ref[d, p, e * GLA_DK:(e + 1) * GLA_DK, e * GLA_DV:(e + 1) * GLA_DV] = s0_ref[0, d, h]

    lane_k = lax.broadcasted_iota(jnp.int32, (CHUNK, _PK), 1)
    row_k = lax.broadcasted_iota(jnp.int32, (CHUNK, _PK), 0)
    lo_k = lane_k < GLA_DK
    tri = (lane_k % CHUNK <= row_k, lane_k % CHUNK >= row_k)
    lo_v = lax.broadcasted_iota(jnp.int32, (CHUNK, _PV), 1) < GLA_DV
    diag = ((lax.broadcasted_iota(jnp.int32, (_PK, _PV), 0) < GLA_DK)
            == (lax.broadcasted_iota(jnp.int32, (_PK, _PV), 1) < GLA_DV))

    def place(g, d):
        c = g // 2
        if d == 0:
            return c, c
        seq = c // seq_chunks
        return c, seq * seq_chunks + (seq_chunks - 1 - (c - seq * seq_chunks))

    def rows_of(pos):
        return pl.ds(pl.multiple_of(pos * CHUNK, CHUNK), CHUNK)

    def prep(g, u, valid):
        d = u % 2
        qk_ref, g_ref, v_ref = views[d]
        _, pos = place(g, d)
        rows = rows_of(pos)
        gc = _cumsum_rows(g_ref[rows, :], reverse=(d == 1))
        gtot = gc[0:1, :] if d == 1 else gc[CHUNK - 1:CHUNK, :]
        k = qk_ref[rows, GLA_K_WIDTH:2 * GLA_K_WIDTH]
        qe = qk_ref[rows, 0:GLA_K_WIDTH] * jnp.exp(gc)
        qe_buf[u] = qe
        q_dec = (qe * q_scale).astype(BF16)
        k_inv = k * jnp.exp(-gc)
        e_tot = jnp.exp(gtot)
        k_end = (k_inv * jnp.where(valid, e_tot, 0.0)).astype(BF16)
        decay = jnp.where(valid, e_tot, 1.0)
        for p in range(n_pairs):
            lanes = slice(p * _PK, (p + 1) * _PK)
            ki = k_inv[:, lanes]
            kst = jnp.concatenate([jnp.where(lo_k, ki, 0.0), jnp.where(lo_k, 0.0, ki)], axis=0).astype(BF16)
            a_buf[u, p] = _dot_nt(q_dec[:, lanes], kst)
            ds_buf[u, p] = jnp.where(diag, _dot_tn(k_end[:, lanes], v_ref[rows, p * _PV:(p + 1) * _PV]), 0.0)
            dc_buf[u, p] = jnp.broadcast_to(decay[:, lanes], (_PK, _PK)).T

    def state(g, u, valid):
        d = u % 2
        c, _ = place(g, d)
        qd_buf[u] = (qe_buf[u] * q_scale).astype(BF16)
        for p in range(n_pairs):
            fresh = (c % seq_chunks == 0) if emit_state else (c < 0)
            s_in = jnp.where(fresh, 0.0, st_ref[d, p])
            sb_buf[u, p] = s_in.astype(BF16)
            a2_buf[u, p] = jnp.where(tri[d], a_buf[u, p], 0.0).astype(BF16)
            dc = dc_buf[u, p]
            s_new = s_in * jnp.concatenate([dc, dc], axis=1) + ds_buf[u, p]
            st_ref[d, p] = s_new
            if emit_state:
                for e in range(2):
                    sout_ref[c // seq_chunks, d, 2 * p + e] = s_new[e * GLA_DK:(e + 1) * GLA_DK,
                                                                    e * GLA_DV:(e + 1) * GLA_DV]

    def out(g, u, valid):
        d = u % 2
        _, _, v_ref = views[d]
        _, pos = place(g, d)
        rows = rows_of(pos)
        for p in range(n_pairs):
            vpf = v_ref[rows, p * _PV:(p + 1) * _PV].astype(F32)
            vbd = jnp.concatenate([jnp.where(lo_v, vpf, 0.0), jnp.where(lo_v, 0.0, vpf)], axis=0).astype(BF16)
            x = jnp.concatenate([a2_buf[u, p], qd_buf[u, :, p * _PK:(p + 1) * _PK]], axis=1)
            w = jnp.concatenate([vbd, sb_buf[u, p]], axis=0)
            out_refs[d][rows, p * _PV:(p + 1) * _PV] = _dot(x, w).astype(out_refs[d].dtype)

    _software_pipeline(n_groups, GLA_PIPE_SLOTS, (prep, state, out))


def _gla(qkg, gf, gb, vg, s0, n_batch, seq):
    tm = GLA_TILE
    long_seq = s0 is not None
    if long_seq:
        nt = seq // tm
        grid = (n_batch, nt)
        fwd = lambda b, i: (b * nt + i, 0)
        bwd = lambda b, i: (b * nt + nt - 1 - i, 0)
        seq_chunks = tm // CHUNK
    else:
        per_step = tm // seq
        grid = (1, n_batch // per_step)
        fwd = bwd = lambda b, i: (i, 0)
        seq_chunks = seq // CHUNK
    in_specs = []
    args = []
    for m, gates in ((fwd, gf), (bwd, gb)):
        in_specs += [pl.BlockSpec((tm, 2 * GLA_K_WIDTH), m), pl.BlockSpec((tm, GLA_K_WIDTH), m),
                     pl.BlockSpec((tm, GLA_V_WIDTH), m)]
        args += [qkg, gates, vg]
    n_tok = n_batch * seq
    out_specs = [pl.BlockSpec((tm, GLA_V_WIDTH), fwd), pl.BlockSpec((tm, GLA_V_WIDTH), bwd)]
    out_shape = [jax.ShapeDtypeStruct((n_tok, GLA_V_WIDTH), BF16)] * 2
    st_dims = (2, N_GLA_HEADS, GLA_DK, GLA_DV)
    if long_seq:
        in_specs.append(pl.BlockSpec((1,) + st_dims, lambda b, i: (b, 0, 0, 0, 0)))
        args.append(s0)
    else:
        out_specs.append(pl.BlockSpec((per_step,) + st_dims, lambda b, i: (i, 0, 0, 0, 0)))
        out_shape.append(jax.ShapeDtypeStruct((n_batch,) + st_dims, F32))
    n = GLA_PIPE_SLOTS
    n_pairs = N_GLA_HEADS // 2
    return pl.pallas_call(
        functools.partial(_gla_kernel, seq_chunks=seq_chunks, has_init=long_seq, emit_state=not long_seq),
        grid=grid,
        in_specs=in_specs,
        out_specs=out_specs,
        out_shape=out_shape,
        scratch_shapes=[pltpu.VMEM((2, n_pairs, _PK, _PV), F32),
                        pltpu.VMEM((n, n_pairs, CHUNK, _PK), F32),
                        pltpu.VMEM((n, n_pairs, CHUNK, _PK), BF16),
                        pltpu.VMEM((n, CHUNK, GLA_K_WIDTH), F32),
                        pltpu.VMEM((n, CHUNK, GLA_K_WIDTH), BF16),
                        pltpu.VMEM((n, n_pairs, _PK, _PV), F32),
                        pltpu.VMEM((n, n_pairs, _PK, _PV), BF16),
                        pltpu.VMEM((n, n_pairs, _PK, _PK), F32)],
        compiler_params=_params("arbitrary", "arbitrary"),
        name="gla_long" if long_seq else "gla_short",
    )(*args)


def _merge_kernel(x_ref, mod_ref, oa_ref, za_ref, of_ref, ob_ref, zg_ref, gng_ref, wo_ref, fng_ref, y_ref):
    ya = _load_stacked(oa_ref).astype(F32) * _silu(za_ref[...].astype(F32))
    og = of_ref[...].astype(F32) + ob_ref[...].astype(F32)
    zg = zg_ref[...].astype(F32)
    gng = gng_ref[...]
    parts = [ya]
    for h in range(N_GLA_HEADS):
        oh = og[:, h * GLA_DV:(h + 1) * GLA_DV]
        nh = oh * lax.rsqrt(jnp.mean(oh * oh, axis=-1, keepdims=True) + EPS)
        parts.append((nh * gng) * _silu(zg[:, h * GLA_DV:(h + 1) * GLA_DV]))
    yin = jnp.concatenate(parts, axis=-1).astype(BF16)
    gate = mod_ref[0][:, 2 * D_MODEL:3 * D_MODEL]
    y = x_ref[...] + gate * _dot(yin, wo_ref[...])
    y_ref[...] = (y * lax.rsqrt(jnp.mean(y * y, axis=-1, keepdims=True) + EPS)) * fng_ref[...]


def _merge(x2d, mod3, mod_row_of_tile, oa, za, of, ob, zg, gla_norm_g, w_out, final_norm_g, tm):
    n_tok = x2d.shape[0]
    row = lambda i: (i, 0)
    const = lambda i: (0, 0)
    wide = pl.BlockSpec((tm, GLA_V_WIDTH), row)
    return pl.pallas_call(
        _merge_kernel,
        grid=(n_tok // tm,),
        in_specs=[
            pl.BlockSpec((tm, D_MODEL), row),
            pl.BlockSpec((1, 1, 3 * D_MODEL), lambda i: (mod_row_of_tile(i), 0, 0)),
            pl.BlockSpec(_stacked_shape(tm), lambda i: (i, 0, 0)), wide, wide, wide, wide,
            pl.BlockSpec((1, GLA_DV), const),
            pl.BlockSpec((D_MODEL, D_MODEL), const),
            pl.BlockSpec((1, D_MODEL), const),
        ],
        out_specs=pl.BlockSpec((tm, D_MODEL), row),
        out_shape=jax.ShapeDtypeStruct((n_tok, D_MODEL), F32),
        compiler_params=_params("arbitrary"),
        name="merge",
    )(x2d, mod3, oa, za, of, ob, zg, gla_norm_g, w_out, final_norm_g)


def _rope_tables(n_tok):
    t = np.arange(n_tok)
    rowp = (t // GRID_W).astype(np.float64)
    colp = (t % GRID_W).astype(np.float64)
    nf = HEAD_DIM // 4
    freqs = ROPE_BASE ** (-np.arange(nf, dtype=np.float64) / nf)
    lane = np.arange(LANES)
    pos = np.where(((lane % HEAD_DIM) < HEAD_DIM // 2)[None, :], rowp[:, None], colp[:, None])
    ang = pos * freqs[lane % nf][None, :]
    sign = np.where((lane % (2 * nf)) < nf, -1.0, 1.0)
    return jnp.asarray(np.cos(ang), F32), jnp.asarray(np.sin(ang) * sign[None, :], F32)


def kernel(x_prompt, x_sample, c, cache_k, cache_v, state_gla, c_ctx, w_mod, b_mod, norm_g, w_in,
           w_gk_f, b_gk_f, w_gk_b, b_gk_b, sink, gla_norm_g, w_out, final_norm_g):
    depth = w_in.shape[0]
    assert depth == 1, "single-layer step"
    l = 0
    bp, sp, _ = x_prompt.shape
    bs, ss, _ = x_sample.shape
    assert (bp * sp) % TOKEN_TILE == 0 and TOKEN_TILE % sp == 0 and ss % TOKEN_TILE == 0
    assert (bp * sp) % MERGE_TILE == 0 and ss % MERGE_TILE == 0
    assert bp % CTX_ATTN_BATCH == 0 and sp % BLOCK == 0 and ss % BLOCK == 0 and ss // BLOCK >= 3
    assert GLA_TILE % sp == 0 and (bp * sp) % GLA_TILE == 0 and ss % GLA_TILE == 0 and sp % CHUNK == 0
    assert ss % GRID_W == 0 and cache_k.shape[2] % LANES == 0

    wt = w_in[l].T.astype(BF16)
    wgk = jnp.zeros((LANES, 2 * GLA_K_WIDTH), F32)
    wgk = wgk.at[0:GATE_RANK, 0:GLA_K_WIDTH].set(w_gk_f[l])
    wgk = wgk.at[GATE_RANK:2 * GATE_RANK, GLA_K_WIDTH:].set(w_gk_b[l]).astype(BF16)
    bgk = jnp.concatenate([b_gk_f[l], b_gk_b[l]])[None, :]
    wo = w_out[l].astype(BF16)
    ng = norm_g[l][None, :]
    gng = gla_norm_g[l][None, :]
    fng = final_norm_g[None, :]
    sk = sink[l]

    n_rows = SUBLANES
    assert 1 + bs <= n_rows
    cvecs = jnp.concatenate([c_ctx[None, :], c, jnp.zeros((n_rows - 1 - bs, D_MODEL), F32)], axis=0)
    mod3 = _modulation(cvecs, w_mod[l], b_mod[l][None, :]).reshape(n_rows, 1, 3 * D_MODEL)
    tm = TOKEN_TILE

    xp2 = x_prompt.reshape(bp * sp, D_MODEL)
    ctx_row = lambda i: 0
    q, kvp, za, vg, zg, qkg, gf, gb, kt, vt = _project(xp2, mod3, ctx_row, ng, wt, wgk, bgk, None, tm, sp)
    oa = _context_attention(q, kvp, sk, sp)
    of, ob, st = _gla(qkg, gf, gb, vg, None, bp, sp)
    y_prompt = _merge(xp2, mod3, ctx_row, oa, za, of, ob, zg, gng, wo, fng, MERGE_TILE).reshape(bp, sp, D_MODEL)
    to_cache = lambda t: t.reshape(bp, 1, N_KV_HEADS, HEAD_DIM, sp).transpose(0, 1, 4, 2, 3)
    new_k, new_v = to_cache(kt), to_cache(vt)
    new_state = st[:, None]

    xs2 = x_sample.reshape(bs * ss, D_MODEL)
    tiles_per_seq = ss // tm
    lat_row = lambda i: 1 + i // tiles_per_seq
    q, kvp, za, vg, zg, qkg, gf, gb = _project(xs2, mod3, lat_row, ng, wt, wgk, bgk, _rope_tables(ss), tm, ss)
    past = cache_k.shape[2]
    from_cache = lambda t: t[:, l].transpose(0, 2, 3, 1).reshape(bs, KV_WIDTH, past)
    ck, cv = from_cache(cache_k), from_cache(cache_v)
    oa = _latent_attention(q, kvp, ck, cv, sk, bs, ss)
    of, ob = _gla(qkg, gf, gb, vg, state_gla[:, l], bs, ss)
    merge_row = lambda i: 1 + i // (ss // MERGE_TILE)
    y_sample = _merge(xs2, mod3, merge_row, oa, za, of, ob, zg, gng, wo, fng, MERGE_TILE).reshape(bs, ss, D_MODEL)

    return (y_prompt, y_sample, new_k, new_v, new_state)
```

```python
import functools
import math

import numpy as np
import jax
import jax.numpy as jnp
from jax import lax
from jax.experimental import pallas as pl
from jax.experimental.pallas import tpu as pltpu

D_MODEL = 1024
GRID_W = 64
HEAD_DIM = 64
N_ATTN_HEADS = 8
N_KV_HEADS = 2
ATTN_WIDTH = 512
KV_WIDTH = 128
BLOCK = 128
N_GLA_HEADS = 4
GLA_DK = 64
GLA_DV = 128
GLA_K_WIDTH = 256
GLA_V_WIDTH = 512
GATE_RANK = 16
GATE_NORMALIZER = 16.0
CHUNK = 64
ROPE_BASE = 10000.0
EPS = 1e-6
NEG_INF = -1e30

KV_GROUP = N_ATTN_HEADS // N_KV_HEADS
LANES = 128
SUBLANES = 8
TOKEN_TILE = 1024
MERGE_TILE = 1024
MERGE_INPUT_BUFFERS = 3
GLA_TILE = 2048
CTX_ATTN_BATCH = 8
PIPE_SLOTS = 4
GLA_PIPE_SLOTS = 4
KVP_WIDTH = 8 * LANES
N_PLACEMENTS = 2 * N_KV_HEADS
SOFTMAX_ROWS = 32
MOD_TILE = 1024
VMEM_LIMIT = 56 * 1024 * 1024
LOG2E = math.log2(math.e)

F32 = jnp.float32
BF16 = jnp.bfloat16


def _dot(a, b):
    return jnp.dot(a, b, preferred_element_type=F32)


def _dot_nt(a, b):
    return lax.dot_general(a, b, (((1,), (1,)), ((), ())), preferred_element_type=F32)


def _dot_tn(a, b):
    return lax.dot_general(a, b, (((0,), (0,)), ((), ())), preferred_element_type=F32)


def _silu(x):
    h = 0.5 * x
    return h + h * jnp.tanh(h)


def _log_sigmoid(x):
    return jnp.minimum(x, 0.0) - jnp.log1p(jnp.exp(-jnp.abs(x)))


def _params(*sem):
    return pltpu.CompilerParams(dimension_semantics=sem, vmem_limit_bytes=VMEM_LIMIT)


def _split_bf16(x):
    hi = x.astype(BF16)
    return hi, (x - hi.astype(F32)).astype(BF16)


def _mod_kernel(c_ref, w_ref, b_ref, o_ref):
    rows = c_ref.shape[0]
    s_hi, s_lo = _split_bf16(_silu(c_ref[...]))
    w_hi, w_lo = _split_bf16(w_ref[...])
    by_hi = _dot(jnp.concatenate([s_hi, s_lo], axis=0), w_hi)
    o_ref[...] = by_hi[0:rows] + by_hi[rows:2 * rows] + _dot(s_hi, w_lo) + b_ref[...]


def _modulation(cvecs, w_mod, b_mod):
    rows = cvecs.shape[0]
    n_out = w_mod.shape[1]
    return pl.pallas_call(
        _mod_kernel,
        grid=(n_out // MOD_TILE,),
        in_specs=[
            pl.BlockSpec((rows, D_MODEL), lambda j: (0, 0)),
            pl.BlockSpec((D_MODEL, MOD_TILE), lambda j: (0, j)),
            pl.BlockSpec((1, MOD_TILE), lambda j: (0, j)),
        ],
        out_specs=pl.BlockSpec((rows, MOD_TILE), lambda j: (0, j)),
        out_shape=jax.ShapeDtypeStruct((rows, n_out), F32),
        compiler_params=_params("arbitrary"),
        name="adaln_mod",
    )(cvecs, w_mod, b_mod)


_Q0, _K0, _V0, _ZA0, _QG0, _KG0, _VG0, _TAIL0 = 0, 512, 640, 768, 1280, 1536, 1792, 2304
_TAIL_COLS = 2 * GATE_RANK + GLA_V_WIDTH
_Q_PRESCALE = (HEAD_DIM ** -0.5) * LOG2E


def _rope(x, cos, sin_signed):
    n = x.shape[1] // LANES
    lane = lax.broadcasted_iota(jnp.int32, (x.shape[0], LANES), 1)
    first = (lane % 32) < 16
    outs = []
    for c in range(n):
        xc = x[:, c * LANES:(c + 1) * LANES]
        partner = jnp.where(first, pltpu.roll(xc, LANES - 16, 1), pltpu.roll(xc, 16, 1))
        outs.append(xc * cos + partner * sin_signed)
    return outs[0] if n == 1 else jnp.concatenate(outs, axis=-1)


def _kv_placements(k, v):
    lane = lax.broadcasted_iota(jnp.int32, k.shape, 1)
    lo = lane < HEAD_DIM
    kr = pltpu.roll(k, HEAD_DIM, 1)
    vr = pltpu.roll(v, HEAD_DIM, 1)
    pieces = [jnp.where(lo, k, 0.0), jnp.where(lo, 0.0, kr), jnp.where(lo, kr, 0.0), jnp.where(lo, 0.0, k),
              jnp.where(lo, v, 1.0), jnp.where(lo, 1.0, vr), jnp.where(lo, vr, 1.0), jnp.where(lo, 1.0, v)]
    return jnp.concatenate(pieces, axis=-1).astype(BF16)


def _proj_kernel(*refs, rope):
    if rope:
        (x_ref, mod_ref, ng_ref, wt_ref, wgk_ref, bgk_ref, cos_ref, sin_ref,
         q_ref, kvp_ref, za_ref, vg_ref, zg_ref, qkg_ref, gf_ref, gb_ref) = refs
    else:
        (x_ref, mod_ref, ng_ref, wt_ref, wgk_ref, bgk_ref,
         q_ref, kvp_ref, za_ref, vg_ref, zg_ref, qkg_ref, gf_ref, gb_ref, k_ref, v_ref) = refs
    x = x_ref[...]
    xn = x * lax.rsqrt(jnp.mean(x * x, axis=-1, keepdims=True) + EPS)
    mod = mod_ref[0]
    shift = mod[:, 0:D_MODEL]
    scale = mod[:, D_MODEL:2 * D_MODEL]
    h = ((xn * ng_ref[...]) * (1.0 + scale) + shift).astype(BF16)

    def piece(c0, c1):
        return _dot_nt(h, wt_ref[c0:c1, :])

    q = piece(_Q0, _K0)
    kv = piece(_K0, _ZA0)
    k = kv[:, 0:KV_WIDTH]
    v = kv[:, KV_WIDTH:2 * KV_WIDTH]
    if rope:
        cos = cos_ref[...]
        sin = sin_ref[...]
        q = _rope(q, cos, sin)
        k = _rope(k, cos, sin)
    else:
        seq = k_ref.shape[2]
        for b in range(k_ref.shape[0]):
            k_ref[b] = k[b * seq:(b + 1) * seq, :].T
            v_ref[b] = v[b * seq:(b + 1) * seq, :].T
    _store_stacked((q * _Q_PRESCALE).astype(BF16), q_ref)
    kvp_ref[...] = _kv_placements(k, v)
    za_ref[...] = piece(_ZA0, _QG0).astype(BF16)
    vg_ref[...] = piece(_VG0, _TAIL0).astype(BF16)
    qkg_ref[...] = piece(_QG0, _VG0)
    tail = piece(_TAIL0, _TAIL0 + _TAIL_COLS)
    zg_ref[...] = tail[:, 2 * GATE_RANK:_TAIL_COLS].astype(BF16)
    lane = lax.broadcasted_iota(jnp.int32, (tail.shape[0], LANES), 1)
    lr = jnp.where(lane < 2 * GATE_RANK, tail[:, 0:LANES], 0.0).astype(BF16)
    g = _log_sigmoid(_dot(lr, wgk_ref[...]) + bgk_ref[...]) * (1.0 / GATE_NORMALIZER)
    gf_ref[...] = g[:, 0:GLA_K_WIDTH]
    gb_ref[...] = g[:, GLA_K_WIDTH:2 * GLA_K_WIDTH]


def _project(x2d, mod3, mod_row_of_tile, norm_g, wt, wgk, bgk, rope_tabs, tm, seq):
    n_tok = x2d.shape[0]
    rope = rope_tabs is not None
    row = lambda i: (i, 0)
    const = lambda i: (0, 0)
    in_specs = [
        pl.BlockSpec((tm, D_MODEL), row),
        pl.BlockSpec((1, 1, 3 * D_MODEL), lambda i: (mod_row_of_tile(i), 0, 0)),
        pl.BlockSpec((1, D_MODEL), const),
        pl.BlockSpec((_TAIL0 + _TAIL_COLS, D_MODEL), const),
        pl.BlockSpec((LANES, 2 * GLA_K_WIDTH), const),
        pl.BlockSpec((1, 2 * GLA_K_WIDTH), const),
    ]
    args = [x2d, mod3, norm_g, wt, wgk, bgk]
    if rope:
        tiles_per_seq = rope_tabs[0].shape[0] // tm
        pos = lambda i: (i % tiles_per_seq, 0)
        in_specs += [pl.BlockSpec((tm, LANES), pos), pl.BlockSpec((tm, LANES), pos)]
        args += list(rope_tabs)
    widths = [(KVP_WIDTH, BF16), (ATTN_WIDTH, BF16), (GLA_V_WIDTH, BF16), (GLA_V_WIDTH, BF16),
              (2 * GLA_K_WIDTH, F32), (GLA_K_WIDTH, F32), (GLA_K_WIDTH, F32)]
    out_specs = [pl.BlockSpec(_stacked_shape(tm), lambda i: (i, 0, 0))] + [pl.BlockSpec((tm, w), row) for w, _ in widths]
    out_shape = ([jax.ShapeDtypeStruct(_stacked_shape(n_tok), BF16)]
                 + [jax.ShapeDtypeStruct((n_tok, w), dt) for w, dt in widths])
    if not rope:
        for _ in range(2):
            out_specs.append(pl.BlockSpec((tm // seq, KV_WIDTH, seq), lambda i: (i, 0, 0)))
            out_shape.append(jax.ShapeDtypeStruct((n_tok // seq, KV_WIDTH, seq), F32))
    return pl.pallas_call(
        functools.partial(_proj_kernel, rope=rope),
        grid=(n_tok // tm,),
        in_specs=in_specs,
        out_specs=out_specs,
        out_shape=out_shape,
        compiler_params=_params("arbitrary"),
        name="project_rope" if rope else "project",
    )(*args)


def _software_pipeline(n_groups, n_slots, stages):
    last_g = n_groups - 1

    def body(k, carry):
        for age in (2, 1, 0):
            for u in range(n_slots):
                g = n_slots * (k - age) + u
                stages[age](jnp.clip(g, 0, last_g), u, jnp.logical_and(g >= 0, g <= last_g))
        return carry

    lax.fori_loop(0, n_groups // n_slots + 2, body, 0)


def _init_pipeline_scratch(first, zero_refs, one_refs=()):
    @pl.when(first)
    def _():
        for ref in zero_refs:
            ref[...] = jnp.zeros_like(ref)
        for ref in one_refs:
            ref[...] = jnp.ones_like(ref)


def _key_rows(start, size):
    return pl.ds(pl.multiple_of(start, BLOCK), size)


def _softmax_rows(s_buf, p_buf, t_buf, slot, n_rows, n_keys, masks_of, sink_of):
    for r0 in range(0, n_rows, SOFTMAX_ROWS):
        rs = slice(r0, r0 + SOFTMAX_ROWS)
        masks = masks_of(r0)
        cols = []
        for c in range(n_keys // LANES):
            sc = s_buf[slot, rs, c * LANES:(c + 1) * LANES]
            cols.append(jnp.where(masks[c], sc, NEG_INF) if c in masks else sc)
        mx = cols[0]
        for sc in cols[1:]:
            mx = jnp.maximum(mx, sc)
        sink = sink_of(r0)
        m = jnp.maximum(jnp.max(mx, axis=-1, keepdims=True), sink)
        for c, sc in enumerate(cols):
            p_buf[slot, rs, c * LANES:(c + 1) * LANES] = jnp.exp2(sc - m).astype(BF16)
        t_buf[slot, rs, :] = jnp.broadcast_to(jnp.exp2(sink - m), (SOFTMAX_ROWS, LANES))


def _normalize(acc, t_buf, slot):
    return acc / (pltpu.roll(acc, HEAD_DIM, 1) + t_buf[slot])


def _attn_scratch(n_keys):
    rows = 2 * BLOCK
    return [pltpu.VMEM((PIPE_SLOTS, rows, n_keys), F32), pltpu.VMEM((PIPE_SLOTS, rows, n_keys), BF16),
            pltpu.VMEM((PIPE_SLOTS, rows, LANES), F32)]


def _stacked_shape(n_tok):
    return (n_tok // BLOCK * N_KV_HEADS, 2 * BLOCK, LANES)


def _store_stacked(x, ref):
    for blk in range(x.shape[0] // BLOCK):
        for j in range(N_KV_HEADS):
            for e in range(2):
                ref[blk * N_KV_HEADS + j, e * BLOCK:(e + 1) * BLOCK, :] = (
                    x[blk * BLOCK:(blk + 1) * BLOCK, (2 * j + e) * LANES:(2 * j + e + 1) * LANES])


def _load_stacked(ref):
    blocks = []
    for blk in range(ref.shape[0] // N_KV_HEADS):
        blocks.append(jnp.concatenate([ref[blk * N_KV_HEADS + j, e * BLOCK:(e + 1) * BLOCK, :]
                                       for j in range(N_KV_HEADS) for e in range(2)], axis=-1))
    return jnp.concatenate(blocks, axis=0)


assert PIPE_SLOTS % N_PLACEMENTS == 0


def _placement_lanes(u, is_v):
    idx = is_v * N_PLACEMENTS + u % N_PLACEMENTS
    return slice(idx * LANES, (idx + 1) * LANES)


def _store_group(o_ref, g, u, o):
    half = slice((u % 2) * HEAD_DIM, (u % 2 + 1) * HEAD_DIM)
    o_ref[g // 2, :, half] = o[:, half].astype(o_ref.dtype)


def _group_sinks(sink_ref, g):
    j, e = (g // 2) % N_KV_HEADS, g % 2
    sink_a = sink_ref[KV_GROUP * j + e] * LOG2E
    sink_b = sink_ref[KV_GROUP * j + 2 + e] * LOG2E
    return lambda r0: sink_a if r0 < BLOCK else sink_b


def _ctx_attn_kernel(sink_ref, q_ref, kvp_ref, o_ref, s_buf, p_buf, t_buf, *, seq):
    n_blocks = q_ref.shape[0] // N_KV_HEADS
    _init_pipeline_scratch(pl.program_id(0) == 0, (s_buf, p_buf), (t_buf,))
    blocks_per_seq = seq // BLOCK

    def keys_of(g):
        return _key_rows(((g // N_PLACEMENTS) // blocks_per_seq) * seq, seq)

    def qk(g, u, valid):
        s_buf[u] = _dot_nt(q_ref[g // 2], kvp_ref[keys_of(g), _placement_lanes(u, 0)])

    def softmax(g, u, valid):
        _softmax_rows(s_buf, p_buf, t_buf, u, 2 * BLOCK, seq, lambda r0: {}, _group_sinks(sink_ref, g))

    def pv(g, u, valid):
        _store_group(o_ref, g, u, _normalize(_dot(p_buf[u], kvp_ref[keys_of(g), _placement_lanes(u, 1)]), t_buf, u))

    _software_pipeline(n_blocks * N_PLACEMENTS, PIPE_SLOTS, (qk, softmax, pv))


def _context_attention(q, kvp, sink, seq):
    n_tok = kvp.shape[0]
    rows = CTX_ATTN_BATCH * seq
    row = lambda b: (b, 0)
    stacked = pl.BlockSpec(_stacked_shape(rows), lambda b: (b, 0, 0))
    return pl.pallas_call(
        functools.partial(_ctx_attn_kernel, seq=seq),
        grid=(n_tok // rows,),
        in_specs=[
            pl.BlockSpec(memory_space=pltpu.SMEM),
            stacked,
            pl.BlockSpec((rows, KVP_WIDTH), row),
        ],
        out_specs=stacked,
        out_shape=jax.ShapeDtypeStruct(_stacked_shape(n_tok), BF16),
        scratch_shapes=_attn_scratch(seq),
        compiler_params=_params("arbitrary"),
        name="context_attention",
    )(sink, q, kvp)


def _lat_attn_kernel(sink_ref, q_ref, kvp_ref, ck_ref, cv_ref, o_ref, kctx, vctx, s_buf, p_buf, t_buf, *, past):
    n_blocks = q_ref.shape[0] // N_KV_HEADS
    win = 3 * BLOCK
    _init_pipeline_scratch(pl.program_id(0) == 0, (s_buf, p_buf), (t_buf,))

    cpl = _kv_placements(ck_ref[0].T, cv_ref[0].T)
    for idx in range(N_PLACEMENTS):
        kctx[idx] = cpl[:, idx * LANES:(idx + 1) * LANES]
        vctx[idx] = cpl[:, (N_PLACEMENTS + idx) * LANES:(N_PLACEMENTS + idx + 1) * LANES]

    row = lax.broadcasted_iota(jnp.int32, (SOFTMAX_ROWS, LANES), 0)
    c = lax.broadcasted_iota(jnp.int32, (SOFTMAX_ROWS, LANES), 1)

    def first_block(g):
        return jnp.clip(g // N_PLACEMENTS - 1, 0, n_blocks - 3)

    def window_of(g):
        return _key_rows(first_block(g) * BLOCK, win)

    def qk(g, u, valid):
        q2 = q_ref[g // 2]
        s_buf[u, :, 0:win] = _dot_nt(q2, kvp_ref[window_of(g), _placement_lanes(u, 0)])
        s_buf[u, :, win:win + past] = _dot_nt(q2, kctx[g % N_PLACEMENTS])

    def softmax(g, u, valid):
        shift = first_block(g) - (g // N_PLACEMENTS - 1)
        off = BLOCK * shift

        def masks_of(r0):
            r = row + r0 % BLOCK
            return {0: c >= r - off, 1: (c - r) * shift <= 0, 2: c <= r - off}

        _softmax_rows(s_buf, p_buf, t_buf, u, 2 * BLOCK, win + past, masks_of, _group_sinks(sink_ref, g))

    def pv(g, u, valid):
        acc = _dot(p_buf[u, :, 0:win], kvp_ref[window_of(g), _placement_lanes(u, 1)])
        acc = acc + _dot(p_buf[u, :, win:win + past], vctx[g % N_PLACEMENTS])
        _store_group(o_ref, g, u, _normalize(acc, t_buf, u))

    _software_pipeline(n_blocks * N_PLACEMENTS, PIPE_SLOTS, (qk, softmax, pv))


def _latent_attention(q, kvp, ck, cv, sink, n_batch, seq):
    past = ck.shape[2]
    stacked = pl.BlockSpec(_stacked_shape(seq), lambda b: (b, 0, 0))
    cache = pl.BlockSpec((1, KV_WIDTH, past), lambda b: (b, 0, 0))
    return pl.pallas_call(
        functools.partial(_lat_attn_kernel, past=past),
        grid=(n_batch,),
        in_specs=[
            pl.BlockSpec(memory_space=pltpu.SMEM),
            stacked,
            pl.BlockSpec((seq, KVP_WIDTH), lambda b: (b, 0)),
            cache,
            cache,
        ],
        out_specs=stacked,
        out_shape=jax.ShapeDtypeStruct(_stacked_shape(n_batch * seq), BF16),
        scratch_shapes=[pltpu.VMEM((N_PLACEMENTS, past, LANES), BF16)] * 2 + _attn_scratch(3 * BLOCK + past),
        compiler_params=_params("arbitrary"),
        name="latent_attention",
    )(sink, q, kvp, ck, cv)


_PK = 2 * GLA_DK
_PV = 2 * GLA_DV


def _cumsum_rows(x, reverse):
    n, width = x.shape
    row = lax.broadcasted_iota(jnp.int32, (SUBLANES, width), 0)
    blocks = [x[j:j + SUBLANES, :] for j in range(0, n, SUBLANES)]
    s = 1
    while s < SUBLANES:
        if reverse:
            blocks = [b + jnp.where(row < SUBLANES - s, pltpu.roll(b, SUBLANES - s, 0), 0.0) for b in blocks]
        else:
            blocks = [b + jnp.where(row >= s, pltpu.roll(b, s, 0), 0.0) for b in blocks]
        s *= 2
    edge = 0 if reverse else SUBLANES - 1
    order = range(len(blocks) - 1, -1, -1) if reverse else range(len(blocks))
    carry = None
    for j in order:
        if carry is not None:
            blocks[j] = blocks[j] + carry
        carry = jnp.broadcast_to(blocks[j][edge:edge + 1, :], (SUBLANES, width))
    return jnp.concatenate(blocks, axis=0)


def _gla_kernel(*refs, seq_chunks, has_init, emit_state):
    it = iter(refs)
    views = [tuple(next(it) for _ in range(3)) for _ in range(2)]
    s0_ref = next(it) if has_init else None
    out_refs = (next(it), next(it))
    sout_ref = next(it) if emit_state else None
    st_ref, a_buf, a2_buf, qe_buf, qd_buf, ds_buf, sb_buf, dc_buf = (next(it) for _ in range(8))
    n_chunks = views[0][0].shape[0] // CHUNK
    n_groups = 2 * n_chunks
    n_pairs = N_GLA_HEADS // 2
    tile = pl.program_id(1)
    q_scale = GLA_DK ** -0.5
    _init_pipeline_scratch(jnp.logical_and(pl.program_id(0) == 0, tile == 0),
                           (st_ref, a_buf, a2_buf, qe_buf, qd_buf, ds_buf, sb_buf), (dc_buf,))

    if has_init:
        @pl.when(tile == 0)
        def _():
            st_ref[...] = jnp.zeros_like(st_ref)
            for d in range(2):
                for h in range(N_GLA_HEADS):
                    p, e = divmod(h, 2)
                    st_ref[d, p, e * GLA_DK:(e + 1) * GLA_DK, e * GLA_DV:(e + 1) * GLA_DV] = s0_ref[0, d, h]

    lane_k = lax.broadcasted_iota(jnp.int32, (CHUNK, _PK), 1)
    row_k = lax.broadcasted_iota(jnp.int32, (CHUNK, _PK), 0)
    lo_k = lane_k < GLA_DK
    tri = (lane_k % CHUNK <= row_k, lane_k % CHUNK >= row_k)
    lo_v = lax.broadcasted_iota(jnp.int32, (CHUNK, _PV), 1) < GLA_DV
    diag = ((lax.broadcasted_iota(jnp.int32, (_PK, _PV), 0) < GLA_DK)
            == (lax.broadcasted_iota(jnp.int32, (_PK, _PV), 1) < GLA_DV))

    def place(g, d):
        c = g // 2
        if d == 0:
            return c, c
        seq = c // seq_chunks
        return c, seq * seq_chunks + (seq_chunks - 1 - (c - seq * seq_chunks))

    def rows_of(pos):
        return pl.ds(pl.multiple_of(pos * CHUNK, CHUNK), CHUNK)

    def prep(g, u, valid):
        d = u % 2
        qk_ref, g_ref, v_ref = views[d]
        _, pos = place(g, d)
        rows = rows_of(pos)
        gc = _cumsum_rows(g_ref[rows, :], reverse=(d == 1))
        gtot = gc[0:1, :] if d == 1 else gc[CHUNK - 1:CHUNK, :]
        k = qk_ref[rows, GLA_K_WIDTH:2 * GLA_K_WIDTH]
        qe = qk_ref[rows, 0:GLA_K_WIDTH] * jnp.exp(gc)
        qe_buf[u] = qe
        q_dec = (qe * q_scale).astype(BF16)
        k_inv = k * jnp.exp(-gc)
        e_tot = jnp.exp(gtot)
        k_end = (k_inv * jnp.where(valid, e_tot, 0.0)).astype(BF16)
        decay = jnp.where(valid, e_tot, 1.0)
        for p in range(n_pairs):
            lanes = slice(p * _PK, (p + 1) * _PK)
            ki = k_inv[:, lanes]
            kst = jnp.concatenate([jnp.where(lo_k, ki, 0.0), jnp.where(lo_k, 0.0, ki)], axis=0).astype(BF16)
            a_buf[u, p] = _dot_nt(q_dec[:, lanes], kst)
            ds_buf[u, p] = jnp.where(diag, _dot_tn(k_end[:, lanes], v_ref[rows, p * _PV:(p + 1) * _PV]), 0.0)
            dc_buf[u, p] = jnp.broadcast_to(decay[:, lanes], (_PK, _PK)).T

    def state(g, u, valid):
        d = u % 2
        c, _ = place(g, d)
        qd_buf[u] = (qe_buf[u] * q_scale).astype(BF16)
        for p in range(n_pairs):
            fresh = (c % seq_chunks == 0) if emit_state else (c < 0)
            s_in = jnp.where(fresh, 0.0, st_ref[d, p])
            sb_buf[u, p] = s_in.astype(BF16)
            a2_buf[u, p] = jnp.where(tri[d], a_buf[u, p], 0.0).astype(BF16)
            dc = dc_buf[u, p]
            s_new = s_in * jnp.concatenate([dc, dc], axis=1) + ds_buf[u, p]
            st_ref[d, p] = s_new
            if emit_state:
                for e in range(2):
                    sout_ref[c // seq_chunks, d, 2 * p + e] = s_new[e * GLA_DK:(e + 1) * GLA_DK,
                                                                    e * GLA_DV:(e + 1) * GLA_DV]

    def out(g, u, valid):
        d = u % 2
        _, _, v_ref = views[d]
        _, pos = place(g, d)
        rows = rows_of(pos)
        for p in range(n_pairs):
            vpf = v_ref[rows, p * _PV:(p + 1) * _PV].astype(F32)
            vbd = jnp.concatenate([jnp.where(lo_v, vpf, 0.0), jnp.where(lo_v, 0.0, vpf)], axis=0).astype(BF16)
            x = jnp.concatenate([a2_buf[u, p], qd_buf[u, :, p * _PK:(p + 1) * _PK]], axis=1)
            w = jnp.concatenate([vbd, sb_buf[u, p]], axis=0)
            out_refs[d][rows, p * _PV:(p + 1) * _PV] = _dot(x, w).astype(out_refs[d].dtype)

    _software_pipeline(n_groups, GLA_PIPE_SLOTS, (prep, state, out))


def _gla(qkg, gf, gb, vg, s0, n_batch, seq):
    tm = GLA_TILE
    long_seq = s0 is not None
    if long_seq:
        nt = seq // tm
        grid = (n_batch, nt)
        fwd = lambda b, i: (b * nt + i, 0)
        bwd = lambda b, i: (b * nt + nt - 1 - i, 0)
        seq_chunks = tm // CHUNK
    else:
        per_step = tm // seq
        grid = (1, n_batch // per_step)
        fwd = bwd = lambda b, i: (i, 0)
        seq_chunks = seq // CHUNK
    in_specs = []
    args = []
    for m, gates in ((fwd, gf), (bwd, gb)):
        in_specs += [pl.BlockSpec((tm, 2 * GLA_K_WIDTH), m), pl.BlockSpec((tm, GLA_K_WIDTH), m),
                     pl.BlockSpec((tm, GLA_V_WIDTH), m)]
        args += [qkg, gates, vg]
    n_tok = n_batch * seq
    out_specs = [pl.BlockSpec((tm, GLA_V_WIDTH), fwd), pl.BlockSpec((tm, GLA_V_WIDTH), bwd)]
    out_shape = [jax.ShapeDtypeStruct((n_tok, GLA_V_WIDTH), BF16)] * 2
    st_dims = (2, N_GLA_HEADS, GLA_DK, GLA_DV)
    if long_seq:
        in_specs.append(pl.BlockSpec((1,) + st_dims, lambda b, i: (b, 0, 0, 0, 0)))
        args.append(s0)
    else:
        out_specs.append(pl.BlockSpec((per_step,) + st_dims, lambda b, i: (i, 0, 0, 0, 0)))
        out_shape.append(jax.ShapeDtypeStruct((n_batch,) + st_dims, F32))
    n = GLA_PIPE_SLOTS
    n_pairs = N_GLA_HEADS // 2
    return pl.pallas_call(
        functools.partial(_gla_kernel, seq_chunks=seq_chunks, has_init=long_seq, emit_state=not long_seq),
        grid=grid,
        in_specs=in_specs,
        out_specs=out_specs,
        out_shape=out_shape,
        scratch_shapes=[pltpu.VMEM((2, n_pairs, _PK, _PV), F32),
                        pltpu.VMEM((n, n_pairs, CHUNK, _PK), F32),
                        pltpu.VMEM((n, n_pairs, CHUNK, _PK), BF16),
                        pltpu.VMEM((n, CHUNK, GLA_K_WIDTH), F32),
                        pltpu.VMEM((n, CHUNK, GLA_K_WIDTH), BF16),
                        pltpu.VMEM((n, n_pairs, _PK, _PV), F32),
                        pltpu.VMEM((n, n_pairs, _PK, _PV), BF16),
                        pltpu.VMEM((n, n_pairs, _PK, _PK), F32)],
        compiler_params=_params("arbitrary", "arbitrary"),
        name="gla_long" if long_seq else "gla_short",
    )(*args)


def _merge_kernel(x_ref, mod_ref, oa_ref, za_ref, of_ref, ob_ref, zg_ref, gng_ref, wo_ref, fng_ref, y_ref):
    ya = _load_stacked(oa_ref).astype(F32) * _silu(za_ref[...].astype(F32))
    og = of_ref[...].astype(F32) + ob_ref[...].astype(F32)
    zg = zg_ref[...].astype(F32)
    gng = gng_ref[...]
    parts = [ya]
    for h in range(N_GLA_HEADS):
        oh = og[:, h * GLA_DV:(h + 1) * GLA_DV]
        nh = oh * lax.rsqrt(jnp.mean(oh * oh, axis=-1, keepdims=True) + EPS)
        parts.append((nh * gng) * _silu(zg[:, h * GLA_DV:(h + 1) * GLA_DV]))
    yin = jnp.concatenate(parts, axis=-1).astype(BF16)
    gate = mod_ref[0][:, 2 * D_MODEL:3 * D_MODEL]
    y = x_ref[...] + gate * _dot(yin, wo_ref[...])
    y_ref[...] = (y * lax.rsqrt(jnp.mean(y * y, axis=-1, keepdims=True) + EPS)) * fng_ref[...]


def _merge(x2d, mod3, mod_row_of_tile, oa, za, of, ob, zg, gla_norm_g, w_out, final_norm_g, tm):
    n_tok = x2d.shape[0]
    row = lambda i: (i, 0)
    const = lambda i: (0, 0)
    ring = pl.Buffered(MERGE_INPUT_BUFFERS)
    wide = pl.BlockSpec((tm, GLA_V_WIDTH), row, pipeline_mode=ring)
    stream_specs = [
        pl.BlockSpec((tm, D_MODEL), row, pipeline_mode=ring),
        pl.BlockSpec((1, 1, 3 * D_MODEL), lambda i: (mod_row_of_tile(i), 0, 0)),
        pl.BlockSpec(_stacked_shape(tm), lambda i: (i, 0, 0), pipeline_mode=ring), wide, wide, wide, wide,
    ]

    def call(*refs):
        streams, (gng_ref, wo_ref, fng_ref, y_hbm) = refs[:len(stream_specs)], refs[len(stream_specs):]

        def step(*tiles):
            _merge_kernel(*tiles[:-1], gng_ref, wo_ref, fng_ref, tiles[-1])

        pltpu.emit_pipeline(step, grid=(n_tok // tm,), in_specs=stream_specs,
                            out_specs=[pl.BlockSpec((tm, D_MODEL), row)])(*streams, y_hbm)

    hbm = pl.BlockSpec(memory_space=pl.ANY)
    vmem = pl.BlockSpec(memory_space=pltpu.VMEM)
    return pl.pallas_call(
        call,
        in_specs=[hbm] * len(stream_specs) + [vmem] * 3,
        out_specs=hbm,
        out_shape=jax.ShapeDtypeStruct((n_tok, D_MODEL), F32),
        compiler_params=_params(),
        name="merge",
    )(x2d, mod3, oa, za, of, ob, zg, gla_norm_g, w_out, final_norm_g)


def _rope_tables(n_tok):
    t = np.arange(n_tok)
    rowp = (t // GRID_W).astype(np.float64)
    colp = (t % GRID_W).astype(np.float64)
    nf = HEAD_DIM // 4
    freqs = ROPE_BASE ** (-np.arange(nf, dtype=np.float64) / nf)
    lane = np.arange(LANES)
    pos = np.where(((lane % HEAD_DIM) < HEAD_DIM // 2)[None, :], rowp[:, None], colp[:, None])
    ang = pos * freqs[lane % nf][None, :]
    sign = np.where((lane % (2 * nf)) < nf, -1.0, 1.0)
    return jnp.asarray(np.cos(ang), F32), jnp.asarray(np.sin(ang) * sign[None, :], F32)


def kernel(x_prompt, x_sample, c, cache_k, cache_v, state_gla, c_ctx, w_mod, b_mod, norm_g, w_in,
           w_gk_f, b_gk_f, w_gk_b, b_gk_b, sink, gla_norm_g, w_out, final_norm_g):
    depth = w_in.shape[0]
    assert depth == 1, "single-layer step"
    l = 0
    bp, sp, _ = x_prompt.shape
    bs, ss, _ = x_sample.shape
    assert (bp * sp) % TOKEN_TILE == 0 and TOKEN_TILE % sp == 0 and ss % TOKEN_TILE == 0
    assert (bp * sp) % MERGE_TILE == 0 and ss % MERGE_TILE == 0
    assert bp % CTX_ATTN_BATCH == 0 and sp % BLOCK == 0 and ss % BLOCK == 0 and ss // BLOCK >= 3
    assert GLA_TILE % sp == 0 and (bp * sp) % GLA_TILE == 0 and ss % GLA_TILE == 0 and sp % CHUNK == 0
    assert ss % GRID_W == 0 and cache_k.shape[2] % LANES == 0

    wt = w_in[l].T.astype(BF16)
    wgk = jnp.zeros((LANES, 2 * GLA_K_WIDTH), F32)
    wgk = wgk.at[0:GATE_RANK, 0:GLA_K_WIDTH].set(w_gk_f[l])
    wgk = wgk.at[GATE_RANK:2 * GATE_RANK, GLA_K_WIDTH:].set(w_gk_b[l]).astype(BF16)
    bgk = jnp.concatenate([b_gk_f[l], b_gk_b[l]])[None, :]
    wo = w_out[l].astype(BF16)
    ng = norm_g[l][None, :]
    gng = gla_norm_g[l][None, :]
    fng = final_norm_g[None, :]
    sk = sink[l]

    n_rows = SUBLANES
    assert 1 + bs <= n_rows
    cvecs = jnp.concatenate([c_ctx[None, :], c, jnp.zeros((n_rows - 1 - bs, D_MODEL), F32)], axis=0)
    mod3 = _modulation(cvecs, w_mod[l], b_mod[l][None, :]).reshape(n_rows, 1, 3 * D_MODEL)
    tm = TOKEN_TILE

    xp2 = x_prompt.reshape(bp * sp, D_MODEL)
    ctx_row = lambda i: 0
    q, kvp, za, vg, zg, qkg, gf, gb, kt, vt = _project(xp2, mod3, ctx_row, ng, wt, wgk, bgk, None, tm, sp)
    oa = _context_attention(q, kvp, sk, sp)
    of, ob, st = _gla(qkg, gf, gb, vg, None, bp, sp)
    y_prompt = _merge(xp2, mod3, ctx_row, oa, za, of, ob, zg, gng, wo, fng, MERGE_TILE).reshape(bp, sp, D_MODEL)
    to_cache = lambda t: t.reshape(bp, 1, N_KV_HEADS, HEAD_DIM, sp).transpose(0, 1, 4, 2, 3)
    new_k, new_v = to_cache(kt), to_cache(vt)
    new_state = st[:, None]

    xs2 = x_sample.reshape(bs * ss, D_MODEL)
    tiles_per_seq = ss // tm
    lat_row = lambda i: 1 + i // tiles_per_seq
    q, kvp, za, vg, zg, qkg, gf, gb = _project(xs2, mod3, lat_row, ng, wt, wgk, bgk, _rope_tables(ss), tm, ss)
    past = cache_k.shape[2]
    from_cache = lambda t: t[:, l].transpose(0, 2, 3, 1).reshape(bs, KV_WIDTH, past)
    ck, cv = from_cache(cache_k), from_cache(cache_v)
    oa = _latent_attention(q, kvp, ck, cv, sk, bs, ss)
    of, ob = _gla(qkg, gf, gb, vg, state_gla[:, l], bs, ss)
    merge_row = lambda i: 1 + i // (ss // MERGE_TILE)
    y_sample = _merge(xs2, mod3, merge_row, oa, za, of, ob, zg, gng, wo, fng, MERGE_TILE).reshape(bs, ss, D_MODEL)

    return (y_prompt, y_sample, new_k, new_v, new_state)
```

```python
import functools
import math

import numpy as np
import jax
import jax.numpy as jnp
from jax import lax
from jax.experimental import pallas as pl
from jax.experimental.pallas import tpu as pltpu

D_MODEL = 1024
GRID_W = 64
HEAD_DIM = 64
N_ATTN_HEADS = 8
N_KV_HEADS = 2
ATTN_WIDTH = 512
KV_WIDTH = 128
BLOCK = 128
N_GLA_HEADS = 4
GLA_DK = 64
GLA_DV = 128
GLA_K_WIDTH = 256
GLA_V_WIDTH = 512
GATE_RANK = 16
GATE_NORMALIZER = 16.0
CHUNK = 64
ROPE_BASE = 10000.0
EPS = 1e-6
NEG_INF = -1e30

KV_GROUP = N_ATTN_HEADS // N_KV_HEADS
LANES = 128
SUBLANES = 8
TOKEN_TILE = 1024
MERGE_TILE = 512
MERGE_INPUT_BUFFERS = 3
GLA_TILE = 2048
CTX_ATTN_BATCH = 8
PIPE_SLOTS = 4
GLA_PIPE_SLOTS = 4
KVP_WIDTH = 8 * LANES
N_PLACEMENTS = 2 * N_KV_HEADS
SOFTMAX_ROWS = 32
MOD_TILE = 1024
VMEM_LIMIT = 56 * 1024 * 1024
LOG2E = math.log2(math.e)

F32 = jnp.float32
BF16 = jnp.bfloat16


def _dot(a, b):
    return jnp.dot(a, b, preferred_element_type=F32)


def _dot_nt(a, b):
    return lax.dot_general(a, b, (((1,), (1,)), ((), ())), preferred_element_type=F32)


def _dot_tn(a, b):
    return lax.dot_general(a, b, (((0,), (0,)), ((), ())), preferred_element_type=F32)


def _silu(x):
    h = 0.5 * x
    return h + h * jnp.tanh(h)


def _log_sigmoid(x):
    return jnp.minimum(x, 0.0) - jnp.log1p(jnp.exp(-jnp.abs(x)))


def _params(*sem):
    return pltpu.CompilerParams(dimension_semantics=sem, vmem_limit_bytes=VMEM_LIMIT)


def _split_bf16(x):
    hi = x.astype(BF16)
    return hi, (x - hi.astype(F32)).astype(BF16)


def _mod_kernel(c_ref, w_ref, b_ref, o_ref):
    rows = c_ref.shape[0]
    s_hi, s_lo = _split_bf16(_silu(c_ref[...]))
    w_hi, w_lo = _split_bf16(w_ref[...])
    by_hi = _dot(jnp.concatenate([s_hi, s_lo], axis=0), w_hi)
    o_ref[...] = by_hi[0:rows] + by_hi[rows:2 * rows] + _dot(s_hi, w_lo) + b_ref[...]


def _modulation(cvecs, w_mod, b_mod):
    rows = cvecs.shape[0]
    n_out = w_mod.shape[1]
    return pl.pallas_call(
        _mod_kernel,
        grid=(n_out // MOD_TILE,),
        in_specs=[
            pl.BlockSpec((rows, D_MODEL), lambda j: (0, 0)),
            pl.BlockSpec((D_MODEL, MOD_TILE), lambda j: (0, j)),
            pl.BlockSpec((1, MOD_TILE), lambda j: (0, j)),
        ],
        out_specs=pl.BlockSpec((rows, MOD_TILE), lambda j: (0, j)),
        out_shape=jax.ShapeDtypeStruct((rows, n_out), F32),
        compiler_params=_params("arbitrary"),
        name="adaln_mod",
    )(cvecs, w_mod, b_mod)


_Q0, _K0, _V0, _ZA0, _QG0, _KG0, _VG0, _TAIL0 = 0, 512, 640, 768, 1280, 1536, 1792, 2304
_TAIL_COLS = 2 * GATE_RANK + GLA_V_WIDTH
_Q_PRESCALE = (HEAD_DIM ** -0.5) * LOG2E


def _rope(x, cos, sin_signed):
    n = x.shape[1] // LANES
    lane = lax.broadcasted_iota(jnp.int32, (x.shape[0], LANES), 1)
    first = (lane % 32) < 16
    outs = []
    for c in range(n):
        xc = x[:, c * LANES:(c + 1) * LANES]
        partner = jnp.where(first, pltpu.roll(xc, LANES - 16, 1), pltpu.roll(xc, 16, 1))
        outs.append(xc * cos + partner * sin_signed)
    return outs[0] if n == 1 else jnp.concatenate(outs, axis=-1)


def _kv_placements(k, v):
    lane = lax.broadcasted_iota(jnp.int32, k.shape, 1)
    lo = lane < HEAD_DIM
    kr = pltpu.roll(k, HEAD_DIM, 1)
    vr = pltpu.roll(v, HEAD_DIM, 1)
    pieces = [jnp.where(lo, k, 0.0), jnp.where(lo, 0.0, kr), jnp.where(lo, kr, 0.0), jnp.where(lo, 0.0, k),
              jnp.where(lo, v, 1.0), jnp.where(lo, 1.0, vr), jnp.where(lo, vr, 1.0), jnp.where(lo, 1.0, v)]
    return jnp.concatenate(pieces, axis=-1).astype(BF16)


def _proj_kernel(*refs, rope):
    if rope:
        (x_ref, mod_ref, ng_ref, wt_ref, wgk_ref, bgk_ref, cos_ref, sin_ref,
         q_ref, kvp_ref, za_ref, vg_ref, zg_ref, qkg_ref, gf_ref, gb_ref) = refs
    else:
        (x_ref, mod_ref, ng_ref, wt_ref, wgk_ref, bgk_ref,
         q_ref, kvp_ref, za_ref, vg_ref, zg_ref, qkg_ref, gf_ref, gb_ref, k_ref, v_ref) = refs
    x = x_ref[...]
    xn = x * lax.rsqrt(jnp.mean(x * x, axis=-1, keepdims=True) + EPS)
    mod = mod_ref[0]
    shift = mod[:, 0:D_MODEL]
    scale = mod[:, D_MODEL:2 * D_MODEL]
    h = ((xn * ng_ref[...]) * (1.0 + scale) + shift).astype(BF16)

    def piece(c0, c1):
        return _dot_nt(h, wt_ref[c0:c1, :])

    q = piece(_Q0, _K0)
    kv = piece(_K0, _ZA0)
    k = kv[:, 0:KV_WIDTH]
    v = kv[:, KV_WIDTH:2 * KV_WIDTH]
    if rope:
        cos = cos_ref[...]
        sin = sin_ref[...]
        q = _rope(q, cos, sin)
        k = _rope(k, cos, sin)
    else:
        seq = k_ref.shape[2]
        for b in range(k_ref.shape[0]):
            k_ref[b] = k[b * seq:(b + 1) * seq, :].T
            v_ref[b] = v[b * seq:(b + 1) * seq, :].T
    _store_stacked((q * _Q_PRESCALE).astype(BF16), q_ref)
    kvp_ref[...] = _kv_placements(k, v)
    za_ref[...] = piece(_ZA0, _QG0).astype(BF16)
    vg_ref[...] = piece(_VG0, _TAIL0).astype(BF16)
    qkg_ref[...] = piece(_QG0, _VG0)
    tail = piece(_TAIL0, _TAIL0 + _TAIL_COLS)
    zg_ref[...] = tail[:, 2 * GATE_RANK:_TAIL_COLS].astype(BF16)
    lane = lax.broadcasted_iota(jnp.int32, (tail.shape[0], LANES), 1)
    lr = jnp.where(lane < 2 * GATE_RANK, tail[:, 0:LANES], 0.0).astype(BF16)
    g = _log_sigmoid(_dot(lr, wgk_ref[...]) + bgk_ref[...]) * (1.0 / GATE_NORMALIZER)
    gf_ref[...] = g[:, 0:GLA_K_WIDTH]
    gb_ref[...] = g[:, GLA_K_WIDTH:2 * GLA_K_WIDTH]


def _project(x2d, mod3, mod_row_of_tile, norm_g, wt, wgk, bgk, rope_tabs, tm, seq):
    n_tok = x2d.shape[0]
    rope = rope_tabs is not None
    row = lambda i: (i, 0)
    const = lambda i: (0, 0)
    in_specs = [
        pl.BlockSpec((tm, D_MODEL), row),
        pl.BlockSpec((1, 1, 3 * D_MODEL), lambda i: (mod_row_of_tile(i), 0, 0)),
        pl.BlockSpec((1, D_MODEL), const),
        pl.BlockSpec((_TAIL0 + _TAIL_COLS, D_MODEL), const),
        pl.BlockSpec((LANES, 2 * GLA_K_WIDTH), const),
        pl.BlockSpec((1, 2 * GLA_K_WIDTH), const),
    ]
    args = [x2d, mod3, norm_g, wt, wgk, bgk]
    if rope:
        tiles_per_seq = rope_tabs[0].shape[0] // tm
        pos = lambda i: (i % tiles_per_seq, 0)
        in_specs += [pl.BlockSpec((tm, LANES), pos), pl.BlockSpec((tm, LANES), pos)]
        args += list(rope_tabs)
    widths = [(KVP_WIDTH, BF16), (ATTN_WIDTH, BF16), (GLA_V_WIDTH, BF16), (GLA_V_WIDTH, BF16),
              (2 * GLA_K_WIDTH, F32), (GLA_K_WIDTH, F32), (GLA_K_WIDTH, F32)]
    out_specs = [pl.BlockSpec(_stacked_shape(tm), lambda i: (i, 0, 0))] + [pl.BlockSpec((tm, w), row) for w, _ in widths]
    out_shape = ([jax.ShapeDtypeStruct(_stacked_shape(n_tok), BF16)]
                 + [jax.ShapeDtypeStruct((n_tok, w), dt) for w, dt in widths])
    if not rope:
        for _ in range(2):
            out_specs.append(pl.BlockSpec((tm // seq, KV_WIDTH, seq), lambda i: (i, 0, 0)))
            out_shape.append(jax.ShapeDtypeStruct((n_tok // seq, KV_WIDTH, seq), F32))
    return pl.pallas_call(
        functools.partial(_proj_kernel, rope=rope),
        grid=(n_tok // tm,),
        in_specs=in_specs,
        out_specs=out_specs,
        out_shape=out_shape,
        compiler_params=_params("arbitrary"),
        name="project_rope" if rope else "project",
    )(*args)


def _software_pipeline(n_groups, n_slots, stages):
    last_g = n_groups - 1

    def body(k, carry):
        for age in (2, 1, 0):
            for u in range(n_slots):
                g = n_slots * (k - age) + u
                stages[age](jnp.clip(g, 0, last_g), u, jnp.logical_and(g >= 0, g <= last_g))
        return carry

    lax.fori_loop(0, n_groups // n_slots + 2, body, 0)


def _init_pipeline_scratch(first, zero_refs, one_refs=()):
    @pl.when(first)
    def _():
        for ref in zero_refs:
            ref[...] = jnp.zeros_like(ref)
        for ref in one_refs:
            ref[...] = jnp.ones_like(ref)


def _key_rows(start, size):
    return pl.ds(pl.multiple_of(start, BLOCK), size)


def _softmax_rows(s_buf, p_buf, t_buf, slot, n_rows, n_keys, masks_of, sink_of):
    for r0 in range(0, n_rows, SOFTMAX_ROWS):
        rs = slice(r0, r0 + SOFTMAX_ROWS)
        masks = masks_of(r0)
        cols = []
        for c in range(n_keys // LANES):
            sc = s_buf[slot, rs, c * LANES:(c + 1) * LANES]
            cols.append(jnp.where(masks[c], sc, NEG_INF) if c in masks else sc)
        mx = cols[0]
        for sc in cols[1:]:
            mx = jnp.maximum(mx, sc)
        sink = sink_of(r0)
        m = jnp.maximum(jnp.max(mx, axis=-1, keepdims=True), sink)
        for c, sc in enumerate(cols):
            p_buf[slot, rs, c * LANES:(c + 1) * LANES] = jnp.exp2(sc - m).astype(BF16)
        t_buf[slot, rs, :] = jnp.broadcast_to(jnp.exp2(sink - m), (SOFTMAX_ROWS, LANES))


def _normalize(acc, t_buf, slot):
    return acc / (pltpu.roll(acc, HEAD_DIM, 1) + t_buf[slot])


def _attn_scratch(n_keys):
    rows = 2 * BLOCK
    return [pltpu.VMEM((PIPE_SLOTS, rows, n_keys), F32), pltpu.VMEM((PIPE_SLOTS, rows, n_keys), BF16),
            pltpu.VMEM((PIPE_SLOTS, rows, LANES), F32)]


def _stacked_shape(n_tok):
    return (n_tok // BLOCK * N_KV_HEADS, 2 * BLOCK, LANES)


def _store_stacked(x, ref):
    for blk in range(x.shape[0] // BLOCK):
        for j in range(N_KV_HEADS):
            for e in range(2):
                ref[blk * N_KV_HEADS + j, e * BLOCK:(e + 1) * BLOCK, :] = (
                    x[blk * BLOCK:(blk + 1) * BLOCK, (2 * j + e) * LANES:(2 * j + e + 1) * LANES])


def _load_stacked(ref):
    blocks = []
    for blk in range(ref.shape[0] // N_KV_HEADS):
        blocks.append(jnp.concatenate([ref[blk * N_KV_HEADS + j, e * BLOCK:(e + 1) * BLOCK, :]
                                       for j in range(N_KV_HEADS) for e in range(2)], axis=-1))
    return jnp.concatenate(blocks, axis=0)


assert PIPE_SLOTS % N_PLACEMENTS == 0


def _placement_lanes(u, is_v):
    idx = is_v * N_PLACEMENTS + u % N_PLACEMENTS
    return slice(idx * LANES, (idx + 1) * LANES)


def _store_group(o_ref, g, u, o):
    half = slice((u % 2) * HEAD_DIM, (u % 2 + 1) * HEAD_DIM)
    o_ref[g // 2, :, half] = o[:, half].astype(o_ref.dtype)


def _group_sinks(sink_ref, g):
    j, e = (g // 2) % N_KV_HEADS, g % 2
    sink_a = sink_ref[KV_GROUP * j + e] * LOG2E
    sink_b = sink_ref[KV_GROUP * j + 2 + e] * LOG2E
    return lambda r0: sink_a if r0 < BLOCK else sink_b


def _ctx_attn_kernel(sink_ref, q_ref, kvp_ref, o_ref, s_buf, p_buf, t_buf, *, seq):
    n_blocks = q_ref.shape[0] // N_KV_HEADS
    _init_pipeline_scratch(pl.program_id(0) == 0, (s_buf, p_buf), (t_buf,))
    blocks_per_seq = seq // BLOCK

    def keys_of(g):
        return _key_rows(((g // N_PLACEMENTS) // blocks_per_seq) * seq, seq)

    def qk(g, u, valid):
        s_buf[u] = _dot_nt(q_ref[g // 2], kvp_ref[keys_of(g), _placement_lanes(u, 0)])

    def softmax(g, u, valid):
        _softmax_rows(s_buf, p_buf, t_buf, u, 2 * BLOCK, seq, lambda r0: {}, _group_sinks(sink_ref, g))

    def pv(g, u, valid):
        _store_group(o_ref, g, u, _normalize(_dot(p_buf[u], kvp_ref[keys_of(g), _placement_lanes(u, 1)]), t_buf, u))

    _software_pipeline(n_blocks * N_PLACEMENTS, PIPE_SLOTS, (qk, softmax, pv))


def _context_attention(q, kvp, sink, seq):
    n_tok = kvp.shape[0]
    rows = CTX_ATTN_BATCH * seq
    row = lambda b: (b, 0)
    stacked = pl.BlockSpec(_stacked_shape(rows), lambda b: (b, 0, 0))
    return pl.pallas_call(
        functools.partial(_ctx_attn_kernel, seq=seq),
        grid=(n_tok // rows,),
        in_specs=[
            pl.BlockSpec(memory_space=pltpu.SMEM),
            stacked,
            pl.BlockSpec((rows, KVP_WIDTH), row),
        ],
        out_specs=stacked,
        out_shape=jax.ShapeDtypeStruct(_stacked_shape(n_tok), BF16),
        scratch_shapes=_attn_scratch(seq),
        compiler_params=_params("arbitrary"),
        name="context_attention",
    )(sink, q, kvp)


def _lat_attn_kernel(sink_ref, q_ref, kvp_ref, ck_ref, cv_ref, o_ref, kctx, vctx, s_buf, p_buf, t_buf, *, past):
    n_blocks = q_ref.shape[0] // N_KV_HEADS
    win = 3 * BLOCK
    _init_pipeline_scratch(pl.program_id(0) == 0, (s_buf, p_buf), (t_buf,))

    cpl = _kv_placements(ck_ref[0].T, cv_ref[0].T)
    for idx in range(N_PLACEMENTS):
        kctx[idx] = cpl[:, idx * LANES:(idx + 1) * LANES]
        vctx[idx] = cpl[:, (N_PLACEMENTS + idx) * LANES:(N_PLACEMENTS + idx + 1) * LANES]

    row = lax.broadcasted_iota(jnp.int32, (SOFTMAX_ROWS, LANES), 0)
    c = lax.broadcasted_iota(jnp.int32, (SOFTMAX_ROWS, LANES), 1)

    def first_block(g):
        return jnp.clip(g // N_PLACEMENTS - 1, 0, n_blocks - 3)

    def window_of(g):
        return _key_rows(first_block(g) * BLOCK, win)

    def qk(g, u, valid):
        q2 = q_ref[g // 2]
        s_buf[u, :, 0:win] = _dot_nt(q2, kvp_ref[window_of(g), _placement_lanes(u, 0)])
        s_buf[u, :, win:win + past] = _dot_nt(q2, kctx[g % N_PLACEMENTS])

    def softmax(g, u, valid):
        shift = first_block(g) - (g // N_PLACEMENTS - 1)
        off = BLOCK * shift

        def masks_of(r0):
            r = row + r0 % BLOCK
            return {0: c >= r - off, 1: (c - r) * shift <= 0, 2: c <= r - off}

        _softmax_rows(s_buf, p_buf, t_buf, u, 2 * BLOCK, win + past, masks_of, _group_sinks(sink_ref, g))

    def pv(g, u, valid):
        acc = _dot(p_buf[u, :, 0:win], kvp_ref[window_of(g), _placement_lanes(u, 1)])
        acc = acc + _dot(p_buf[u, :, win:win + past], vctx[g % N_PLACEMENTS])
        _store_group(o_ref, g, u, _normalize(acc, t_buf, u))

    _software_pipeline(n_blocks * N_PLACEMENTS, PIPE_SLOTS, (qk, softmax, pv))


def _latent_attention(q, kvp, ck, cv, sink, n_batch, seq):
    past = ck.shape[2]
    stacked = pl.BlockSpec(_stacked_shape(seq), lambda b: (b, 0, 0))
    cache = pl.BlockSpec((1, KV_WIDTH, past), lambda b: (b, 0, 0))
    return pl.pallas_call(
        functools.partial(_lat_attn_kernel, past=past),
        grid=(n_batch,),
        in_specs=[
            pl.BlockSpec(memory_space=pltpu.SMEM),
            stacked,
            pl.BlockSpec((seq, KVP_WIDTH), lambda b: (b, 0)),
            cache,
            cache,
        ],
        out_specs=stacked,
        out_shape=jax.ShapeDtypeStruct(_stacked_shape(n_batch * seq), BF16),
        scratch_shapes=[pltpu.VMEM((N_PLACEMENTS, past, LANES), BF16)] * 2 + _attn_scratch(3 * BLOCK + past),
        compiler_params=_params("arbitrary"),
        name="latent_attention",
    )(sink, q, kvp, ck, cv)


_PK = 2 * GLA_DK
_PV = 2 * GLA_DV


def _cumsum_rows(x, reverse):
    n, width = x.shape
    row = lax.broadcasted_iota(jnp.int32, (SUBLANES, width), 0)
    blocks = [x[j:j + SUBLANES, :] for j in range(0, n, SUBLANES)]
    s = 1
    while s < SUBLANES:
        if reverse:
            blocks = [b + jnp.where(row < SUBLANES - s, pltpu.roll(b, SUBLANES - s, 0), 0.0) for b in blocks]
        else:
            blocks = [b + jnp.where(row >= s, pltpu.roll(b, s, 0), 0.0) for b in blocks]
        s *= 2
    edge = 0 if reverse else SUBLANES - 1
    order = range(len(blocks) - 1, -1, -1) if reverse else range(len(blocks))
    carry = None
    for j in order:
        if carry is not None:
            blocks[j] = blocks[j] + carry
        carry = jnp.broadcast_to(blocks[j][edge:edge + 1, :], (SUBLANES, width))
    return jnp.concatenate(blocks, axis=0)


def _gla_kernel(*refs, seq_chunks, has_init, emit_state):
    it = iter(refs)
    views = [tuple(next(it) for _ in range(3)) for _ in range(2)]
    s0_ref = next(it) if has_init else None
    out_refs = (next(it), next(it))
    sout_ref = next(it) if emit_state else None
    st_ref, a_buf, a2_buf, qe_buf, qd_buf, ds_buf, sb_buf, dc_buf = (next(it) for _ in range(8))
    n_chunks = views[0][0].shape[0] // CHUNK
    n_groups = 2 * n_chunks
    n_pairs = N_GLA_HEADS // 2
    tile = pl.program_id(1)
    q_scale = GLA_DK ** -0.5
    _init_pipeline_scratch(jnp.logical_and(pl.program_id(0) == 0, tile == 0),
                           (st_ref, a_buf, a2_buf, qe_buf, qd_buf, ds_buf, sb_buf), (dc_buf,))

    if has_init:
        @pl.when(tile == 0)
        def _():
            st_ref[...] = jnp.zeros_like(st_ref)
            for d in range(2):
                for h in range(N_GLA_HEADS):
                    p, e = divmod(h, 2)
                    st_ref[d, p, e * GLA_DK:(e + 1) * GLA_DK, e * GLA_DV:(e + 1) * GLA_DV] = s0_ref[0, d, h]

    lane_k = lax.broadcasted_iota(jnp.int32, (CHUNK, _PK), 1)
    row_k = lax.broadcasted_iota(jnp.int32, (CHUNK, _PK), 0)
    lo_k = lane_k < GLA_DK
    tri = (lane_k % CHUNK <= row_k, lane_k % CHUNK >= row_k)
    lo_v = lax.broadcasted_iota(jnp.int32, (CHUNK, _PV), 1) < GLA_DV
    diag = ((lax.broadcasted_iota(jnp.int32, (_PK, _PV), 0) < GLA_DK)
            == (lax.broadcasted_iota(jnp.int32, (_PK, _PV), 1) < GLA_DV))

    def place(g, d):
        c = g // 2
        if d == 0:
            return c, c
        seq = c // seq_chunks
        return c, seq * seq_chunks + (seq_chunks - 1 - (c - seq * seq_chunks))

    def rows_of(pos):
        return pl.ds(pl.multiple_of(pos * CHUNK, CHUNK), CHUNK)

    def prep(g, u, valid):
        d = u % 2
        qk_ref, g_ref, v_ref = views[d]
        _, pos = place(g, d)
        rows = rows_of(pos)
        gc = _cumsum_rows(g_ref[rows, :], reverse=(d == 1))
        gtot = gc[0:1, :] if d == 1 else gc[CHUNK - 1:CHUNK, :]
        k = qk_ref[rows, GLA_K_WIDTH:2 * GLA_K_WIDTH]
        qe = qk_ref[rows, 0:GLA_K_WIDTH] * jnp.exp(gc)
        qe_buf[u] = qe
        q_dec = (qe * q_scale).astype(BF16)
        k_inv = k * jnp.exp(-gc)
        e_tot = jnp.exp(gtot)
        k_end = (k_inv * jnp.where(valid, e_tot, 0.0)).astype(BF16)
        decay = jnp.where(valid, e_tot, 1.0)
        for p in range(n_pairs):
            lanes = slice(p * _PK, (p + 1) * _PK)
            ki = k_inv[:, lanes]
            kst = jnp.concatenate([jnp.where(lo_k, ki, 0.0), jnp.where(lo_k, 0.0, ki)], axis=0).astype(BF16)
            a_buf[u, p] = _dot_nt(q_dec[:, lanes], kst)
            ds_buf[u, p] = jnp.where(diag, _dot_tn(k_end[:, lanes], v_ref[rows, p * _PV:(p + 1) * _PV]), 0.0)
            dc_buf[u, p] = jnp.broadcast_to(decay[:, lanes], (_PK, _PK)).T

    def state(g, u, valid):
        d = u % 2
        c, _ = place(g, d)
        qd_buf[u] = (qe_buf[u] * q_scale).astype(BF16)
        for p in range(n_pairs):
            fresh = (c % seq_chunks == 0) if emit_state else (c < 0)
            s_in = jnp.where(fresh, 0.0, st_ref[d, p])
            sb_buf[u, p] = s_in.astype(BF16)
            a2_buf[u, p] = jnp.where(tri[d], a_buf[u, p], 0.0).astype(BF16)
            dc = dc_buf[u, p]
            s_new = s_in * jnp.concatenate([dc, dc], axis=1) + ds_buf[u, p]
            st_ref[d, p] = s_new
            if emit_state:
                for e in range(2):
                    sout_ref[c // seq_chunks, d, 2 * p + e] = s_new[e * GLA_DK:(e + 1) * GLA_DK,
                                                                    e * GLA_DV:(e + 1) * GLA_DV]

    def out(g, u, valid):
        d = u % 2
        _, _, v_ref = views[d]
        _, pos = place(g, d)
        rows = rows_of(pos)
        for p in range(n_pairs):
            vpf = v_ref[rows, p * _PV:(p + 1) * _PV].astype(F32)
            vbd = jnp.concatenate([jnp.where(lo_v, vpf, 0.0), jnp.where(lo_v, 0.0, vpf)], axis=0).astype(BF16)
            x = jnp.concatenate([a2_buf[u, p], qd_buf[u, :, p * _PK:(p + 1) * _PK]], axis=1)
            w = jnp.concatenate([vbd, sb_buf[u, p]], axis=0)
            out_refs[d][rows, p * _PV:(p + 1) * _PV] = _dot(x, w).astype(out_refs[d].dtype)

    _software_pipeline(n_groups, GLA_PIPE_SLOTS, (prep, state, out))


def _gla(qkg, gf, gb, vg, s0, n_batch, seq):
    tm = GLA_TILE
    long_seq = s0 is not None
    if long_seq:
        nt = seq // tm
        grid = (n_batch, nt)
        fwd = lambda b, i: (b * nt + i, 0)
        bwd = lambda b, i: (b * nt + nt - 1 - i, 0)
        seq_chunks = tm // CHUNK
    else:
        per_step = tm // seq
        grid = (1, n_batch // per_step)
        fwd = bwd = lambda b, i: (i, 0)
        seq_chunks = seq // CHUNK
    in_specs = []
    args = []
    for m, gates in ((fwd, gf), (bwd, gb)):
        in_specs += [pl.BlockSpec((tm, 2 * GLA_K_WIDTH), m), pl.BlockSpec((tm, GLA_K_WIDTH), m),
                     pl.BlockSpec((tm, GLA_V_WIDTH), m)]
        args += [qkg, gates, vg]
    n_tok = n_batch * seq
    out_specs = [pl.BlockSpec((tm, GLA_V_WIDTH), fwd), pl.BlockSpec((tm, GLA_V_WIDTH), bwd)]
    out_shape = [jax.ShapeDtypeStruct((n_tok, GLA_V_WIDTH), BF16)] * 2
    st_dims = (2, N_GLA_HEADS, GLA_DK, GLA_DV)
    if long_seq:
        in_specs.append(pl.BlockSpec((1,) + st_dims, lambda b, i: (b, 0, 0, 0, 0)))
        args.append(s0)
    else:
        out_specs.append(pl.BlockSpec((per_step,) + st_dims, lambda b, i: (i, 0, 0, 0, 0)))
        out_shape.append(jax.ShapeDtypeStruct((n_batch,) + st_dims, F32))
    n = GLA_PIPE_SLOTS
    n_pairs = N_GLA_HEADS // 2
    return pl.pallas_call(
        functools.partial(_gla_kernel, seq_chunks=seq_chunks, has_init=long_seq, emit_state=not long_seq),
        grid=grid,
        in_specs=in_specs,
        out_specs=out_specs,
        out_shape=out_shape,
        scratch_shapes=[pltpu.VMEM((2, n_pairs, _PK, _PV), F32),
                        pltpu.VMEM((n, n_pairs, CHUNK, _PK), F32),
                        pltpu.VMEM((n, n_pairs, CHUNK, _PK), BF16),
                        pltpu.VMEM((n, CHUNK, GLA_K_WIDTH), F32),
                        pltpu.VMEM((n, CHUNK, GLA_K_WIDTH), BF16),
                        pltpu.VMEM((n, n_pairs, _PK, _PV), F32),
                        pltpu.VMEM((n, n_pairs, _PK, _PV), BF16),
                        pltpu.VMEM((n, n_pairs, _PK, _PK), F32)],
        compiler_params=_params("arbitrary", "arbitrary"),
        name="gla_long" if long_seq else "gla_short",
    )(*args)


def _merge_kernel(x_ref, mod_ref, oa_ref, za_ref, of_ref, ob_ref, zg_ref, gng_ref, wo_ref, fng_ref, y_ref):
    ya = _load_stacked(oa_ref).astype(F32) * _silu(za_ref[...].astype(F32))
    og = of_ref[...].astype(F32) + ob_ref[...].astype(F32)
    zg = zg_ref[...].astype(F32)
    gng = gng_ref[...]
    parts = [ya]
    for h in range(N_GLA_HEADS):
        oh = og[:, h * GLA_DV:(h + 1) * GLA_DV]
        nh = oh * lax.rsqrt(jnp.mean(oh * oh, axis=-1, keepdims=True) + EPS)
        parts.append((nh * gng) * _silu(zg[:, h * GLA_DV:(h + 1) * GLA_DV]))
    yin = jnp.concatenate(parts, axis=-1).astype(BF16)
    gate = mod_ref[0][:, 2 * D_MODEL:3 * D_MODEL]
    y = x_ref[...] + gate * _dot(yin, wo_ref[...])
    y_ref[...] = (y * lax.rsqrt(jnp.mean(y * y, axis=-1, keepdims=True) + EPS)) * fng_ref[...]


def _merge(x2d, mod3, mod_row_of_tile, oa, za, of, ob, zg, gla_norm_g, w_out, final_norm_g, tm):
    n_tok = x2d.shape[0]
    row = lambda i: (i, 0)
    const = lambda i: (0, 0)
    ring = pl.Buffered(MERGE_INPUT_BUFFERS)
    wide = pl.BlockSpec((tm, GLA_V_WIDTH), row, pipeline_mode=ring)
    stream_specs = [
        pl.BlockSpec((tm, D_MODEL), row, pipeline_mode=ring),
        pl.BlockSpec((1, 1, 3 * D_MODEL), lambda i: (mod_row_of_tile(i), 0, 0)),
        pl.BlockSpec(_stacked_shape(tm), lambda i: (i, 0, 0), pipeline_mode=ring), wide, wide, wide, wide,
    ]

    def call(*refs):
        streams, (gng_ref, wo_ref, fng_ref, y_hbm) = refs[:len(stream_specs)], refs[len(stream_specs):]

        def step(*tiles):
            _merge_kernel(*tiles[:-1], gng_ref, wo_ref, fng_ref, tiles[-1])

        pltpu.emit_pipeline(step, grid=(n_tok // tm,), in_specs=stream_specs,
                            out_specs=[pl.BlockSpec((tm, D_MODEL), row)])(*streams, y_hbm)

    hbm = pl.BlockSpec(memory_space=pl.ANY)
    vmem = pl.BlockSpec(memory_space=pltpu.VMEM)
    return pl.pallas_call(
        call,
        in_specs=[hbm] * len(stream_specs) + [vmem] * 3,
        out_specs=hbm,
        out_shape=jax.ShapeDtypeStruct((n_tok, D_MODEL), F32),
        compiler_params=_params(),
        name="merge",
    )(x2d, mod3, oa, za, of, ob, zg, gla_norm_g, w_out, final_norm_g)


def _rope_tables(n_tok):
    t = np.arange(n_tok)
    rowp = (t // GRID_W).astype(np.float64)
    colp = (t % GRID_W).astype(np.float64)
    nf = HEAD_DIM // 4
    freqs = ROPE_BASE ** (-np.arange(nf, dtype=np.float64) / nf)
    lane = np.arange(LANES)
    pos = np.where(((lane % HEAD_DIM) < HEAD_DIM // 2)[None, :], rowp[:, None], colp[:, None])
    ang = pos * freqs[lane % nf][None, :]
    sign = np.where((lane % (2 * nf)) < nf, -1.0, 1.0)
    return jnp.asarray(np.cos(ang), F32), jnp.asarray(np.sin(ang) * sign[None, :], F32)


def kernel(x_prompt, x_sample, c, cache_k, cache_v, state_gla, c_ctx, w_mod, b_mod, norm_g, w_in,
           w_gk_f, b_gk_f, w_gk_b, b_gk_b, sink, gla_norm_g, w_out, final_norm_g):
    depth = w_in.shape[0]
    assert depth == 1, "single-layer step"
    l = 0
    bp, sp, _ = x_prompt.shape
    bs, ss, _ = x_sample.shape
    assert (bp * sp) % TOKEN_TILE == 0 and TOKEN_TILE % sp == 0 and ss % TOKEN_TILE == 0
    assert (bp * sp) % MERGE_TILE == 0 and ss % MERGE_TILE == 0
    assert bp % CTX_ATTN_BATCH == 0 and sp % BLOCK == 0 and ss % BLOCK == 0 and ss // BLOCK >= 3
    assert GLA_TILE % sp == 0 and (bp * sp) % GLA_TILE == 0 and ss % GLA_TILE == 0 and sp % CHUNK == 0
    assert ss % GRID_W == 0 and cache_k.shape[2] % LANES == 0

    wt = w_in[l].T.astype(BF16)
    wgk = jnp.zeros((LANES, 2 * GLA_K_WIDTH), F32)
    wgk = wgk.at[0:GATE_RANK, 0:GLA_K_WIDTH].set(w_gk_f[l])
    wgk = wgk.at[GATE_RANK:2 * GATE_RANK, GLA_K_WIDTH:].set(w_gk_b[l]).astype(BF16)
    bgk = jnp.concatenate([b_gk_f[l], b_gk_b[l]])[None, :]
    wo = w_out[l].astype(BF16)
    ng = norm_g[l][None, :]
    gng = gla_norm_g[l][None, :]
    fng = final_norm_g[None, :]
    sk = sink[l]

    n_rows = SUBLANES
    assert 1 + bs <= n_rows
    cvecs = jnp.concatenate([c_ctx[None, :], c, jnp.zeros((n_rows - 1 - bs, D_MODEL), F32)], axis=0)
    mod3 = _modulation(cvecs, w_mod[l], b_mod[l][None, :]).reshape(n_rows, 1, 3 * D_MODEL)
    tm = TOKEN_TILE

    xp2 = x_prompt.reshape(bp * sp, D_MODEL)
    ctx_row = lambda i: 0
    q, kvp, za, vg, zg, qkg, gf, gb, kt, vt = _project(xp2, mod3, ctx_row, ng, wt, wgk, bgk, None, tm, sp)
    oa = _context_attention(q, kvp, sk, sp)
    of, ob, st = _gla(qkg, gf, gb, vg, None, bp, sp)
    y_prompt = _merge(xp2, mod3, ctx_row, oa, za, of, ob, zg, gng, wo, fng, MERGE_TILE).reshape(bp, sp, D_MODEL)
    to_cache = lambda t: t.reshape(bp, 1, N_KV_HEADS, HEAD_DIM, sp).transpose(0, 1, 4, 2, 3)
    new_k, new_v = to_cache(kt), to_cache(vt)
    new_state = st[:, None]

    xs2 = x_sample.reshape(bs * ss, D_MODEL)
    tiles_per_seq = ss // tm
    lat_row = lambda i: 1 + i // tiles_per_seq
    q, kvp, za, vg, zg, qkg, gf, gb = _project(xs2, mod3, lat_row, ng, wt, wgk, bgk, _rope_tables(ss), tm, ss)
    past = cache_k.shape[2]
    from_cache = lambda t: t[:, l].transpose(0, 2, 3, 1).reshape(bs, KV_WIDTH, past)
    ck, cv = from_cache(cache_k), from_cache(cache_v)
    oa = _latent_attention(q, kvp, ck, cv, sk, bs, ss)
    of, ob = _gla(qkg, gf, gb, vg, state_gla[:, l], bs, ss)
    merge_row = lambda i: 1 + i // (ss // MERGE_TILE)
    y_sample = _merge(xs2, mod3, merge_row, oa, za, of, ob, zg, gng, wo, fng, MERGE_TILE).reshape(bs, ss, D_MODEL)

    return (y_prompt, y_sample, new_k, new_v, new_state)
```
